```python
import math
import jax, jax.numpy as jnp
from jax import lax
import numpy as np

D_MODEL = 1024
BATCH = 4
SEQ = 4096
DEPTH = 1
DEC_BATCH = 128
DEC_SEQ = 8
PAST_LEN = 8192
PAGE_SIZE = 128

N_Q_HEADS = 8
N_KV_HEADS = 2
GROUP = N_Q_HEADS // N_KV_HEADS
HEAD_DIM = 64
WINDOW = 128
N_BUCKETS = 32
MAX_DISTANCE = 128
ATTN_WIDTH = N_Q_HEADS * HEAD_DIM
KV_WIDTH = N_KV_HEADS * HEAD_DIM
HG_HEADS = 4
HG_KDIM = 128
HG_VDIM = 128
HG_WIDTH = HG_HEADS * HG_KDIM
HG_VWIDTH = HG_HEADS * HG_VDIM
HG_CHUNK = 64
D_FF = 2816
EPS = 1e-6
F_TINY = 1e-30
SPLIT_SIZES = (ATTN_WIDTH, KV_WIDTH, KV_WIDTH, HG_WIDTH, HG_WIDTH, HG_VWIDTH, HG_VWIDTH, D_MODEL, D_MODEL)
IN_WIDTH = sum(SPLIT_SIZES)

kernel_name = "hybrid_swa_sink_hgrn2_macaron_step"


def _rmsnorm(x, g):
    xf = x.astype(jnp.float32)
    y = xf * lax.rsqrt(jnp.mean(xf * xf, axis=-1, keepdims=True) + EPS)
    return (y * g.astype(jnp.float32)).astype(x.dtype)


def _swiglu(x, w1, w3, w2):
    return (jax.nn.silu(x @ w1) * (x @ w3)) @ w2


def _t5_bucket(dist):
    max_exact = N_BUCKETS // 2
    d = np.maximum(dist, 0)
    large = max_exact + (np.log(np.maximum(d, 1) / max_exact) / np.log(MAX_DISTANCE / max_exact)
                         * (N_BUCKETS - max_exact)).astype(np.int32)
    large = np.minimum(large, N_BUCKETS - 1)
    return np.where(d < max_exact, d, large).astype(np.int32)


def _rel_bias(table, dist):
    b = table.astype(jnp.float32)[_t5_bucket(dist)]
    return jnp.transpose(b, (2, 0, 1)).reshape(N_KV_HEADS, GROUP, dist.shape[0], dist.shape[1])


def _sink_softmax(s, sinks):
    sk = sinks.astype(jnp.float32).reshape(N_KV_HEADS, GROUP, 1, 1)
    m = jnp.maximum(jnp.max(s, axis=-1, keepdims=True), sk)
    e = jnp.exp(s - m)
    return e / (jnp.sum(e, axis=-1, keepdims=True) + jnp.exp(sk - m))


def _swa_prompt(q, k, v, table, sinks):
    B, L = q.shape[:2]
    nb = L // WINDOW
    qb = q.reshape(B, nb, WINDOW, N_KV_HEADS, GROUP, HEAD_DIM)
    kb = k.reshape(B, nb, WINDOW, N_KV_HEADS, HEAD_DIM)
    vb = v.reshape(B, nb, WINDOW, N_KV_HEADS, HEAD_DIM)
    kk = jnp.concatenate([jnp.concatenate([jnp.zeros_like(kb[:, :1]), kb[:, :-1]], axis=1), kb], axis=2)
    vv = jnp.concatenate([jnp.concatenate([jnp.zeros_like(vb[:, :1]), vb[:, :-1]], axis=1), vb], axis=2)
    s = jnp.einsum('bnqkgd,bnjkd->bnkgqj', qb, kk).astype(jnp.float32) * (HEAD_DIM ** -0.5)
    dist = np.arange(WINDOW)[:, None] + WINDOW - np.arange(2 * WINDOW)[None, :]
    valid = (dist >= 0) & (dist <= WINDOW)
    blk_valid = valid[None] & ((np.arange(nb)[:, None, None] > 0) | (np.arange(2 * WINDOW) >= WINDOW)[None, None, :])
    s = jnp.where(blk_valid[None, :, None, None], s + _rel_bias(table, dist)[None, None], -jnp.inf)
    p = _sink_softmax(s, sinks)
    o = jnp.einsum('bnkgqj,bnjkd->bnqkgd', p.astype(v.dtype), vv)
    return o.reshape(B, L, ATTN_WIDTH)


def _swa_sample(q, k, v, win_k, win_v, table, sinks):
    Bd, Ld = q.shape[:2]
    kk = jnp.concatenate([win_k.astype(k.dtype), k], axis=1)
    vv = jnp.concatenate([win_v.astype(v.dtype), v], axis=1)
    qg = q.reshape(Bd, Ld, N_KV_HEADS, GROUP, HEAD_DIM)
    s = jnp.einsum('bqkgd,bjkd->bkgqj', qg, kk).astype(jnp.float32) * (HEAD_DIM ** -0.5)
    dist = np.arange(Ld)[:, None] + WINDOW - np.arange(WINDOW + Ld)[None, :]
    valid = (dist >= 0) & (dist <= WINDOW)
    s = jnp.where(valid, s + _rel_bias(table, dist)[None], -jnp.inf)
    p = _sink_softmax(s, sinks)
    o = jnp.einsum('bkgqj,bjkd->bqkgd', p.astype(v.dtype), vv)
    return o.reshape(Bd, Ld, ATTN_WIDTH), kk[:, -WINDOW:], vv[:, -WINDOW:]


def _hgrn2(q, f_logit, v, lb, S0):
    B, L = q.shape[:2]
    C = math.gcd(L, HG_CHUNK)
    n = L // C
    f = lb + (1.0 - lb) * jax.nn.sigmoid(f_logit.astype(jnp.float32))
    log_f = jnp.log(jnp.maximum(f, F_TINY))
    k = 1.0 - f
    qf = jax.nn.silu(q.astype(jnp.float32))
    vf = v.astype(jnp.float32)

    def chunks(t, d):
        return t.reshape(B, n, C, HG_HEADS, d).transpose(1, 0, 3, 2, 4)

    tri = jnp.tril(jnp.ones((C, C), dtype=bool))[:, :, None]

    def step(S, inp):
        qc, kc, vc, lc = inp
        G = jnp.cumsum(lc, axis=2)
        inter = jnp.einsum('bhtk,bhkv->bhtv', qc * jnp.exp(G), S)
        diff = G[:, :, :, None, :] - G[:, :, None, :, :]
        decay = jnp.exp(jnp.where(tri, diff, -jnp.inf))
        att = jnp.einsum('bhtk,bhtsk->bhts', qc, decay * kc[:, :, None, :, :])
        out = inter + jnp.einsum('bhts,bhsv->bhtv', att, vc)
        G_last = G[:, :, -1:, :]
        S = jnp.exp(G_last[:, :, 0, :, None]) * S + jnp.einsum('bhsk,bhsv->bhkv', kc * jnp.exp(G_last - G), vc)
        return S, out

    S, o = lax.scan(step, S0.astype(jnp.float32),
                    (chunks(qf, HG_KDIM), chunks(k, HG_KDIM), chunks(vf, HG_VDIM), chunks(log_f, HG_KDIM)))
    o = o.transpose(1, 0, 3, 2, 4).reshape(B, L, HG_HEADS, HG_VDIM)
    return o, S


def _layer(x, win_k, win_v, S0, lb, table, w):
    B, L = x.shape[:2]
    h = x + 0.5 * _swiglu(_rmsnorm(x, w['ffn1_norm']), w['ffn1_w1'], w['ffn1_w3'], w['ffn1_w2'])
    u = _rmsnorm(h, w['mix_norm'])
    proj = u @ w['w_in']
    pts = np.cumsum(np.array(SPLIT_SIZES[:-1])).tolist()
    qa, ka, va, qr, fr, ir, gr, gate_a, gate_r = jnp.split(proj, pts, axis=-1)
    qa = _rmsnorm(qa.reshape(B, L, N_Q_HEADS, HEAD_DIM), w['q_norm'])
    ka = _rmsnorm(ka.reshape(B, L, N_KV_HEADS, HEAD_DIM), w['k_norm'])
    va = va.reshape(B, L, N_KV_HEADS, HEAD_DIM)
    if win_k is None:
        att = _swa_prompt(qa, ka, va, table, w['sinks'])
        new_k, new_v = ka[:, -WINDOW:], va[:, -WINDOW:]
    else:
        att, new_k, new_v = _swa_sample(qa, ka, va, win_k, win_v, table, w['sinks'])
    o_r, S = _hgrn2(qr, fr, ir, lb, S0)
    o_r = (_rmsnorm(o_r, w['hg_norm']).reshape(B, L, HG_VWIDTH).astype(x.dtype)) * jax.nn.silu(gr)
    merged = jax.nn.sigmoid(gate_a) * (att @ w['w_up_attn']) + jax.nn.sigmoid(gate_r) * (o_r @ w['w_up_hgrn'])
    h = h + merged @ w['w_out']
    y = h + 0.5 * _swiglu(_rmsnorm(h, w['ffn2_norm']), w['ffn2_w1'], w['ffn2_w3'], w['ffn2_w2'])
    return y, new_k, new_v, S.astype(x.dtype)


def setup_inputs(seed: int = 0) -> dict:
    key = jax.random.key(seed)
    ks = iter(jax.random.split(key, 32))

    def nrm(shape, scale):
        return jax.random.normal(next(ks), shape, jnp.float32) * scale

    def gain(shape):
        return 1.0 + nrm(shape, 0.02)

    return {
        "x_prompt": nrm((BATCH, SEQ, D_MODEL), 1.0),
        "x_sample": nrm((DEC_BATCH, DEC_SEQ, D_MODEL), 1.0),
        "cache_win_k": nrm((DEPTH, DEC_BATCH, WINDOW, N_KV_HEADS, HEAD_DIM), 1.0),
        "cache_win_v": nrm((DEPTH, DEC_BATCH, WINDOW, N_KV_HEADS, HEAD_DIM), 1.0),
        "state_hgrn": nrm((DEPTH, DEC_BATCH, HG_HEADS, HG_KDIM, HG_VDIM), 0.5),
        "ffn1_norm": gain((DEPTH, D_MODEL)),
        "ffn1_w1": nrm((DEPTH, D_MODEL, D_FF), D_MODEL ** -0.5),
        "ffn1_w3": nrm((DEPTH, D_MODEL, D_FF), D_MODEL ** -0.5),
        "ffn1_w2": nrm((DEPTH, D_FF, D_MODEL), D_FF ** -0.5),
        "mix_norm": gain((DEPTH, D_MODEL)),
        "w_in": nrm((DEPTH, D_MODEL, IN_WIDTH), D_MODEL ** -0.5),
        "q_norm": gain((DEPTH, HEAD_DIM)),
        "k_norm": gain((DEPTH, HEAD_DIM)),
        "sinks": nrm((DEPTH, N_Q_HEADS), 0.5),
        "rel_bias_table": nrm((N_BUCKETS, N_Q_HEADS), 0.5),
        "hgrn_lb_logits": nrm((DEPTH + 1, HG_WIDTH), 0.5),
        "hg_norm": gain((DEPTH, HG_VDIM)),
        "w_up_attn": nrm((DEPTH, ATTN_WIDTH, D_MODEL), ATTN_WIDTH ** -0.5),
        "w_up_hgrn": nrm((DEPTH, HG_VWIDTH, D_MODEL), HG_VWIDTH ** -0.5),
        "w_out": nrm((DEPTH, D_MODEL, D_MODEL), D_MODEL ** -0.5),
        "ffn2_norm": gain((DEPTH, D_MODEL)),
        "ffn2_w1": nrm((DEPTH, D_MODEL, D_FF), D_MODEL ** -0.5),
        "ffn2_w3": nrm((DEPTH, D_MODEL, D_FF), D_MODEL ** -0.5),
        "ffn2_w2": nrm((DEPTH, D_FF, D_MODEL), D_FF ** -0.5),
    }


def reference(x_prompt, x_sample, cache_win_k, cache_win_v, state_hgrn,
              ffn1_norm, ffn1_w1, ffn1_w3, ffn1_w2, mix_norm, w_in, q_norm, k_norm, sinks,
              rel_bias_table, hgrn_lb_logits, hg_norm, w_up_attn, w_up_hgrn, w_out,
              ffn2_norm, ffn2_w1, ffn2_w3, ffn2_w2):
    lb_all = jnp.cumsum(jax.nn.softmax(hgrn_lb_logits.astype(jnp.float32), axis=0), axis=0)
    yp, ys = x_prompt, x_sample
    pk, pv, ps, sk, sv, ss = [], [], [], [], [], []
    for l in range(DEPTH):
        w = {
            'ffn1_norm': ffn1_norm[l], 'ffn1_w1': ffn1_w1[l], 'ffn1_w3': ffn1_w3[l], 'ffn1_w2': ffn1_w2[l],
            'mix_norm': mix_norm[l], 'w_in': w_in[l], 'q_norm': q_norm[l], 'k_norm': k_norm[l],
            'sinks': sinks[l], 'hg_norm': hg_norm[l], 'w_up_attn': w_up_attn[l], 'w_up_hgrn': w_up_hgrn[l],
            'w_out': w_out[l], 'ffn2_norm': ffn2_norm[l], 'ffn2_w1': ffn2_w1[l], 'ffn2_w3': ffn2_w3[l],
            'ffn2_w2': ffn2_w2[l],
        }
        S0 = jnp.zeros((yp.shape[0], HG_HEADS, HG_KDIM, HG_VDIM), jnp.float32)
        yp, k1, v1, s1 = _layer(yp, None, None, S0, lb_all[l], rel_bias_table, w)
        ys, k2, v2, s2 = _layer(ys, cache_win_k[l], cache_win_v[l], state_hgrn[l], lb_all[l], rel_bias_table, w)
        pk.append(k1); pv.append(v1); ps.append(s1)
        sk.append(k2); sv.append(v2); ss.append(s2)
    return (yp, ys, jnp.stack(pk), jnp.stack(pv), jnp.stack(ps), jnp.stack(sk), jnp.stack(sv), jnp.stack(ss))
```

```python
import functools

import numpy as np
import jax
import jax.numpy as jnp
from jax import lax
from jax.experimental import pallas as pl
from jax.experimental.pallas import tpu as pltpu

F32 = jnp.float32
BF16 = jnp.bfloat16

LANES = 128
HEAD_DIM = 64
N_Q_HEADS = 8
N_KV_HEADS = 2
GROUP = N_Q_HEADS // N_KV_HEADS
WINDOW = 128
N_BUCKETS = 32
MAX_DISTANCE = 128
HG_HEADS = 4
HG_DIM = 128
EPS = 1e-6
F_TINY = 1e-30
NEG_BIG = -1e30
EXP_CLAMP = 80.0
SPLIT_SIZES = (512, 128, 128, 512, 512, 512, 512, 1024, 1024)
SPLIT_OFFS = tuple(int(v) for v in np.cumsum((0,) + SPLIT_SIZES))
VMEM_LIMIT = 56 * 1024 * 1024


def _dot(a, b):
    return jnp.dot(a, b, preferred_element_type=F32)


def _dot_nt(a, b):
    return lax.dot_general(a, b, (((1,), (1,)), ((), ())), preferred_element_type=F32)


def _split3(x):
    hi = x.astype(BF16)
    r1 = x - hi.astype(F32)
    mid = r1.astype(BF16)
    lo = (r1 - mid.astype(F32)).astype(BF16)
    return hi, mid, lo


def _split3_dot(x, w):
    hi, mid, lo = _split3(x)
    return _dot(hi, w) + _dot(mid, w) + _dot(lo, w)


def _split3_ldot(w, x):
    hi, mid, lo = _split3(x)
    return _dot(w, hi) + _dot(w, mid) + _dot(w, lo)


def _rms_rows(x, g):
    ms = jnp.mean(x * x, axis=-1, keepdims=True)
    return x * lax.rsqrt(ms + EPS) * g


def _seg_rms(x, seg, inv_n, g):
    ms = _split3_dot(x * x, seg) * inv_n
    return x * lax.rsqrt(ms + EPS) * g


def _const_spec(shape):
    nd = len(shape)
    return pl.BlockSpec(shape, lambda *_: (0,) * nd, pipeline_mode=pl.Buffered(1))


def _ff_chunks(d_ff, step=1024):
    return tuple((lo, min(lo + step, d_ff)) for lo in range(0, d_ff, step))


def _ffn_proj_kernel(x_ref, g1_ref, w1_ref, w3_ref, w2_ref, gm_ref, win_ref, qg_ref, kg_ref, seg_ref,
                     h_ref, q_ref, k_ref, v_ref, qr_ref, fr_ref, ir_ref, gr_ref, ga_ref, gb_ref):
    x = x_ref[...]
    xn = _rms_rows(x, g1_ref[...]).astype(BF16)
    acc = jnp.zeros(x.shape, F32)
    for lo, hi in _ff_chunks(w1_ref.shape[1]):
        a = _dot(xn, w1_ref[:, lo:hi])
        b = _dot(xn, w3_ref[:, lo:hi])
        acc = acc + _dot((jax.nn.silu(a) * b).astype(BF16), w2_ref[lo:hi, :])
    h = x + 0.5 * acc
    h_ref[...] = h
    u = _rms_rows(h, gm_ref[...]).astype(BF16)

    seg = seg_ref[...]
    o = SPLIT_OFFS
    for s in range(SPLIT_SIZES[0] // LANES):
        qs = _dot(u, win_ref[:, o[0] + s * LANES:o[0] + (s + 1) * LANES])
        q_ref[:, s * LANES:(s + 1) * LANES] = _seg_rms(qs, seg, 1.0 / HEAD_DIM, qg_ref[...]).astype(q_ref.dtype)
    ks = _dot(u, win_ref[:, o[1]:o[2]])
    k_ref[...] = _seg_rms(ks, seg, 1.0 / HEAD_DIM, kg_ref[...])
    v_ref[...] = _dot(u, win_ref[:, o[2]:o[3]])
    qr_ref[...] = _dot(u, win_ref[:, o[3]:o[4]])
    fr_ref[...] = _dot(u, win_ref[:, o[4]:o[5]])
    ir_ref[...] = _dot(u, win_ref[:, o[5]:o[6]]).astype(ir_ref.dtype)
    gr_ref[...] = _dot(u, win_ref[:, o[6]:o[7]])
    ga_ref[...] = _dot(u, win_ref[:, o[7]:o[8]])
    gb_ref[...] = _dot(u, win_ref[:, o[8]:o[9]])


def _ffn_proj(x, g1, w1, w3, w2, gm, w_in, qg, kg, seg, tm):
    t, d = x.shape
    row = lambda w: pl.BlockSpec((tm, w), lambda i: (i, 0))
    out_widths = (d,) + SPLIT_SIZES
    out_dtypes = (F32, BF16, F32, F32, F32, F32, BF16, F32, F32, F32)
    return pl.pallas_call(
        _ffn_proj_kernel,
        grid=(t // tm,),
        in_specs=[row(d), _const_spec(g1.shape), _const_spec(w1.shape), _const_spec(w3.shape),
                  _const_spec(w2.shape), _const_spec(gm.shape), _const_spec(w_in.shape),
                  _const_spec(qg.shape), _const_spec(kg.shape), _const_spec(seg.shape)],
        out_specs=[row(w) for w in out_widths],
        out_shape=[jax.ShapeDtypeStruct((t, w), dt) for w, dt in zip(out_widths, out_dtypes)],
        compiler_params=pltpu.CompilerParams(dimension_semantics=("parallel",), vmem_limit_bytes=VMEM_LIMIT),
        name="ffn_proj",
    )(x, g1, w1, w3, w2, gm, w_in, qg, kg, seg)


def _t5_bucket(dist):
    max_exact = N_BUCKETS // 2
    d = np.maximum(dist, 0)
    large = max_exact + (np.log(np.maximum(d, 1) / max_exact) / np.log(MAX_DISTANCE / max_exact)
                         * (N_BUCKETS - max_exact)).astype(np.int32)
    large = np.minimum(large, N_BUCKETS - 1)
    return np.where(d < max_exact, d, large).astype(np.int32)


def _bias_kernel(table_ref, bucket_ref, out_ref):
    bucket = bucket_ref[...]
    masked = jnp.where(bucket < 0, NEG_BIG, 0.0).astype(F32)
    for h in range(N_Q_HEADS):
        acc = masked
        for b in range(N_BUCKETS):
            acc = acc + jnp.where(bucket == b, table_ref[b, h], 0.0)
        out_ref[h] = acc


def _rel_bias(table, bucket_map):
    r, c = bucket_map.shape
    return pl.pallas_call(
        _bias_kernel,
        in_specs=[pl.BlockSpec(memory_space=pltpu.SMEM), pl.BlockSpec((r, c), lambda: (0, 0))],
        out_specs=pl.BlockSpec((N_Q_HEADS, r, c), lambda: (0, 0, 0)),
        out_shape=jax.ShapeDtypeStruct((N_Q_HEADS, r, c), F32),
        name="rel_bias",
    )(table, jnp.asarray(bucket_map))


def _stack_group(q, kv):
    a = q[:, (2 * kv) * LANES:(2 * kv + 1) * LANES]
    b = q[:, (2 * kv + 1) * LANES:(2 * kv + 2) * LANES]
    ar, br = pltpu.roll(a, HEAD_DIM, 1), pltpu.roll(b, HEAD_DIM, 1)
    parts = (a, ar, b, br) if kv == 0 else (ar, a, br, b)
    return jnp.concatenate(parts, axis=0)


def _kv_half(x, kv, duplicate):
    lane = lax.broadcasted_iota(jnp.int32, x.shape, 1)
    keep = (lane < HEAD_DIM) if kv == 0 else (lane >= HEAD_DIM)
    xm = jnp.where(keep, x, 0.0)
    return xm + pltpu.roll(xm, HEAD_DIM, 1) if duplicate else xm


def _sink_softmax(s, sink):
    m = jnp.maximum(jnp.max(s, axis=-1, keepdims=True), sink)
    e = jnp.exp(s - m)
    return e / (jnp.sum(e, axis=-1, keepdims=True) + jnp.exp(sink - m))


def _attend(q, kk, vv, bias_ref, sink_ref, r):
    outs = []
    for kv in range(N_KV_HEADS):
        qs = _stack_group(q.astype(F32), kv).astype(BF16)
        kh = _kv_half(kk, kv, False).astype(BF16)
        s = _dot_nt(qs, kh) * (HEAD_DIM ** -0.5) + bias_ref[kv]
        p = _sink_softmax(s, sink_ref[kv])
        o = _dot(p.astype(BF16), _kv_half(vv, kv, True).astype(BF16))
        lane = lax.broadcasted_iota(jnp.int32, (r, LANES), 1)
        for pair in range(GROUP // 2):
            outs.append(jnp.where(lane < HEAD_DIM, o[(2 * pair) * r:(2 * pair + 1) * r],
                                  o[(2 * pair + 1) * r:(2 * pair + 2) * r]))
    return jnp.concatenate(outs, axis=1)


def _swa_prompt_kernel(q_ref, kp_ref, kc_ref, vp_ref, vc_ref, bias_ref, sink_ref, o_ref):
    kk = jnp.concatenate([kp_ref[...], kc_ref[...]], axis=0)
    vv = jnp.concatenate([vp_ref[...], vc_ref[...]], axis=0)
    o_ref[...] = _attend(q_ref[...], kk, vv, bias_ref.at[0], sink_ref, WINDOW).astype(o_ref.dtype)


def _swa_prompt(q, k, v, bias, sink_col, batch, nb):
    cur = lambda b, n: (b * nb + n, 0)
    prev = lambda b, n: (b * nb + jnp.maximum(n - 1, 0), 0)
    kv_spec = lambda im: pl.BlockSpec((WINDOW, LANES), im)
    return pl.pallas_call(
        _swa_prompt_kernel,
        grid=(batch, nb),
        in_specs=[pl.BlockSpec((WINDOW, 4 * LANES), cur), kv_spec(prev), kv_spec(cur), kv_spec(prev), kv_spec(cur),
                  pl.BlockSpec((1, N_KV_HEADS, GROUP * WINDOW, 2 * WINDOW), lambda b, n: (jnp.minimum(n, 1), 0, 0, 0)),
                  pl.BlockSpec(sink_col.shape, lambda b, n: (0, 0, 0))],
        out_specs=pl.BlockSpec((WINDOW, 4 * LANES), cur),
        out_shape=jax.ShapeDtypeStruct(q.shape, BF16),
        compiler_params=pltpu.CompilerParams(dimension_semantics=("parallel", "parallel")),
        name="swa_prompt",
    )(q, k, k, v, v, bias, sink_col)


def _swa_sample_kernel(q_ref, kn_ref, vn_ref, ck_ref, cv_ref, bias_ref, sink_ref, o_ref, nk_ref, nv_ref, *, ld):
    for i in range(ck_ref.shape[0]):
        rows = slice(i * ld, (i + 1) * ld)
        kk = jnp.concatenate([ck_ref[i], kn_ref[rows, :]], axis=0)
        vv = jnp.concatenate([cv_ref[i], vn_ref[rows, :]], axis=0)
        o_ref[rows, :] = _attend(q_ref[rows, :], kk, vv, bias_ref, sink_ref, ld).astype(o_ref.dtype)
        nk_ref[i] = kk[ld:, :]
        nv_ref[i] = vv[ld:, :]


def _swa_sample(q, k, v, cache_k, cache_v, bias, sink_col, ld, sb):
    bd = cache_k.shape[0]
    tok = lambda w: pl.BlockSpec((sb * ld, w), lambda i: (i, 0))
    cache = pl.BlockSpec((sb, WINDOW, LANES), lambda i: (i, 0, 0))
    return pl.pallas_call(
        functools.partial(_swa_sample_kernel, ld=ld),
        grid=(bd // sb,),
        in_specs=[tok(4 * LANES), tok(LANES), tok(LANES), cache, cache,
                  pl.BlockSpec(bias.shape, lambda i: (0, 0, 0)), pl.BlockSpec(sink_col.shape, lambda i: (0, 0, 0))],
        out_specs=[tok(4 * LANES), cache, cache],
        out_shape=[jax.ShapeDtypeStruct(q.shape, BF16), jax.ShapeDtypeStruct(cache_k.shape, F32),
                   jax.ShapeDtypeStruct(cache_v.shape, F32)],
        compiler_params=pltpu.CompilerParams(dimension_semantics=("parallel",)),
        name="swa_sample",
    )(q, k, v, cache_k, cache_v, bias, sink_col)


def _lower_bound(lb_logits):
    z = lb_logits - jnp.max(lb_logits, axis=0, keepdims=True)
    e = jnp.exp(z)
    return e[0:1, :] / jnp.sum(e, axis=0, keepdims=True)


def _gates(qr, fr, lb):
    f = lb + (1.0 - lb) * jax.nn.sigmoid(fr)
    return jax.nn.silu(qr), 1.0 - f, jnp.log(jnp.maximum(f, F_TINY))


def _head_out(o, gr, hg_gain, ones_seg):
    return _seg_rms(o, ones_seg, 1.0 / HG_DIM, hg_gain) * jax.nn.silu(gr)


def _hgrn_prompt_kernel(qr_ref, fr_ref, ir_ref, gr_ref, lb_ref, hg_ref, tri_ref, ones_ref,
                        o_ref, s_out_ref, s_ref, *, sub):
    c = qr_ref.shape[0]

    @pl.when(pl.program_id(1) == 0)
    def _():
        s_ref[...] = jnp.zeros(s_ref.shape, F32)

    lb = _lower_bound(lb_ref[...])
    qf_all, kk_all, g_all = _gates(qr_ref[...], fr_ref[...], lb)
    tri = tri_ref[...]
    row = lax.broadcasted_iota(jnp.int32, (sub, c), 0)
    col = lax.broadcasted_iota(jnp.int32, (sub, c), 1)
    for hd in range(HG_HEADS):
        lanes = slice(hd * HG_DIM, (hd + 1) * HG_DIM)
        qf, kk, v = qf_all[:, lanes], kk_all[:, lanes], ir_ref[:, lanes]
        gc = _split3_ldot(tri, g_all[:, lanes])
        g_last = gc[c - 1:c, :]
        s_prev = s_ref[hd]
        inter = _dot((qf * jnp.exp(gc)).astype(BF16), s_prev.astype(BF16))
        att = []
        for i in range(c // sub):
            g_base = gc[i * sub - 1:i * sub, :] if i else jnp.zeros((1, HG_DIM), F32)
            q_hat = qf[i * sub:(i + 1) * sub] * jnp.exp(gc[i * sub:(i + 1) * sub] - g_base)
            k_hat = kk * jnp.exp(jnp.minimum(g_base - gc, EXP_CLAMP))
            a = _dot_nt(q_hat.astype(BF16), k_hat.astype(BF16))
            att.append(jnp.where(col <= row + i * sub, a, 0.0))
        att = jnp.concatenate(att, axis=0)
        o = inter + _dot(att.astype(BF16), v)
        k_end_t = (kk * jnp.exp(g_last - gc)).T
        decay = jnp.broadcast_to(jnp.exp(g_last), (HG_DIM, HG_DIM)).T
        s_ref[hd] = decay * s_prev + _dot(k_end_t.astype(BF16), v)
        o_ref[:, lanes] = _head_out(o, gr_ref[:, lanes], hg_ref[...], ones_ref[...]).astype(o_ref.dtype)

    @pl.when(pl.program_id(1) == pl.num_programs(1) - 1)
    def _():
        s_out_ref[0] = s_ref[...]


def _hgrn_prompt(qr, fr, ir, gr, lb_logits, hg_gain, batch, nchunk, chunk, sub):
    tri = jnp.asarray(np.tril(np.ones((chunk, chunk), np.float32)), BF16)
    ones = jnp.ones((HG_DIM, HG_DIM), BF16)
    tok = pl.BlockSpec((chunk, HG_HEADS * HG_DIM), lambda b, n: (b * nchunk + n, 0))
    const = lambda a: pl.BlockSpec(a.shape, lambda b, n: (0,) * a.ndim)
    return pl.pallas_call(
        functools.partial(_hgrn_prompt_kernel, sub=sub),
        grid=(batch, nchunk),
        in_specs=[tok, tok, tok, tok, const(lb_logits), const(hg_gain), const(tri), const(ones)],
        out_specs=[tok, pl.BlockSpec((1, HG_HEADS, HG_DIM, HG_DIM), lambda b, n: (b, 0, 0, 0))],
        out_shape=[jax.ShapeDtypeStruct(qr.shape, BF16),
                   jax.ShapeDtypeStruct((batch, HG_HEADS, HG_DIM, HG_DIM), F32)],
        scratch_shapes=[pltpu.VMEM((HG_HEADS, HG_DIM, HG_DIM), F32)],
        compiler_params=pltpu.CompilerParams(dimension_semantics=("parallel", "arbitrary")),
        name="hgrn_prompt",
    )(qr, fr, ir, gr, lb_logits, hg_gain, tri, ones)


def _hgrn_sample_kernel(qr_ref, fr_ref, ir_ref, gr_ref, s0_ref, lb_ref, hg_ref, tri_ref, tot_ref, ones_ref,
                        o_ref, s_out_ref, *, ld):
    nseq = s0_ref.shape[0]
    r = nseq * ld
    lb = _lower_bound(lb_ref[...])
    qf_all, kk_all, g_all = _gates(qr_ref[...], fr_ref[...], lb)
    tri, tot = tri_ref[...], tot_ref[...]
    causal = tri.astype(F32) > 0
    lane = lax.broadcasted_iota(jnp.int32, (HG_DIM, r), 1)
    for hd in range(HG_HEADS):
        lanes = slice(hd * HG_DIM, (hd + 1) * HG_DIM)
        qf, kk, v = qf_all[:, lanes], kk_all[:, lanes], ir_ref[:, lanes]
        g = _split3_ldot(tri, g_all[:, lanes])
        g_last = _split3_ldot(tot, g_all[:, lanes])
        q_t = (qf * jnp.exp(g)).astype(BF16)
        k_t = kk * jnp.exp(jnp.minimum(-g, EXP_CLAMP))
        a = jnp.where(causal, _dot_nt(q_t, k_t.astype(BF16)), 0.0)
        intra = _dot(a.astype(BF16), v)
        k_end_t = (kk * jnp.exp(g_last - g)).T
        decay_t = jnp.exp(g_last).T
        outs = []
        for i in range(nseq):
            s_prev = s0_ref[i, hd]
            outs.append(_dot(q_t[i * ld:(i + 1) * ld], s_prev.astype(BF16)))
            own = (lane >= i * ld) & (lane < (i + 1) * ld)
            upd = _dot(jnp.where(own, k_end_t, 0.0).astype(BF16), v)
            s_out_ref[i, hd] = decay_t[:, i * ld:i * ld + 1] * s_prev + upd
        o = jnp.concatenate(outs, axis=0) + intra
        o_ref[:, lanes] = _head_out(o, gr_ref[:, lanes], hg_ref[...], ones_ref[...]).astype(o_ref.dtype)


def _hgrn_sample(qr, fr, ir, gr, s0, lb_logits, hg_gain, ld, sb):
    bd = s0.shape[0]
    r = sb * ld
    seq = np.arange(r) // ld
    same = seq[:, None] == seq[None, :]
    tri = jnp.asarray((same & (np.arange(r)[:, None] >= np.arange(r)[None, :])).astype(np.float32), BF16)
    tot = jnp.asarray(same.astype(np.float32), BF16)
    ones = jnp.ones((HG_DIM, HG_DIM), BF16)
    tok = pl.BlockSpec((r, HG_HEADS * HG_DIM), lambda i: (i, 0))
    state = pl.BlockSpec((sb, HG_HEADS, HG_DIM, HG_DIM), lambda i: (i, 0, 0, 0))
    const = lambda a: pl.BlockSpec(a.shape, lambda i: (0,) * a.ndim)
    return pl.pallas_call(
        functools.partial(_hgrn_sample_kernel, ld=ld),
        grid=(bd // sb,),
        in_specs=[tok, tok, tok, tok, state, const(lb_logits), const(hg_gain), const(tri), const(tot), const(ones)],
        out_specs=[tok, state],
        out_shape=[jax.ShapeDtypeStruct(qr.shape, BF16), jax.ShapeDtypeStruct(s0.shape, F32)],
        compiler_params=pltpu.CompilerParams(dimension_semantics=("parallel",)),
        name="hgrn_sample",
    )(qr, fr, ir, gr, s0, lb_logits, hg_gain, tri, tot, ones)


def _out_ffn_kernel(h_ref, att_ref, orr_ref, ga_ref, gb_ref, wua_ref, wur_ref, wo_ref, g2_ref,
                    w1_ref, w3_ref, w2_ref, y_ref):
    merged = (jax.nn.sigmoid(ga_ref[...]) * _dot(att_ref[...], wua_ref[...])
              + jax.nn.sigmoid(gb_ref[...]) * _dot(orr_ref[...], wur_ref[...]))
    h = h_ref[...] + _dot(merged.astype(BF16), wo_ref[...])
    hn = _rms_rows(h, g2_ref[...]).astype(BF16)
    acc = jnp.zeros(h.shape, F32)
    for lo, hi in _ff_chunks(w1_ref.shape[1]):
        a = _dot(hn, w1_ref[:, lo:hi])
        b = _dot(hn, w3_ref[:, lo:hi])
        acc = acc + _dot((jax.nn.silu(a) * b).astype(BF16), w2_ref[lo:hi, :])
    y_ref[...] = h + 0.5 * acc


def _out_ffn(h, att, orr, ga, gb, wua, wur, wo, g2, w1, w3, w2, tm):
    t, d = h.shape
    row = lambda a: pl.BlockSpec((tm, a.shape[1]), lambda i: (i, 0))
    consts = (wua, wur, wo, g2, w1, w3, w2)
    return pl.pallas_call(
        _out_ffn_kernel,
        grid=(t // tm,),
        in_specs=[row(a) for a in (h, att, orr, ga, gb)] + [_const_spec(a.shape) for a in consts],
        out_specs=pl.BlockSpec((tm, d), lambda i: (i, 0)),
        out_shape=jax.ShapeDtypeStruct((t, d), F32),
        compiler_params=pltpu.CompilerParams(dimension_semantics=("parallel",), vmem_limit_bytes=VMEM_LIMIT),
        name="out_ffn",
    )(h, att, orr, ga, gb, *consts)


def _bucket_maps(ld):
    dist_p = np.arange(WINDOW)[:, None] + WINDOW - np.arange(2 * WINDOW)[None, :]
    valid_p = (dist_p >= 0) & (dist_p <= WINDOW)
    bucket_p = np.where(valid_p, _t5_bucket(dist_p), -1).astype(np.int32)
    first = np.where(np.arange(2 * WINDOW)[None, :] >= WINDOW, bucket_p, -1).astype(np.int32)
    dist_s = np.arange(ld)[:, None] + WINDOW - np.arange(WINDOW + ld)[None, :]
    valid_s = (dist_s >= 0) & (dist_s <= WINDOW)
    bucket_s = np.where(valid_s, _t5_bucket(dist_s), -1).astype(np.int32)
    return first, bucket_p, bucket_s


def _stacked(bias, rows):
    return bias.reshape(N_KV_HEADS, GROUP * rows, bias.shape[-1])


def kernel(x_prompt, x_sample, cache_win_k, cache_win_v, state_hgrn, ffn1_norm, ffn1_w1, ffn1_w3, ffn1_w2, mix_norm, w_in, q_norm, k_norm, sinks, rel_bias_table, hgrn_lb_logits, hg_norm, w_up_attn, w_up_hgrn, w_out, ffn2_norm, ffn2_w1, ffn2_w3, ffn2_w2):
    depth = ffn1_norm.shape[0]
    assert depth == 1 and hgrn_lb_logits.shape[0] == 2, "single-layer step only"
    batch, seq, d = x_prompt.shape
    bd, ld, _ = x_sample.shape
    nb = seq // WINDOW

    bf = lambda w: w[0].astype(BF16)
    row = lambda g: g[0].reshape(1, -1).astype(F32)
    w1a, w3a, w2a, w_in_b = bf(ffn1_w1), bf(ffn1_w3), bf(ffn1_w2), bf(w_in)
    w1b, w3b, w2b = bf(ffn2_w1), bf(ffn2_w3), bf(ffn2_w2)
    wua, wur, wo = bf(w_up_attn), bf(w_up_hgrn), bf(w_out)
    qg = jnp.tile(row(q_norm), (1, LANES // HEAD_DIM))
    kg = jnp.tile(row(k_norm), (1, LANES // HEAD_DIM))
    head_of_lane = np.arange(LANES) // HEAD_DIM
    seg = jnp.asarray((head_of_lane[:, None] == head_of_lane[None, :]).astype(np.float32), BF16)
    lb_logits = hgrn_lb_logits.astype(F32)
    hg_gain = row(hg_norm)

    first, later, bucket_s = _bucket_maps(ld)
    table = rel_bias_table.astype(F32)
    bias_p = jnp.stack([_stacked(_rel_bias(table, first), WINDOW), _stacked(_rel_bias(table, later), WINDOW)])
    bias_s = _stacked(_rel_bias(table, bucket_s), ld)
    sink = sinks[0].astype(F32).reshape(N_KV_HEADS, GROUP, 1)
    sink_p = jnp.repeat(sink, WINDOW, axis=1).reshape(N_KV_HEADS, GROUP * WINDOW, 1)
    sink_s = jnp.repeat(sink, ld, axis=1).reshape(N_KV_HEADS, GROUP * ld, 1)

    proj = functools.partial(_ffn_proj, g1=row(ffn1_norm), w1=w1a, w3=w3a, w2=w2a, gm=row(mix_norm),
                             w_in=w_in_b, qg=qg, kg=kg, seg=seg, tm=256)
    fin = functools.partial(_out_ffn, wua=wua, wur=wur, wo=wo, g2=row(ffn2_norm), w1=w1b, w3=w3b, w2=w2b, tm=256)

    h, q, k, v, qr, fr, ir, gr, ga, gb = proj(x_prompt.reshape(batch * seq, d))
    att = _swa_prompt(q, k, v, bias_p, sink_p, batch, nb)
    orr, s_p = _hgrn_prompt(qr, fr, ir, gr, lb_logits, hg_gain, batch, seq // 128, 128, 32)
    y_p = fin(h, att, orr, ga, gb).reshape(batch, seq, d)
    win = lambda a: a.reshape(batch, seq, N_KV_HEADS, HEAD_DIM)[:, seq - WINDOW:][None]
    pk, pv, ps = win(k), win(v), s_p[None]

    h, q, k, v, qr, fr, ir, gr, ga, gb = proj(x_sample.reshape(bd * ld, d))
    ck = cache_win_k[0].reshape(bd, WINDOW, LANES)
    cv = cache_win_v[0].reshape(bd, WINDOW, LANES)
    att, nk, nv = _swa_sample(q, k, v, ck, cv, bias_s, sink_s, ld, 8)
    orr, s_s = _hgrn_sample(qr, fr, ir, gr, state_hgrn[0], lb_logits, hg_gain, ld, 8)
    y_s = fin(h, att, orr, ga, gb).reshape(bd, ld, d)
    unwin = lambda a: a.reshape(1, bd, WINDOW, N_KV_HEADS, HEAD_DIM)
    return (y_p, y_s, pk, pv, ps, unwin(nk), unwin(nv), s_s[None])
```

```python
import functools

import numpy as np
import jax
import jax.numpy as jnp
from jax import lax
from jax.experimental import pallas as pl
from jax.experimental.pallas import tpu as pltpu

F32 = jnp.float32
BF16 = jnp.bfloat16

LANES = 128
HEAD_DIM = 64
N_Q_HEADS = 8
N_KV_HEADS = 2
GROUP = N_Q_HEADS // N_KV_HEADS
WINDOW = 128
N_BUCKETS = 32
MAX_DISTANCE = 128
HG_HEADS = 4
HG_DIM = 128
HG_CHUNK = 128
HG_SUB = 32
EPS = 1e-6
F_TINY = 1e-30
NEG_BIG = -1e30
EXP_CLAMP = 80.0
SPLIT_SIZES = (512, 128, 128, 512, 512, 512, 512, 1024, 1024)
SPLIT_OFFS = tuple(int(v) for v in np.cumsum((0,) + SPLIT_SIZES))
O_Q, O_K, O_V, O_QR, O_FR, O_IR, O_GR, O_GA, O_GB, O_END = SPLIT_OFFS
VMEM_LIMIT = 56 * 1024 * 1024


def _dot(a, b):
    return jnp.dot(a, b, preferred_element_type=F32)


def _dot_nt(a, b):
    return lax.dot_general(a, b, (((1,), (1,)), ((), ())), preferred_element_type=F32)


def _split3(x):
    hi = x.astype(BF16)
    r1 = x - hi.astype(F32)
    mid = r1.astype(BF16)
    lo = (r1 - mid.astype(F32)).astype(BF16)
    return hi, mid, lo


def _split3_dot(x, w):
    hi, mid, lo = _split3(x)
    return _dot(hi, w) + _dot(mid, w) + _dot(lo, w)


def _split3_ldot(w, x):
    hi, mid, lo = _split3(x)
    return _dot(w, hi) + _dot(w, mid) + _dot(w, lo)


def _rms_rows(x, g):
    ms = jnp.mean(x * x, axis=-1, keepdims=True)
    return x * lax.rsqrt(ms + EPS) * g


def _seg_rms(x, seg, inv_n, g):
    ms = _split3_dot(x * x, seg) * inv_n
    return x * lax.rsqrt(ms + EPS) * g


def _const_spec(a):
    nd = a.ndim
    return pl.BlockSpec(a.shape, lambda *_: (0,) * nd, pipeline_mode=pl.Buffered(1))


def _ff_chunks(d_ff, step=1024):
    return tuple((lo, min(lo + step, d_ff)) for lo in range(0, d_ff, step))


def _swiglu(xn, w1_ref, w3_ref, w2_ref):
    acc = jnp.zeros((xn.shape[0], w2_ref.shape[1]), F32)
    for lo, hi in _ff_chunks(w1_ref.shape[1]):
        a = _dot(xn, w1_ref[:, lo:hi])
        b = _dot(xn, w3_ref[:, lo:hi])
        acc = acc + _dot((jax.nn.silu(a) * b).astype(BF16), w2_ref[lo:hi, :])
    return acc


def _ffn1_kernel(x_ref, g1_ref, w1_ref, w3_ref, w2_ref, h_ref):
    x = x_ref[...]
    xn = _rms_rows(x, g1_ref[...]).astype(BF16)
    h_ref[...] = x + 0.5 * _swiglu(xn, w1_ref, w3_ref, w2_ref)


def _ffn1(x, g1, w1, w3, w2, tm):
    t, d = x.shape
    row = pl.BlockSpec((tm, d), lambda i: (i, 0))
    return pl.pallas_call(
        _ffn1_kernel,
        grid=(t // tm,),
        in_specs=[row] + [_const_spec(a) for a in (g1, w1, w3, w2)],
        out_specs=row,
        out_shape=jax.ShapeDtypeStruct((t, d), F32),
        compiler_params=pltpu.CompilerParams(dimension_semantics=("parallel",), vmem_limit_bytes=VMEM_LIMIT),
        name="ffn1",
    )(x, g1, w1, w3, w2)


def _t5_bucket(dist):
    max_exact = N_BUCKETS // 2
    d = np.maximum(dist, 0)
    large = max_exact + (np.log(np.maximum(d, 1) / max_exact) / np.log(MAX_DISTANCE / max_exact)
                         * (N_BUCKETS - max_exact)).astype(np.int32)
    large = np.minimum(large, N_BUCKETS - 1)
    return np.where(d < max_exact, d, large).astype(np.int32)


def _bias_kernel(table_ref, bucket_ref, out_ref):
    bucket = bucket_ref[...]
    masked = jnp.where(bucket < 0, NEG_BIG, 0.0).astype(F32)
    for h in range(N_Q_HEADS):
        acc = masked
        for b in range(N_BUCKETS):
            acc = acc + jnp.where(bucket == b, table_ref[b, h], 0.0)
        out_ref[h] = acc


def _rel_bias(table, bucket_map):
    r, c = bucket_map.shape
    return pl.pallas_call(
        _bias_kernel,
        in_specs=[pl.BlockSpec(memory_space=pltpu.SMEM), pl.BlockSpec((r, c), lambda: (0, 0))],
        out_specs=pl.BlockSpec((N_Q_HEADS, r, c), lambda: (0, 0, 0)),
        out_shape=jax.ShapeDtypeStruct((N_Q_HEADS, r, c), F32),
        name="rel_bias",
    )(table, jnp.asarray(bucket_map))


def _project_qkv(u, win_ref, qg, kg, seg):
    qs = []
    for s in range(SPLIT_SIZES[0] // LANES):
        x = _dot(u, win_ref[:, O_Q + s * LANES:O_Q + (s + 1) * LANES])
        qs.append(_seg_rms(x, seg, 1.0 / HEAD_DIM, qg).astype(BF16))
    k = _seg_rms(_dot(u, win_ref[:, O_K:O_V]), seg, 1.0 / HEAD_DIM, kg)
    v = _dot(u, win_ref[:, O_V:O_QR])
    return jnp.concatenate(qs, axis=1), k, v


def _stack_group(q, kv):
    a = q[:, (2 * kv) * LANES:(2 * kv + 1) * LANES]
    b = q[:, (2 * kv + 1) * LANES:(2 * kv + 2) * LANES]
    ar, br = pltpu.roll(a, HEAD_DIM, 1), pltpu.roll(b, HEAD_DIM, 1)
    parts = (a, ar, b, br) if kv == 0 else (ar, a, br, b)
    return jnp.concatenate(parts, axis=0)


def _kv_half(x, kv, duplicate):
    lane = lax.broadcasted_iota(jnp.int32, x.shape, 1)
    keep = (lane < HEAD_DIM) if kv == 0 else (lane >= HEAD_DIM)
    xm = jnp.where(keep, x, 0.0)
    return xm + pltpu.roll(xm, HEAD_DIM, 1) if duplicate else xm


def _sink_softmax(s, sink):
    m = jnp.maximum(jnp.max(s, axis=-1, keepdims=True), sink)
    e = jnp.exp(s - m)
    return e / (jnp.sum(e, axis=-1, keepdims=True) + jnp.exp(sink - m))


def _attend(q, kk, vv, bias_ref, sink_ref, r):
    outs = []
    for kv in range(N_KV_HEADS):
        qs = _stack_group(q.astype(F32), kv).astype(BF16)
        kh = _kv_half(kk, kv, False).astype(BF16)
        s = _dot_nt(qs, kh) * (HEAD_DIM ** -0.5) + bias_ref[kv]
        p = _sink_softmax(s, sink_ref[kv])
        o = _dot(p.astype(BF16), _kv_half(vv, kv, True).astype(BF16))
        lane = lax.broadcasted_iota(jnp.int32, (r, LANES), 1)
        for pair in range(GROUP // 2):
            outs.append(jnp.where(lane < HEAD_DIM, o[(2 * pair) * r:(2 * pair + 1) * r],
                                  o[(2 * pair + 1) * r:(2 * pair + 2) * r]))
    return jnp.concatenate(outs, axis=1)


def _swa_sample_kernel(q_ref, kn_ref, vn_ref, ck_ref, cv_ref, bias_ref, sink_ref, o_ref, nk_ref, nv_ref, *, ld):
    for i in range(ck_ref.shape[0]):
        rows = slice(i * ld, (i + 1) * ld)
        kk = jnp.concatenate([ck_ref[i], kn_ref[rows, :]], axis=0)
        vv = jnp.concatenate([cv_ref[i], vn_ref[rows, :]], axis=0)
        o_ref[rows, :] = _attend(q_ref[rows, :], kk, vv, bias_ref, sink_ref, ld).astype(o_ref.dtype)
        nk_ref[i] = kk[ld:, :]
        nv_ref[i] = vv[ld:, :]


def _swa_sample(q, k, v, cache_k, cache_v, bias, sink_col, ld, sb):
    bd = cache_k.shape[0]
    tok = lambda w: pl.BlockSpec((sb * ld, w), lambda i: (i, 0))
    cache = pl.BlockSpec((sb, WINDOW, LANES), lambda i: (i, 0, 0))
    return pl.pallas_call(
        functools.partial(_swa_sample_kernel, ld=ld),
        grid=(bd // sb,),
        in_specs=[tok(4 * LANES), tok(LANES), tok(LANES), cache, cache,
                  pl.BlockSpec(bias.shape, lambda i: (0, 0, 0)), pl.BlockSpec(sink_col.shape, lambda i: (0, 0, 0))],
        out_specs=[tok(4 * LANES), cache, cache],
        out_shape=[jax.ShapeDtypeStruct(q.shape, BF16), jax.ShapeDtypeStruct(cache_k.shape, F32),
                   jax.ShapeDtypeStruct(cache_v.shape, F32)],
        compiler_params=pltpu.CompilerParams(dimension_semantics=("parallel",)),
        name="swa_sample",
    )(q, k, v, cache_k, cache_v, bias, sink_col)


def _lower_bound(lb_logits):
    z = lb_logits - jnp.max(lb_logits, axis=0, keepdims=True)
    e = jnp.exp(z)
    return e[0:1, :] / jnp.sum(e, axis=0, keepdims=True)


def _gates(qr, fr, lb):
    f = lb + (1.0 - lb) * jax.nn.sigmoid(fr)
    return jax.nn.silu(qr), 1.0 - f, jnp.log(jnp.maximum(f, F_TINY))


def _head_out(o, gr, hg_gain, ones_seg):
    return _seg_rms(o, ones_seg, 1.0 / HG_DIM, hg_gain) * jax.nn.silu(gr)


def _hgrn_chunk(qf, kk, g, v, s_prev, tri):
    c = qf.shape[0]
    row = lax.broadcasted_iota(jnp.int32, (HG_SUB, c), 0)
    col = lax.broadcasted_iota(jnp.int32, (HG_SUB, c), 1)
    gc = _split3_ldot(tri, g)
    g_last = gc[c - 1:c, :]
    inter = _dot((qf * jnp.exp(gc)).astype(BF16), s_prev.astype(BF16))
    att = []
    for i in range(c // HG_SUB):
        lo, hi = i * HG_SUB, (i + 1) * HG_SUB
        g_base = gc[lo - 1:lo, :] if i else jnp.zeros((1, HG_DIM), F32)
        q_hat = qf[lo:hi] * jnp.exp(gc[lo:hi] - g_base)
        k_hat = kk * jnp.exp(jnp.minimum(g_base - gc, EXP_CLAMP))
        a = _dot_nt(q_hat.astype(BF16), k_hat.astype(BF16))
        att.append(jnp.where(col <= row + lo, a, 0.0))
    o = inter + _dot(jnp.concatenate(att, axis=0).astype(BF16), v)
    k_end_t = (kk * jnp.exp(g_last - gc)).T
    decay = jnp.broadcast_to(jnp.exp(g_last), (HG_DIM, HG_DIM)).T
    return o, decay * s_prev + _dot(k_end_t.astype(BF16), v)


def _hgrn_sample_kernel(qr_ref, fr_ref, ir_ref, gr_ref, s0_ref, lb_ref, hg_ref, tri_ref, tot_ref, ones_ref,
                        o_ref, s_out_ref, *, ld):
    nseq = s0_ref.shape[0]
    r = nseq * ld
    lb = _lower_bound(lb_ref[...])
    qf_all, kk_all, g_all = _gates(qr_ref[...], fr_ref[...], lb)
    tri, tot = tri_ref[...], tot_ref[...]
    causal = tri.astype(F32) > 0
    lane = lax.broadcasted_iota(jnp.int32, (HG_DIM, r), 1)
    for hd in range(HG_HEADS):
        lanes = slice(hd * HG_DIM, (hd + 1) * HG_DIM)
        qf, kk, v = qf_all[:, lanes], kk_all[:, lanes], ir_ref[:, lanes]
        g = _split3_ldot(tri, g_all[:, lanes])
        g_last = _split3_ldot(tot, g_all[:, lanes])
        q_t = (qf * jnp.exp(g)).astype(BF16)
        k_t = kk * jnp.exp(jnp.minimum(-g, EXP_CLAMP))
        a = jnp.where(causal, _dot_nt(q_t, k_t.astype(BF16)), 0.0)
        intra = _dot(a.astype(BF16), v)
        k_end_t = (kk * jnp.exp(g_last - g)).T
        decay_t = jnp.exp(g_last).T
        outs = []
        for i in range(nseq):
            s_prev = s0_ref[i, hd]
            outs.append(_dot(q_t[i * ld:(i + 1) * ld], s_prev.astype(BF16)))
            own = (lane >= i * ld) & (lane < (i + 1) * ld)
            upd = _dot(jnp.where(own, k_end_t, 0.0).astype(BF16), v)
            s_out_ref[i, hd] = decay_t[:, i * ld:i * ld + 1] * s_prev + upd
        o = jnp.concatenate(outs, axis=0) + intra
        o_ref[:, lanes] = _head_out(o, gr_ref[:, lanes], hg_ref[...], ones_ref[...]).astype(o_ref.dtype)


def _hgrn_sample(qr, fr, ir, gr, s0, lb_logits, hg_gain, ld, sb):
    bd = s0.shape[0]
    r = sb * ld
    seq = np.arange(r) // ld
    same = seq[:, None] == seq[None, :]
    tri = jnp.asarray((same & (np.arange(r)[:, None] >= np.arange(r)[None, :])).astype(np.float32), BF16)
    tot = jnp.asarray(same.astype(np.float32), BF16)
    ones = jnp.ones((HG_DIM, HG_DIM), BF16)
    tok = pl.BlockSpec((r, HG_HEADS * HG_DIM), lambda i: (i, 0))
    state = pl.BlockSpec((sb, HG_HEADS, HG_DIM, HG_DIM), lambda i: (i, 0, 0, 0))
    const = lambda a: pl.BlockSpec(a.shape, lambda i: (0,) * a.ndim)
    return pl.pallas_call(
        functools.partial(_hgrn_sample_kernel, ld=ld),
        grid=(bd // sb,),
        in_specs=[tok, tok, tok, tok, state, const(lb_logits), const(hg_gain), const(tri), const(tot), const(ones)],
        out_specs=[tok, state],
        out_shape=[jax.ShapeDtypeStruct(qr.shape, BF16), jax.ShapeDtypeStruct(s0.shape, F32)],
        compiler_params=pltpu.CompilerParams(dimension_semantics=("parallel",)),
        name="hgrn_sample",
    )(qr, fr, ir, gr, s0, lb_logits, hg_gain, tri, tot, ones)


def _merge_out_ffn(h, u, att, orr, win_ref, wua_ref, wur_ref, wo_ref, g2_ref, w1_ref, w3_ref, w2_ref):
    ga = _dot(u, win_ref[:, O_GA:O_GB])
    gb = _dot(u, win_ref[:, O_GB:O_END])
    merged = (jax.nn.sigmoid(ga) * _dot(att, wua_ref[...]) + jax.nn.sigmoid(gb) * _dot(orr, wur_ref[...]))
    h2 = h + _dot(merged.astype(BF16), wo_ref[...])
    hn = _rms_rows(h2, g2_ref[...]).astype(BF16)
    return h2 + 0.5 * _swiglu(hn, w1_ref, w3_ref, w2_ref)


def _mix_ffn_kernel(h_ref, gm_ref, win_ref, qg_ref, kg_ref, seg_ref, bias_ref, sink_ref, lb_ref, hg_ref, tri_ref,
                    ones_ref, wua_ref, wur_ref, wo_ref, g2_ref, w1_ref, w3_ref, w2_ref,
                    y_ref, k_out_ref, v_out_ref, s_out_ref, kprev_ref, vprev_ref, s_ref):
    tm = h_ref.shape[0]
    first = pl.program_id(1) == 0

    @pl.when(first)
    def _():
        kprev_ref[...] = jnp.zeros(kprev_ref.shape, F32)
        vprev_ref[...] = jnp.zeros(vprev_ref.shape, F32)
        s_ref[...] = jnp.zeros(s_ref.shape, F32)

    h = h_ref[...]
    u = _rms_rows(h, gm_ref[...]).astype(BF16)

    q, k, v = _project_qkv(u, win_ref, qg_ref[...], kg_ref[...], seg_ref[...])
    kk_all = jnp.concatenate([kprev_ref[...], k], axis=0)
    vv_all = jnp.concatenate([vprev_ref[...], v], axis=0)
    att = []
    for blk in range(tm // WINDOW):
        lo = blk * WINDOW
        bias_blk = bias_ref.at[jnp.where(first, 0, 1)] if blk == 0 else bias_ref.at[1]
        att.append(_attend(q[lo:lo + WINDOW], kk_all[lo:lo + 2 * WINDOW], vv_all[lo:lo + 2 * WINDOW],
                           bias_blk, sink_ref, WINDOW))
    att = jnp.concatenate(att, axis=0).astype(BF16)
    k_tail, v_tail = k[tm - WINDOW:], v[tm - WINDOW:]
    kprev_ref[...] = k_tail
    vprev_ref[...] = v_tail
    k_out_ref[...] = k_tail
    v_out_ref[...] = v_tail

    lb = _lower_bound(lb_ref[...])
    tri = tri_ref[...]
    orr = []
    for hd in range(HG_HEADS):
        lanes = lambda off: slice(off + hd * HG_DIM, off + (hd + 1) * HG_DIM)
        qf, kk, g = _gates(_dot(u, win_ref[:, lanes(O_QR)]), _dot(u, win_ref[:, lanes(O_FR)]),
                           lb[:, hd * HG_DIM:(hd + 1) * HG_DIM])
        vals = _dot(u, win_ref[:, lanes(O_IR)]).astype(BF16)
        gr = _dot(u, win_ref[:, lanes(O_GR)])
        s = s_ref[hd]
        outs = []
        for c in range(tm // HG_CHUNK):
            rows = slice(c * HG_CHUNK, (c + 1) * HG_CHUNK)
            o, s = _hgrn_chunk(qf[rows], kk[rows], g[rows], vals[rows], s, tri)
            outs.append(o)
        s_ref[hd] = s
        s_out_ref[0, hd] = s
        orr.append(_head_out(jnp.concatenate(outs, axis=0), gr, hg_ref[...], ones_ref[...]).astype(BF16))
    orr = jnp.concatenate(orr, axis=1)

    y_ref[...] = _merge_out_ffn(h, u, att, orr, win_ref, wua_ref, wur_ref, wo_ref, g2_ref, w1_ref, w3_ref, w2_ref)


def _mix_ffn(h, consts, batch, seq, tm):
    t, d = h.shape
    nt = seq // tm
    row = pl.BlockSpec((tm, d), lambda b, i: (b * nt + i, 0))
    win_out = pl.BlockSpec((WINDOW, LANES), lambda b, i: (b, 0))
    return pl.pallas_call(
        _mix_ffn_kernel,
        grid=(batch, nt),
        in_specs=[row] + [_const_spec(a) for a in consts],
        out_specs=[row, win_out, win_out, pl.BlockSpec((1, HG_HEADS, HG_DIM, HG_DIM), lambda b, i: (b, 0, 0, 0))],
        out_shape=[jax.ShapeDtypeStruct((t, d), F32),
                   jax.ShapeDtypeStruct((batch * WINDOW, LANES), F32),
                   jax.ShapeDtypeStruct((batch * WINDOW, LANES), F32),
                   jax.ShapeDtypeStruct((batch, HG_HEADS, HG_DIM, HG_DIM), F32)],
        scratch_shapes=[pltpu.VMEM((WINDOW, LANES), F32), pltpu.VMEM((WINDOW, LANES), F32),
                        pltpu.VMEM((HG_HEADS, HG_DIM, HG_DIM), F32)],
        compiler_params=pltpu.CompilerParams(dimension_semantics=("parallel", "arbitrary"),
                                             vmem_limit_bytes=VMEM_LIMIT),
        name="mix_ffn",
    )(h, *consts)


def _proj_kernel(h_ref, gm_ref, win_ref, qg_ref, kg_ref, seg_ref,
                 q_ref, k_ref, v_ref, qr_ref, fr_ref, ir_ref, gr_ref):
    u = _rms_rows(h_ref[...], gm_ref[...]).astype(BF16)
    q, k, v = _project_qkv(u, win_ref, qg_ref[...], kg_ref[...], seg_ref[...])
    q_ref[...] = q
    k_ref[...] = k
    v_ref[...] = v
    qr_ref[...] = _dot(u, win_ref[:, O_QR:O_FR])
    fr_ref[...] = _dot(u, win_ref[:, O_FR:O_IR])
    ir_ref[...] = _dot(u, win_ref[:, O_IR:O_GR]).astype(ir_ref.dtype)
    gr_ref[...] = _dot(u, win_ref[:, O_GR:O_GA])


def _proj(h, gm, w_in, qg, kg, seg, tm):
    t, d = h.shape
    row = lambda w: pl.BlockSpec((tm, w), lambda i: (i, 0))
    widths = SPLIT_SIZES[:7]
    dtypes = (BF16, F32, F32, F32, F32, BF16, F32)
    return pl.pallas_call(
        _proj_kernel,
        grid=(t // tm,),
        in_specs=[row(d)] + [_const_spec(a) for a in (gm, w_in, qg, kg, seg)],
        out_specs=[row(w) for w in widths],
        out_shape=[jax.ShapeDtypeStruct((t, w), dt) for w, dt in zip(widths, dtypes)],
        compiler_params=pltpu.CompilerParams(dimension_semantics=("parallel",), vmem_limit_bytes=VMEM_LIMIT),
        name="proj",
    )(h, gm, w_in, qg, kg, seg)


def _out_ffn_kernel(h_ref, att_ref, orr_ref, gm_ref, win_ref, wua_ref, wur_ref, wo_ref, g2_ref,
                    w1_ref, w3_ref, w2_ref, y_ref):
    h = h_ref[...]
    u = _rms_rows(h, gm_ref[...]).astype(BF16)
    y_ref[...] = _merge_out_ffn(h, u, att_ref[...], orr_ref[...], win_ref, wua_ref, wur_ref, wo_ref, g2_ref,
                                w1_ref, w3_ref, w2_ref)


def _out_ffn(h, att, orr, consts, tm):
    t, d = h.shape
    row = lambda a: pl.BlockSpec((tm, a.shape[1]), lambda i: (i, 0))
    return pl.pallas_call(
        _out_ffn_kernel,
        grid=(t // tm,),
        in_specs=[row(a) for a in (h, att, orr)] + [_const_spec(a) for a in consts],
        out_specs=pl.BlockSpec((tm, d), lambda i: (i, 0)),
        out_shape=jax.ShapeDtypeStruct((t, d), F32),
        compiler_params=pltpu.CompilerParams(dimension_semantics=("parallel",), vmem_limit_bytes=VMEM_LIMIT),
        name="out_ffn",
    )(h, att, orr, *consts)


def _bucket_maps(ld):
    dist_p = np.arange(WINDOW)[:, None] + WINDOW - np.arange(2 * WINDOW)[None, :]
    valid_p = (dist_p >= 0) & (dist_p <= WINDOW)
    bucket_p = np.where(valid_p, _t5_bucket(dist_p), -1).astype(np.int32)
    first = np.where(np.arange(2 * WINDOW)[None, :] >= WINDOW, bucket_p, -1).astype(np.int32)
    dist_s = np.arange(ld)[:, None] + WINDOW - np.arange(WINDOW + ld)[None, :]
    valid_s = (dist_s >= 0) & (dist_s <= WINDOW)
    bucket_s = np.where(valid_s, _t5_bucket(dist_s), -1).astype(np.int32)
    return first, bucket_p, bucket_s


def _stacked(bias, rows):
    return bias.reshape(N_KV_HEADS, GROUP * rows, bias.shape[-1])


def kernel(x_prompt, x_sample, cache_win_k, cache_win_v, state_hgrn, ffn1_norm, ffn1_w1, ffn1_w3, ffn1_w2, mix_norm, w_in, q_norm, k_norm, sinks, rel_bias_table, hgrn_lb_logits, hg_norm, w_up_attn, w_up_hgrn, w_out, ffn2_norm, ffn2_w1, ffn2_w3, ffn2_w2):
    depth = ffn1_norm.shape[0]
    assert depth == 1 and hgrn_lb_logits.shape[0] == 2, "single-layer step only"
    batch, seq, d = x_prompt.shape
    bd, ld, _ = x_sample.shape

    bf = lambda w: w[0].astype(BF16)
    row = lambda g: g[0].reshape(1, -1).astype(F32)
    w1a, w3a, w2a, w_in_b = bf(ffn1_w1), bf(ffn1_w3), bf(ffn1_w2), bf(w_in)
    w1b, w3b, w2b = bf(ffn2_w1), bf(ffn2_w3), bf(ffn2_w2)
    wua, wur, wo = bf(w_up_attn), bf(w_up_hgrn), bf(w_out)
    g1, gm, g2 = row(ffn1_norm), row(mix_norm), row(ffn2_norm)
    qg = jnp.tile(row(q_norm), (1, LANES // HEAD_DIM))
    kg = jnp.tile(row(k_norm), (1, LANES // HEAD_DIM))
    head_of_lane = np.arange(LANES) // HEAD_DIM
    seg = jnp.asarray((head_of_lane[:, None] == head_of_lane[None, :]).astype(np.float32), BF16)
    ones = jnp.ones((HG_DIM, HG_DIM), BF16)
    tri = jnp.asarray(np.tril(np.ones((HG_CHUNK, HG_CHUNK), np.float32)), BF16)
    lb_logits = hgrn_lb_logits.astype(F32)
    hg_gain = row(hg_norm)

    first, later, bucket_s = _bucket_maps(ld)
    table = rel_bias_table.astype(F32)
    bias_p = jnp.stack([_stacked(_rel_bias(table, first), WINDOW), _stacked(_rel_bias(table, later), WINDOW)])
    bias_s = _stacked(_rel_bias(table, bucket_s), ld)
    sink = sinks[0].astype(F32).reshape(N_KV_HEADS, GROUP, 1)
    sink_p = jnp.repeat(sink, WINDOW, axis=1).reshape(N_KV_HEADS, GROUP * WINDOW, 1)
    sink_s = jnp.repeat(sink, ld, axis=1).reshape(N_KV_HEADS, GROUP * ld, 1)
    tail = (wua, wur, wo, g2, w1b, w3b, w2b)

    h = _ffn1(x_prompt.reshape(batch * seq, d), g1, w1a, w3a, w2a, 512)
    consts = (gm, w_in_b, qg, kg, seg, bias_p, sink_p, lb_logits, hg_gain, tri, ones) + tail
    y_p, k_p, v_p, s_p = _mix_ffn(h, consts, batch, seq, 256)
    win = lambda a: a.reshape(1, batch, WINDOW, N_KV_HEADS, HEAD_DIM)

    h = _ffn1(x_sample.reshape(bd * ld, d), g1, w1a, w3a, w2a, 512)
    q, k, v, qr, fr, ir, gr = _proj(h, gm, w_in_b, qg, kg, seg, 256)
    ck = cache_win_k[0].reshape(bd, WINDOW, LANES)
    cv = cache_win_v[0].reshape(bd, WINDOW, LANES)
    att, nk, nv = _swa_sample(q, k, v, ck, cv, bias_s, sink_s, ld, 8)
    orr, s_s = _hgrn_sample(qr, fr, ir, gr, state_hgrn[0], lb_logits, hg_gain, ld, 8)
    y_s = _out_ffn(h, att, orr, (gm, w_in_b) + tail, 256)
    unwin = lambda a: a.reshape(1, bd, WINDOW, N_KV_HEADS, HEAD_DIM)
    return (y_p.reshape(batch, seq, d), y_s.reshape(bd, ld, d), win(k_p), win(v_p), s_p[None],
            unwin(nk), unwin(nv), s_s[None])
```

```python
import functools

import numpy as np
import jax
import jax.numpy as jnp
from jax import lax
from jax.experimental import pallas as pl
from jax.experimental.pallas import tpu as pltpu

F32 = jnp.float32
BF16 = jnp.bfloat16

LANES = 128
HEAD_DIM = 64
N_Q_HEADS = 8
N_KV_HEADS = 2
GROUP = N_Q_HEADS // N_KV_HEADS
WINDOW = 128
N_BUCKETS = 32
MAX_DISTANCE = 128
HG_HEADS = 4
HG_DIM = 128
HG_CHUNK = 128
HG_SUB = 32
EPS = 1e-6
F_TINY = 1e-30
NEG_BIG = -1e30
EXP_CLAMP = 80.0
SPLIT_SIZES = (512, 128, 128, 512, 512, 512, 512, 1024, 1024)
SPLIT_OFFS = tuple(int(v) for v in np.cumsum((0,) + SPLIT_SIZES))
O_Q, O_K, O_V, O_QR, O_FR, O_IR, O_GR, O_GA, O_GB, O_END = SPLIT_OFFS
VMEM_LIMIT = 56 * 1024 * 1024


def _dot(a, b):
    return jnp.dot(a, b, preferred_element_type=F32)


def _dot_nt(a, b):
    return lax.dot_general(a, b, (((1,), (1,)), ((), ())), preferred_element_type=F32)


def _split3(x):
    hi = x.astype(BF16)
    r1 = x - hi.astype(F32)
    mid = r1.astype(BF16)
    lo = (r1 - mid.astype(F32)).astype(BF16)
    return hi, mid, lo


def _split3_dot(x, w):
    hi, mid, lo = _split3(x)
    return _dot(hi, w) + _dot(mid, w) + _dot(lo, w)


def _split3_ldot(w, x):
    hi, mid, lo = _split3(x)
    return _dot(w, hi) + _dot(w, mid) + _dot(w, lo)


def _rms_rows(x, g):
    ms = jnp.mean(x * x, axis=-1, keepdims=True)
    return x * lax.rsqrt(ms + EPS) * g


def _seg_rms(x, seg, inv_n, g):
    ms = _split3_dot(x * x, seg) * inv_n
    return x * lax.rsqrt(ms + EPS) * g


def _run_interleaved(gens):
    out = [None] * len(gens)
    live = list(range(len(gens)))
    while live:
        still = []
        for i in live:
            try:
                next(gens[i])
                still.append(i)
            except StopIteration as done:
                out[i] = done.value
        live = still
    return out


def _const_spec(a):
    nd = a.ndim
    return pl.BlockSpec(a.shape, lambda *_: (0,) * nd, pipeline_mode=pl.Buffered(1))


def _ff_chunks(d_ff, step=1024):
    return tuple((lo, min(lo + step, d_ff)) for lo in range(0, d_ff, step))


def _swiglu(xn, w1_ref, w3_ref, w2_ref):
    acc = jnp.zeros((xn.shape[0], w2_ref.shape[1]), F32)
    for lo, hi in _ff_chunks(w1_ref.shape[1]):
        a = _dot(xn, w1_ref[:, lo:hi])
        b = _dot(xn, w3_ref[:, lo:hi])
        acc = acc + _dot((jax.nn.silu(a) * b).astype(BF16), w2_ref[lo:hi, :])
    return acc


def _ffn1_kernel(x_ref, g1_ref, w1_ref, w3_ref, w2_ref, h_ref):
    x = x_ref[...]
    xn = _rms_rows(x, g1_ref[...]).astype(BF16)
    h_ref[...] = x + 0.5 * _swiglu(xn, w1_ref, w3_ref, w2_ref)


def _ffn1(x, g1, w1, w3, w2, tm):
    t, d = x.shape
    row = pl.BlockSpec((tm, d), lambda i: (i, 0))
    return pl.pallas_call(
        _ffn1_kernel,
        grid=(t // tm,),
        in_specs=[row] + [_const_spec(a) for a in (g1, w1, w3, w2)],
        out_specs=row,
        out_shape=jax.ShapeDtypeStruct((t, d), F32),
        compiler_params=pltpu.CompilerParams(dimension_semantics=("parallel",), vmem_limit_bytes=VMEM_LIMIT),
        name="ffn1",
    )(x, g1, w1, w3, w2)


def _t5_bucket(dist):
    max_exact = N_BUCKETS // 2
    d = np.maximum(dist, 0)
    large = max_exact + (np.log(np.maximum(d, 1) / max_exact) / np.log(MAX_DISTANCE / max_exact)
                         * (N_BUCKETS - max_exact)).astype(np.int32)
    large = np.minimum(large, N_BUCKETS - 1)
    return np.where(d < max_exact, d, large).astype(np.int32)


def _bias_kernel(table_ref, bucket_ref, out_ref):
    bucket = bucket_ref[...]
    masked = jnp.where(bucket < 0, NEG_BIG, 0.0).astype(F32)
    for h in range(N_Q_HEADS):
        acc = masked
        for b in range(N_BUCKETS):
            acc = acc + jnp.where(bucket == b, table_ref[b, h], 0.0)
        out_ref[h] = acc


def _rel_bias(table, bucket_map):
    r, c = bucket_map.shape
    return pl.pallas_call(
        _bias_kernel,
        in_specs=[pl.BlockSpec(memory_space=pltpu.SMEM), pl.BlockSpec((r, c), lambda: (0, 0))],
        out_specs=pl.BlockSpec((N_Q_HEADS, r, c), lambda: (0, 0, 0)),
        out_shape=jax.ShapeDtypeStruct((N_Q_HEADS, r, c), F32),
        name="rel_bias",
    )(table, jnp.asarray(bucket_map))


def _project_qkv(u, win_ref, qg, kg, seg):
    qs = []
    for s in range(SPLIT_SIZES[0] // LANES):
        x = _dot(u, win_ref[:, O_Q + s * LANES:O_Q + (s + 1) * LANES])
        qs.append(_seg_rms(x, seg, 1.0 / HEAD_DIM, qg).astype(BF16))
    k = _seg_rms(_dot(u, win_ref[:, O_K:O_V]), seg, 1.0 / HEAD_DIM, kg)
    v = _dot(u, win_ref[:, O_V:O_QR])
    return jnp.concatenate(qs, axis=1), k, v


def _stack_group(q, kv):
    a = q[:, (2 * kv) * LANES:(2 * kv + 1) * LANES]
    b = q[:, (2 * kv + 1) * LANES:(2 * kv + 2) * LANES]
    ar, br = pltpu.roll(a, HEAD_DIM, 1), pltpu.roll(b, HEAD_DIM, 1)
    parts = (a, ar, b, br) if kv == 0 else (ar, a, br, b)
    return jnp.concatenate(parts, axis=0)


def _kv_half(x, kv, duplicate):
    lane = lax.broadcasted_iota(jnp.int32, x.shape, 1)
    keep = (lane < HEAD_DIM) if kv == 0 else (lane >= HEAD_DIM)
    xm = jnp.where(keep, x, 0.0)
    return xm + pltpu.roll(xm, HEAD_DIM, 1) if duplicate else xm


def _sink_softmax(s, sink):
    m = jnp.maximum(jnp.max(s, axis=-1, keepdims=True), sink)
    e = jnp.exp(s - m)
    return e / (jnp.sum(e, axis=-1, keepdims=True) + jnp.exp(sink - m))


def _attend(q, kk, vv, bias_ref, sink_ref, r):
    outs = []
    for kv in range(N_KV_HEADS):
        qs = _stack_group(q.astype(F32), kv).astype(BF16)
        kh = _kv_half(kk, kv, False).astype(BF16)
        s = _dot_nt(qs, kh) + bias_ref[kv]
        p = _sink_softmax(s, sink_ref[kv])
        o = _dot(p.astype(BF16), _kv_half(vv, kv, True).astype(BF16))
        lane = lax.broadcasted_iota(jnp.int32, (r, LANES), 1)
        for pair in range(GROUP // 2):
            outs.append(jnp.where(lane < HEAD_DIM, o[(2 * pair) * r:(2 * pair + 1) * r],
                                  o[(2 * pair + 1) * r:(2 * pair + 2) * r]))
    return jnp.concatenate(outs, axis=1)


def _swa_sample_kernel(q_ref, kn_ref, vn_ref, ck_ref, cv_ref, bias_ref, sink_ref, o_ref, nk_ref, nv_ref, *, ld):
    for i in range(ck_ref.shape[0]):
        rows = slice(i * ld, (i + 1) * ld)
        kk = jnp.concatenate([ck_ref[i], kn_ref[rows, :]], axis=0)
        vv = jnp.concatenate([cv_ref[i], vn_ref[rows, :]], axis=0)
        o_ref[rows, :] = _attend(q_ref[rows, :], kk, vv, bias_ref, sink_ref, ld).astype(o_ref.dtype)
        nk_ref[i] = kk[ld:, :]
        nv_ref[i] = vv[ld:, :]


def _swa_sample(q, k, v, cache_k, cache_v, bias, sink_col, ld, sb):
    bd = cache_k.shape[0]
    tok = lambda w: pl.BlockSpec((sb * ld, w), lambda i: (i, 0))
    cache = pl.BlockSpec((sb, WINDOW, LANES), lambda i: (i, 0, 0))
    return pl.pallas_call(
        functools.partial(_swa_sample_kernel, ld=ld),
        grid=(bd // sb,),
        in_specs=[tok(4 * LANES), tok(LANES), tok(LANES), cache, cache,
                  pl.BlockSpec(bias.shape, lambda i: (0, 0, 0)), pl.BlockSpec(sink_col.shape, lambda i: (0, 0, 0))],
        out_specs=[tok(4 * LANES), cache, cache],
        out_shape=[jax.ShapeDtypeStruct(q.shape, BF16), jax.ShapeDtypeStruct(cache_k.shape, F32),
                   jax.ShapeDtypeStruct(cache_v.shape, F32)],
        compiler_params=pltpu.CompilerParams(dimension_semantics=("parallel",)),
        name="swa_sample",
    )(q, k, v, cache_k, cache_v, bias, sink_col)


def _lower_bound(lb_logits):
    z = lb_logits - jnp.max(lb_logits, axis=0, keepdims=True)
    e = jnp.exp(z)
    return e[0:1, :] / jnp.sum(e, axis=0, keepdims=True)


def _gates(qr, fr, lb):
    f = lb + (1.0 - lb) * jax.nn.sigmoid(fr)
    return jax.nn.silu(qr), 1.0 - f, jnp.log(jnp.maximum(f, F_TINY))


def _head_out(o, gr, hg_gain, ones_seg):
    return _seg_rms(o, ones_seg, 1.0 / HG_DIM, hg_gain) * jax.nn.silu(gr)


def _hgrn_head(u, win_ref, hd, lb, s_ref, s_out_ref, hg_gain, ones, tri):
    lanes = lambda off: slice(off + hd * HG_DIM, off + (hd + 1) * HG_DIM)
    qf, kk, g = _gates(_dot(u, win_ref[:, lanes(O_QR)]), _dot(u, win_ref[:, lanes(O_FR)]),
                       lb[:, hd * HG_DIM:(hd + 1) * HG_DIM])
    yield
    vals = _dot(u, win_ref[:, lanes(O_IR)]).astype(BF16)
    gr = _dot(u, win_ref[:, lanes(O_GR)])
    nchunk = u.shape[0] // HG_CHUNK
    chunks = [slice(c * HG_CHUNK, (c + 1) * HG_CHUNK) for c in range(nchunk)]
    yield
    gcs = [_split3_ldot(tri, g[rows]) for rows in chunks]
    yield
    q_in, upd, decay = [], [], []
    for rows, gc in zip(chunks, gcs):
        g_last = gc[HG_CHUNK - 1:HG_CHUNK, :]
        q_in.append((qf[rows] * jnp.exp(gc)).astype(BF16))
        k_end_t = (kk[rows] * jnp.exp(g_last - gc)).T
        upd.append(_dot(k_end_t.astype(BF16), vals[rows]))
        decay.append(jnp.broadcast_to(jnp.exp(g_last), (HG_DIM, HG_DIM)).T)
        yield
    intra = []
    for rows, gc in zip(chunks, gcs):
        qc, kc, vc = qf[rows], kk[rows], vals[rows]
        parts = []
        for i in range(HG_CHUNK // HG_SUB):
            lo, hi = i * HG_SUB, (i + 1) * HG_SUB
            g_base = gc[lo - 1:lo, :] if i else jnp.zeros((1, HG_DIM), F32)
            q_hat = qc[lo:hi] * jnp.exp(gc[lo:hi] - g_base)
            k_hat = kc[:hi] * jnp.exp(jnp.minimum(g_base - gc[:hi], EXP_CLAMP))
            a = _dot_nt(q_hat.astype(BF16), k_hat.astype(BF16))
            row = lax.broadcasted_iota(jnp.int32, (HG_SUB, hi), 0)
            col = lax.broadcasted_iota(jnp.int32, (HG_SUB, hi), 1)
            a = jnp.where(col <= row + lo, a, 0.0)
            yield
            parts.append(_dot(a.astype(BF16), vc[:hi]))
        intra.append(jnp.concatenate(parts, axis=0))
    yield
    s = s_ref[hd]
    outs = []
    for c in range(nchunk):
        outs.append(_dot(q_in[c], s.astype(BF16)) + intra[c])
        s = decay[c] * s + upd[c]
    s_ref[hd] = s
    s_out_ref[0, hd] = s
    yield
    o = jnp.concatenate(outs, axis=0)
    return _head_out(o, gr, hg_gain, ones).astype(BF16)


def _attn_chain(q_slab, k_stack, v_t, bias_ref, sink_lo, sink_hi, kv):
    s = _dot_nt(k_stack, q_slab) + bias_ref[...]
    yield
    outs = []
    nk = k_stack.shape[0] // 2
    for half, sink in ((0, sink_lo), (1, sink_hi)):
        sh = s[half * nk:(half + 1) * nk]
        m = jnp.maximum(jnp.max(sh, axis=0, keepdims=True), sink)
        e = jnp.exp(sh - m)
        denom = jnp.sum(e, axis=0, keepdims=True) + jnp.exp(sink - m)
        yield
        o_t = _dot(v_t, e.astype(BF16))[kv * HEAD_DIM:(kv + 1) * HEAD_DIM]
        yield
        outs.append(o_t * (1.0 / denom))
    return outs


def _gate_stream(u, win_ref, step=256):
    res = []
    for off in (O_GA, O_GB):
        cols = []
        for lo in range(0, SPLIT_SIZES[7], step):
            cols.append(jax.nn.sigmoid(_dot(u, win_ref[:, off + lo:off + lo + step])))
            yield
        res.append(jnp.concatenate(cols, axis=1))
    return res


def _hgrn_sample_kernel(qr_ref, fr_ref, ir_ref, gr_ref, s0_ref, lb_ref, hg_ref, tri_ref, tot_ref, ones_ref,
                        o_ref, s_out_ref, *, ld):
    nseq = s0_ref.shape[0]
    r = nseq * ld
    lb = _lower_bound(lb_ref[...])
    qf_all, kk_all, g_all = _gates(qr_ref[...], fr_ref[...], lb)
    tri, tot = tri_ref[...], tot_ref[...]
    causal = tri.astype(F32) > 0
    lane = lax.broadcasted_iota(jnp.int32, (HG_DIM, r), 1)
    for hd in range(HG_HEADS):
        lanes = slice(hd * HG_DIM, (hd + 1) * HG_DIM)
        qf, kk, v = qf_all[:, lanes], kk_all[:, lanes], ir_ref[:, lanes]
        g = _split3_ldot(tri, g_all[:, lanes])
        g_last = _split3_ldot(tot, g_all[:, lanes])
        q_t = (qf * jnp.exp(g)).astype(BF16)
        k_t = kk * jnp.exp(jnp.minimum(-g, EXP_CLAMP))
        a = jnp.where(causal, _dot_nt(q_t, k_t.astype(BF16)), 0.0)
        intra = _dot(a.astype(BF16), v)
        k_end_t = (kk * jnp.exp(g_last - g)).T
        decay_t = jnp.exp(g_last).T
        outs = []
        for i in range(nseq):
            s_prev = s0_ref[i, hd]
            outs.append(_dot(q_t[i * ld:(i + 1) * ld], s_prev.astype(BF16)))
            own = (lane >= i * ld) & (lane < (i + 1) * ld)
            upd = _dot(jnp.where(own, k_end_t, 0.0).astype(BF16), v)
            s_out_ref[i, hd] = decay_t[:, i * ld:i * ld + 1] * s_prev + upd
        o = jnp.concatenate(outs, axis=0) + intra
        o_ref[:, lanes] = _head_out(o, gr_ref[:, lanes], hg_ref[...], ones_ref[...]).astype(o_ref.dtype)


def _hgrn_sample(qr, fr, ir, gr, s0, lb_logits, hg_gain, ld, sb):
    bd = s0.shape[0]
    r = sb * ld
    seq = np.arange(r) // ld
    same = seq[:, None] == seq[None, :]
    tri = jnp.asarray((same & (np.arange(r)[:, None] >= np.arange(r)[None, :])).astype(np.float32), BF16)
    tot = jnp.asarray(same.astype(np.float32), BF16)
    ones = jnp.ones((HG_DIM, HG_DIM), BF16)
    tok = pl.BlockSpec((r, HG_HEADS * HG_DIM), lambda i: (i, 0))
    state = pl.BlockSpec((sb, HG_HEADS, HG_DIM, HG_DIM), lambda i: (i, 0, 0, 0))
    const = lambda a: pl.BlockSpec(a.shape, lambda i: (0,) * a.ndim)
    return pl.pallas_call(
        functools.partial(_hgrn_sample_kernel, ld=ld),
        grid=(bd // sb,),
        in_specs=[tok, tok, tok, tok, state, const(lb_logits), const(hg_gain), const(tri), const(tot), const(ones)],
        out_specs=[tok, state],
        out_shape=[jax.ShapeDtypeStruct(qr.shape, BF16), jax.ShapeDtypeStruct(s0.shape, F32)],
        compiler_params=pltpu.CompilerParams(dimension_semantics=("parallel",)),
        name="hgrn_sample",
    )(qr, fr, ir, gr, s0, lb_logits, hg_gain, tri, tot, ones)


def _merge_out_ffn(h, sga, sgb, att, orr, wua_ref, wur_ref, wo_ref, g2_ref, w1_ref, w3_ref, w2_ref):
    merged = sga * _dot(att, wua_ref[...]) + sgb * _dot(orr, wur_ref[...])
    h2 = h + _dot(merged.astype(BF16), wo_ref[...])
    hn = _rms_rows(h2, g2_ref[...]).astype(BF16)
    return h2 + 0.5 * _swiglu(hn, w1_ref, w3_ref, w2_ref)


def _mix_ffn_kernel(h_ref, gm_ref, win_ref, qg_ref, kg_ref, seg_ref, bias_ref, sink_ref, lb_ref, hg_ref, tri_ref,
                    ones_ref, wua_ref, wur_ref, wo_ref, g2_ref, w1_ref, w3_ref, w2_ref,
                    y_ref, k_out_ref, v_out_ref, s_out_ref, kprev_ref, vprev_ref, s_ref):
    tm = h_ref.shape[0]
    first = pl.program_id(1) == 0

    @pl.when(first)
    def _():
        kprev_ref[...] = jnp.zeros(kprev_ref.shape, F32)
        vprev_ref[...] = jnp.zeros(vprev_ref.shape, F32)
        s_ref[...] = jnp.zeros(s_ref.shape, F32)

    h = h_ref[...]
    u = _rms_rows(h, gm_ref[...]).astype(BF16)

    q, k, v = _project_qkv(u, win_ref, qg_ref[...], kg_ref[...], seg_ref[...])
    kk_all = jnp.concatenate([kprev_ref[...], k], axis=0)
    vv_all = jnp.concatenate([vprev_ref[...], v], axis=0)
    k_tail, v_tail = k[tm - WINDOW:], v[tm - WINDOW:]
    kprev_ref[...] = k_tail
    vprev_ref[...] = v_tail
    k_out_ref[...] = k_tail
    v_out_ref[...] = v_tail
    lane = lax.broadcasted_iota(jnp.int32, kk_all.shape, 1)
    k_lo0 = jnp.where(lane < HEAD_DIM, kk_all, 0.0)
    k_hi1 = jnp.where(lane >= HEAD_DIM, kk_all, 0.0)
    k_halves = ((k_lo0.astype(BF16), pltpu.roll(k_lo0, HEAD_DIM, 1).astype(BF16)),
                (pltpu.roll(k_hi1, HEAD_DIM, 1).astype(BF16), k_hi1.astype(BF16)))
    v_t = vv_all.T.astype(BF16)

    chains = []
    for blk in range(tm // WINDOW):
        lo = blk * WINDOW
        variant = jnp.where(first, 0, 1) if blk == 0 else 1
        for slab in range(N_Q_HEADS // 2):
            kv = slab // (GROUP // 2)
            k_stack = jnp.concatenate([half[lo:lo + 2 * WINDOW] for half in k_halves[kv]], axis=0)
            chains.append(_attn_chain(q[lo:lo + WINDOW, slab * LANES:(slab + 1) * LANES], k_stack,
                                      v_t[:, lo:lo + 2 * WINDOW], bias_ref.at[variant, slab],
                                      sink_ref[2 * slab], sink_ref[2 * slab + 1], kv))
    n_attn = len(chains)
    lb = _lower_bound(lb_ref[...])
    for hd in range(HG_HEADS):
        chains.append(_hgrn_head(u, win_ref, hd, lb, s_ref, s_out_ref, hg_ref[...], ones_ref[...], tri_ref[...]))
    chains.append(_gate_stream(u, win_ref))
    res = _run_interleaved(chains)

    att = []
    for blk in range(tm // WINDOW):
        heads = [o for pair in res[blk * 4:(blk + 1) * 4] for o in pair]
        att.append(jnp.concatenate(heads, axis=0).T)
    att = jnp.concatenate(att, axis=0).astype(BF16)
    orr = jnp.concatenate(res[n_attn:n_attn + HG_HEADS], axis=1)
    sga, sgb = res[-1]
    y_ref[...] = _merge_out_ffn(h, sga, sgb, att, orr, wua_ref, wur_ref, wo_ref, g2_ref, w1_ref, w3_ref, w2_ref)


def _mix_ffn(h, consts, batch, seq, tm):
    t, d = h.shape
    nt = seq // tm
    row = pl.BlockSpec((tm, d), lambda b, i: (b * nt + i, 0))
    win_out = pl.BlockSpec((WINDOW, LANES), lambda b, i: (b, 0))
    return pl.pallas_call(
        _mix_ffn_kernel,
        grid=(batch, nt),
        in_specs=[row] + [pl.BlockSpec(memory_space=pltpu.SMEM) if a.ndim == 1 else _const_spec(a) for a in consts],
        out_specs=[row, win_out, win_out, pl.BlockSpec((1, HG_HEADS, HG_DIM, HG_DIM), lambda b, i: (b, 0, 0, 0))],
        out_shape=[jax.ShapeDtypeStruct((t, d), F32),
                   jax.ShapeDtypeStruct((batch * WINDOW, LANES), F32),
                   jax.ShapeDtypeStruct((batch * WINDOW, LANES), F32),
                   jax.ShapeDtypeStruct((batch, HG_HEADS, HG_DIM, HG_DIM), F32)],
        scratch_shapes=[pltpu.VMEM((WINDOW, LANES), F32), pltpu.VMEM((WINDOW, LANES), F32),
                        pltpu.VMEM((HG_HEADS, HG_DIM, HG_DIM), F32)],
        compiler_params=pltpu.CompilerParams(dimension_semantics=("parallel", "arbitrary"),
                                             vmem_limit_bytes=VMEM_LIMIT),
        name="mix_ffn",
    )(h, *consts)


def _proj_kernel(h_ref, gm_ref, win_ref, qg_ref, kg_ref, seg_ref,
                 q_ref, k_ref, v_ref, qr_ref, fr_ref, ir_ref, gr_ref):
    u = _rms_rows(h_ref[...], gm_ref[...]).astype(BF16)
    q, k, v = _project_qkv(u, win_ref, qg_ref[...], kg_ref[...], seg_ref[...])
    q_ref[...] = q
    k_ref[...] = k
    v_ref[...] = v
    qr_ref[...] = _dot(u, win_ref[:, O_QR:O_FR])
    fr_ref[...] = _dot(u, win_ref[:, O_FR:O_IR])
    ir_ref[...] = _dot(u, win_ref[:, O_IR:O_GR]).astype(ir_ref.dtype)
    gr_ref[...] = _dot(u, win_ref[:, O_GR:O_GA])


def _proj(h, gm, w_in, qg, kg, seg, tm):
    t, d = h.shape
    row = lambda w: pl.BlockSpec((tm, w), lambda i: (i, 0))
    widths = SPLIT_SIZES[:7]
    dtypes = (BF16, F32, F32, F32, F32, BF16, F32)
    return pl.pallas_call(
        _proj_kernel,
        grid=(t // tm,),
        in_specs=[row(d)] + [_const_spec(a) for a in (gm, w_in, qg, kg, seg)],
        out_specs=[row(w) for w in widths],
        out_shape=[jax.ShapeDtypeStruct((t, w), dt) for w, dt in zip(widths, dtypes)],
        compiler_params=pltpu.CompilerParams(dimension_semantics=("parallel",), vmem_limit_bytes=VMEM_LIMIT),
        name="proj",
    )(h, gm, w_in, qg, kg, seg)


def _out_ffn_kernel(h_ref, att_ref, orr_ref, gm_ref, win_ref, wua_ref, wur_ref, wo_ref, g2_ref,
                    w1_ref, w3_ref, w2_ref, y_ref):
    h = h_ref[...]
    u = _rms_rows(h, gm_ref[...]).astype(BF16)
    sga = jax.nn.sigmoid(_dot(u, win_ref[:, O_GA:O_GB]))
    sgb = jax.nn.sigmoid(_dot(u, win_ref[:, O_GB:O_END]))
    y_ref[...] = _merge_out_ffn(h, sga, sgb, att_ref[...], orr_ref[...], wua_ref, wur_ref, wo_ref, g2_ref,
                                w1_ref, w3_ref, w2_ref)


def _out_ffn(h, att, orr, consts, tm):
    t, d = h.shape
    row = lambda a: pl.BlockSpec((tm, a.shape[1]), lambda i: (i, 0))
    return pl.pallas_call(
        _out_ffn_kernel,
        grid=(t // tm,),
        in_specs=[row(a) for a in (h, att, orr)] + [_const_spec(a) for a in consts],
        out_specs=pl.BlockSpec((tm, d), lambda i: (i, 0)),
        out_shape=jax.ShapeDtypeStruct((t, d), F32),
        compiler_params=pltpu.CompilerParams(dimension_semantics=("parallel",), vmem_limit_bytes=VMEM_LIMIT),
        name="out_ffn",
    )(h, att, orr, *consts)


def _bucket_maps(ld):
    dist_p = np.arange(WINDOW)[:, None] + WINDOW - np.arange(2 * WINDOW)[None, :]
    valid_p = (dist_p >= 0) & (dist_p <= WINDOW)
    bucket_p = np.where(valid_p, _t5_bucket(dist_p), -1).astype(np.int32)
    first = np.where(np.arange(2 * WINDOW)[None, :] >= WINDOW, bucket_p, -1).astype(np.int32)
    dist_s = np.arange(ld)[:, None] + WINDOW - np.arange(WINDOW + ld)[None, :]
    valid_s = (dist_s >= 0) & (dist_s <= WINDOW)
    bucket_s = np.where(valid_s, _t5_bucket(dist_s), -1).astype(np.int32)
    return first, bucket_p, bucket_s


def _stacked(bias, rows):
    return bias.reshape(N_KV_HEADS, GROUP * rows, bias.shape[-1])


def kernel(x_prompt, x_sample, cache_win_k, cache_win_v, state_hgrn, ffn1_norm, ffn1_w1, ffn1_w3, ffn1_w2, mix_norm, w_in, q_norm, k_norm, sinks, rel_bias_table, hgrn_lb_logits, hg_norm, w_up_attn, w_up_hgrn, w_out, ffn2_norm, ffn2_w1, ffn2_w3, ffn2_w2):
    depth = ffn1_norm.shape[0]
    assert depth == 1 and hgrn_lb_logits.shape[0] == 2, "single-layer step only"
    batch, seq, d = x_prompt.shape
    bd, ld, _ = x_sample.shape

    bf = lambda w: w[0].astype(BF16)
    row = lambda g: g[0].reshape(1, -1).astype(F32)
    w1a, w3a, w2a, w_in_b = bf(ffn1_w1), bf(ffn1_w3), bf(ffn1_w2), bf(w_in)
    w1b, w3b, w2b = bf(ffn2_w1), bf(ffn2_w3), bf(ffn2_w2)
    wua, wur, wo = bf(w_up_attn), bf(w_up_hgrn), bf(w_out)
    g1, gm, g2 = row(ffn1_norm), row(mix_norm), row(ffn2_norm)
    qg = jnp.tile(row(q_norm), (1, LANES // HEAD_DIM)) * (HEAD_DIM ** -0.5)
    kg = jnp.tile(row(k_norm), (1, LANES // HEAD_DIM))
    head_of_lane = np.arange(LANES) // HEAD_DIM
    seg = jnp.asarray((head_of_lane[:, None] == head_of_lane[None, :]).astype(np.float32), BF16)
    ones = jnp.ones((HG_DIM, HG_DIM), BF16)
    tri = jnp.asarray(np.tril(np.ones((HG_CHUNK, HG_CHUNK), np.float32)), BF16)
    lb_logits = hgrn_lb_logits.astype(F32)
    hg_gain = row(hg_norm)

    first, later, bucket_s = _bucket_maps(ld)
    table = rel_bias_table.astype(F32)
    slabbed = lambda m: _rel_bias(table, np.ascontiguousarray(m.T)).reshape(N_Q_HEADS // 2, 4 * WINDOW, WINDOW)
    bias_p = jnp.stack([slabbed(first), slabbed(later)])
    bias_s = _stacked(_rel_bias(table, bucket_s), ld)
    sink_1d = sinks[0].astype(F32)
    sink = sink_1d.reshape(N_KV_HEADS, GROUP, 1)
    sink_s = jnp.repeat(sink, ld, axis=1).reshape(N_KV_HEADS, GROUP * ld, 1)
    tail = (wua, wur, wo, g2, w1b, w3b, w2b)

    h = _ffn1(x_prompt.reshape(batch * seq, d), g1, w1a, w3a, w2a, 512)
    consts = (gm, w_in_b, qg, kg, seg, bias_p, sink_1d, lb_logits, hg_gain, tri, ones) + tail
    y_p, k_p, v_p, s_p = _mix_ffn(h, consts, batch, seq, 256)
    win = lambda a: a.reshape(1, batch, WINDOW, N_KV_HEADS, HEAD_DIM)

    h = _ffn1(x_sample.reshape(bd * ld, d), g1, w1a, w3a, w2a, 512)
    q, k, v, qr, fr, ir, gr = _proj(h, gm, w_in_b, qg, kg, seg, 256)
    ck = cache_win_k[0].reshape(bd, WINDOW, LANES)
    cv = cache_win_v[0].reshape(bd, WINDOW, LANES)
    att, nk, nv = _swa_sample(q, k, v, ck, cv, bias_s, sink_s, ld, 8)
    orr, s_s = _hgrn_sample(qr, fr, ir, gr, state_hgrn[0], lb_logits, hg_gain, ld, 8)
    y_s = _out_ffn(h, att, orr, (gm, w_in_b) + tail, 256)
    unwin = lambda a: a.reshape(1, bd, WINDOW, N_KV_HEADS, HEAD_DIM)
    return (y_p.reshape(batch, seq, d), y_s.reshape(bd, ld, d), win(k_p), win(v_p), s_p[None],
            unwin(nk), unwin(nv), s_s[None])
```

```python
import functools

import numpy as np
import jax
import jax.numpy as jnp
from jax import lax
from jax.experimental import pallas as pl
from jax.experimental.pallas import tpu as pltpu

F32 = jnp.float32
BF16 = jnp.bfloat16

LANES = 128
HEAD_DIM = 64
N_Q_HEADS = 8
N_KV_HEADS = 2
GROUP = N_Q_HEADS // N_KV_HEADS
WINDOW = 128
N_BUCKETS = 32
MAX_DISTANCE = 128
HG_HEADS = 4
HG_DIM = 128
HG_CHUNK = 128
HG_SUB = 32
EPS = 1e-6
F_TINY = 1e-30
NEG_BIG = -1e30
EXP_CLAMP = 80.0
SPLIT_SIZES = (512, 128, 128, 512, 512, 512, 512, 1024, 1024)
SPLIT_OFFS = tuple(int(v) for v in np.cumsum((0,) + SPLIT_SIZES))
O_Q, O_K, O_V, O_QR, O_FR, O_IR, O_GR, O_GA, O_GB, O_END = SPLIT_OFFS
VMEM_LIMIT = 56 * 1024 * 1024


def _dot(a, b):
    return jnp.dot(a, b, preferred_element_type=F32)


def _dot_nt(a, b):
    return lax.dot_general(a, b, (((1,), (1,)), ((), ())), preferred_element_type=F32)


def _split3(x):
    hi = x.astype(BF16)
    r1 = x - hi.astype(F32)
    mid = r1.astype(BF16)
    lo = (r1 - mid.astype(F32)).astype(BF16)
    return hi, mid, lo


def _split3_dot(x, w):
    hi, mid, lo = _split3(x)
    return _dot(hi, w) + _dot(mid, w) + _dot(lo, w)


def _split3_ldot(w, x):
    hi, mid, lo = _split3(x)
    return _dot(w, hi) + _dot(w, mid) + _dot(w, lo)


def _split2_ldot(w, x):
    hi = x.astype(BF16)
    lo = (x - hi.astype(F32)).astype(BF16)
    return _dot(w, hi) + _dot(w, lo)


def _rms_rows(x, g):
    ms = jnp.mean(x * x, axis=-1, keepdims=True)
    return x * lax.rsqrt(ms + EPS) * g


def _seg_rms(x, seg, inv_n, g):
    ms = _dot((x * x).astype(BF16), seg) * inv_n
    return x * lax.rsqrt(ms + EPS) * g


class _Pool:
    def __init__(self):
        self.live = []
        self.out = {}

    def add(self, key, gen):
        self.live.append((key, gen))

    def round(self):
        still = []
        for key, gen in self.live:
            try:
                next(gen)
                still.append((key, gen))
            except StopIteration as done:
                self.out[key] = done.value
        self.live = still

    def has(self, keys):
        return all(k in self.out for k in keys)


def _const_spec(a):
    nd = a.ndim
    return pl.BlockSpec(a.shape, lambda *_: (0,) * nd, pipeline_mode=pl.Buffered(1))


def _ff_chunks(d_ff, step=1024):
    return tuple((lo, min(lo + step, d_ff)) for lo in range(0, d_ff, step))


def _swiglu(xn, w1_ref, w3_ref, w2_ref):
    acc = jnp.zeros((xn.shape[0], w2_ref.shape[1]), F32)
    for lo, hi in _ff_chunks(w1_ref.shape[1]):
        a = _dot(xn, w1_ref[:, lo:hi])
        b = _dot(xn, w3_ref[:, lo:hi])
        acc = acc + _dot((jax.nn.silu(a) * b).astype(BF16), w2_ref[lo:hi, :])
    return acc


def _ffn_kernel(x_ref, g1_ref, w1_ref, w3_ref, w2_ref, h_ref):
    x = x_ref[...]
    xn = _rms_rows(x, g1_ref[...]).astype(BF16)
    h_ref[...] = x + 0.5 * _swiglu(xn, w1_ref, w3_ref, w2_ref)


def _ffn(x, g1, w1, w3, w2, tm):
    t, d = x.shape
    row = pl.BlockSpec((tm, d), lambda i: (i, 0))
    return pl.pallas_call(
        _ffn_kernel,
        grid=(t // tm,),
        in_specs=[row] + [_const_spec(a) for a in (g1, w1, w3, w2)],
        out_specs=row,
        out_shape=jax.ShapeDtypeStruct((t, d), F32),
        compiler_params=pltpu.CompilerParams(dimension_semantics=("parallel",), vmem_limit_bytes=VMEM_LIMIT),
        name="ffn",
    )(x, g1, w1, w3, w2)


def _t5_bucket(dist):
    max_exact = N_BUCKETS // 2
    d = np.maximum(dist, 0)
    large = max_exact + (np.log(np.maximum(d, 1) / max_exact) / np.log(MAX_DISTANCE / max_exact)
                         * (N_BUCKETS - max_exact)).astype(np.int32)
    large = np.minimum(large, N_BUCKETS - 1)
    return np.where(d < max_exact, d, large).astype(np.int32)


def _bias_kernel(table_ref, bucket_ref, out_ref):
    bucket = bucket_ref[...]
    masked = jnp.where(bucket < 0, NEG_BIG, 0.0).astype(F32)
    for h in range(N_Q_HEADS):
        acc = masked
        for b in range(N_BUCKETS):
            acc = acc + jnp.where(bucket == b, table_ref[b, h], 0.0)
        out_ref[h] = acc


def _rel_bias(table, bucket_map):
    r, c = bucket_map.shape
    return pl.pallas_call(
        _bias_kernel,
        in_specs=[pl.BlockSpec(memory_space=pltpu.SMEM), pl.BlockSpec((r, c), lambda: (0, 0))],
        out_specs=pl.BlockSpec((N_Q_HEADS, r, c), lambda: (0, 0, 0)),
        out_shape=jax.ShapeDtypeStruct((N_Q_HEADS, r, c), F32),
        name="rel_bias",
    )(table, jnp.asarray(bucket_map))


def _project_qkv(u, win_ref, qg, kg, seg):
    qs = []
    for s in range(SPLIT_SIZES[0] // LANES):
        x = _dot(u, win_ref[:, O_Q + s * LANES:O_Q + (s + 1) * LANES])
        qs.append(_seg_rms(x, seg, 1.0 / HEAD_DIM, qg).astype(BF16))
    k = _seg_rms(_dot(u, win_ref[:, O_K:O_V]), seg, 1.0 / HEAD_DIM, kg)
    v = _dot(u, win_ref[:, O_V:O_QR])
    return jnp.concatenate(qs, axis=1), k, v


def _stack_group(q, kv):
    a = q[:, (2 * kv) * LANES:(2 * kv + 1) * LANES]
    b = q[:, (2 * kv + 1) * LANES:(2 * kv + 2) * LANES]
    ar, br = pltpu.roll(a, HEAD_DIM, 1), pltpu.roll(b, HEAD_DIM, 1)
    parts = (a, ar, b, br) if kv == 0 else (ar, a, br, b)
    return jnp.concatenate(parts, axis=0)


def _kv_half(x, kv, duplicate):
    lane = lax.broadcasted_iota(jnp.int32, x.shape, 1)
    keep = (lane < HEAD_DIM) if kv == 0 else (lane >= HEAD_DIM)
    xm = jnp.where(keep, x, 0.0)
    return xm + pltpu.roll(xm, HEAD_DIM, 1) if duplicate else xm


def _sink_softmax(s, sink):
    m = jnp.maximum(jnp.max(s, axis=-1, keepdims=True), sink)
    e = jnp.exp(s - m)
    return e / (jnp.sum(e, axis=-1, keepdims=True) + jnp.exp(sink - m))


def _attend(q, kk, vv, bias_ref, sink_ref, r):
    outs = []
    for kv in range(N_KV_HEADS):
        qs = _stack_group(q.astype(F32), kv).astype(BF16)
        kh = _kv_half(kk, kv, False).astype(BF16)
        s = _dot_nt(qs, kh) + bias_ref[kv]
        p = _sink_softmax(s, sink_ref[kv])
        o = _dot(p.astype(BF16), _kv_half(vv, kv, True).astype(BF16))
        lane = lax.broadcasted_iota(jnp.int32, (r, LANES), 1)
        for pair in range(GROUP // 2):
            outs.append(jnp.where(lane < HEAD_DIM, o[(2 * pair) * r:(2 * pair + 1) * r],
                                  o[(2 * pair + 1) * r:(2 * pair + 2) * r]))
    return jnp.concatenate(outs, axis=1)


def _swa_sample_kernel(q_ref, kn_ref, vn_ref, ck_ref, cv_ref, bias_ref, sink_ref, o_ref, nk_ref, nv_ref, *, ld):
    for i in range(ck_ref.shape[0]):
        rows = slice(i * ld, (i + 1) * ld)
        kk = jnp.concatenate([ck_ref[i], kn_ref[rows, :]], axis=0)
        vv = jnp.concatenate([cv_ref[i], vn_ref[rows, :]], axis=0)
        o_ref[rows, :] = _attend(q_ref[rows, :], kk, vv, bias_ref, sink_ref, ld).astype(o_ref.dtype)
        nk_ref[i] = kk[ld:, :]
        nv_ref[i] = vv[ld:, :]


def _swa_sample(q, k, v, cache_k, cache_v, bias, sink_col, ld, sb):
    bd = cache_k.shape[0]
    tok = lambda w: pl.BlockSpec((sb * ld, w), lambda i: (i, 0))
    cache = pl.BlockSpec((sb, WINDOW, LANES), lambda i: (i, 0, 0))
    return pl.pallas_call(
        functools.partial(_swa_sample_kernel, ld=ld),
        grid=(bd // sb,),
        in_specs=[tok(4 * LANES), tok(LANES), tok(LANES), cache, cache,
                  pl.BlockSpec(bias.shape, lambda i: (0, 0, 0)), pl.BlockSpec(sink_col.shape, lambda i: (0, 0, 0))],
        out_specs=[tok(4 * LANES), cache, cache],
        out_shape=[jax.ShapeDtypeStruct(q.shape, BF16), jax.ShapeDtypeStruct(cache_k.shape, F32),
                   jax.ShapeDtypeStruct(cache_v.shape, F32)],
        compiler_params=pltpu.CompilerParams(dimension_semantics=("parallel",)),
        name="swa_sample",
    )(q, k, v, cache_k, cache_v, bias, sink_col)


def _lower_bound(lb_logits):
    z = lb_logits - jnp.max(lb_logits, axis=0, keepdims=True)
    e = jnp.exp(z)
    return e[0:1, :] / jnp.sum(e, axis=0, keepdims=True)


def _gates(qr, fr, lb):
    f = lb + (1.0 - lb) * jax.nn.sigmoid(fr)
    return jax.nn.silu(qr), 1.0 - f, jnp.log(jnp.maximum(f, F_TINY))


def _head_out(o, gr, hg_gain, ones_seg):
    return _seg_rms(o, ones_seg, 1.0 / HG_DIM, hg_gain) * jax.nn.silu(gr)


def _hgrn_proj(u, win_ref, pair):
    cols = lambda off: slice(off + 2 * pair * HG_DIM, off + 2 * (pair + 1) * HG_DIM)
    qr = _dot(u, win_ref[:, cols(O_QR)])
    fr = _dot(u, win_ref[:, cols(O_FR)])
    yield
    ir = _dot(u, win_ref[:, cols(O_IR)]).astype(BF16)
    gr = _dot(u, win_ref[:, cols(O_GR)])
    yield
    return qr, fr, ir, gr


def _hgrn_head(qr, fr, vals, gr, lb, hd, s_ref, s_out_ref, hg_gain, ones, tri):
    qf, kk, g = _gates(qr, fr, lb)
    nchunk = qr.shape[0] // HG_CHUNK
    chunks = [slice(c * HG_CHUNK, (c + 1) * HG_CHUNK) for c in range(nchunk)]
    yield
    gcs = [_split2_ldot(tri, g[rows]) for rows in chunks]
    yield
    q_in, upd, decay, scores = [], [], [], []
    for rows, gc in zip(chunks, gcs):
        g_last = gc[HG_CHUNK - 1:HG_CHUNK, :]
        q_in.append((qf[rows] * jnp.exp(gc)).astype(BF16))
        k_end_t = (kk[rows] * jnp.exp(g_last - gc)).T
        upd.append(_dot(k_end_t.astype(BF16), vals[rows]))
        decay.append(jnp.broadcast_to(jnp.exp(g_last), (HG_DIM, HG_DIM)).T)
        qc, kc = qf[rows], kk[rows]
        for i in range(HG_CHUNK // HG_SUB):
            lo, hi = i * HG_SUB, (i + 1) * HG_SUB
            g_base = gc[lo - 1:lo, :] if i else jnp.zeros((1, HG_DIM), F32)
            q_hat = qc[lo:hi] * jnp.exp(gc[lo:hi] - g_base)
            k_hat = kc[:hi] * jnp.exp(jnp.minimum(g_base - gc[:hi], EXP_CLAMP))
            a = _dot_nt(q_hat.astype(BF16), k_hat.astype(BF16))
            row = lax.broadcasted_iota(jnp.int32, (HG_SUB, hi), 0)
            col = lax.broadcasted_iota(jnp.int32, (HG_SUB, hi), 1)
            scores.append(jnp.where(col <= row + lo, a, 0.0).astype(BF16))
    yield
    intra = []
    nsub = HG_CHUNK // HG_SUB
    for c, rows in enumerate(chunks):
        vc = vals[rows]
        parts = [_dot(scores[c * nsub + i], vc[:(i + 1) * HG_SUB]) for i in range(nsub)]
        intra.append(jnp.concatenate(parts, axis=0))
    yield
    s = s_ref[hd]
    outs = []
    for c in range(nchunk):
        outs.append(_dot(q_in[c], s.astype(BF16)) + intra[c])
        s = decay[c] * s + upd[c]
    s_ref[hd] = s
    s_out_ref[0, hd] = s
    yield
    o = jnp.concatenate(outs, axis=0)
    return _head_out(o, gr, hg_gain, ones).astype(BF16)


def _attn_chain(q_pair, k_half, v_t, bias_ref, sink_a, sink_b, kv):
    s = _dot_nt(k_half, q_pair) + bias_ref[...]
    yield
    col = lax.broadcasted_iota(jnp.int32, (1, s.shape[1]), 1)
    sink = jnp.where(col < WINDOW, sink_a, sink_b)
    m = jnp.maximum(jnp.max(s, axis=0, keepdims=True), sink)
    e = jnp.exp(s - m)
    yield
    denom = jnp.sum(e, axis=0, keepdims=True) + jnp.exp(sink - m)
    o_t = _dot(v_t, e.astype(BF16))[kv * HEAD_DIM:(kv + 1) * HEAD_DIM]
    yield
    return o_t * (1.0 / denom)


def _gate_stream(u, win_ref, step=256):
    res = []
    for off in (O_GA, O_GB):
        cols = []
        for lo in range(0, SPLIT_SIZES[7], step):
            cols.append(jax.nn.sigmoid(_dot(u, win_ref[:, off + lo:off + lo + step])))
            yield
        res.append(jnp.concatenate(cols, axis=1))
    return res


def _q_cols(u, win_ref, kv, seg2, gain2):
    x = _dot(u, win_ref[:, O_Q + 2 * kv * LANES:O_Q + 2 * (kv + 1) * LANES])
    yield
    ms = _dot((x * x).astype(BF16), seg2) * (1.0 / HEAD_DIM)
    yield
    return (x * lax.rsqrt(ms + EPS) * gain2).astype(BF16)


def _kv_cols(u, win_ref, seg, gain):
    x = _dot(u, win_ref[:, O_K:O_QR])
    yield
    k, v = x[:, :LANES], x[:, LANES:]
    ms = _dot((k * k).astype(BF16), seg) * (1.0 / HEAD_DIM)
    yield
    return k * lax.rsqrt(ms + EPS) * gain, v


def _attn_up(parts, wua_ref):
    att = jnp.concatenate(parts, axis=0).T.astype(BF16)
    yield
    return _dot(att, wua_ref[...])


def _hgrn_sample_kernel(qr_ref, fr_ref, ir_ref, gr_ref, s0_ref, lb_ref, hg_ref, tri_ref, tot_ref, ones_ref,
                        o_ref, s_out_ref, *, ld):
    nseq = s0_ref.shape[0]
    r = nseq * ld
    lb = _lower_bound(lb_ref[...])
    qf_all, kk_all, g_all = _gates(qr_ref[...], fr_ref[...], lb)
    tri, tot = tri_ref[...], tot_ref[...]
    causal = tri.astype(F32) > 0
    lane = lax.broadcasted_iota(jnp.int32, (HG_DIM, r), 1)
    for hd in range(HG_HEADS):
        lanes = slice(hd * HG_DIM, (hd + 1) * HG_DIM)
        qf, kk, v = qf_all[:, lanes], kk_all[:, lanes], ir_ref[:, lanes]
        g = _split3_ldot(tri, g_all[:, lanes])
        g_last = _split3_ldot(tot, g_all[:, lanes])
        q_t = (qf * jnp.exp(g)).astype(BF16)
        k_t = kk * jnp.exp(jnp.minimum(-g, EXP_CLAMP))
        a = jnp.where(causal, _dot_nt(q_t, k_t.astype(BF16)), 0.0)
        intra = _dot(a.astype(BF16), v)
        k_end_t = (kk * jnp.exp(g_last - g)).T
        decay_t = jnp.exp(g_last).T
        outs = []
        for i in range(nseq):
            s_prev = s0_ref[i, hd]
            outs.append(_dot(q_t[i * ld:(i + 1) * ld], s_prev.astype(BF16)))
            own = (lane >= i * ld) & (lane < (i + 1) * ld)
            upd = _dot(jnp.where(own, k_end_t, 0.0).astype(BF16), v)
            s_out_ref[i, hd] = decay_t[:, i * ld:i * ld + 1] * s_prev + upd
        o = jnp.concatenate(outs, axis=0) + intra
        o_ref[:, lanes] = _head_out(o, gr_ref[:, lanes], hg_ref[...], ones_ref[...]).astype(o_ref.dtype)


def _hgrn_sample(qr, fr, ir, gr, s0, lb_logits, hg_gain, ld, sb):
    bd = s0.shape[0]
    r = sb * ld
    seq = np.arange(r) // ld
    same = seq[:, None] == seq[None, :]
    tri = jnp.asarray((same & (np.arange(r)[:, None] >= np.arange(r)[None, :])).astype(np.float32), BF16)
    tot = jnp.asarray(same.astype(np.float32), BF16)
    ones = jnp.ones((HG_DIM, HG_DIM), BF16)
    tok = pl.BlockSpec((r, HG_HEADS * HG_DIM), lambda i: (i, 0))
    state = pl.BlockSpec((sb, HG_HEADS, HG_DIM, HG_DIM), lambda i: (i, 0, 0, 0))
    const = lambda a: pl.BlockSpec(a.shape, lambda i: (0,) * a.ndim)
    return pl.pallas_call(
        functools.partial(_hgrn_sample_kernel, ld=ld),
        grid=(bd // sb,),
        in_specs=[tok, tok, tok, tok, state, const(lb_logits), const(hg_gain), const(tri), const(tot), const(ones)],
        out_specs=[tok, state],
        out_shape=[jax.ShapeDtypeStruct(qr.shape, BF16), jax.ShapeDtypeStruct(s0.shape, F32)],
        compiler_params=pltpu.CompilerParams(dimension_semantics=("parallel",)),
        name="hgrn_sample",
    )(qr, fr, ir, gr, s0, lb_logits, hg_gain, tri, tot, ones)


def _merge_out(h, sga, sgb, att, orr, wua_ref, wur_ref, wo_ref):
    merged = sga * _dot(att, wua_ref[...]) + sgb * _dot(orr, wur_ref[...])
    return h + _dot(merged.astype(BF16), wo_ref[...])


def _mix_kernel(h_ref, gm_ref, win_ref, qg_ref, kg_ref, seg_ref, seg2_ref, bias_ref, sink_ref, lb_ref, hg_ref,
                tri_ref, ones_ref, wua_ref, wur_ref, wo_ref,
                h2_ref, k_out_ref, v_out_ref, s_out_ref, kprev_ref, vprev_ref, s_ref):
    tm = h_ref.shape[0]
    first = pl.program_id(1) == 0

    @pl.when(first)
    def _():
        kprev_ref[...] = jnp.zeros(kprev_ref.shape, F32)
        vprev_ref[...] = jnp.zeros(vprev_ref.shape, F32)
        s_ref[...] = jnp.zeros(s_ref.shape, F32)

    h = h_ref[...]
    u = _rms_rows(h, gm_ref[...]).astype(BF16)

    pool = _Pool()
    lb = _lower_bound(lb_ref[...])
    for kv in range(N_KV_HEADS):
        pool.add(("q", kv), _q_cols(u, win_ref, kv, seg2_ref[...], qg_ref[...]))
    pool.add("kv", _kv_cols(u, win_ref, seg_ref[...], kg_ref[...]))
    for pair in range(HG_HEADS // 2):
        pool.add(("hproj", pair), _hgrn_proj(u, win_ref, pair))
    pool.add("gates", _gate_stream(u, win_ref))
    first_keys = [("q", kv) for kv in range(N_KV_HEADS)] + ["kv"] + [("hproj", p) for p in range(HG_HEADS // 2)]
    while not pool.has(first_keys):
        pool.round()
    for hd in range(HG_HEADS):
        qr, fr, ir, gr = (a[:, (hd % 2) * HG_DIM:(hd % 2 + 1) * HG_DIM] for a in pool.out[("hproj", hd // 2)])
        pool.add(("hgrn", hd), _hgrn_head(qr, fr, ir, gr, lb[:, hd * HG_DIM:(hd + 1) * HG_DIM], hd, s_ref, s_out_ref,
                                          hg_ref[...], ones_ref[...], tri_ref[...]))

    q = [pool.out[("q", j // 2)][:, (j % 2) * LANES:(j % 2 + 1) * LANES] for j in range(SPLIT_SIZES[0] // LANES)]
    k, v = pool.out["kv"]
    kk_all = jnp.concatenate([kprev_ref[...], k], axis=0)
    vv_all = jnp.concatenate([vprev_ref[...], v], axis=0)
    k_tail, v_tail = k[tm - WINDOW:], v[tm - WINDOW:]
    kprev_ref[...] = k_tail
    vprev_ref[...] = v_tail
    k_out_ref[...] = k_tail
    v_out_ref[...] = v_tail
    lane = lax.broadcasted_iota(jnp.int32, kk_all.shape, 1)
    k_lo0 = jnp.where(lane < HEAD_DIM, kk_all, 0.0)
    k_hi1 = jnp.where(lane >= HEAD_DIM, kk_all, 0.0)
    k_halves = ((k_lo0.astype(BF16), pltpu.roll(k_lo0, HEAD_DIM, 1).astype(BF16)),
                (pltpu.roll(k_hi1, HEAD_DIM, 1).astype(BF16), k_hi1.astype(BF16)))
    v_t = vv_all.T.astype(BF16)

    n_blk = tm // WINDOW
    chain_keys = lambda blk: [("att", blk, kv, half) for kv in range(N_KV_HEADS) for half in range(2)]
    for blk in range(n_blk):
        lo = blk * WINDOW
        variant = jnp.where(first, 0, 1) if blk == 0 else 1
        for kv in range(N_KV_HEADS):
            q_pair = jnp.concatenate([q[2 * kv + j][lo:lo + WINDOW] for j in range(GROUP // 2)], axis=0)
            for half in range(2):
                pool.add(("att", blk, kv, half),
                         _attn_chain(q_pair, k_halves[kv][half][lo:lo + 2 * WINDOW], v_t[:, lo:lo + 2 * WINDOW],
                                     bias_ref.at[variant, kv, half], sink_ref[GROUP * kv + half],
                                     sink_ref[GROUP * kv + 2 + half], kv))

    waiting = list(range(n_blk))
    while pool.live:
        pool.round()
        for blk in [b for b in waiting if pool.has(chain_keys(b))]:
            waiting.remove(blk)
            parts = []
            for hq in range(N_Q_HEADS):
                kv, j, half = hq // GROUP, (hq % GROUP) // 2, hq % 2
                parts.append(pool.out[("att", blk, kv, half)][:, j * WINDOW:(j + 1) * WINDOW])
            pool.add(("up", blk), _attn_up(parts, wua_ref))

    up_a = jnp.concatenate([pool.out[("up", blk)] for blk in range(n_blk)], axis=0)
    orr = jnp.concatenate([pool.out[("hgrn", hd)] for hd in range(HG_HEADS)], axis=1)
    sga, sgb = pool.out["gates"]
    merged = sga * up_a + sgb * _dot(orr, wur_ref[...])
    h2_ref[...] = h + _dot(merged.astype(BF16), wo_ref[...])


def _mix(h, consts, batch, seq, tm):
    t, d = h.shape
    nt = seq // tm
    row = pl.BlockSpec((tm, d), lambda b, i: (b * nt + i, 0))
    win_out = pl.BlockSpec((WINDOW, LANES), lambda b, i: (b, 0))
    return pl.pallas_call(
        _mix_kernel,
        grid=(batch, nt),
        in_specs=[row] + [pl.BlockSpec(memory_space=pltpu.SMEM) if a.ndim == 1 else _const_spec(a) for a in consts],
        out_specs=[row, win_out, win_out, pl.BlockSpec((1, HG_HEADS, HG_DIM, HG_DIM), lambda b, i: (b, 0, 0, 0))],
        out_shape=[jax.ShapeDtypeStruct((t, d), F32),
                   jax.ShapeDtypeStruct((batch * WINDOW, LANES), F32),
                   jax.ShapeDtypeStruct((batch * WINDOW, LANES), F32),
                   jax.ShapeDtypeStruct((batch, HG_HEADS, HG_DIM, HG_DIM), F32)],
        scratch_shapes=[pltpu.VMEM((WINDOW, LANES), F32), pltpu.VMEM((WINDOW, LANES), F32),
                        pltpu.VMEM((HG_HEADS, HG_DIM, HG_DIM), F32)],
        compiler_params=pltpu.CompilerParams(dimension_semantics=("parallel", "arbitrary"),
                                             vmem_limit_bytes=VMEM_LIMIT),
        name="mix",
    )(h, *consts)


def _proj_kernel(h_ref, gm_ref, win_ref, qg_ref, kg_ref, seg_ref,
                 q_ref, k_ref, v_ref, qr_ref, fr_ref, ir_ref, gr_ref):
    u = _rms_rows(h_ref[...], gm_ref[...]).astype(BF16)
    q, k, v = _project_qkv(u, win_ref, qg_ref[...], kg_ref[...], seg_ref[...])
    q_ref[...] = q
    k_ref[...] = k
    v_ref[...] = v
    qr_ref[...] = _dot(u, win_ref[:, O_QR:O_FR])
    fr_ref[...] = _dot(u, win_ref[:, O_FR:O_IR])
    ir_ref[...] = _dot(u, win_ref[:, O_IR:O_GR]).astype(ir_ref.dtype)
    gr_ref[...] = _dot(u, win_ref[:, O_GR:O_GA])


def _proj(h, gm, w_in, qg, kg, seg, tm):
    t, d = h.shape
    row = lambda w: pl.BlockSpec((tm, w), lambda i: (i, 0))
    widths = SPLIT_SIZES[:7]
    dtypes = (BF16, F32, F32, F32, F32, BF16, F32)
    return pl.pallas_call(
        _proj_kernel,
        grid=(t // tm,),
        in_specs=[row(d)] + [_const_spec(a) for a in (gm, w_in, qg, kg, seg)],
        out_specs=[row(w) for w in widths],
        out_shape=[jax.ShapeDtypeStruct((t, w), dt) for w, dt in zip(widths, dtypes)],
        compiler_params=pltpu.CompilerParams(dimension_semantics=("parallel",), vmem_limit_bytes=VMEM_LIMIT),
        name="proj",
    )(h, gm, w_in, qg, kg, seg)


def _out_kernel(h_ref, att_ref, orr_ref, gm_ref, win_ref, wua_ref, wur_ref, wo_ref, h2_ref):
    h = h_ref[...]
    u = _rms_rows(h, gm_ref[...]).astype(BF16)
    sga = jax.nn.sigmoid(_dot(u, win_ref[:, O_GA:O_GB]))
    sgb = jax.nn.sigmoid(_dot(u, win_ref[:, O_GB:O_END]))
    h2_ref[...] = _merge_out(h, sga, sgb, att_ref[...], orr_ref[...], wua_ref, wur_ref, wo_ref)


def _out(h, att, orr, consts, tm):
    t, d = h.shape
    row = lambda a: pl.BlockSpec((tm, a.shape[1]), lambda i: (i, 0))
    return pl.pallas_call(
        _out_kernel,
        grid=(t // tm,),
        in_specs=[row(a) for a in (h, att, orr)] + [_const_spec(a) for a in consts],
        out_specs=pl.BlockSpec((tm, d), lambda i: (i, 0)),
        out_shape=jax.ShapeDtypeStruct((t, d), F32),
        compiler_params=pltpu.CompilerParams(dimension_semantics=("parallel",), vmem_limit_bytes=VMEM_LIMIT),
        name="out",
    )(h, att, orr, *consts)


def _bucket_maps(ld):
    dist_p = np.arange(WINDOW)[:, None] + WINDOW - np.arange(2 * WINDOW)[None, :]
    valid_p = (dist_p >= 0) & (dist_p <= WINDOW)
    bucket_p = np.where(valid_p, _t5_bucket(dist_p), -1).astype(np.int32)
    first = np.where(np.arange(2 * WINDOW)[None, :] >= WINDOW, bucket_p, -1).astype(np.int32)
    dist_s = np.arange(ld)[:, None] + WINDOW - np.arange(WINDOW + ld)[None, :]
    valid_s = (dist_s >= 0) & (dist_s <= WINDOW)
    bucket_s = np.where(valid_s, _t5_bucket(dist_s), -1).astype(np.int32)
    return first, bucket_p, bucket_s


def _stacked(bias, rows):
    return bias.reshape(N_KV_HEADS, GROUP * rows, bias.shape[-1])


def kernel(x_prompt, x_sample, cache_win_k, cache_win_v, state_hgrn, ffn1_norm, ffn1_w1, ffn1_w3, ffn1_w2, mix_norm, w_in, q_norm, k_norm, sinks, rel_bias_table, hgrn_lb_logits, hg_norm, w_up_attn, w_up_hgrn, w_out, ffn2_norm, ffn2_w1, ffn2_w3, ffn2_w2):
    depth = ffn1_norm.shape[0]
    assert depth == 1 and hgrn_lb_logits.shape[0] == 2, "single-layer step only"
    batch, seq, d = x_prompt.shape
    bd, ld, _ = x_sample.shape

    bf = lambda w: w[0].astype(BF16)
    row = lambda g: g[0].reshape(1, -1).astype(F32)
    w1a, w3a, w2a, w_in_b = bf(ffn1_w1), bf(ffn1_w3), bf(ffn1_w2), bf(w_in)
    w1b, w3b, w2b = bf(ffn2_w1), bf(ffn2_w3), bf(ffn2_w2)
    wua, wur, wo = bf(w_up_attn), bf(w_up_hgrn), bf(w_out)
    g1, gm, g2 = row(ffn1_norm), row(mix_norm), row(ffn2_norm)
    qg = jnp.tile(row(q_norm), (1, LANES // HEAD_DIM)) * (HEAD_DIM ** -0.5)
    kg = jnp.tile(row(k_norm), (1, LANES // HEAD_DIM))
    head_of_lane = np.arange(2 * LANES) // HEAD_DIM
    seg2 = jnp.asarray((head_of_lane[:, None] == head_of_lane[None, :]).astype(np.float32), BF16)
    seg = seg2[:LANES, :LANES]
    qg2 = jnp.tile(qg, (1, 2))
    ones = jnp.ones((HG_DIM, HG_DIM), BF16)
    tri = jnp.asarray(np.tril(np.ones((HG_CHUNK, HG_CHUNK), np.float32)), BF16)
    lb_logits = hgrn_lb_logits.astype(F32)
    hg_gain = row(hg_norm)

    first, later, bucket_s = _bucket_maps(ld)
    table = rel_bias_table.astype(F32)
    def paired(m):
        b = _rel_bias(table, np.ascontiguousarray(m.T))
        b = b.reshape(N_KV_HEADS, GROUP // 2, 2, 2 * WINDOW, WINDOW)
        return b.transpose(0, 2, 3, 1, 4).reshape(N_KV_HEADS, 2, 2 * WINDOW, 2 * WINDOW)
    bias_p = jnp.stack([paired(first), paired(later)])
    bias_s = _stacked(_rel_bias(table, bucket_s), ld)
    sink_1d = sinks[0].astype(F32)
    sink = sink_1d.reshape(N_KV_HEADS, GROUP, 1)
    sink_s = jnp.repeat(sink, ld, axis=1).reshape(N_KV_HEADS, GROUP * ld, 1)

    h = _ffn(x_prompt.reshape(batch * seq, d), g1, w1a, w3a, w2a, 512)
    consts = (gm, w_in_b, qg2, kg, seg, seg2, bias_p, sink_1d, lb_logits, hg_gain, tri, ones, wua, wur, wo)
    h2, k_p, v_p, s_p = _mix(h, consts, batch, seq, 512)
    y_p = _ffn(h2, g2, w1b, w3b, w2b, 512)
    win = lambda a: a.reshape(1, batch, WINDOW, N_KV_HEADS, HEAD_DIM)

    h = _ffn(x_sample.reshape(bd * ld, d), g1, w1a, w3a, w2a, 512)
    q, k, v, qr, fr, ir, gr = _proj(h, gm, w_in_b, qg, kg, seg, 256)
    ck = cache_win_k[0].reshape(bd, WINDOW, LANES)
    cv = cache_win_v[0].reshape(bd, WINDOW, LANES)
    att, nk, nv = _swa_sample(q, k, v, ck, cv, bias_s, sink_s, ld, 8)
    orr, s_s = _hgrn_sample(qr, fr, ir, gr, state_hgrn[0], lb_logits, hg_gain, ld, 8)
    h2 = _out(h, att, orr, (gm, w_in_b, wua, wur, wo), 256)
    y_s = _ffn(h2, g2, w1b, w3b, w2b, 512)
    unwin = lambda a: a.reshape(1, bd, WINDOW, N_KV_HEADS, HEAD_DIM)
    return (y_p.reshape(batch, seq, d), y_s.reshape(bd, ld, d), win(k_p), win(v_p), s_p[None],
            unwin(nk), unwin(nv), s_s[None])
```

```python
import functools

import numpy as np
import jax
import jax.numpy as jnp
from jax import lax
from jax.experimental import pallas as pl
from jax.experimental.pallas import tpu as pltpu

F32 = jnp.float32
BF16 = jnp.bfloat16

LANES = 128
HEAD_DIM = 64
N_Q_HEADS = 8
N_KV_HEADS = 2
GROUP = N_Q_HEADS // N_KV_HEADS
WINDOW = 128
N_BUCKETS = 32
MAX_DISTANCE = 128
HG_HEADS = 4
HG_DIM = 128
HG_CHUNK = 128
HG_SUB = 32
EPS = 1e-6
F_TINY = 1e-30
NEG_BIG = -1e30
EXP_CLAMP = 80.0
SPLIT_SIZES = (512, 128, 128, 512, 512, 512, 512, 1024, 1024)
SPLIT_OFFS = tuple(int(v) for v in np.cumsum((0,) + SPLIT_SIZES))
O_Q, O_K, O_V, O_QR, O_FR, O_IR, O_GR, O_GA, O_GB, O_END = SPLIT_OFFS
VMEM_LIMIT = 56 * 1024 * 1024


def _dot(a, b):
    return jnp.dot(a, b, preferred_element_type=F32)


def _dot_nt(a, b):
    return lax.dot_general(a, b, (((1,), (1,)), ((), ())), preferred_element_type=F32)


def _split3(x):
    hi = x.astype(BF16)
    r1 = x - hi.astype(F32)
    mid = r1.astype(BF16)
    lo = (r1 - mid.astype(F32)).astype(BF16)
    return hi, mid, lo


def _split3_dot(x, w):
    hi, mid, lo = _split3(x)
    return _dot(hi, w) + _dot(mid, w) + _dot(lo, w)


def _split3_ldot(w, x):
    hi, mid, lo = _split3(x)
    return _dot(w, hi) + _dot(w, mid) + _dot(w, lo)


def _split2_ldot(w, x):
    hi = x.astype(BF16)
    lo = (x - hi.astype(F32)).astype(BF16)
    return _dot(w, hi) + _dot(w, lo)


def _rms_rows(x, g):
    ms = jnp.mean(x * x, axis=-1, keepdims=True)
    return x * lax.rsqrt(ms + EPS) * g


def _seg_rms(x, seg, inv_n, g):
    ms = _dot((x * x).astype(BF16), seg) * inv_n
    return x * lax.rsqrt(ms + EPS) * g


class _Pool:
    def __init__(self):
        self.live = []
        self.out = {}

    def add(self, key, gen):
        self.live.append((key, gen))

    def round(self):
        still = []
        for key, gen in self.live:
            try:
                next(gen)
                still.append((key, gen))
            except StopIteration as done:
                self.out[key] = done.value
        self.live = still

    def has(self, keys):
        return all(k in self.out for k in keys)


def _const_spec(a):
    nd = a.ndim
    return pl.BlockSpec(a.shape, lambda *_: (0,) * nd, pipeline_mode=pl.Buffered(1))


def _ff_chunks(d_ff, step=1024):
    return tuple((lo, min(lo + step, d_ff)) for lo in range(0, d_ff, step))


def _swiglu(xn, w1_ref, w3_ref, w2_ref):
    acc = jnp.zeros((xn.shape[0], w2_ref.shape[1]), F32)
    for lo, hi in _ff_chunks(w1_ref.shape[1]):
        a = _dot(xn, w1_ref[:, lo:hi])
        b = _dot(xn, w3_ref[:, lo:hi])
        acc = acc + _dot((jax.nn.silu(a) * b).astype(BF16), w2_ref[lo:hi, :])
    return acc


def _ffn_kernel(x_ref, g1_ref, w1_ref, w3_ref, w2_ref, h_ref):
    x = x_ref[...]
    xn = _rms_rows(x, g1_ref[...]).astype(BF16)
    h_ref[...] = x + 0.5 * _swiglu(xn, w1_ref, w3_ref, w2_ref)


def _ffn(x, g1, w1, w3, w2, tm):
    t, d = x.shape
    row = pl.BlockSpec((tm, d), lambda i: (i, 0))
    return pl.pallas_call(
        _ffn_kernel,
        grid=(t // tm,),
        in_specs=[row] + [_const_spec(a) for a in (g1, w1, w3, w2)],
        out_specs=row,
        out_shape=jax.ShapeDtypeStruct((t, d), F32),
        compiler_params=pltpu.CompilerParams(dimension_semantics=("parallel",), vmem_limit_bytes=VMEM_LIMIT),
        name="ffn",
    )(x, g1, w1, w3, w2)


def _t5_bucket(dist):
    max_exact = N_BUCKETS // 2
    d = np.maximum(dist, 0)
    large = max_exact + (np.log(np.maximum(d, 1) / max_exact) / np.log(MAX_DISTANCE / max_exact)
                         * (N_BUCKETS - max_exact)).astype(np.int32)
    large = np.minimum(large, N_BUCKETS - 1)
    return np.where(d < max_exact, d, large).astype(np.int32)


def _bias_kernel(table_ref, bucket_ref, out_ref):
    bucket = bucket_ref[...]
    masked = jnp.where(bucket < 0, NEG_BIG, 0.0).astype(F32)
    for h in range(N_Q_HEADS):
        acc = masked
        for b in range(N_BUCKETS):
            acc = acc + jnp.where(bucket == b, table_ref[b, h], 0.0)
        out_ref[h] = acc


def _rel_bias(table, bucket_map):
    r, c = bucket_map.shape
    return pl.pallas_call(
        _bias_kernel,
        in_specs=[pl.BlockSpec(memory_space=pltpu.SMEM), pl.BlockSpec((r, c), lambda: (0, 0))],
        out_specs=pl.BlockSpec((N_Q_HEADS, r, c), lambda: (0, 0, 0)),
        out_shape=jax.ShapeDtypeStruct((N_Q_HEADS, r, c), F32),
        name="rel_bias",
    )(table, jnp.asarray(bucket_map))


def _project_qkv(u, win_ref, qg, kg, seg):
    qs = []
    for s in range(SPLIT_SIZES[0] // LANES):
        x = _dot(u, win_ref[:, O_Q + s * LANES:O_Q + (s + 1) * LANES])
        qs.append(_seg_rms(x, seg, 1.0 / HEAD_DIM, qg).astype(BF16))
    k = _seg_rms(_dot(u, win_ref[:, O_K:O_V]), seg, 1.0 / HEAD_DIM, kg)
    v = _dot(u, win_ref[:, O_V:O_QR])
    return jnp.concatenate(qs, axis=1), k, v


def _stack_group(q, kv):
    a = q[:, (2 * kv) * LANES:(2 * kv + 1) * LANES]
    b = q[:, (2 * kv + 1) * LANES:(2 * kv + 2) * LANES]
    ar, br = pltpu.roll(a, HEAD_DIM, 1), pltpu.roll(b, HEAD_DIM, 1)
    parts = (a, ar, b, br) if kv == 0 else (ar, a, br, b)
    return jnp.concatenate(parts, axis=0)


def _kv_half(x, kv, duplicate):
    lane = lax.broadcasted_iota(jnp.int32, x.shape, 1)
    keep = (lane < HEAD_DIM) if kv == 0 else (lane >= HEAD_DIM)
    xm = jnp.where(keep, x, 0.0)
    return xm + pltpu.roll(xm, HEAD_DIM, 1) if duplicate else xm


def _sink_softmax(s, sink):
    m = jnp.maximum(jnp.max(s, axis=-1, keepdims=True), sink)
    e = jnp.exp(s - m)
    return e / (jnp.sum(e, axis=-1, keepdims=True) + jnp.exp(sink - m))


def _attend_chain(q, kk, vv, bias_ref, sink_ref, r, kv):
    qs = _stack_group(q.astype(F32), kv).astype(BF16)
    kh = _kv_half(kk, kv, False).astype(BF16)
    s = _dot_nt(qs, kh) + bias_ref[kv]
    yield
    p = _sink_softmax(s, sink_ref[kv])
    yield
    o = _dot(p.astype(BF16), _kv_half(vv, kv, True).astype(BF16))
    yield
    lane = lax.broadcasted_iota(jnp.int32, (r, LANES), 1)
    return [jnp.where(lane < HEAD_DIM, o[(2 * pair) * r:(2 * pair + 1) * r], o[(2 * pair + 1) * r:(2 * pair + 2) * r])
            for pair in range(GROUP // 2)]


def _swa_sample_kernel(q_ref, kn_ref, vn_ref, ck_ref, cv_ref, bias_ref, sink_ref, o_ref, nk_ref, nv_ref, *, ld):
    pool = _Pool()
    nseq = ck_ref.shape[0]
    for i in range(nseq):
        rows = slice(i * ld, (i + 1) * ld)
        kk = jnp.concatenate([ck_ref[i], kn_ref[rows, :]], axis=0)
        vv = jnp.concatenate([cv_ref[i], vn_ref[rows, :]], axis=0)
        nk_ref[i] = kk[ld:, :]
        nv_ref[i] = vv[ld:, :]
        for kv in range(N_KV_HEADS):
            pool.add((i, kv), _attend_chain(q_ref[rows, :], kk, vv, bias_ref, sink_ref, ld, kv))
    while pool.live:
        pool.round()
    for i in range(nseq):
        slabs = [s for kv in range(N_KV_HEADS) for s in pool.out[(i, kv)]]
        o_ref[i * ld:(i + 1) * ld, :] = jnp.concatenate(slabs, axis=1).astype(o_ref.dtype)


def _swa_sample(q, k, v, cache_k, cache_v, bias, sink_col, ld, sb):
    bd = cache_k.shape[0]
    tok = lambda w: pl.BlockSpec((sb * ld, w), lambda i: (i, 0))
    cache = pl.BlockSpec((sb, WINDOW, LANES), lambda i: (i, 0, 0))
    return pl.pallas_call(
        functools.partial(_swa_sample_kernel, ld=ld),
        grid=(bd // sb,),
        in_specs=[tok(4 * LANES), tok(LANES), tok(LANES), cache, cache,
                  pl.BlockSpec(bias.shape, lambda i: (0, 0, 0)), pl.BlockSpec(sink_col.shape, lambda i: (0, 0, 0))],
        out_specs=[tok(4 * LANES), cache, cache],
        out_shape=[jax.ShapeDtypeStruct(q.shape, BF16), jax.ShapeDtypeStruct(cache_k.shape, F32),
                   jax.ShapeDtypeStruct(cache_v.shape, F32)],
        compiler_params=pltpu.CompilerParams(dimension_semantics=("parallel",)),
        name="swa_sample",
    )(q, k, v, cache_k, cache_v, bias, sink_col)


def _lower_bound(lb_logits):
    z = lb_logits - jnp.max(lb_logits, axis=0, keepdims=True)
    e = jnp.exp(z)
    return e[0:1, :] / jnp.sum(e, axis=0, keepdims=True)


def _gates(qr, fr, lb):
    f = lb + (1.0 - lb) * jax.nn.sigmoid(fr)
    return jax.nn.silu(qr), 1.0 - f, jnp.log(jnp.maximum(f, F_TINY))


def _head_out(o, gr, hg_gain, ones_seg):
    return _seg_rms(o, ones_seg, 1.0 / HG_DIM, hg_gain) * jax.nn.silu(gr)


def _hgrn_proj(u, win_ref, pair):
    cols = lambda off: slice(off + 2 * pair * HG_DIM, off + 2 * (pair + 1) * HG_DIM)
    qr = _dot(u, win_ref[:, cols(O_QR)])
    fr = _dot(u, win_ref[:, cols(O_FR)])
    yield
    ir = _dot(u, win_ref[:, cols(O_IR)]).astype(BF16)
    gr = _dot(u, win_ref[:, cols(O_GR)])
    yield
    return qr, fr, ir, gr


def _hgrn_head(qr, fr, vals, gr, lb, hd, s_ref, s_out_ref, hg_gain, ones, tri):
    qf, kk, g = _gates(qr, fr, lb)
    nchunk = qr.shape[0] // HG_CHUNK
    chunks = [slice(c * HG_CHUNK, (c + 1) * HG_CHUNK) for c in range(nchunk)]
    yield
    gcs = [_split2_ldot(tri, g[rows]) for rows in chunks]
    yield
    q_in, upd, decay, scores = [], [], [], []
    for rows, gc in zip(chunks, gcs):
        g_last = gc[HG_CHUNK - 1:HG_CHUNK, :]
        q_in.append((qf[rows] * jnp.exp(gc)).astype(BF16))
        k_end_t = (kk[rows] * jnp.exp(g_last - gc)).T
        upd.append(_dot(k_end_t.astype(BF16), vals[rows]))
        decay.append(jnp.broadcast_to(jnp.exp(g_last), (HG_DIM, HG_DIM)).T)
        qc, kc = qf[rows], kk[rows]
        for i in range(HG_CHUNK // HG_SUB):
            lo, hi = i * HG_SUB, (i + 1) * HG_SUB
            g_base = gc[lo - 1:lo, :] if i else jnp.zeros((1, HG_DIM), F32)
            q_hat = qc[lo:hi] * jnp.exp(gc[lo:hi] - g_base)
            k_hat = kc[:hi] * jnp.exp(jnp.minimum(g_base - gc[:hi], EXP_CLAMP))
            a = _dot_nt(q_hat.astype(BF16), k_hat.astype(BF16))
            row = lax.broadcasted_iota(jnp.int32, (HG_SUB, hi), 0)
            col = lax.broadcasted_iota(jnp.int32, (HG_SUB, hi), 1)
            scores.append(jnp.where(col <= row + lo, a, 0.0).astype(BF16))
    yield
    intra = []
    nsub = HG_CHUNK // HG_SUB
    for c, rows in enumerate(chunks):
        vc = vals[rows]
        parts = [_dot(scores[c * nsub + i], vc[:(i + 1) * HG_SUB]) for i in range(nsub)]
        intra.append(jnp.concatenate(parts, axis=0))
    yield
    s = s_ref[hd]
    outs = []
    for c in range(nchunk):
        outs.append(_dot(q_in[c], s.astype(BF16)) + intra[c])
        s = decay[c] * s + upd[c]
    s_ref[hd] = s
    s_out_ref[0, hd] = s
    yield
    o = jnp.concatenate(outs, axis=0)
    return _head_out(o, gr, hg_gain, ones).astype(BF16)


def _attn_chain(q_pair, k_half, v_t, bias_ref, sink_a, sink_b, kv):
    s = _dot_nt(k_half, q_pair) + bias_ref[...]
    yield
    col = lax.broadcasted_iota(jnp.int32, (1, s.shape[1]), 1)
    sink = jnp.where(col < WINDOW, sink_a, sink_b)
    m = jnp.maximum(jnp.max(s, axis=0, keepdims=True), sink)
    e = jnp.exp(s - m)
    yield
    denom = jnp.sum(e, axis=0, keepdims=True) + jnp.exp(sink - m)
    o_t = _dot(v_t, e.astype(BF16))[kv * HEAD_DIM:(kv + 1) * HEAD_DIM]
    yield
    return o_t * (1.0 / denom)


def _gate_stream(u, win_ref, step=256):
    res = []
    for off in (O_GA, O_GB):
        cols = []
        for lo in range(0, SPLIT_SIZES[7], step):
            cols.append(jax.nn.sigmoid(_dot(u, win_ref[:, off + lo:off + lo + step])))
            yield
        res.append(jnp.concatenate(cols, axis=1))
    return res


def _q_cols(u, win_ref, kv, seg2, gain2):
    x = _dot(u, win_ref[:, O_Q + 2 * kv * LANES:O_Q + 2 * (kv + 1) * LANES])
    yield
    ms = _dot((x * x).astype(BF16), seg2) * (1.0 / HEAD_DIM)
    yield
    return (x * lax.rsqrt(ms + EPS) * gain2).astype(BF16)


def _kv_cols(u, win_ref, seg, gain):
    x = _dot(u, win_ref[:, O_K:O_QR])
    yield
    k, v = x[:, :LANES], x[:, LANES:]
    ms = _dot((k * k).astype(BF16), seg) * (1.0 / HEAD_DIM)
    yield
    return k * lax.rsqrt(ms + EPS) * gain, v


def _attn_up(parts, wua_ref):
    att = jnp.concatenate(parts, axis=0).T.astype(BF16)
    yield
    return _dot(att, wua_ref[...])


def _hgrn_sample_kernel(qr_ref, fr_ref, ir_ref, gr_ref, s0_ref, lb_ref, hg_ref, tri_ref, tot_ref, ones_ref,
                        o_ref, s_out_ref, *, ld):
    nseq = s0_ref.shape[0]
    r = nseq * ld
    lb = _lower_bound(lb_ref[...])
    qf_all, kk_all, g_all = _gates(qr_ref[...], fr_ref[...], lb)
    tri, tot = tri_ref[...], tot_ref[...]
    causal = tri.astype(F32) > 0
    lane = lax.broadcasted_iota(jnp.int32, (HG_DIM, r), 1)

    def head(hd):
        lanes = slice(hd * HG_DIM, (hd + 1) * HG_DIM)
        qf, kk, v = qf_all[:, lanes], kk_all[:, lanes], ir_ref[:, lanes]
        g = _split2_ldot(tri, g_all[:, lanes])
        g_last = _split2_ldot(tot, g_all[:, lanes])
        yield
        q_t = (qf * jnp.exp(g)).astype(BF16)
        k_t = kk * jnp.exp(jnp.minimum(-g, EXP_CLAMP))
        a = jnp.where(causal, _dot_nt(q_t, k_t.astype(BF16)), 0.0)
        k_end_t = (kk * jnp.exp(g_last - g)).T
        decay_t = jnp.exp(g_last).T
        yield
        intra = _dot(a.astype(BF16), v)
        outs = []
        for i in range(nseq):
            s_prev = s0_ref[i, hd]
            outs.append(_dot(q_t[i * ld:(i + 1) * ld], s_prev.astype(BF16)))
            own = (lane >= i * ld) & (lane < (i + 1) * ld)
            upd = _dot(jnp.where(own, k_end_t, 0.0).astype(BF16), v)
            s_out_ref[i, hd] = decay_t[:, i * ld:i * ld + 1] * s_prev + upd
            if i % 2:
                yield
        o = jnp.concatenate(outs, axis=0) + intra
        yield
        o_ref[:, lanes] = _head_out(o, gr_ref[:, lanes], hg_ref[...], ones_ref[...]).astype(o_ref.dtype)

    pool = _Pool()
    for hd in range(HG_HEADS):
        pool.add(hd, head(hd))
    while pool.live:
        pool.round()


def _hgrn_sample(qr, fr, ir, gr, s0, lb_logits, hg_gain, ld, sb):
    bd = s0.shape[0]
    r = sb * ld
    seq = np.arange(r) // ld
    same = seq[:, None] == seq[None, :]
    tri = jnp.asarray((same & (np.arange(r)[:, None] >= np.arange(r)[None, :])).astype(np.float32), BF16)
    tot = jnp.asarray(same.astype(np.float32), BF16)
    ones = jnp.ones((HG_DIM, HG_DIM), BF16)
    tok = pl.BlockSpec((r, HG_HEADS * HG_DIM), lambda i: (i, 0))
    state = pl.BlockSpec((sb, HG_HEADS, HG_DIM, HG_DIM), lambda i: (i, 0, 0, 0))
    const = lambda a: pl.BlockSpec(a.shape, lambda i: (0,) * a.ndim)
    return pl.pallas_call(
        functools.partial(_hgrn_sample_kernel, ld=ld),
        grid=(bd // sb,),
        in_specs=[tok, tok, tok, tok, state, const(lb_logits), const(hg_gain), const(tri), const(tot), const(ones)],
        out_specs=[tok, state],
        out_shape=[jax.ShapeDtypeStruct(qr.shape, BF16), jax.ShapeDtypeStruct(s0.shape, F32)],
        compiler_params=pltpu.CompilerParams(dimension_semantics=("parallel",)),
        name="hgrn_sample",
    )(qr, fr, ir, gr, s0, lb_logits, hg_gain, tri, tot, ones)


def _merge_out(h, sga, sgb, att, orr, wua_ref, wur_ref, wo_ref):
    merged = sga * _dot(att, wua_ref[...]) + sgb * _dot(orr, wur_ref[...])
    return h + _dot(merged.astype(BF16), wo_ref[...])


def _mix_kernel(h_ref, gm_ref, win_ref, qg_ref, kg_ref, seg_ref, seg2_ref, bias_ref, sink_ref, lb_ref, hg_ref,
                tri_ref, ones_ref, wua_ref, wur_ref, wo_ref,
                h2_ref, k_out_ref, v_out_ref, s_out_ref, kprev_ref, vprev_ref, s_ref):
    tm = h_ref.shape[0]
    first = pl.program_id(1) == 0

    @pl.when(first)
    def _():
        kprev_ref[...] = jnp.zeros(kprev_ref.shape, F32)
        vprev_ref[...] = jnp.zeros(vprev_ref.shape, F32)
        s_ref[...] = jnp.zeros(s_ref.shape, F32)

    h = h_ref[...]
    u = _rms_rows(h, gm_ref[...]).astype(BF16)

    pool = _Pool()
    lb = _lower_bound(lb_ref[...])
    for kv in range(N_KV_HEADS):
        pool.add(("q", kv), _q_cols(u, win_ref, kv, seg2_ref[...], qg_ref[...]))
    pool.add("kv", _kv_cols(u, win_ref, seg_ref[...], kg_ref[...]))
    for pair in range(HG_HEADS // 2):
        pool.add(("hproj", pair), _hgrn_proj(u, win_ref, pair))
    pool.add("gates", _gate_stream(u, win_ref))
    first_keys = [("q", kv) for kv in range(N_KV_HEADS)] + ["kv"] + [("hproj", p) for p in range(HG_HEADS // 2)]
    while not pool.has(first_keys):
        pool.round()
    for hd in range(HG_HEADS):
        qr, fr, ir, gr = (a[:, (hd % 2) * HG_DIM:(hd % 2 + 1) * HG_DIM] for a in pool.out[("hproj", hd // 2)])
        pool.add(("hgrn", hd), _hgrn_head(qr, fr, ir, gr, lb[:, hd * HG_DIM:(hd + 1) * HG_DIM], hd, s_ref, s_out_ref,
                                          hg_ref[...], ones_ref[...], tri_ref[...]))

    q = [pool.out[("q", j // 2)][:, (j % 2) * LANES:(j % 2 + 1) * LANES] for j in range(SPLIT_SIZES[0] // LANES)]
    k, v = pool.out["kv"]
    kk_all = jnp.concatenate([kprev_ref[...], k], axis=0)
    vv_all = jnp.concatenate([vprev_ref[...], v], axis=0)
    k_tail, v_tail = k[tm - WINDOW:], v[tm - WINDOW:]
    kprev_ref[...] = k_tail
    vprev_ref[...] = v_tail
    k_out_ref[...] = k_tail
    v_out_ref[...] = v_tail
    lane = lax.broadcasted_iota(jnp.int32, kk_all.shape, 1)
    k_lo0 = jnp.where(lane < HEAD_DIM, kk_all, 0.0)
    k_hi1 = jnp.where(lane >= HEAD_DIM, kk_all, 0.0)
    k_halves = ((k_lo0.astype(BF16), pltpu.roll(k_lo0, HEAD_DIM, 1).astype(BF16)),
                (pltpu.roll(k_hi1, HEAD_DIM, 1).astype(BF16), k_hi1.astype(BF16)))
    v_t = vv_all.T.astype(BF16)

    n_blk = tm // WINDOW
    chain_keys = lambda blk: [("att", blk, kv, half) for kv in range(N_KV_HEADS) for half in range(2)]
    for blk in range(n_blk):
        lo = blk * WINDOW
        variant = jnp.where(first, 0, 1) if blk == 0 else 1
        for kv in range(N_KV_HEADS):
            q_pair = jnp.concatenate([q[2 * kv + j][lo:lo + WINDOW] for j in range(GROUP // 2)], axis=0)
            for half in range(2):
                pool.add(("att", blk, kv, half),
                         _attn_chain(q_pair, k_halves[kv][half][lo:lo + 2 * WINDOW], v_t[:, lo:lo + 2 * WINDOW],
                                     bias_ref.at[variant, kv, half], sink_ref[GROUP * kv + half],
                                     sink_ref[GROUP * kv + 2 + half], kv))

    waiting = list(range(n_blk))
    while pool.live:
        pool.round()
        for blk in [b for b in waiting if pool.has(chain_keys(b))]:
            waiting.remove(blk)
            parts = []
            for hq in range(N_Q_HEADS):
                kv, j, half = hq // GROUP, (hq % GROUP) // 2, hq % 2
                parts.append(pool.out[("att", blk, kv, half)][:, j * WINDOW:(j + 1) * WINDOW])
            pool.add(("up", blk), _attn_up(parts, wua_ref))

    up_a = jnp.concatenate([pool.out[("up", blk)] for blk in range(n_blk)], axis=0)
    orr = jnp.concatenate([pool.out[("hgrn", hd)] for hd in range(HG_HEADS)], axis=1)
    sga, sgb = pool.out["gates"]
    merged = sga * up_a + sgb * _dot(orr, wur_ref[...])
    h2_ref[...] = h + _dot(merged.astype(BF16), wo_ref[...])


def _mix(h, consts, batch, seq, tm):
    t, d = h.shape
    nt = seq // tm
    row = pl.BlockSpec((tm, d), lambda b, i: (b * nt + i, 0))
    win_out = pl.BlockSpec((WINDOW, LANES), lambda b, i: (b, 0))
    return pl.pallas_call(
        _mix_kernel,
        grid=(batch, nt),
        in_specs=[row] + [pl.BlockSpec(memory_space=pltpu.SMEM) if a.ndim == 1 else _const_spec(a) for a in consts],
        out_specs=[row, win_out, win_out, pl.BlockSpec((1, HG_HEADS, HG_DIM, HG_DIM), lambda b, i: (b, 0, 0, 0))],
        out_shape=[jax.ShapeDtypeStruct((t, d), F32),
                   jax.ShapeDtypeStruct((batch * WINDOW, LANES), F32),
                   jax.ShapeDtypeStruct((batch * WINDOW, LANES), F32),
                   jax.ShapeDtypeStruct((batch, HG_HEADS, HG_DIM, HG_DIM), F32)],
        scratch_shapes=[pltpu.VMEM((WINDOW, LANES), F32), pltpu.VMEM((WINDOW, LANES), F32),
                        pltpu.VMEM((HG_HEADS, HG_DIM, HG_DIM), F32)],
        compiler_params=pltpu.CompilerParams(dimension_semantics=("parallel", "arbitrary"),
                                             vmem_limit_bytes=VMEM_LIMIT),
        name="mix",
    )(h, *consts)


def _proj_kernel(h_ref, gm_ref, win_ref, qg_ref, kg_ref, seg_ref,
                 q_ref, k_ref, v_ref, qr_ref, fr_ref, ir_ref, gr_ref):
    u = _rms_rows(h_ref[...], gm_ref[...]).astype(BF16)
    q, k, v = _project_qkv(u, win_ref, qg_ref[...], kg_ref[...], seg_ref[...])
    q_ref[...] = q
    k_ref[...] = k
    v_ref[...] = v
    qr_ref[...] = _dot(u, win_ref[:, O_QR:O_FR])
    fr_ref[...] = _dot(u, win_ref[:, O_FR:O_IR])
    ir_ref[...] = _dot(u, win_ref[:, O_IR:O_GR]).astype(ir_ref.dtype)
    gr_ref[...] = _dot(u, win_ref[:, O_GR:O_GA])


def _proj(h, gm, w_in, qg, kg, seg, tm):
    t, d = h.shape
    row = lambda w: pl.BlockSpec((tm, w), lambda i: (i, 0))
    widths = SPLIT_SIZES[:7]
    dtypes = (BF16, F32, F32, F32, F32, BF16, F32)
    return pl.pallas_call(
        _proj_kernel,
        grid=(t // tm,),
        in_specs=[row(d)] + [_const_spec(a) for a in (gm, w_in, qg, kg, seg)],
        out_specs=[row(w) for w in widths],
        out_shape=[jax.ShapeDtypeStruct((t, w), dt) for w, dt in zip(widths, dtypes)],
        compiler_params=pltpu.CompilerParams(dimension_semantics=("parallel",), vmem_limit_bytes=VMEM_LIMIT),
        name="proj",
    )(h, gm, w_in, qg, kg, seg)


def _out_kernel(h_ref, att_ref, orr_ref, gm_ref, win_ref, wua_ref, wur_ref, wo_ref, h2_ref):
    h = h_ref[...]
    u = _rms_rows(h, gm_ref[...]).astype(BF16)
    sga = jax.nn.sigmoid(_dot(u, win_ref[:, O_GA:O_GB]))
    sgb = jax.nn.sigmoid(_dot(u, win_ref[:, O_GB:O_END]))
    h2_ref[...] = _merge_out(h, sga, sgb, att_ref[...], orr_ref[...], wua_ref, wur_ref, wo_ref)


def _out(h, att, orr, consts, tm):
    t, d = h.shape
    row = lambda a: pl.BlockSpec((tm, a.shape[1]), lambda i: (i, 0))
    return pl.pallas_call(
        _out_kernel,
        grid=(t // tm,),
        in_specs=[row(a) for a in (h, att, orr)] + [_const_spec(a) for a in consts],
        out_specs=pl.BlockSpec((tm, d), lambda i: (i, 0)),
        out_shape=jax.ShapeDtypeStruct((t, d), F32),
        compiler_params=pltpu.CompilerParams(dimension_semantics=("parallel",), vmem_limit_bytes=VMEM_LIMIT),
        name="out",
    )(h, att, orr, *consts)


def _bucket_maps(ld):
    dist_p = np.arange(WINDOW)[:, None] + WINDOW - np.arange(2 * WINDOW)[None, :]
    valid_p = (dist_p >= 0) & (dist_p <= WINDOW)
    bucket_p = np.where(valid_p, _t5_bucket(dist_p), -1).astype(np.int32)
    first = np.where(np.arange(2 * WINDOW)[None, :] >= WINDOW, bucket_p, -1).astype(np.int32)
    dist_s = np.arange(ld)[:, None] + WINDOW - np.arange(WINDOW + ld)[None, :]
    valid_s = (dist_s >= 0) & (dist_s <= WINDOW)
    bucket_s = np.where(valid_s, _t5_bucket(dist_s), -1).astype(np.int32)
    return first, bucket_p, bucket_s


def _stacked(bias, rows):
    return bias.reshape(N_KV_HEADS, GROUP * rows, bias.shape[-1])


def kernel(x_prompt, x_sample, cache_win_k, cache_win_v, state_hgrn, ffn1_norm, ffn1_w1, ffn1_w3, ffn1_w2, mix_norm, w_in, q_norm, k_norm, sinks, rel_bias_table, hgrn_lb_logits, hg_norm, w_up_attn, w_up_hgrn, w_out, ffn2_norm, ffn2_w1, ffn2_w3, ffn2_w2):
    depth = ffn1_norm.shape[0]
    assert depth == 1 and hgrn_lb_logits.shape[0] == 2, "single-layer step only"
    batch, seq, d = x_prompt.shape
    bd, ld, _ = x_sample.shape

    bf = lambda w: w[0].astype(BF16)
    row = lambda g: g[0].reshape(1, -1).astype(F32)
    w1a, w3a, w2a, w_in_b = bf(ffn1_w1), bf(ffn1_w3), bf(ffn1_w2), bf(w_in)
    w1b, w3b, w2b = bf(ffn2_w1), bf(ffn2_w3), bf(ffn2_w2)
    wua, wur, wo = bf(w_up_attn), bf(w_up_hgrn), bf(w_out)
    g1, gm, g2 = row(ffn1_norm), row(mix_norm), row(ffn2_norm)
    qg = jnp.tile(row(q_norm), (1, LANES // HEAD_DIM)) * (HEAD_DIM ** -0.5)
    kg = jnp.tile(row(k_norm), (1, LANES // HEAD_DIM))
    head_of_lane = np.arange(2 * LANES) // HEAD_DIM
    seg2 = jnp.asarray((head_of_lane[:, None] == head_of_lane[None, :]).astype(np.float32), BF16)
    seg = seg2[:LANES, :LANES]
    qg2 = jnp.tile(qg, (1, 2))
    ones = jnp.ones((HG_DIM, HG_DIM), BF16)
    tri = jnp.asarray(np.tril(np.ones((HG_CHUNK, HG_CHUNK), np.float32)), BF16)
    lb_logits = hgrn_lb_logits.astype(F32)
    hg_gain = row(hg_norm)

    first, later, bucket_s = _bucket_maps(ld)
    table = rel_bias_table.astype(F32)
    def paired(m):
        b = _rel_bias(table, np.ascontiguousarray(m.T))
        b = b.reshape(N_KV_HEADS, GROUP // 2, 2, 2 * WINDOW, WINDOW)
        return b.transpose(0, 2, 3, 1, 4).reshape(N_KV_HEADS, 2, 2 * WINDOW, 2 * WINDOW)
    bias_p = jnp.stack([paired(first), paired(later)])
    bias_s = _stacked(_rel_bias(table, bucket_s), ld)
    sink_1d = sinks[0].astype(F32)
    sink = sink_1d.reshape(N_KV_HEADS, GROUP, 1)
    sink_s = jnp.repeat(sink, ld, axis=1).reshape(N_KV_HEADS, GROUP * ld, 1)

    h = _ffn(x_prompt.reshape(batch * seq, d), g1, w1a, w3a, w2a, 512)
    consts = (gm, w_in_b, qg2, kg, seg, seg2, bias_p, sink_1d, lb_logits, hg_gain, tri, ones, wua, wur, wo)
    h2, k_p, v_p, s_p = _mix(h, consts, batch, seq, 512)
    y_p = _ffn(h2, g2, w1b, w3b, w2b, 512)
    win = lambda a: a.reshape(1, batch, WINDOW, N_KV_HEADS, HEAD_DIM)

    h = _ffn(x_sample.reshape(bd * ld, d), g1, w1a, w3a, w2a, 512)
    q, k, v, qr, fr, ir, gr = _proj(h, gm, w_in_b, qg, kg, seg, 256)
    ck = cache_win_k[0].reshape(bd, WINDOW, LANES)
    cv = cache_win_v[0].reshape(bd, WINDOW, LANES)
    att, nk, nv = _swa_sample(q, k, v, ck, cv, bias_s, sink_s, ld, 16)
    orr, s_s = _hgrn_sample(qr, fr, ir, gr, state_hgrn[0], lb_logits, hg_gain, ld, 16)
    h2 = _out(h, att, orr, (gm, w_in_b, wua, wur, wo), 256)
    y_s = _ffn(h2, g2, w1b, w3b, w2b, 512)
    unwin = lambda a: a.reshape(1, bd, WINDOW, N_KV_HEADS, HEAD_DIM)
    return (y_p.reshape(batch, seq, d), y_s.reshape(bd, ld, d), win(k_p), win(v_p), s_p[None],
            unwin(nk), unwin(nv), s_s[None])
```

```python
import functools

import numpy as np
import jax
import jax.numpy as jnp
from jax import lax
from jax.experimental import pallas as pl
from jax.experimental.pallas import tpu as pltpu

F32 = jnp.float32
BF16 = jnp.bfloat16

LANES = 128
HEAD_DIM = 64
N_Q_HEADS = 8
N_KV_HEADS = 2
GROUP = N_Q_HEADS // N_KV_HEADS
WINDOW = 128
N_BUCKETS = 32
MAX_DISTANCE = 128
HG_HEADS = 4
HG_DIM = 128
HG_CHUNK = 128
HG_SUB = 32
EPS = 1e-6
F_TINY = 1e-30
NEG_BIG = -1e30
EXP_CLAMP = 80.0
SPLIT_SIZES = (512, 128, 128, 512, 512, 512, 512, 1024, 1024)
SPLIT_OFFS = tuple(int(v) for v in np.cumsum((0,) + SPLIT_SIZES))
O_Q, O_K, O_V, O_QR, O_FR, O_IR, O_GR, O_GA, O_GB, O_END = SPLIT_OFFS
VMEM_LIMIT = 56 * 1024 * 1024
TM_FFN = 512
TM_MIX = 512
TM_SAMPLE = 256
SEQS_PER_STEP = 16


def _dot(a, b):
    return jnp.dot(a, b, preferred_element_type=F32)


def _dot_nt(a, b):
    return lax.dot_general(a, b, (((1,), (1,)), ((), ())), preferred_element_type=F32)


def _split2_ldot(w, x):
    hi = x.astype(BF16)
    lo = (x - hi.astype(F32)).astype(BF16)
    return _dot(w, hi) + _dot(w, lo)


def _rms_rows(x, g):
    ms = jnp.mean(x * x, axis=-1, keepdims=True)
    return x * lax.rsqrt(ms + EPS) * g


def _seg_rms(x, seg, inv_n, g):
    ms = _dot((x * x).astype(BF16), seg) * inv_n
    return x * lax.rsqrt(ms + EPS) * g


class _Pool:
    def __init__(self):
        self.live = []
        self.out = {}

    def add(self, key, gen):
        self.live.append((key, gen))

    def round(self):
        still = []
        for key, gen in self.live:
            try:
                next(gen)
                still.append((key, gen))
            except StopIteration as done:
                self.out[key] = done.value
        self.live = still

    def has(self, keys):
        return all(k in self.out for k in keys)


def _const_spec(a):
    nd = a.ndim
    return pl.BlockSpec(a.shape, lambda *_: (0,) * nd, pipeline_mode=pl.Buffered(1))


def _ff_chunks(d_ff, step=1024):
    return tuple((lo, min(lo + step, d_ff)) for lo in range(0, d_ff, step))


def _swiglu(xn, w1_ref, w3_ref, w2_ref):
    acc = jnp.zeros((xn.shape[0], w2_ref.shape[1]), F32)
    for lo, hi in _ff_chunks(w1_ref.shape[1]):
        a = _dot(xn, w1_ref[:, lo:hi])
        b = _dot(xn, w3_ref[:, lo:hi])
        acc = acc + _dot((jax.nn.silu(a) * b).astype(BF16), w2_ref[lo:hi, :])
    return acc


def _ffn_kernel(x_ref, g1_ref, w1_ref, w3_ref, w2_ref, h_ref):
    x = x_ref[...]
    xn = _rms_rows(x, g1_ref[...]).astype(BF16)
    h_ref[...] = x + 0.5 * _swiglu(xn, w1_ref, w3_ref, w2_ref)


def _ffn(x, g1, w1, w3, w2, tm):
    t, d = x.shape
    row = pl.BlockSpec((tm, d), lambda i: (i, 0))
    return pl.pallas_call(
        _ffn_kernel,
        grid=(t // tm,),
        in_specs=[row] + [_const_spec(a) for a in (g1, w1, w3, w2)],
        out_specs=row,
        out_shape=jax.ShapeDtypeStruct((t, d), F32),
        compiler_params=pltpu.CompilerParams(dimension_semantics=("parallel",), vmem_limit_bytes=VMEM_LIMIT),
        name="ffn",
    )(x, g1, w1, w3, w2)


def _t5_bucket(dist):
    max_exact = N_BUCKETS // 2
    d = np.maximum(dist, 0)
    large = max_exact + (np.log(np.maximum(d, 1) / max_exact) / np.log(MAX_DISTANCE / max_exact)
                         * (N_BUCKETS - max_exact)).astype(np.int32)
    large = np.minimum(large, N_BUCKETS - 1)
    return np.where(d < max_exact, d, large).astype(np.int32)


def _bias_kernel(table_ref, bucket_ref, out_ref):
    bucket = bucket_ref[...]
    masked = jnp.where(bucket < 0, NEG_BIG, 0.0).astype(F32)
    for h in range(N_Q_HEADS):
        acc = masked
        for b in range(N_BUCKETS):
            acc = acc + jnp.where(bucket == b, table_ref[b, h], 0.0)
        out_ref[h] = acc


def _rel_bias(table, bucket_map):
    r, c = bucket_map.shape
    return pl.pallas_call(
        _bias_kernel,
        in_specs=[pl.BlockSpec(memory_space=pltpu.SMEM), pl.BlockSpec((r, c), lambda: (0, 0))],
        out_specs=pl.BlockSpec((N_Q_HEADS, r, c), lambda: (0, 0, 0)),
        out_shape=jax.ShapeDtypeStruct((N_Q_HEADS, r, c), F32),
        name="rel_bias",
    )(table, jnp.asarray(bucket_map))


def _project_qkv(u, win_ref, qg, kg, seg):
    qs = []
    for s in range(SPLIT_SIZES[0] // LANES):
        x = _dot(u, win_ref[:, O_Q + s * LANES:O_Q + (s + 1) * LANES])
        qs.append(_seg_rms(x, seg, 1.0 / HEAD_DIM, qg).astype(BF16))
    k = _seg_rms(_dot(u, win_ref[:, O_K:O_V]), seg, 1.0 / HEAD_DIM, kg)
    v = _dot(u, win_ref[:, O_V:O_QR])
    return jnp.concatenate(qs, axis=1), k, v


def _stack_group(q, kv):
    a = q[:, (2 * kv) * LANES:(2 * kv + 1) * LANES]
    b = q[:, (2 * kv + 1) * LANES:(2 * kv + 2) * LANES]
    ar, br = pltpu.roll(a, HEAD_DIM, 1), pltpu.roll(b, HEAD_DIM, 1)
    parts = (a, ar, b, br) if kv == 0 else (ar, a, br, b)
    return jnp.concatenate(parts, axis=0)


def _kv_half(x, kv, duplicate):
    lane = lax.broadcasted_iota(jnp.int32, x.shape, 1)
    keep = (lane < HEAD_DIM) if kv == 0 else (lane >= HEAD_DIM)
    xm = jnp.where(keep, x, 0.0)
    return xm + pltpu.roll(xm, HEAD_DIM, 1) if duplicate else xm


def _sink_softmax(s, sink):
    m = jnp.maximum(jnp.max(s, axis=-1, keepdims=True), sink)
    e = jnp.exp(s - m)
    return e / (jnp.sum(e, axis=-1, keepdims=True) + jnp.exp(sink - m))


def _attend_chain(q, kk, vv, bias_ref, sink_ref, r, kv):
    qs = _stack_group(q.astype(F32), kv).astype(BF16)
    kh = _kv_half(kk, kv, False).astype(BF16)
    s = _dot_nt(qs, kh) + bias_ref[kv]
    yield
    p = _sink_softmax(s, sink_ref[kv])
    yield
    o = _dot(p.astype(BF16), _kv_half(vv, kv, True).astype(BF16))
    yield
    lane = lax.broadcasted_iota(jnp.int32, (r, LANES), 1)
    return [jnp.where(lane < HEAD_DIM, o[(2 * pair) * r:(2 * pair + 1) * r], o[(2 * pair + 1) * r:(2 * pair + 2) * r])
            for pair in range(GROUP // 2)]


def _swa_sample_kernel(q_ref, kn_ref, vn_ref, ck_ref, cv_ref, bias_ref, sink_ref, o_ref, nk_ref, nv_ref, *, ld):
    pool = _Pool()
    nseq = ck_ref.shape[0]
    for i in range(nseq):
        rows = slice(i * ld, (i + 1) * ld)
        kk = jnp.concatenate([ck_ref[i], kn_ref[rows, :]], axis=0)
        vv = jnp.concatenate([cv_ref[i], vn_ref[rows, :]], axis=0)
        nk_ref[i] = kk[ld:, :]
        nv_ref[i] = vv[ld:, :]
        for kv in range(N_KV_HEADS):
            pool.add((i, kv), _attend_chain(q_ref[rows, :], kk, vv, bias_ref, sink_ref, ld, kv))
    while pool.live:
        pool.round()
    for i in range(nseq):
        slabs = [s for kv in range(N_KV_HEADS) for s in pool.out[(i, kv)]]
        o_ref[i * ld:(i + 1) * ld, :] = jnp.concatenate(slabs, axis=1).astype(o_ref.dtype)


def _swa_sample(q, k, v, cache_k, cache_v, bias, sink_col, ld, sb):
    bd = cache_k.shape[0]
    tok = lambda w: pl.BlockSpec((sb * ld, w), lambda i: (i, 0))
    cache = pl.BlockSpec((sb, WINDOW, LANES), lambda i: (i, 0, 0))
    return pl.pallas_call(
        functools.partial(_swa_sample_kernel, ld=ld),
        grid=(bd // sb,),
        in_specs=[tok(4 * LANES), tok(LANES), tok(LANES), cache, cache,
                  pl.BlockSpec(bias.shape, lambda i: (0, 0, 0)), pl.BlockSpec(sink_col.shape, lambda i: (0, 0, 0))],
        out_specs=[tok(4 * LANES), cache, cache],
        out_shape=[jax.ShapeDtypeStruct(q.shape, BF16), jax.ShapeDtypeStruct(cache_k.shape, F32),
                   jax.ShapeDtypeStruct(cache_v.shape, F32)],
        compiler_params=pltpu.CompilerParams(dimension_semantics=("parallel",)),
        name="swa_sample",
    )(q, k, v, cache_k, cache_v, bias, sink_col)


def _lower_bound(lb_logits):
    z = lb_logits - jnp.max(lb_logits, axis=0, keepdims=True)
    e = jnp.exp(z)
    return e[0:1, :] / jnp.sum(e, axis=0, keepdims=True)


def _gates(qr, fr, lb):
    f = lb + (1.0 - lb) * jax.nn.sigmoid(fr)
    return jax.nn.silu(qr), 1.0 - f, jnp.log(jnp.maximum(f, F_TINY))


def _head_out(o, gr, hg_gain, ones_seg):
    return _seg_rms(o, ones_seg, 1.0 / HG_DIM, hg_gain) * jax.nn.silu(gr)


def _hgrn_proj(u, win_ref, pair):
    cols = lambda off: slice(off + 2 * pair * HG_DIM, off + 2 * (pair + 1) * HG_DIM)
    qr = _dot(u, win_ref[:, cols(O_QR)])
    fr = _dot(u, win_ref[:, cols(O_FR)])
    yield
    ir = _dot(u, win_ref[:, cols(O_IR)]).astype(BF16)
    gr = _dot(u, win_ref[:, cols(O_GR)])
    yield
    return qr, fr, ir, gr


def _hgrn_head(qr, fr, vals, gr, lb, hd, s_ref, s_out_ref, hg_gain, ones, tri):
    qf, kk, g = _gates(qr, fr, lb)
    nchunk = qr.shape[0] // HG_CHUNK
    chunks = [slice(c * HG_CHUNK, (c + 1) * HG_CHUNK) for c in range(nchunk)]
    yield
    gcs = [_split2_ldot(tri, g[rows]) for rows in chunks]
    yield
    q_in, upd, decay, scores = [], [], [], []
    worst = jnp.zeros((1, HG_DIM), F32)
    for rows, gc in zip(chunks, gcs):
        g_last = gc[HG_CHUNK - 1:HG_CHUNK, :]
        q_in.append((qf[rows] * jnp.exp(gc)).astype(BF16))
        k_end_t = (kk[rows] * jnp.exp(g_last - gc)).T
        upd.append(_dot(k_end_t.astype(BF16), vals[rows]))
        decay.append(jnp.broadcast_to(jnp.exp(g_last), (HG_DIM, HG_DIM)).T)
        qc, kc = qf[rows], kk[rows]
        for i in range(HG_CHUNK // HG_SUB):
            lo, hi = i * HG_SUB, (i + 1) * HG_SUB
            g_base = gc[lo - 1:lo, :] if i else jnp.zeros((1, HG_DIM), F32)
            worst = jnp.minimum(worst, gc[hi - 1:hi, :] - g_base)
            q_hat = qc[lo:hi] * jnp.exp(gc[lo:hi] - g_base)
            k_hat = kc[:hi] * jnp.exp(jnp.minimum(g_base - gc[:hi], EXP_CLAMP))
            a = _dot_nt(q_hat.astype(BF16), k_hat.astype(BF16))
            row = lax.broadcasted_iota(jnp.int32, (HG_SUB, hi), 0)
            col = lax.broadcasted_iota(jnp.int32, (HG_SUB, hi), 1)
            scores.append(jnp.where(col <= row + lo, a, 0.0).astype(BF16))
    yield
    intra = []
    nsub = HG_CHUNK // HG_SUB
    for c, rows in enumerate(chunks):
        vc = vals[rows]
        parts = [_dot(scores[c * nsub + i], vc[:(i + 1) * HG_SUB]) for i in range(nsub)]
        intra.append(jnp.concatenate(parts, axis=0))
    yield
    s = s_ref[hd]
    inter = []
    for c in range(nchunk):
        inter.append(_dot(q_in[c], s.astype(BF16)))
        s = decay[c] * s + upd[c]
    s_ref[hd] = s
    s_out_ref[0, hd] = s
    yield
    o = jnp.concatenate([a + b for a, b in zip(inter, intra)], axis=0)
    redo = dict(qf=qf, kk=kk, gcs=gcs, vals=vals, inter=inter, gr=gr)
    return _head_out(o, gr, hg_gain, ones).astype(BF16), worst, redo


def _exact_intra(qc, kc, gc, vc):
    c = qc.shape[0]
    group = 8
    row = lax.broadcasted_iota(jnp.int32, (c, c), 0)
    col = lax.broadcasted_iota(jnp.int32, (c, c), 1)
    col_g = lax.broadcasted_iota(jnp.int32, (group, c), 1)
    earlier = []
    for i in range(c // group):
        lo, hi = i * group, (i + 1) * group
        g_base = gc[lo - 1:lo, :] if i else jnp.zeros((1, HG_DIM), F32)
        q_hat = qc[lo:hi] * jnp.exp(gc[lo:hi] - g_base)
        k_hat = kc * jnp.exp(jnp.minimum(g_base - gc, 0.0))
        earlier.append(jnp.where(col_g < lo, _dot_nt(q_hat.astype(BF16), k_hat.astype(BF16)), 0.0))
    a = jnp.concatenate(earlier, axis=0)
    qb = qc.astype(BF16)
    g3 = gc.reshape(c // group, group, HG_DIM)
    own = (col <= row) & (col >= row - lax.rem(row, group))
    for p in range(group):
        g_p = jnp.broadcast_to(g3[:, p:p + 1, :], g3.shape).reshape(c, HG_DIM)
        k_p = kc * jnp.exp(jnp.minimum(g_p - gc, 0.0))
        a = a + jnp.where(own & (lax.rem(row, group) == p), _dot_nt(qb, k_p.astype(BF16)), 0.0)
    return _dot(a.astype(BF16), vc)


def _hgrn_redo(redo, hg_gain, ones):
    outs = []
    for c, gc in enumerate(redo["gcs"]):
        rows = slice(c * HG_CHUNK, (c + 1) * HG_CHUNK)
        outs.append(redo["inter"][c] + _exact_intra(redo["qf"][rows], redo["kk"][rows], gc, redo["vals"][rows]))
    return _head_out(jnp.concatenate(outs, axis=0), redo["gr"], hg_gain, ones).astype(BF16)


def _attn_chain(q_pair, k_half, v_t, bias_ref, sink_a, sink_b, kv):
    s = _dot_nt(k_half, q_pair) + bias_ref[...]
    yield
    col = lax.broadcasted_iota(jnp.int32, (1, s.shape[1]), 1)
    sink = jnp.where(col < WINDOW, sink_a, sink_b)
    m = jnp.maximum(jnp.max(s, axis=0, keepdims=True), sink)
    e = jnp.exp(s - m)
    yield
    denom = jnp.sum(e, axis=0, keepdims=True) + jnp.exp(sink - m)
    o_t = _dot(v_t, e.astype(BF16))[kv * HEAD_DIM:(kv + 1) * HEAD_DIM]
    yield
    return o_t * (1.0 / denom)


def _gate_stream(u, win_ref, step=256):
    res = []
    for off in (O_GA, O_GB):
        cols = []
        for lo in range(0, SPLIT_SIZES[7], step):
            cols.append(jax.nn.sigmoid(_dot(u, win_ref[:, off + lo:off + lo + step])))
            yield
        res.append(jnp.concatenate(cols, axis=1))
    return res


def _q_cols(u, win_ref, kv, seg2, gain2):
    x = _dot(u, win_ref[:, O_Q + 2 * kv * LANES:O_Q + 2 * (kv + 1) * LANES])
    yield
    ms = _dot((x * x).astype(BF16), seg2) * (1.0 / HEAD_DIM)
    yield
    return (x * lax.rsqrt(ms + EPS) * gain2).astype(BF16)


def _kv_cols(u, win_ref, seg, gain):
    x = _dot(u, win_ref[:, O_K:O_QR])
    yield
    k, v = x[:, :LANES], x[:, LANES:]
    ms = _dot((k * k).astype(BF16), seg) * (1.0 / HEAD_DIM)
    yield
    return k * lax.rsqrt(ms + EPS) * gain, v


def _attn_up(parts, wua_ref):
    att = jnp.concatenate(parts, axis=0).T.astype(BF16)
    yield
    return _dot(att, wua_ref[...])


def _hgrn_sample_kernel(qr_ref, fr_ref, ir_ref, gr_ref, s0_ref, lb_ref, hg_ref, tri_ref, tot_ref, ones_ref,
                        o_ref, s_out_ref, *, ld):
    nseq = s0_ref.shape[0]
    r = nseq * ld
    lb = _lower_bound(lb_ref[...])
    qf_all, kk_all, g_all = _gates(qr_ref[...], fr_ref[...], lb)
    tri, tot = tri_ref[...], tot_ref[...]
    causal = tri.astype(F32) > 0
    lane = lax.broadcasted_iota(jnp.int32, (HG_DIM, r), 1)
    pos = lax.rem(lax.broadcasted_iota(jnp.int32, (r, r), 0), ld)

    def head(hd):
        lanes = slice(hd * HG_DIM, (hd + 1) * HG_DIM)
        qf, kk, v = qf_all[:, lanes], kk_all[:, lanes], ir_ref[:, lanes]
        g = _split2_ldot(tri, g_all[:, lanes])
        g_last = _split2_ldot(tot, g_all[:, lanes])
        yield
        q_t = (qf * jnp.exp(g)).astype(BF16)
        qb = qf.astype(BF16)
        g3 = g.reshape(nseq, ld, HG_DIM)
        a = jnp.zeros((r, r), F32)
        for p in range(ld):
            g_p = jnp.broadcast_to(g3[:, p:p + 1, :], g3.shape).reshape(r, HG_DIM)
            k_p = kk * jnp.exp(jnp.minimum(g_p - g, 0.0))
            a = a + jnp.where(causal & (pos == p), _dot_nt(qb, k_p.astype(BF16)), 0.0)
        k_end_t = (kk * jnp.exp(g_last - g)).T
        decay_t = jnp.exp(g_last).T
        yield
        intra = _dot(a.astype(BF16), v)
        outs = []
        for i in range(nseq):
            s_prev = s0_ref[i, hd]
            outs.append(_dot(q_t[i * ld:(i + 1) * ld], s_prev.astype(BF16)))
            own = (lane >= i * ld) & (lane < (i + 1) * ld)
            upd = _dot(jnp.where(own, k_end_t, 0.0).astype(BF16), v)
            s_out_ref[i, hd] = decay_t[:, i * ld:i * ld + 1] * s_prev + upd
            if i % 2:
                yield
        o = jnp.concatenate(outs, axis=0) + intra
        yield
        o_ref[:, lanes] = _head_out(o, gr_ref[:, lanes], hg_ref[...], ones_ref[...]).astype(o_ref.dtype)

    pool = _Pool()
    for hd in range(HG_HEADS):
        pool.add(hd, head(hd))
    while pool.live:
        pool.round()


def _hgrn_sample(qr, fr, ir, gr, s0, lb_logits, hg_gain, ld, sb):
    bd = s0.shape[0]
    r = sb * ld
    seq = np.arange(r) // ld
    same = seq[:, None] == seq[None, :]
    tri = jnp.asarray((same & (np.arange(r)[:, None] >= np.arange(r)[None, :])).astype(np.float32), BF16)
    tot = jnp.asarray(same.astype(np.float32), BF16)
    ones = jnp.ones((HG_DIM, HG_DIM), BF16)
    tok = pl.BlockSpec((r, HG_HEADS * HG_DIM), lambda i: (i, 0))
    state = pl.BlockSpec((sb, HG_HEADS, HG_DIM, HG_DIM), lambda i: (i, 0, 0, 0))
    const = lambda a: pl.BlockSpec(a.shape, lambda i: (0,) * a.ndim)
    return pl.pallas_call(
        functools.partial(_hgrn_sample_kernel, ld=ld),
        grid=(bd // sb,),
        in_specs=[tok, tok, tok, tok, state, const(lb_logits), const(hg_gain), const(tri), const(tot), const(ones)],
        out_specs=[tok, state],
        out_shape=[jax.ShapeDtypeStruct(qr.shape, BF16), jax.ShapeDtypeStruct(s0.shape, F32)],
        compiler_params=pltpu.CompilerParams(dimension_semantics=("parallel",)),
        name="hgrn_sample",
    )(qr, fr, ir, gr, s0, lb_logits, hg_gain, tri, tot, ones)


def _merge_out(h, sga, sgb, att, orr, wua_ref, wur_ref, wo_ref):
    merged = sga * _dot(att, wua_ref[...]) + sgb * _dot(orr, wur_ref[...])
    return h + _dot(merged.astype(BF16), wo_ref[...])


def _mix_kernel(h_ref, gm_ref, win_ref, qg_ref, kg_ref, seg_ref, seg2_ref, bias_ref, sink_ref, lb_ref, hg_ref,
                tri_ref, ones_ref, wua_ref, wur_ref, wo_ref,
                h2_ref, k_out_ref, v_out_ref, s_out_ref, kprev_ref, vprev_ref, s_ref):
    tm = h_ref.shape[0]
    first = pl.program_id(1) == 0

    @pl.when(first)
    def _():
        kprev_ref[...] = jnp.zeros(kprev_ref.shape, F32)
        vprev_ref[...] = jnp.zeros(vprev_ref.shape, F32)
        s_ref[...] = jnp.zeros(s_ref.shape, F32)

    h = h_ref[...]
    u = _rms_rows(h, gm_ref[...]).astype(BF16)

    pool = _Pool()
    lb = _lower_bound(lb_ref[...])
    for kv in range(N_KV_HEADS):
        pool.add(("q", kv), _q_cols(u, win_ref, kv, seg2_ref[...], qg_ref[...]))
    pool.add("kv", _kv_cols(u, win_ref, seg_ref[...], kg_ref[...]))
    for pair in range(HG_HEADS // 2):
        pool.add(("hproj", pair), _hgrn_proj(u, win_ref, pair))
    pool.add("gates", _gate_stream(u, win_ref))
    first_keys = [("q", kv) for kv in range(N_KV_HEADS)] + ["kv"] + [("hproj", p) for p in range(HG_HEADS // 2)]
    while not pool.has(first_keys):
        pool.round()
    for hd in range(HG_HEADS):
        qr, fr, ir, gr = (a[:, (hd % 2) * HG_DIM:(hd % 2 + 1) * HG_DIM] for a in pool.out[("hproj", hd // 2)])
        pool.add(("hgrn", hd), _hgrn_head(qr, fr, ir, gr, lb[:, hd * HG_DIM:(hd + 1) * HG_DIM], hd, s_ref, s_out_ref,
                                          hg_ref[...], ones_ref[...], tri_ref[...]))

    q = [pool.out[("q", j // 2)][:, (j % 2) * LANES:(j % 2 + 1) * LANES] for j in range(SPLIT_SIZES[0] // LANES)]
    k, v = pool.out["kv"]
    kk_all = jnp.concatenate([kprev_ref[...], k], axis=0)
    vv_all = jnp.concatenate([vprev_ref[...], v], axis=0)
    k_tail, v_tail = k[tm - WINDOW:], v[tm - WINDOW:]
    kprev_ref[...] = k_tail
    vprev_ref[...] = v_tail
    k_out_ref[...] = k_tail
    v_out_ref[...] = v_tail
    lane = lax.broadcasted_iota(jnp.int32, kk_all.shape, 1)
    k_lo0 = jnp.where(lane < HEAD_DIM, kk_all, 0.0)
    k_hi1 = jnp.where(lane >= HEAD_DIM, kk_all, 0.0)
    k_halves = ((k_lo0.astype(BF16), pltpu.roll(k_lo0, HEAD_DIM, 1).astype(BF16)),
                (pltpu.roll(k_hi1, HEAD_DIM, 1).astype(BF16), k_hi1.astype(BF16)))
    v_t = vv_all.T.astype(BF16)

    n_blk = tm // WINDOW
    chain_keys = lambda blk: [("att", blk, kv, half) for kv in range(N_KV_HEADS) for half in range(2)]
    for blk in range(n_blk):
        lo = blk * WINDOW
        variant = jnp.where(first, 0, 1) if blk == 0 else 1
        for kv in range(N_KV_HEADS):
            q_pair = jnp.concatenate([q[2 * kv + j][lo:lo + WINDOW] for j in range(GROUP // 2)], axis=0)
            for half in range(2):
                pool.add(("att", blk, kv, half),
                         _attn_chain(q_pair, k_halves[kv][half][lo:lo + 2 * WINDOW], v_t[:, lo:lo + 2 * WINDOW],
                                     bias_ref.at[variant, kv, half], sink_ref[GROUP * kv + half],
                                     sink_ref[GROUP * kv + 2 + half], kv))

    waiting = list(range(n_blk))
    while pool.live:
        pool.round()
        for blk in [b for b in waiting if pool.has(chain_keys(b))]:
            waiting.remove(blk)
            parts = []
            for hq in range(N_Q_HEADS):
                kv, j, half = hq // GROUP, (hq % GROUP) // 2, hq % 2
                parts.append(pool.out[("att", blk, kv, half)][:, j * WINDOW:(j + 1) * WINDOW])
            pool.add(("up", blk), _attn_up(parts, wua_ref))

    up_a = jnp.concatenate([pool.out[("up", blk)] for blk in range(n_blk)], axis=0)
    sga, sgb = pool.out["gates"]
    heads = [pool.out[("hgrn", hd)] for hd in range(HG_HEADS)]

    def finish(orr_heads):
        orr = jnp.concatenate(orr_heads, axis=1)
        merged = sga * up_a + sgb * _dot(orr, wur_ref[...])
        h2_ref[...] = h + _dot(merged.astype(BF16), wo_ref[...])

    finish([o for o, _, _ in heads])

    worst = functools.reduce(jnp.minimum, [w for _, w, _ in heads])

    @pl.when(jnp.min(worst) < -EXP_CLAMP)
    def _():
        finish([_hgrn_redo(redo, hg_ref[...], ones_ref[...]) for _, _, redo in heads])


def _mix(h, consts, batch, seq, tm):
    t, d = h.shape
    nt = seq // tm
    row = pl.BlockSpec((tm, d), lambda b, i: (b * nt + i, 0))
    win_out = pl.BlockSpec((WINDOW, LANES), lambda b, i: (b, 0))
    return pl.pallas_call(
        _mix_kernel,
        grid=(batch, nt),
        in_specs=[row] + [pl.BlockSpec(memory_space=pltpu.SMEM) if a.ndim == 1 else _const_spec(a) for a in consts],
        out_specs=[row, win_out, win_out, pl.BlockSpec((1, HG_HEADS, HG_DIM, HG_DIM), lambda b, i: (b, 0, 0, 0))],
        out_shape=[jax.ShapeDtypeStruct((t, d), F32),
                   jax.ShapeDtypeStruct((batch * WINDOW, LANES), F32),
                   jax.ShapeDtypeStruct((batch * WINDOW, LANES), F32),
                   jax.ShapeDtypeStruct((batch, HG_HEADS, HG_DIM, HG_DIM), F32)],
        scratch_shapes=[pltpu.VMEM((WINDOW, LANES), F32), pltpu.VMEM((WINDOW, LANES), F32),
                        pltpu.VMEM((HG_HEADS, HG_DIM, HG_DIM), F32)],
        compiler_params=pltpu.CompilerParams(dimension_semantics=("parallel", "arbitrary"),
                                             vmem_limit_bytes=VMEM_LIMIT),
        name="mix",
    )(h, *consts)


def _proj_kernel(h_ref, gm_ref, win_ref, qg_ref, kg_ref, seg_ref,
                 q_ref, k_ref, v_ref, qr_ref, fr_ref, ir_ref, gr_ref):
    u = _rms_rows(h_ref[...], gm_ref[...]).astype(BF16)
    q, k, v = _project_qkv(u, win_ref, qg_ref[...], kg_ref[...], seg_ref[...])
    q_ref[...] = q
    k_ref[...] = k
    v_ref[...] = v
    qr_ref[...] = _dot(u, win_ref[:, O_QR:O_FR])
    fr_ref[...] = _dot(u, win_ref[:, O_FR:O_IR])
    ir_ref[...] = _dot(u, win_ref[:, O_IR:O_GR]).astype(ir_ref.dtype)
    gr_ref[...] = _dot(u, win_ref[:, O_GR:O_GA])


def _proj(h, gm, w_in, qg, kg, seg, tm):
    t, d = h.shape
    row = lambda w: pl.BlockSpec((tm, w), lambda i: (i, 0))
    widths = SPLIT_SIZES[:7]
    dtypes = (BF16, F32, F32, F32, F32, BF16, F32)
    return pl.pallas_call(
        _proj_kernel,
        grid=(t // tm,),
        in_specs=[row(d)] + [_const_spec(a) for a in (gm, w_in, qg, kg, seg)],
        out_specs=[row(w) for w in widths],
        out_shape=[jax.ShapeDtypeStruct((t, w), dt) for w, dt in zip(widths, dtypes)],
        compiler_params=pltpu.CompilerParams(dimension_semantics=("parallel",), vmem_limit_bytes=VMEM_LIMIT),
        name="proj",
    )(h, gm, w_in, qg, kg, seg)


def _out_kernel(h_ref, att_ref, orr_ref, gm_ref, win_ref, wua_ref, wur_ref, wo_ref, h2_ref):
    h = h_ref[...]
    u = _rms_rows(h, gm_ref[...]).astype(BF16)
    sga = jax.nn.sigmoid(_dot(u, win_ref[:, O_GA:O_GB]))
    sgb = jax.nn.sigmoid(_dot(u, win_ref[:, O_GB:O_END]))
    h2_ref[...] = _merge_out(h, sga, sgb, att_ref[...], orr_ref[...], wua_ref, wur_ref, wo_ref)


def _out(h, att, orr, consts, tm):
    t, d = h.shape
    row = lambda a: pl.BlockSpec((tm, a.shape[1]), lambda i: (i, 0))
    return pl.pallas_call(
        _out_kernel,
        grid=(t // tm,),
        in_specs=[row(a) for a in (h, att, orr)] + [_const_spec(a) for a in consts],
        out_specs=pl.BlockSpec((tm, d), lambda i: (i, 0)),
        out_shape=jax.ShapeDtypeStruct((t, d), F32),
        compiler_params=pltpu.CompilerParams(dimension_semantics=("parallel",), vmem_limit_bytes=VMEM_LIMIT),
        name="out",
    )(h, att, orr, *consts)


def _bucket_maps(ld):
    dist_p = np.arange(WINDOW)[:, None] + WINDOW - np.arange(2 * WINDOW)[None, :]
    valid_p = (dist_p >= 0) & (dist_p <= WINDOW)
    bucket_p = np.where(valid_p, _t5_bucket(dist_p), -1).astype(np.int32)
    first = np.where(np.arange(2 * WINDOW)[None, :] >= WINDOW, bucket_p, -1).astype(np.int32)
    dist_s = np.arange(ld)[:, None] + WINDOW - np.arange(WINDOW + ld)[None, :]
    valid_s = (dist_s >= 0) & (dist_s <= WINDOW)
    bucket_s = np.where(valid_s, _t5_bucket(dist_s), -1).astype(np.int32)
    return first, bucket_p, bucket_s


def _stacked(bias, rows):
    return bias.reshape(N_KV_HEADS, GROUP * rows, bias.shape[-1])


def kernel(x_prompt, x_sample, cache_win_k, cache_win_v, state_hgrn, ffn1_norm, ffn1_w1, ffn1_w3, ffn1_w2, mix_norm, w_in, q_norm, k_norm, sinks, rel_bias_table, hgrn_lb_logits, hg_norm, w_up_attn, w_up_hgrn, w_out, ffn2_norm, ffn2_w1, ffn2_w3, ffn2_w2):
    depth = ffn1_norm.shape[0]
    assert depth == 1 and hgrn_lb_logits.shape[0] == 2, "single-layer step only"
    batch, seq, d = x_prompt.shape
    bd, ld, _ = x_sample.shape

    bf = lambda w: w[0].astype(BF16)
    row = lambda g: g[0].reshape(1, -1).astype(F32)
    w1a, w3a, w2a, w_in_b = bf(ffn1_w1), bf(ffn1_w3), bf(ffn1_w2), bf(w_in)
    w1b, w3b, w2b = bf(ffn2_w1), bf(ffn2_w3), bf(ffn2_w2)
    wua, wur, wo = bf(w_up_attn), bf(w_up_hgrn), bf(w_out)
    g1, gm, g2 = row(ffn1_norm), row(mix_norm), row(ffn2_norm)
    qg = jnp.tile(row(q_norm), (1, LANES // HEAD_DIM)) * (HEAD_DIM ** -0.5)
    kg = jnp.tile(row(k_norm), (1, LANES // HEAD_DIM))
    head_of_lane = np.arange(2 * LANES) // HEAD_DIM
    seg2 = jnp.asarray((head_of_lane[:, None] == head_of_lane[None, :]).astype(np.float32), BF16)
    seg = seg2[:LANES, :LANES]
    qg2 = jnp.tile(qg, (1, 2))
    ones = jnp.ones((HG_DIM, HG_DIM), BF16)
    tri = jnp.asarray(np.tril(np.ones((HG_CHUNK, HG_CHUNK), np.float32)), BF16)
    lb_logits = hgrn_lb_logits.astype(F32)
    hg_gain = row(hg_norm)

    first, later, bucket_s = _bucket_maps(ld)
    table = rel_bias_table.astype(F32)
    def paired(m):
        b = _rel_bias(table, np.ascontiguousarray(m.T))
        b = b.reshape(N_KV_HEADS, GROUP // 2, 2, 2 * WINDOW, WINDOW)
        return b.transpose(0, 2, 3, 1, 4).reshape(N_KV_HEADS, 2, 2 * WINDOW, 2 * WINDOW)
    bias_p = jnp.stack([paired(first), paired(later)])
    bias_s = _stacked(_rel_bias(table, bucket_s), ld)
    sink_1d = sinks[0].astype(F32)
    sink = sink_1d.reshape(N_KV_HEADS, GROUP, 1)
    sink_s = jnp.repeat(sink, ld, axis=1).reshape(N_KV_HEADS, GROUP * ld, 1)

    h = _ffn(x_prompt.reshape(batch * seq, d), g1, w1a, w3a, w2a, TM_FFN)
    consts = (gm, w_in_b, qg2, kg, seg, seg2, bias_p, sink_1d, lb_logits, hg_gain, tri, ones, wua, wur, wo)
    h2, k_p, v_p, s_p = _mix(h, consts, batch, seq, TM_MIX)
    y_p = _ffn(h2, g2, w1b, w3b, w2b, TM_FFN)
    win = lambda a: a.reshape(1, batch, WINDOW, N_KV_HEADS, HEAD_DIM)

    h = _ffn(x_sample.reshape(bd * ld, d), g1, w1a, w3a, w2a, TM_FFN)
    q, k, v, qr, fr, ir, gr = _proj(h, gm, w_in_b, qg, kg, seg, TM_SAMPLE)
    ck = cache_win_k[0].reshape(bd, WINDOW, LANES)
    cv = cache_win_v[0].reshape(bd, WINDOW, LANES)
    att, nk, nv = _swa_sample(q, k, v, ck, cv, bias_s, sink_s, ld, SEQS_PER_STEP)
    orr, s_s = _hgrn_sample(qr, fr, ir, gr, state_hgrn[0], lb_logits, hg_gain, ld, SEQS_PER_STEP)
    h2 = _out(h, att, orr, (gm, w_in_b, wua, wur, wo), TM_SAMPLE)
    y_s = _ffn(h2, g2, w1b, w3b, w2b, TM_FFN)
    unwin = lambda a: a.reshape(1, bd, WINDOW, N_KV_HEADS, HEAD_DIM)
    return (y_p.reshape(batch, seq, d), y_s.reshape(bd, ld, d), win(k_p), win(v_p), s_p[None],
            unwin(nk), unwin(nv), s_s[None])
```

```python
import functools

import numpy as np
import jax
import jax.numpy as jnp
from jax import lax
from jax.experimental import pallas as pl
from jax.experimental.pallas import tpu as pltpu

F32 = jnp.float32
BF16 = jnp.bfloat16

LANES = 128
HEAD_DIM = 64
N_Q_HEADS = 8
N_KV_HEADS = 2
GROUP = N_Q_HEADS // N_KV_HEADS
WINDOW = 128
N_BUCKETS = 32
MAX_DISTANCE = 128
HG_HEADS = 4
HG_DIM = 128
HG_CHUNK = 128
HG_SUB = 32
EPS = 1e-6
F_TINY = 1e-30
NEG_BIG = -1e30
EXP_CLAMP = 80.0
SPLIT_SIZES = (512, 128, 128, 512, 512, 512, 512, 1024, 1024)
SPLIT_OFFS = tuple(int(v) for v in np.cumsum((0,) + SPLIT_SIZES))
O_Q, O_K, O_V, O_QR, O_FR, O_IR, O_GR, O_GA, O_GB, O_END = SPLIT_OFFS
VMEM_LIMIT = 56 * 1024 * 1024
TM_FFN = 512
TM_MIX = 512
TM_SAMPLE = 256
SEQS_PER_STEP = 16


def _dot(a, b):
    return jnp.dot(a, b, preferred_element_type=F32)


def _dot_nt(a, b):
    return lax.dot_general(a, b, (((1,), (1,)), ((), ())), preferred_element_type=F32)


def _split2_ldot(w, x):
    hi = x.astype(BF16)
    lo = (x - hi.astype(F32)).astype(BF16)
    return _dot(w, hi) + _dot(w, lo)


def _rms_rows(x, g):
    ms = jnp.mean(x * x, axis=-1, keepdims=True)
    return x * lax.rsqrt(ms + EPS) * g


def _seg_rms(x, seg, inv_n, g):
    ms = _dot((x * x).astype(BF16), seg) * inv_n
    return x * lax.rsqrt(ms + EPS) * g


class _Pool:
    def __init__(self):
        self.live = []
        self.out = {}

    def add(self, key, gen):
        self.live.append((key, gen))

    def round(self):
        still = []
        for key, gen in self.live:
            try:
                next(gen)
                still.append((key, gen))
            except StopIteration as done:
                self.out[key] = done.value
        self.live = still

    def has(self, keys):
        return all(k in self.out for k in keys)


def _const_spec(a):
    nd = a.ndim
    return pl.BlockSpec(a.shape, lambda *_: (0,) * nd, pipeline_mode=pl.Buffered(1))


def _ff_chunks(d_ff, step=1024):
    return tuple((lo, min(lo + step, d_ff)) for lo in range(0, d_ff, step))


def _swiglu(xn, w1_ref, w3_ref, w2_ref):
    acc = jnp.zeros((xn.shape[0], w2_ref.shape[1]), F32)
    for lo, hi in _ff_chunks(w1_ref.shape[1]):
        a = _dot(xn, w1_ref[:, lo:hi])
        b = _dot(xn, w3_ref[:, lo:hi])
        acc = acc + _dot((jax.nn.silu(a) * b).astype(BF16), w2_ref[lo:hi, :])
    return acc


def _ffn_kernel(xa_ref, xb_ref, g1_ref, w1_ref, w3_ref, w2_ref, ya_ref, yb_ref, *, steps_a):
    from_a = pl.program_id(0) < steps_a
    x = jnp.where(from_a, xa_ref[...], xb_ref[...])
    xn = _rms_rows(x, g1_ref[...]).astype(BF16)
    y = x + 0.5 * _swiglu(xn, w1_ref, w3_ref, w2_ref)

    @pl.when(from_a)
    def _():
        ya_ref[...] = y

    @pl.when(jnp.logical_not(from_a))
    def _():
        yb_ref[...] = y


def _ffn(xa, xb, g1, w1, w3, w2, tm):
    (ta, d), tb = xa.shape, xb.shape[0]
    steps_a, steps_b = ta // tm, tb // tm
    rows_a = pl.BlockSpec((tm, d), lambda i: (jnp.minimum(i, steps_a - 1), 0))
    rows_b = pl.BlockSpec((tm, d), lambda i: (jnp.maximum(i - steps_a, 0), 0))
    return pl.pallas_call(
        functools.partial(_ffn_kernel, steps_a=steps_a),
        grid=(steps_a + steps_b,),
        in_specs=[rows_a, rows_b] + [_const_spec(a) for a in (g1, w1, w3, w2)],
        out_specs=[rows_a, rows_b],
        out_shape=[jax.ShapeDtypeStruct((ta, d), F32), jax.ShapeDtypeStruct((tb, d), F32)],
        compiler_params=pltpu.CompilerParams(dimension_semantics=("arbitrary",), vmem_limit_bytes=VMEM_LIMIT),
        name="ffn",
    )(xa, xb, g1, w1, w3, w2)


def _t5_bucket(dist):
    max_exact = N_BUCKETS // 2
    d = np.maximum(dist, 0)
    large = max_exact + (np.log(np.maximum(d, 1) / max_exact) / np.log(MAX_DISTANCE / max_exact)
                         * (N_BUCKETS - max_exact)).astype(np.int32)
    large = np.minimum(large, N_BUCKETS - 1)
    return np.where(d < max_exact, d, large).astype(np.int32)


def _bias_kernel(table_ref, bucket_ref, out_ref):
    bucket = bucket_ref[...]
    masked = jnp.where(bucket < 0, NEG_BIG, 0.0).astype(F32)
    for h in range(N_Q_HEADS):
        acc = masked
        for b in range(N_BUCKETS):
            acc = acc + jnp.where(bucket == b, table_ref[b, h], 0.0)
        out_ref[h] = acc


def _rel_bias(table, bucket_map):
    r, c = bucket_map.shape
    return pl.pallas_call(
        _bias_kernel,
        in_specs=[pl.BlockSpec(memory_space=pltpu.SMEM), pl.BlockSpec((r, c), lambda: (0, 0))],
        out_specs=pl.BlockSpec((N_Q_HEADS, r, c), lambda: (0, 0, 0)),
        out_shape=jax.ShapeDtypeStruct((N_Q_HEADS, r, c), F32),
        name="rel_bias",
    )(table, jnp.asarray(bucket_map))


def _project_qkv(u, win_ref, qg, kg, seg):
    qs = []
    for s in range(SPLIT_SIZES[0] // LANES):
        x = _dot(u, win_ref[:, O_Q + s * LANES:O_Q + (s + 1) * LANES])
        qs.append(_seg_rms(x, seg, 1.0 / HEAD_DIM, qg).astype(BF16))
    k = _seg_rms(_dot(u, win_ref[:, O_K:O_V]), seg, 1.0 / HEAD_DIM, kg)
    v = _dot(u, win_ref[:, O_V:O_QR])
    return jnp.concatenate(qs, axis=1), k, v


def _stack_group(q, kv):
    a = q[:, (2 * kv) * LANES:(2 * kv + 1) * LANES]
    b = q[:, (2 * kv + 1) * LANES:(2 * kv + 2) * LANES]
    ar, br = pltpu.roll(a, HEAD_DIM, 1), pltpu.roll(b, HEAD_DIM, 1)
    parts = (a, ar, b, br) if kv == 0 else (ar, a, br, b)
    return jnp.concatenate(parts, axis=0)


def _kv_half(x, kv, duplicate):
    lane = lax.broadcasted_iota(jnp.int32, x.shape, 1)
    keep = (lane < HEAD_DIM) if kv == 0 else (lane >= HEAD_DIM)
    xm = jnp.where(keep, x, 0.0)
    return xm + pltpu.roll(xm, HEAD_DIM, 1) if duplicate else xm


def _sink_softmax(s, sink):
    m = jnp.maximum(jnp.max(s, axis=-1, keepdims=True), sink)
    e = jnp.exp(s - m)
    return e / (jnp.sum(e, axis=-1, keepdims=True) + jnp.exp(sink - m))


def _attend_chain(q, kk, vv, bias_ref, sink_ref, r, kv):
    qs = _stack_group(q.astype(F32), kv).astype(BF16)
    kh = _kv_half(kk, kv, False).astype(BF16)
    s = _dot_nt(qs, kh) + bias_ref[kv]
    yield
    p = _sink_softmax(s, sink_ref[kv])
    yield
    o = _dot(p.astype(BF16), _kv_half(vv, kv, True).astype(BF16))
    yield
    lane = lax.broadcasted_iota(jnp.int32, (r, LANES), 1)
    return [jnp.where(lane < HEAD_DIM, o[(2 * pair) * r:(2 * pair + 1) * r], o[(2 * pair + 1) * r:(2 * pair + 2) * r])
            for pair in range(GROUP // 2)]


def _swa_sample_kernel(q_ref, kn_ref, vn_ref, ck_ref, cv_ref, bias_ref, sink_ref, o_ref, nk_ref, nv_ref, *, ld):
    pool = _Pool()
    nseq = ck_ref.shape[0]
    for i in range(nseq):
        rows = slice(i * ld, (i + 1) * ld)
        kk = jnp.concatenate([ck_ref[i], kn_ref[rows, :]], axis=0)
        vv = jnp.concatenate([cv_ref[i], vn_ref[rows, :]], axis=0)
        nk_ref[i] = kk[ld:, :]
        nv_ref[i] = vv[ld:, :]
        for kv in range(N_KV_HEADS):
            pool.add((i, kv), _attend_chain(q_ref[rows, :], kk, vv, bias_ref, sink_ref, ld, kv))
    while pool.live:
        pool.round()
    for i in range(nseq):
        slabs = [s for kv in range(N_KV_HEADS) for s in pool.out[(i, kv)]]
        o_ref[i * ld:(i + 1) * ld, :] = jnp.concatenate(slabs, axis=1).astype(o_ref.dtype)


def _swa_sample(q, k, v, cache_k, cache_v, bias, sink_col, ld, sb):
    bd = cache_k.shape[0]
    tok = lambda w: pl.BlockSpec((sb * ld, w), lambda i: (i, 0))
    cache = pl.BlockSpec((sb, WINDOW, LANES), lambda i: (i, 0, 0))
    return pl.pallas_call(
        functools.partial(_swa_sample_kernel, ld=ld),
        grid=(bd // sb,),
        in_specs=[tok(4 * LANES), tok(LANES), tok(LANES), cache, cache,
                  pl.BlockSpec(bias.shape, lambda i: (0, 0, 0)), pl.BlockSpec(sink_col.shape, lambda i: (0, 0, 0))],
        out_specs=[tok(4 * LANES), cache, cache],
        out_shape=[jax.ShapeDtypeStruct(q.shape, BF16), jax.ShapeDtypeStruct(cache_k.shape, F32),
                   jax.ShapeDtypeStruct(cache_v.shape, F32)],
        compiler_params=pltpu.CompilerParams(dimension_semantics=("parallel",)),
        name="swa_sample",
    )(q, k, v, cache_k, cache_v, bias, sink_col)


def _lower_bound(lb_logits):
    z = lb_logits - jnp.max(lb_logits, axis=0, keepdims=True)
    e = jnp.exp(z)
    return e[0:1, :] / jnp.sum(e, axis=0, keepdims=True)


def _gates(qr, fr, lb):
    f = lb + (1.0 - lb) * jax.nn.sigmoid(fr)
    return jax.nn.silu(qr), 1.0 - f, jnp.log(jnp.maximum(f, F_TINY))


def _head_out(o, gr, hg_gain, ones_seg):
    return _seg_rms(o, ones_seg, 1.0 / HG_DIM, hg_gain) * jax.nn.silu(gr)


def _hgrn_proj(u, win_ref, pair):
    cols = lambda off: slice(off + 2 * pair * HG_DIM, off + 2 * (pair + 1) * HG_DIM)
    qr = _dot(u, win_ref[:, cols(O_QR)])
    fr = _dot(u, win_ref[:, cols(O_FR)])
    yield
    ir = _dot(u, win_ref[:, cols(O_IR)]).astype(BF16)
    gr = _dot(u, win_ref[:, cols(O_GR)])
    yield
    return qr, fr, ir, gr


def _hgrn_head(qr, fr, vals, gr, lb, hd, s_ref, s_out_ref, hg_gain, ones, tri):
    qf, kk, g = _gates(qr, fr, lb)
    nchunk = qr.shape[0] // HG_CHUNK
    chunks = [slice(c * HG_CHUNK, (c + 1) * HG_CHUNK) for c in range(nchunk)]
    yield
    gcs = [_split2_ldot(tri, g[rows]) for rows in chunks]
    yield
    q_in, upd, decay, scores = [], [], [], []
    worst = jnp.zeros((1, HG_DIM), F32)
    for rows, gc in zip(chunks, gcs):
        g_last = gc[HG_CHUNK - 1:HG_CHUNK, :]
        q_in.append((qf[rows] * jnp.exp(gc)).astype(BF16))
        k_end_t = (kk[rows] * jnp.exp(g_last - gc)).T
        upd.append(_dot(k_end_t.astype(BF16), vals[rows]))
        decay.append(jnp.broadcast_to(jnp.exp(g_last), (HG_DIM, HG_DIM)).T)
        qc, kc = qf[rows], kk[rows]
        for i in range(HG_CHUNK // HG_SUB):
            lo, hi = i * HG_SUB, (i + 1) * HG_SUB
            g_base = gc[lo - 1:lo, :] if i else jnp.zeros((1, HG_DIM), F32)
            worst = jnp.minimum(worst, gc[hi - 1:hi, :] - g_base)
            q_hat = qc[lo:hi] * jnp.exp(gc[lo:hi] - g_base)
            k_hat = kc[:hi] * jnp.exp(jnp.minimum(g_base - gc[:hi], EXP_CLAMP))
            a = _dot_nt(q_hat.astype(BF16), k_hat.astype(BF16))
            row = lax.broadcasted_iota(jnp.int32, (HG_SUB, hi), 0)
            col = lax.broadcasted_iota(jnp.int32, (HG_SUB, hi), 1)
            scores.append(jnp.where(col <= row + lo, a, 0.0).astype(BF16))
    yield
    intra = []
    nsub = HG_CHUNK // HG_SUB
    for c, rows in enumerate(chunks):
        vc = vals[rows]
        parts = [_dot(scores[c * nsub + i], vc[:(i + 1) * HG_SUB]) for i in range(nsub)]
        intra.append(jnp.concatenate(parts, axis=0))
    yield
    s = s_ref[hd]
    inter = []
    for c in range(nchunk):
        inter.append(_dot(q_in[c], s.astype(BF16)))
        s = decay[c] * s + upd[c]
    s_ref[hd] = s
    s_out_ref[0, hd] = s
    yield
    o = jnp.concatenate([a + b for a, b in zip(inter, intra)], axis=0)
    redo = dict(qf=qf, kk=kk, gcs=gcs, vals=vals, inter=inter, gr=gr)
    return _head_out(o, gr, hg_gain, ones).astype(BF16), worst, redo


def _exact_intra(qc, kc, gc, vc):
    c = qc.shape[0]
    group = 8
    row = lax.broadcasted_iota(jnp.int32, (c, c), 0)
    col = lax.broadcasted_iota(jnp.int32, (c, c), 1)
    col_g = lax.broadcasted_iota(jnp.int32, (group, c), 1)
    earlier = []
    for i in range(c // group):
        lo, hi = i * group, (i + 1) * group
        g_base = gc[lo - 1:lo, :] if i else jnp.zeros((1, HG_DIM), F32)
        q_hat = qc[lo:hi] * jnp.exp(gc[lo:hi] - g_base)
        k_hat = kc * jnp.exp(jnp.minimum(g_base - gc, 0.0))
        earlier.append(jnp.where(col_g < lo, _dot_nt(q_hat.astype(BF16), k_hat.astype(BF16)), 0.0))
    a = jnp.concatenate(earlier, axis=0)
    qb = qc.astype(BF16)
    g3 = gc.reshape(c // group, group, HG_DIM)
    own = (col <= row) & (col >= row - lax.rem(row, group))
    for p in range(group):
        g_p = jnp.broadcast_to(g3[:, p:p + 1, :], g3.shape).reshape(c, HG_DIM)
        k_p = kc * jnp.exp(jnp.minimum(g_p - gc, 0.0))
        a = a + jnp.where(own & (lax.rem(row, group) == p), _dot_nt(qb, k_p.astype(BF16)), 0.0)
    return _dot(a.astype(BF16), vc)


def _hgrn_redo(redo, hg_gain, ones):
    outs = []
    for c, gc in enumerate(redo["gcs"]):
        rows = slice(c * HG_CHUNK, (c + 1) * HG_CHUNK)
        outs.append(redo["inter"][c] + _exact_intra(redo["qf"][rows], redo["kk"][rows], gc, redo["vals"][rows]))
    return _head_out(jnp.concatenate(outs, axis=0), redo["gr"], hg_gain, ones).astype(BF16)


def _attn_chain(q_pair, k_half, v_t, bias_ref, sink_a, sink_b, kv):
    s = _dot_nt(k_half, q_pair) + bias_ref[...]
    yield
    col = lax.broadcasted_iota(jnp.int32, (1, s.shape[1]), 1)
    sink = jnp.where(col < WINDOW, sink_a, sink_b)
    m = jnp.maximum(jnp.max(s, axis=0, keepdims=True), sink)
    e = jnp.exp(s - m)
    yield
    denom = jnp.sum(e, axis=0, keepdims=True) + jnp.exp(sink - m)
    o_t = _dot(v_t, e.astype(BF16))[kv * HEAD_DIM:(kv + 1) * HEAD_DIM]
    yield
    return o_t * (1.0 / denom)


def _gate_stream(u, win_ref, step=256):
    res = []
    for off in (O_GA, O_GB):
        cols = []
        for lo in range(0, SPLIT_SIZES[7], step):
            cols.append(jax.nn.sigmoid(_dot(u, win_ref[:, off + lo:off + lo + step])))
            yield
        res.append(jnp.concatenate(cols, axis=1))
    return res


def _q_cols(u, win_ref, kv, seg2, gain2):
    x = _dot(u, win_ref[:, O_Q + 2 * kv * LANES:O_Q + 2 * (kv + 1) * LANES])
    yield
    ms = _dot((x * x).astype(BF16), seg2) * (1.0 / HEAD_DIM)
    yield
    return (x * lax.rsqrt(ms + EPS) * gain2).astype(BF16)


def _kv_cols(u, win_ref, seg, gain):
    x = _dot(u, win_ref[:, O_K:O_QR])
    yield
    k, v = x[:, :LANES], x[:, LANES:]
    ms = _dot((k * k).astype(BF16), seg) * (1.0 / HEAD_DIM)
    yield
    return k * lax.rsqrt(ms + EPS) * gain, v


def _attn_up(parts, wua_ref):
    att = jnp.concatenate(parts, axis=0).T.astype(BF16)
    yield
    return _dot(att, wua_ref[...])


def _hgrn_sample_kernel(qr_ref, fr_ref, ir_ref, gr_ref, s0_ref, lb_ref, hg_ref, tri_ref, tot_ref, ones_ref,
                        o_ref, s_out_ref, *, ld):
    nseq = s0_ref.shape[0]
    r = nseq * ld
    lb = _lower_bound(lb_ref[...])
    qf_all, kk_all, g_all = _gates(qr_ref[...], fr_ref[...], lb)
    tri, tot = tri_ref[...], tot_ref[...]
    causal = tri.astype(F32) > 0
    lane = lax.broadcasted_iota(jnp.int32, (HG_DIM, r), 1)
    pos = lax.rem(lax.broadcasted_iota(jnp.int32, (r, r), 0), ld)

    def head(hd):
        lanes = slice(hd * HG_DIM, (hd + 1) * HG_DIM)
        qf, kk, v = qf_all[:, lanes], kk_all[:, lanes], ir_ref[:, lanes]
        g = _split2_ldot(tri, g_all[:, lanes])
        g_last = _split2_ldot(tot, g_all[:, lanes])
        yield
        q_t = (qf * jnp.exp(g)).astype(BF16)
        qb = qf.astype(BF16)
        g3 = g.reshape(nseq, ld, HG_DIM)
        a = jnp.zeros((r, r), F32)
        for p in range(ld):
            g_p = jnp.broadcast_to(g3[:, p:p + 1, :], g3.shape).reshape(r, HG_DIM)
            k_p = kk * jnp.exp(jnp.minimum(g_p - g, 0.0))
            a = a + jnp.where(causal & (pos == p), _dot_nt(qb, k_p.astype(BF16)), 0.0)
        k_end_t = (kk * jnp.exp(g_last - g)).T
        decay_t = jnp.exp(g_last).T
        yield
        intra = _dot(a.astype(BF16), v)
        outs = []
        for i in range(nseq):
            s_prev = s0_ref[i, hd]
            outs.append(_dot(q_t[i * ld:(i + 1) * ld], s_prev.astype(BF16)))
            own = (lane >= i * ld) & (lane < (i + 1) * ld)
            upd = _dot(jnp.where(own, k_end_t, 0.0).astype(BF16), v)
            s_out_ref[i, hd] = decay_t[:, i * ld:i * ld + 1] * s_prev + upd
            if i % 2:
                yield
        o = jnp.concatenate(outs, axis=0) + intra
        yield
        o_ref[:, lanes] = _head_out(o, gr_ref[:, lanes], hg_ref[...], ones_ref[...]).astype(o_ref.dtype)

    pool = _Pool()
    for hd in range(HG_HEADS):
        pool.add(hd, head(hd))
    while pool.live:
        pool.round()


def _hgrn_sample(qr, fr, ir, gr, s0, lb_logits, hg_gain, ld, sb):
    bd = s0.shape[0]
    r = sb * ld
    seq = np.arange(r) // ld
    same = seq[:, None] == seq[None, :]
    tri = jnp.asarray((same & (np.arange(r)[:, None] >= np.arange(r)[None, :])).astype(np.float32), BF16)
    tot = jnp.asarray(same.astype(np.float32), BF16)
    ones = jnp.ones((HG_DIM, HG_DIM), BF16)
    tok = pl.BlockSpec((r, HG_HEADS * HG_DIM), lambda i: (i, 0))
    state = pl.BlockSpec((sb, HG_HEADS, HG_DIM, HG_DIM), lambda i: (i, 0, 0, 0))
    const = lambda a: pl.BlockSpec(a.shape, lambda i: (0,) * a.ndim)
    return pl.pallas_call(
        functools.partial(_hgrn_sample_kernel, ld=ld),
        grid=(bd // sb,),
        in_specs=[tok, tok, tok, tok, state, const(lb_logits), const(hg_gain), const(tri), const(tot), const(ones)],
        out_specs=[tok, state],
        out_shape=[jax.ShapeDtypeStruct(qr.shape, BF16), jax.ShapeDtypeStruct(s0.shape, F32)],
        compiler_params=pltpu.CompilerParams(dimension_semantics=("parallel",)),
        name="hgrn_sample",
    )(qr, fr, ir, gr, s0, lb_logits, hg_gain, tri, tot, ones)


def _merge_out(h, sga, sgb, att, orr, wua_ref, wur_ref, wo_ref):
    merged = sga * _dot(att, wua_ref[...]) + sgb * _dot(orr, wur_ref[...])
    return h + _dot(merged.astype(BF16), wo_ref[...])


def _mix_kernel(h_ref, gm_ref, win_ref, qg_ref, kg_ref, seg_ref, seg2_ref, bias_ref, sink_ref, lb_ref, hg_ref,
                tri_ref, ones_ref, wua_ref, wur_ref, wo_ref,
                h2_ref, k_out_ref, v_out_ref, s_out_ref, kprev_ref, vprev_ref, s_ref):
    tm = h_ref.shape[0]
    first = pl.program_id(1) == 0

    @pl.when(first)
    def _():
        kprev_ref[...] = jnp.zeros(kprev_ref.shape, F32)
        vprev_ref[...] = jnp.zeros(vprev_ref.shape, F32)
        s_ref[...] = jnp.zeros(s_ref.shape, F32)

    h = h_ref[...]
    u = _rms_rows(h, gm_ref[...]).astype(BF16)

    pool = _Pool()
    lb = _lower_bound(lb_ref[...])
    for kv in range(N_KV_HEADS):
        pool.add(("q", kv), _q_cols(u, win_ref, kv, seg2_ref[...], qg_ref[...]))
    pool.add("kv", _kv_cols(u, win_ref, seg_ref[...], kg_ref[...]))
    for pair in range(HG_HEADS // 2):
        pool.add(("hproj", pair), _hgrn_proj(u, win_ref, pair))
    first_keys = [("q", kv) for kv in range(N_KV_HEADS)] + ["kv"] + [("hproj", p) for p in range(HG_HEADS // 2)]
    while not pool.has(first_keys):
        pool.round()
    pool.add("gates", _gate_stream(u, win_ref))
    for hd in range(HG_HEADS):
        qr, fr, ir, gr = (a[:, (hd % 2) * HG_DIM:(hd % 2 + 1) * HG_DIM] for a in pool.out[("hproj", hd // 2)])
        pool.add(("hgrn", hd), _hgrn_head(qr, fr, ir, gr, lb[:, hd * HG_DIM:(hd + 1) * HG_DIM], hd, s_ref, s_out_ref,
                                          hg_ref[...], ones_ref[...], tri_ref[...]))

    q = [pool.out[("q", j // 2)][:, (j % 2) * LANES:(j % 2 + 1) * LANES] for j in range(SPLIT_SIZES[0] // LANES)]
    k, v = pool.out["kv"]
    kk_all = jnp.concatenate([kprev_ref[...], k], axis=0)
    vv_all = jnp.concatenate([vprev_ref[...], v], axis=0)
    k_tail, v_tail = k[tm - WINDOW:], v[tm - WINDOW:]
    kprev_ref[...] = k_tail
    vprev_ref[...] = v_tail
    k_out_ref[...] = k_tail
    v_out_ref[...] = v_tail
    lane = lax.broadcasted_iota(jnp.int32, kk_all.shape, 1)
    k_lo0 = jnp.where(lane < HEAD_DIM, kk_all, 0.0)
    k_hi1 = jnp.where(lane >= HEAD_DIM, kk_all, 0.0)
    k_halves = ((k_lo0.astype(BF16), pltpu.roll(k_lo0, HEAD_DIM, 1).astype(BF16)),
                (pltpu.roll(k_hi1, HEAD_DIM, 1).astype(BF16), k_hi1.astype(BF16)))
    v_t = vv_all.T.astype(BF16)

    n_blk = tm // WINDOW
    chain_keys = lambda blk: [("att", blk, kv, half) for kv in range(N_KV_HEADS) for half in range(2)]
    for blk in range(n_blk):
        lo = blk * WINDOW
        variant = jnp.where(first, 0, 1) if blk == 0 else 1
        for kv in range(N_KV_HEADS):
            q_pair = jnp.concatenate([q[2 * kv + j][lo:lo + WINDOW] for j in range(GROUP // 2)], axis=0)
            for half in range(2):
                pool.add(("att", blk, kv, half),
                         _attn_chain(q_pair, k_halves[kv][half][lo:lo + 2 * WINDOW], v_t[:, lo:lo + 2 * WINDOW],
                                     bias_ref.at[variant, kv, half], sink_ref[GROUP * kv + half],
                                     sink_ref[GROUP * kv + 2 + half], kv))

    waiting = list(range(n_blk))
    while pool.live:
        pool.round()
        for blk in [b for b in waiting if pool.has(chain_keys(b))]:
            waiting.remove(blk)
            parts = []
            for hq in range(N_Q_HEADS):
                kv, j, half = hq // GROUP, (hq % GROUP) // 2, hq % 2
                parts.append(pool.out[("att", blk, kv, half)][:, j * WINDOW:(j + 1) * WINDOW])
            pool.add(("up", blk), _attn_up(parts, wua_ref))

    up_a = jnp.concatenate([pool.out[("up", blk)] for blk in range(n_blk)], axis=0)
    sga, sgb = pool.out["gates"]
    heads = [pool.out[("hgrn", hd)] for hd in range(HG_HEADS)]

    def finish(orr_heads):
        orr = jnp.concatenate(orr_heads, axis=1)
        merged = sga * up_a + sgb * _dot(orr, wur_ref[...])
        h2_ref[...] = h + _dot(merged.astype(BF16), wo_ref[...])

    finish([o for o, _, _ in heads])

    worst = functools.reduce(jnp.minimum, [w for _, w, _ in heads])

    @pl.when(jnp.min(worst) < -EXP_CLAMP)
    def _():
        finish([_hgrn_redo(redo, hg_ref[...], ones_ref[...]) for _, _, redo in heads])


def _mix(h, consts, batch, seq, tm):
    t, d = h.shape
    nt = seq // tm
    row = pl.BlockSpec((tm, d), lambda b, i: (b * nt + i, 0))
    win_out = pl.BlockSpec((WINDOW, LANES), lambda b, i: (b, 0))
    return pl.pallas_call(
        _mix_kernel,
        grid=(batch, nt),
        in_specs=[row] + [pl.BlockSpec(memory_space=pltpu.SMEM) if a.ndim == 1 else _const_spec(a) for a in consts],
        out_specs=[row, win_out, win_out, pl.BlockSpec((1, HG_HEADS, HG_DIM, HG_DIM), lambda b, i: (b, 0, 0, 0))],
        out_shape=[jax.ShapeDtypeStruct((t, d), F32),
                   jax.ShapeDtypeStruct((batch * WINDOW, LANES), F32),
                   jax.ShapeDtypeStruct((batch * WINDOW, LANES), F32),
                   jax.ShapeDtypeStruct((batch, HG_HEADS, HG_DIM, HG_DIM), F32)],
        scratch_shapes=[pltpu.VMEM((WINDOW, LANES), F32), pltpu.VMEM((WINDOW, LANES), F32),
                        pltpu.VMEM((HG_HEADS, HG_DIM, HG_DIM), F32)],
        compiler_params=pltpu.CompilerParams(dimension_semantics=("parallel", "arbitrary"),
                                             vmem_limit_bytes=VMEM_LIMIT),
        name="mix",
    )(h, *consts)


def _proj_kernel(h_ref, gm_ref, win_ref, qg_ref, kg_ref, seg_ref,
                 q_ref, k_ref, v_ref, qr_ref, fr_ref, ir_ref, gr_ref):
    u = _rms_rows(h_ref[...], gm_ref[...]).astype(BF16)
    q, k, v = _project_qkv(u, win_ref, qg_ref[...], kg_ref[...], seg_ref[...])
    q_ref[...] = q
    k_ref[...] = k
    v_ref[...] = v
    qr_ref[...] = _dot(u, win_ref[:, O_QR:O_FR])
    fr_ref[...] = _dot(u, win_ref[:, O_FR:O_IR])
    ir_ref[...] = _dot(u, win_ref[:, O_IR:O_GR]).astype(ir_ref.dtype)
    gr_ref[...] = _dot(u, win_ref[:, O_GR:O_GA])


def _proj(h, gm, w_in, qg, kg, seg, tm):
    t, d = h.shape
    row = lambda w: pl.BlockSpec((tm, w), lambda i: (i, 0))
    widths = SPLIT_SIZES[:7]
    dtypes = (BF16, F32, F32, F32, F32, BF16, F32)
    return pl.pallas_call(
        _proj_kernel,
        grid=(t // tm,),
        in_specs=[row(d)] + [_const_spec(a) for a in (gm, w_in, qg, kg, seg)],
        out_specs=[row(w) for w in widths],
        out_shape=[jax.ShapeDtypeStruct((t, w), dt) for w, dt in zip(widths, dtypes)],
        compiler_params=pltpu.CompilerParams(dimension_semantics=("parallel",), vmem_limit_bytes=VMEM_LIMIT),
        name="proj",
    )(h, gm, w_in, qg, kg, seg)


def _out_kernel(h_ref, att_ref, orr_ref, gm_ref, win_ref, wua_ref, wur_ref, wo_ref, h2_ref):
    h = h_ref[...]
    u = _rms_rows(h, gm_ref[...]).astype(BF16)
    sga = jax.nn.sigmoid(_dot(u, win_ref[:, O_GA:O_GB]))
    sgb = jax.nn.sigmoid(_dot(u, win_ref[:, O_GB:O_END]))
    h2_ref[...] = _merge_out(h, sga, sgb, att_ref[...], orr_ref[...], wua_ref, wur_ref, wo_ref)


def _out(h, att, orr, consts, tm):
    t, d = h.shape
    row = lambda a: pl.BlockSpec((tm, a.shape[1]), lambda i: (i, 0))
    return pl.pallas_call(
        _out_kernel,
        grid=(t // tm,),
        in_specs=[row(a) for a in (h, att, orr)] + [_const_spec(a) for a in consts],
        out_specs=pl.BlockSpec((tm, d), lambda i: (i, 0)),
        out_shape=jax.ShapeDtypeStruct((t, d), F32),
        compiler_params=pltpu.CompilerParams(dimension_semantics=("parallel",), vmem_limit_bytes=VMEM_LIMIT),
        name="out",
    )(h, att, orr, *consts)


def _bucket_maps(ld):
    dist_p = np.arange(WINDOW)[:, None] + WINDOW - np.arange(2 * WINDOW)[None, :]
    valid_p = (dist_p >= 0) & (dist_p <= WINDOW)
    bucket_p = np.where(valid_p, _t5_bucket(dist_p), -1).astype(np.int32)
    first = np.where(np.arange(2 * WINDOW)[None, :] >= WINDOW, bucket_p, -1).astype(np.int32)
    dist_s = np.arange(ld)[:, None] + WINDOW - np.arange(WINDOW + ld)[None, :]
    valid_s = (dist_s >= 0) & (dist_s <= WINDOW)
    bucket_s = np.where(valid_s, _t5_bucket(dist_s), -1).astype(np.int32)
    return first, bucket_p, bucket_s


def _stacked(bias, rows):
    return bias.reshape(N_KV_HEADS, GROUP * rows, bias.shape[-1])


def kernel(x_prompt, x_sample, cache_win_k, cache_win_v, state_hgrn, ffn1_norm, ffn1_w1, ffn1_w3, ffn1_w2, mix_norm, w_in, q_norm, k_norm, sinks, rel_bias_table, hgrn_lb_logits, hg_norm, w_up_attn, w_up_hgrn, w_out, ffn2_norm, ffn2_w1, ffn2_w3, ffn2_w2):
    depth = ffn1_norm.shape[0]
    assert depth == 1 and hgrn_lb_logits.shape[0] == 2, "single-layer step only"
    batch, seq, d = x_prompt.shape
    bd, ld, _ = x_sample.shape

    bf = lambda w: w[0].astype(BF16)
    row = lambda g: g[0].reshape(1, -1).astype(F32)
    w1a, w3a, w2a, w_in_b = bf(ffn1_w1), bf(ffn1_w3), bf(ffn1_w2), bf(w_in)
    w1b, w3b, w2b = bf(ffn2_w1), bf(ffn2_w3), bf(ffn2_w2)
    wua, wur, wo = bf(w_up_attn), bf(w_up_hgrn), bf(w_out)
    g1, gm, g2 = row(ffn1_norm), row(mix_norm), row(ffn2_norm)
    qg = jnp.tile(row(q_norm), (1, LANES // HEAD_DIM)) * (HEAD_DIM ** -0.5)
    kg = jnp.tile(row(k_norm), (1, LANES // HEAD_DIM))
    head_of_lane = np.arange(2 * LANES) // HEAD_DIM
    seg2 = jnp.asarray((head_of_lane[:, None] == head_of_lane[None, :]).astype(np.float32), BF16)
    seg = seg2[:LANES, :LANES]
    qg2 = jnp.tile(qg, (1, 2))
    ones = jnp.ones((HG_DIM, HG_DIM), BF16)
    tri = jnp.asarray(np.tril(np.ones((HG_CHUNK, HG_CHUNK), np.float32)), BF16)
    lb_logits = hgrn_lb_logits.astype(F32)
    hg_gain = row(hg_norm)

    first, later, bucket_s = _bucket_maps(ld)
    table = rel_bias_table.astype(F32)
    def paired(m):
        b = _rel_bias(table, np.ascontiguousarray(m.T))
        b = b.reshape(N_KV_HEADS, GROUP // 2, 2, 2 * WINDOW, WINDOW)
        return b.transpose(0, 2, 3, 1, 4).reshape(N_KV_HEADS, 2, 2 * WINDOW, 2 * WINDOW)
    bias_p = jnp.stack([paired(first), paired(later)])
    bias_s = _stacked(_rel_bias(table, bucket_s), ld)
    sink_1d = sinks[0].astype(F32)
    sink = sink_1d.reshape(N_KV_HEADS, GROUP, 1)
    sink_s = jnp.repeat(sink, ld, axis=1).reshape(N_KV_HEADS, GROUP * ld, 1)

    h_p, h_s = _ffn(x_prompt.reshape(batch * seq, d), x_sample.reshape(bd * ld, d), g1, w1a, w3a, w2a, TM_FFN)

    consts = (gm, w_in_b, qg2, kg, seg, seg2, bias_p, sink_1d, lb_logits, hg_gain, tri, ones, wua, wur, wo)
    h2_p, k_p, v_p, s_p = _mix(h_p, consts, batch, seq, TM_MIX)
    win = lambda a: a.reshape(1, batch, WINDOW, N_KV_HEADS, HEAD_DIM)

    q, k, v, qr, fr, ir, gr = _proj(h_s, gm, w_in_b, qg, kg, seg, TM_SAMPLE)
    ck = cache_win_k[0].reshape(bd, WINDOW, LANES)
    cv = cache_win_v[0].reshape(bd, WINDOW, LANES)
    att, nk, nv = _swa_sample(q, k, v, ck, cv, bias_s, sink_s, ld, SEQS_PER_STEP)
    orr, s_s = _hgrn_sample(qr, fr, ir, gr, state_hgrn[0], lb_logits, hg_gain, ld, SEQS_PER_STEP)
    h2_s = _out(h_s, att, orr, (gm, w_in_b, wua, wur, wo), TM_SAMPLE)

    y_p, y_s = _ffn(h2_p, h2_s, g2, w1b, w3b, w2b, TM_FFN)
    unwin = lambda a: a.reshape(1, bd, WINDOW, N_KV_HEADS, HEAD_DIM)
    return (y_p.reshape(batch, seq, d), y_s.reshape(bd, ld, d), win(k_p), win(v_p), s_p[None],
            unwin(nk), unwin(nv), s_s[None])
```

```python
import functools

import numpy as np
import jax
import jax.numpy as jnp
from jax import lax
from jax.experimental import pallas as pl
from jax.experimental.pallas import tpu as pltpu

F32 = jnp.float32
BF16 = jnp.bfloat16

LANES = 128
HEAD_DIM = 64
N_Q_HEADS = 8
N_KV_HEADS = 2
GROUP = N_Q_HEADS // N_KV_HEADS
WINDOW = 128
N_BUCKETS = 32
MAX_DISTANCE = 128
HG_HEADS = 4
HG_DIM = 128
HG_CHUNK = 128
HG_SUB = 32
EPS = 1e-6
F_TINY = 1e-30
NEG_BIG = -1e30
EXP_CLAMP = 80.0
SPLIT_SIZES = (512, 128, 128, 512, 512, 512, 512, 1024, 1024)
SPLIT_OFFS = tuple(int(v) for v in np.cumsum((0,) + SPLIT_SIZES))
O_Q, O_K, O_V, O_QR, O_FR, O_IR, O_GR, O_GA, O_GB, O_END = SPLIT_OFFS
VMEM_LIMIT = 56 * 1024 * 1024
TM_FFN = 1024
TM_MIX = 512
TM_SAMPLE = 256
SEQS_PER_STEP = 16


def _dot(a, b):
    return jnp.dot(a, b, preferred_element_type=F32)


def _dot_nt(a, b):
    return lax.dot_general(a, b, (((1,), (1,)), ((), ())), preferred_element_type=F32)


def _split2_ldot(w, x):
    hi = x.astype(BF16)
    lo = (x - hi.astype(F32)).astype(BF16)
    return _dot(w, hi) + _dot(w, lo)


def _rms_rows(x, g):
    ms = jnp.mean(x * x, axis=-1, keepdims=True)
    return x * lax.rsqrt(ms + EPS) * g


def _seg_rms(x, seg, inv_n, g):
    ms = _dot((x * x).astype(BF16), seg) * inv_n
    return x * lax.rsqrt(ms + EPS) * g


class _Pool:
    def __init__(self):
        self.live = []
        self.out = {}

    def add(self, key, gen):
        self.live.append((key, gen))

    def round(self):
        still = []
        for key, gen in self.live:
            try:
                next(gen)
                still.append((key, gen))
            except StopIteration as done:
                self.out[key] = done.value
        self.live = still

    def has(self, keys):
        return all(k in self.out for k in keys)


def _const_spec(a):
    nd = a.ndim
    return pl.BlockSpec(a.shape, lambda *_: (0,) * nd, pipeline_mode=pl.Buffered(1))


def _ff_chunks(d_ff, step=1024):
    return tuple((lo, min(lo + step, d_ff)) for lo in range(0, d_ff, step))


def _swiglu(xn, w1_ref, w3_ref, w2_ref):
    acc = jnp.zeros((xn.shape[0], w2_ref.shape[1]), F32)
    for lo, hi in _ff_chunks(w1_ref.shape[1]):
        a = _dot(xn, w1_ref[:, lo:hi])
        b = _dot(xn, w3_ref[:, lo:hi])
        acc = acc + _dot((jax.nn.silu(a) * b).astype(BF16), w2_ref[lo:hi, :])
    return acc


def _ffn_kernel(x_ref, g1_ref, w1_ref, w3_ref, w2_ref, h_ref):
    x = x_ref[...]
    xn = _rms_rows(x, g1_ref[...]).astype(BF16)
    h_ref[...] = x + 0.5 * _swiglu(xn, w1_ref, w3_ref, w2_ref)


def _ffn(x, g1, w1, w3, w2, tm):
    t, d = x.shape
    row = pl.BlockSpec((tm, d), lambda i: (i, 0))
    return pl.pallas_call(
        _ffn_kernel,
        grid=(t // tm,),
        in_specs=[row] + [_const_spec(a) for a in (g1, w1, w3, w2)],
        out_specs=row,
        out_shape=jax.ShapeDtypeStruct((t, d), F32),
        compiler_params=pltpu.CompilerParams(dimension_semantics=("parallel",), vmem_limit_bytes=VMEM_LIMIT),
        name="ffn",
    )(x, g1, w1, w3, w2)


def _t5_bucket(dist):
    max_exact = N_BUCKETS // 2
    d = np.maximum(dist, 0)
    large = max_exact + (np.log(np.maximum(d, 1) / max_exact) / np.log(MAX_DISTANCE / max_exact)
                         * (N_BUCKETS - max_exact)).astype(np.int32)
    large = np.minimum(large, N_BUCKETS - 1)
    return np.where(d < max_exact, d, large).astype(np.int32)


def _bias_kernel(table_ref, bucket_ref, out_ref):
    bucket = bucket_ref[...]
    masked = jnp.where(bucket < 0, NEG_BIG, 0.0).astype(F32)
    for h in range(N_Q_HEADS):
        acc = masked
        for b in range(N_BUCKETS):
            acc = acc + jnp.where(bucket == b, table_ref[b, h], 0.0)
        out_ref[h] = acc


def _rel_bias(table, bucket_map):
    r, c = bucket_map.shape
    return pl.pallas_call(
        _bias_kernel,
        in_specs=[pl.BlockSpec(memory_space=pltpu.SMEM), pl.BlockSpec((r, c), lambda: (0, 0))],
        out_specs=pl.BlockSpec((N_Q_HEADS, r, c), lambda: (0, 0, 0)),
        out_shape=jax.ShapeDtypeStruct((N_Q_HEADS, r, c), F32),
        name="rel_bias",
    )(table, jnp.asarray(bucket_map))


def _project_qkv(u, win_ref, qg, kg, seg):
    qs = []
    for s in range(SPLIT_SIZES[0] // LANES):
        x = _dot(u, win_ref[:, O_Q + s * LANES:O_Q + (s + 1) * LANES])
        qs.append(_seg_rms(x, seg, 1.0 / HEAD_DIM, qg).astype(BF16))
    k = _seg_rms(_dot(u, win_ref[:, O_K:O_V]), seg, 1.0 / HEAD_DIM, kg)
    v = _dot(u, win_ref[:, O_V:O_QR])
    return jnp.concatenate(qs, axis=1), k, v


def _stack_group(q, kv):
    a = q[:, (2 * kv) * LANES:(2 * kv + 1) * LANES]
    b = q[:, (2 * kv + 1) * LANES:(2 * kv + 2) * LANES]
    ar, br = pltpu.roll(a, HEAD_DIM, 1), pltpu.roll(b, HEAD_DIM, 1)
    parts = (a, ar, b, br) if kv == 0 else (ar, a, br, b)
    return jnp.concatenate(parts, axis=0)


def _kv_half(x, kv, fill):
    lane = lax.broadcasted_iota(jnp.int32, x.shape, 1)
    keep = (lane < HEAD_DIM) if kv == 0 else (lane >= HEAD_DIM)
    return jnp.where(keep, x, fill)


def _attend_chain(q, kk, vv, bias_ref, sink_ref, r, kv):
    qs = _stack_group(q.astype(F32), kv).astype(BF16)
    s = _dot_nt(qs, _kv_half(kk, kv, 0.0).astype(BF16)) + bias_ref[kv]
    yield
    sink = sink_ref[kv]
    m = jnp.maximum(jnp.max(s, axis=-1, keepdims=True), sink)
    e = jnp.exp(s - m).astype(BF16)
    yield
    o = _dot(e, _kv_half(vv, kv, 1.0).astype(BF16))
    yield
    o = o / (pltpu.roll(o, HEAD_DIM, 1) + jnp.exp(sink - m))
    lane = lax.broadcasted_iota(jnp.int32, (r, LANES), 1)
    slabs = []
    for pair in range(GROUP // 2):
        a, b = o[(2 * pair) * r:(2 * pair + 1) * r], o[(2 * pair + 1) * r:(2 * pair + 2) * r]
        if kv == 0:
            slabs.append(jnp.where(lane < HEAD_DIM, a, pltpu.roll(b, HEAD_DIM, 1)))
        else:
            slabs.append(jnp.where(lane < HEAD_DIM, pltpu.roll(a, HEAD_DIM, 1), b))
    return slabs


def _swa_sample_kernel(q_ref, kn_ref, vn_ref, ck_ref, cv_ref, bias_ref, sink_ref, o_ref, nk_ref, nv_ref, *, ld):
    pool = _Pool()
    nseq = ck_ref.shape[0]
    for i in range(nseq):
        rows = slice(i * ld, (i + 1) * ld)
        kk = jnp.concatenate([ck_ref[i], kn_ref[rows, :]], axis=0)
        vv = jnp.concatenate([cv_ref[i], vn_ref[rows, :]], axis=0)
        nk_ref[i] = kk[ld:, :]
        nv_ref[i] = vv[ld:, :]
        for kv in range(N_KV_HEADS):
            pool.add((i, kv), _attend_chain(q_ref[rows, :], kk, vv, bias_ref, sink_ref, ld, kv))
    while pool.live:
        pool.round()
    for i in range(nseq):
        slabs = [s for kv in range(N_KV_HEADS) for s in pool.out[(i, kv)]]
        o_ref[i * ld:(i + 1) * ld, :] = jnp.concatenate(slabs, axis=1).astype(o_ref.dtype)


def _swa_sample(q, k, v, cache_k, cache_v, bias, sink_col, ld, sb):
    bd = cache_k.shape[0]
    tok = lambda w: pl.BlockSpec((sb * ld, w), lambda i: (i, 0))
    cache = pl.BlockSpec((sb, WINDOW, LANES), lambda i: (i, 0, 0))
    return pl.pallas_call(
        functools.partial(_swa_sample_kernel, ld=ld),
        grid=(bd // sb,),
        in_specs=[tok(4 * LANES), tok(LANES), tok(LANES), cache, cache,
                  pl.BlockSpec(bias.shape, lambda i: (0, 0, 0)), pl.BlockSpec(sink_col.shape, lambda i: (0, 0, 0))],
        out_specs=[tok(4 * LANES), cache, cache],
        out_shape=[jax.ShapeDtypeStruct(q.shape, BF16), jax.ShapeDtypeStruct(cache_k.shape, F32),
                   jax.ShapeDtypeStruct(cache_v.shape, F32)],
        compiler_params=pltpu.CompilerParams(dimension_semantics=("parallel",)),
        name="swa_sample",
    )(q, k, v, cache_k, cache_v, bias, sink_col)


def _lower_bound(lb_logits):
    z = lb_logits - jnp.max(lb_logits, axis=0, keepdims=True)
    e = jnp.exp(z)
    return e[0:1, :] / jnp.sum(e, axis=0, keepdims=True)


def _gates(qr, fr, lb):
    f = lb + (1.0 - lb) * jax.nn.sigmoid(fr)
    return jax.nn.silu(qr), 1.0 - f, jnp.log(jnp.maximum(f, F_TINY))


def _head_out(o, gr, hg_gain, ones_seg):
    return _seg_rms(o, ones_seg, 1.0 / HG_DIM, hg_gain) * jax.nn.silu(gr)


def _hgrn_proj(u, win_ref, pair):
    cols = lambda off: slice(off + 2 * pair * HG_DIM, off + 2 * (pair + 1) * HG_DIM)
    qr = _dot(u, win_ref[:, cols(O_QR)])
    fr = _dot(u, win_ref[:, cols(O_FR)])
    yield
    ir = _dot(u, win_ref[:, cols(O_IR)]).astype(BF16)
    gr = _dot(u, win_ref[:, cols(O_GR)])
    yield
    return qr, fr, ir, gr


def _hgrn_head(qr, fr, vals, gr, lb, hd, s_ref, s_out_ref, hg_gain, ones, tri):
    qf, kk, g = _gates(qr, fr, lb)
    nchunk = qr.shape[0] // HG_CHUNK
    chunks = [slice(c * HG_CHUNK, (c + 1) * HG_CHUNK) for c in range(nchunk)]
    yield
    gcs = [_split2_ldot(tri, g[rows]) for rows in chunks]
    yield
    q_in, upd, decay, scores = [], [], [], []
    worst = jnp.zeros((1, HG_DIM), F32)
    for rows, gc in zip(chunks, gcs):
        g_last = gc[HG_CHUNK - 1:HG_CHUNK, :]
        q_in.append((qf[rows] * jnp.exp(gc)).astype(BF16))
        k_end_t = (kk[rows] * jnp.exp(g_last - gc)).T
        upd.append(_dot(k_end_t.astype(BF16), vals[rows]))
        decay.append(jnp.broadcast_to(jnp.exp(g_last), (HG_DIM, HG_DIM)).T)
        qc, kc = qf[rows], kk[rows]
        for i in range(HG_CHUNK // HG_SUB):
            lo, hi = i * HG_SUB, (i + 1) * HG_SUB
            g_base = gc[lo - 1:lo, :] if i else jnp.zeros((1, HG_DIM), F32)
            worst = jnp.minimum(worst, gc[hi - 1:hi, :] - g_base)
            q_hat = qc[lo:hi] * jnp.exp(gc[lo:hi] - g_base)
            k_hat = kc[:hi] * jnp.exp(jnp.minimum(g_base - gc[:hi], EXP_CLAMP))
            a = _dot_nt(q_hat.astype(BF16), k_hat.astype(BF16))
            row = lax.broadcasted_iota(jnp.int32, (HG_SUB, hi), 0)
            col = lax.broadcasted_iota(jnp.int32, (HG_SUB, hi), 1)
            scores.append(jnp.where(col <= row + lo, a, 0.0).astype(BF16))
    yield
    intra = []
    nsub = HG_CHUNK // HG_SUB
    for c, rows in enumerate(chunks):
        vc = vals[rows]
        parts = [_dot(scores[c * nsub + i], vc[:(i + 1) * HG_SUB]) for i in range(nsub)]
        intra.append(jnp.concatenate(parts, axis=0))
    yield
    s = s_ref[hd]
    inter = []
    for c in range(nchunk):
        inter.append(_dot(q_in[c], s.astype(BF16)))
        s = decay[c] * s + upd[c]
    s_ref[hd] = s
    s_out_ref[0, hd] = s
    yield
    o = jnp.concatenate([a + b for a, b in zip(inter, intra)], axis=0)
    redo = dict(qf=qf, kk=kk, gcs=gcs, vals=vals, inter=inter, gr=gr)
    return _head_out(o, gr, hg_gain, ones).astype(BF16), worst, redo


def _exact_intra(qc, kc, gc, vc):
    c = qc.shape[0]
    group = 8
    row = lax.broadcasted_iota(jnp.int32, (c, c), 0)
    col = lax.broadcasted_iota(jnp.int32, (c, c), 1)
    col_g = lax.broadcasted_iota(jnp.int32, (group, c), 1)
    earlier = []
    for i in range(c // group):
        lo, hi = i * group, (i + 1) * group
        g_base = gc[lo - 1:lo, :] if i else jnp.zeros((1, HG_DIM), F32)
        q_hat = qc[lo:hi] * jnp.exp(gc[lo:hi] - g_base)
        k_hat = kc * jnp.exp(jnp.minimum(g_base - gc, 0.0))
        earlier.append(jnp.where(col_g < lo, _dot_nt(q_hat.astype(BF16), k_hat.astype(BF16)), 0.0))
    a = jnp.concatenate(earlier, axis=0)
    qb = qc.astype(BF16)
    g3 = gc.reshape(c // group, group, HG_DIM)
    own = (col <= row) & (col >= row - lax.rem(row, group))
    for p in range(group):
        g_p = jnp.broadcast_to(g3[:, p:p + 1, :], g3.shape).reshape(c, HG_DIM)
        k_p = kc * jnp.exp(jnp.minimum(g_p - gc, 0.0))
        a = a + jnp.where(own & (lax.rem(row, group) == p), _dot_nt(qb, k_p.astype(BF16)), 0.0)
    return _dot(a.astype(BF16), vc)


def _hgrn_redo(redo, hg_gain, ones):
    outs = []
    for c, gc in enumerate(redo["gcs"]):
        rows = slice(c * HG_CHUNK, (c + 1) * HG_CHUNK)
        outs.append(redo["inter"][c] + _exact_intra(redo["qf"][rows], redo["kk"][rows], gc, redo["vals"][rows]))
    return _head_out(jnp.concatenate(outs, axis=0), redo["gr"], hg_gain, ones).astype(BF16)


def _attn_chain(q_pair, k_half, v_t, bias_ref, sink_a, sink_b, kv):
    s = _dot_nt(k_half, q_pair) + bias_ref[...]
    yield
    col = lax.broadcasted_iota(jnp.int32, (1, s.shape[1]), 1)
    sink = jnp.where(col < WINDOW, sink_a, sink_b)
    m = jnp.maximum(jnp.max(s, axis=0, keepdims=True), sink)
    e = jnp.exp(s - m)
    yield
    denom = jnp.sum(e, axis=0, keepdims=True) + jnp.exp(sink - m)
    o_t = _dot(v_t, e.astype(BF16))[kv * HEAD_DIM:(kv + 1) * HEAD_DIM]
    yield
    return o_t * (1.0 / denom)


def _gate_stream(u, win_ref, step=256):
    res = []
    for off in (O_GA, O_GB):
        cols = []
        for lo in range(0, SPLIT_SIZES[7], step):
            cols.append(jax.nn.sigmoid(_dot(u, win_ref[:, off + lo:off + lo + step])))
            yield
        res.append(jnp.concatenate(cols, axis=1))
    return res


def _q_cols(u, win_ref, kv, seg2, gain2):
    x = _dot(u, win_ref[:, O_Q + 2 * kv * LANES:O_Q + 2 * (kv + 1) * LANES])
    yield
    ms = _dot((x * x).astype(BF16), seg2) * (1.0 / HEAD_DIM)
    yield
    return (x * lax.rsqrt(ms + EPS) * gain2).astype(BF16)


def _kv_cols(u, win_ref, seg, gain):
    x = _dot(u, win_ref[:, O_K:O_QR])
    yield
    k, v = x[:, :LANES], x[:, LANES:]
    ms = _dot((k * k).astype(BF16), seg) * (1.0 / HEAD_DIM)
    yield
    return k * lax.rsqrt(ms + EPS) * gain, v


def _attn_up(parts, wua_ref):
    att = jnp.concatenate(parts, axis=0).T.astype(BF16)
    yield
    return _dot(att, wua_ref[...])


def _hgrn_sample_kernel(qr_ref, fr_ref, ir_ref, gr_ref, s0_ref, lb_ref, hg_ref, tri_ref, tot_ref, ones_ref,
                        o_ref, s_out_ref, *, ld):
    nseq = s0_ref.shape[0]
    r = nseq * ld
    lb = _lower_bound(lb_ref[...])
    qf_all, kk_all, g_all = _gates(qr_ref[...], fr_ref[...], lb)
    tri, tot = tri_ref[...], tot_ref[...]
    causal = tri.astype(F32) > 0
    lane = lax.broadcasted_iota(jnp.int32, (HG_DIM, r), 1)
    pos = lax.rem(lax.broadcasted_iota(jnp.int32, (r, r), 0), ld)

    def head(hd):
        lanes = slice(hd * HG_DIM, (hd + 1) * HG_DIM)
        qf, kk, v = qf_all[:, lanes], kk_all[:, lanes], ir_ref[:, lanes]
        g = _split2_ldot(tri, g_all[:, lanes])
        g_last = _split2_ldot(tot, g_all[:, lanes])
        yield
        q_t = (qf * jnp.exp(g)).astype(BF16)
        qb = qf.astype(BF16)
        g3 = g.reshape(nseq, ld, HG_DIM)
        a = jnp.zeros((r, r), F32)
        for p in range(ld):
            g_p = jnp.broadcast_to(g3[:, p:p + 1, :], g3.shape).reshape(r, HG_DIM)
            k_p = kk * jnp.exp(jnp.minimum(g_p - g, 0.0))
            a = a + jnp.where(causal & (pos == p), _dot_nt(qb, k_p.astype(BF16)), 0.0)
        k_end_t = (kk * jnp.exp(g_last - g)).T
        decay_t = jnp.exp(g_last).T
        yield
        intra = _dot(a.astype(BF16), v)
        outs = []
        for i in range(nseq):
            s_prev = s0_ref[i, hd]
            outs.append(_dot(q_t[i * ld:(i + 1) * ld], s_prev.astype(BF16)))
            own = (lane >= i * ld) & (lane < (i + 1) * ld)
            upd = _dot(jnp.where(own, k_end_t, 0.0).astype(BF16), v)
            s_out_ref[i, hd] = decay_t[:, i * ld:i * ld + 1] * s_prev + upd
            if i % 2:
                yield
        o = jnp.concatenate(outs, axis=0) + intra
        yield
        o_ref[:, lanes] = _head_out(o, gr_ref[:, lanes], hg_ref[...], ones_ref[...]).astype(o_ref.dtype)

    pool = _Pool()
    for hd in range(HG_HEADS):
        pool.add(hd, head(hd))
    while pool.live:
        pool.round()


def _hgrn_sample(qr, fr, ir, gr, s0, lb_logits, hg_gain, ld, sb):
    bd = s0.shape[0]
    r = sb * ld
    seq = np.arange(r) // ld
    same = seq[:, None] == seq[None, :]
    tri = jnp.asarray((same & (np.arange(r)[:, None] >= np.arange(r)[None, :])).astype(np.float32), BF16)
    tot = jnp.asarray(same.astype(np.float32), BF16)
    ones = jnp.ones((HG_DIM, HG_DIM), BF16)
    tok = pl.BlockSpec((r, HG_HEADS * HG_DIM), lambda i: (i, 0))
    state = pl.BlockSpec((sb, HG_HEADS, HG_DIM, HG_DIM), lambda i: (i, 0, 0, 0))
    const = lambda a: pl.BlockSpec(a.shape, lambda i: (0,) * a.ndim)
    return pl.pallas_call(
        functools.partial(_hgrn_sample_kernel, ld=ld),
        grid=(bd // sb,),
        in_specs=[tok, tok, tok, tok, state, const(lb_logits), const(hg_gain), const(tri), const(tot), const(ones)],
        out_specs=[tok, state],
        out_shape=[jax.ShapeDtypeStruct(qr.shape, BF16), jax.ShapeDtypeStruct(s0.shape, F32)],
        compiler_params=pltpu.CompilerParams(dimension_semantics=("parallel",)),
        name="hgrn_sample",
    )(qr, fr, ir, gr, s0, lb_logits, hg_gain, tri, tot, ones)


def _merge_out(h, sga, sgb, att, orr, wua_ref, wur_ref, wo_ref):
    merged = sga * _dot(att, wua_ref[...]) + sgb * _dot(orr, wur_ref[...])
    return h + _dot(merged.astype(BF16), wo_ref[...])


def _mix_kernel(h_ref, gm_ref, win_ref, qg_ref, kg_ref, seg_ref, seg2_ref, bias_ref, sink_ref, lb_ref, hg_ref,
                tri_ref, ones_ref, wua_ref, wur_ref, wo_ref,
                h2_ref, k_out_ref, v_out_ref, s_out_ref, kprev_ref, vprev_ref, s_ref):
    tm = h_ref.shape[0]
    first = pl.program_id(1) == 0

    @pl.when(first)
    def _():
        kprev_ref[...] = jnp.zeros(kprev_ref.shape, F32)
        vprev_ref[...] = jnp.zeros(vprev_ref.shape, F32)
        s_ref[...] = jnp.zeros(s_ref.shape, F32)

    h = h_ref[...]
    u = _rms_rows(h, gm_ref[...]).astype(BF16)

    pool = _Pool()
    lb = _lower_bound(lb_ref[...])
    for kv in range(N_KV_HEADS):
        pool.add(("q", kv), _q_cols(u, win_ref, kv, seg2_ref[...], qg_ref[...]))
    pool.add("kv", _kv_cols(u, win_ref, seg_ref[...], kg_ref[...]))
    for pair in range(HG_HEADS // 2):
        pool.add(("hproj", pair), _hgrn_proj(u, win_ref, pair))
    pool.add("gates", _gate_stream(u, win_ref))
    first_keys = [("q", kv) for kv in range(N_KV_HEADS)] + ["kv"] + [("hproj", p) for p in range(HG_HEADS // 2)]
    while not pool.has(first_keys):
        pool.round()
    for hd in range(HG_HEADS):
        qr, fr, ir, gr = (a[:, (hd % 2) * HG_DIM:(hd % 2 + 1) * HG_DIM] for a in pool.out[("hproj", hd // 2)])
        pool.add(("hgrn", hd), _hgrn_head(qr, fr, ir, gr, lb[:, hd * HG_DIM:(hd + 1) * HG_DIM], hd, s_ref, s_out_ref,
                                          hg_ref[...], ones_ref[...], tri_ref[...]))

    q = [pool.out[("q", j // 2)][:, (j % 2) * LANES:(j % 2 + 1) * LANES] for j in range(SPLIT_SIZES[0] // LANES)]
    k, v = pool.out["kv"]
    kk_all = jnp.concatenate([kprev_ref[...], k], axis=0)
    vv_all = jnp.concatenate([vprev_ref[...], v], axis=0)
    k_tail, v_tail = k[tm - WINDOW:], v[tm - WINDOW:]
    kprev_ref[...] = k_tail
    vprev_ref[...] = v_tail
    k_out_ref[...] = k_tail
    v_out_ref[...] = v_tail
    lane = lax.broadcasted_iota(jnp.int32, kk_all.shape, 1)
    k_lo0 = jnp.where(lane < HEAD_DIM, kk_all, 0.0)
    k_hi1 = jnp.where(lane >= HEAD_DIM, kk_all, 0.0)
    k_halves = ((k_lo0.astype(BF16), pltpu.roll(k_lo0, HEAD_DIM, 1).astype(BF16)),
                (pltpu.roll(k_hi1, HEAD_DIM, 1).astype(BF16), k_hi1.astype(BF16)))
    v_t = vv_all.T.astype(BF16)

    n_blk = tm // WINDOW
    chain_keys = lambda blk: [("att", blk, kv, half) for kv in range(N_KV_HEADS) for half in range(2)]
    for blk in range(n_blk):
        lo = blk * WINDOW
        variant = jnp.where(first, 0, 1) if blk == 0 else 1
        for kv in range(N_KV_HEADS):
            q_pair = jnp.concatenate([q[2 * kv + j][lo:lo + WINDOW] for j in range(GROUP // 2)], axis=0)
            for half in range(2):
                pool.add(("att", blk, kv, half),
                         _attn_chain(q_pair, k_halves[kv][half][lo:lo + 2 * WINDOW], v_t[:, lo:lo + 2 * WINDOW],
                                     bias_ref.at[variant, kv, half], sink_ref[GROUP * kv + half],
                                     sink_ref[GROUP * kv + 2 + half], kv))

    waiting = list(range(n_blk))
    while pool.live:
        pool.round()
        for blk in [b for b in waiting if pool.has(chain_keys(b))]:
            waiting.remove(blk)
            parts = []
            for hq in range(N_Q_HEADS):
                kv, j, half = hq // GROUP, (hq % GROUP) // 2, hq % 2
                parts.append(pool.out[("att", blk, kv, half)][:, j * WINDOW:(j + 1) * WINDOW])
            pool.add(("up", blk), _attn_up(parts, wua_ref))

    up_a = jnp.concatenate([pool.out[("up", blk)] for blk in range(n_blk)], axis=0)
    sga, sgb = pool.out["gates"]
    heads = [pool.out[("hgrn", hd)] for hd in range(HG_HEADS)]

    def finish(orr_heads):
        orr = jnp.concatenate(orr_heads, axis=1)
        merged = sga * up_a + sgb * _dot(orr, wur_ref[...])
        h2_ref[...] = h + _dot(merged.astype(BF16), wo_ref[...])

    finish([o for o, _, _ in heads])

    worst = functools.reduce(jnp.minimum, [w for _, w, _ in heads])

    @pl.when(jnp.min(worst) < -EXP_CLAMP)
    def _():
        finish([_hgrn_redo(redo, hg_ref[...], ones_ref[...]) for _, _, redo in heads])


def _mix(h, consts, batch, seq, tm):
    t, d = h.shape
    nt = seq // tm
    row = pl.BlockSpec((tm, d), lambda b, i: (b * nt + i, 0))
    win_out = pl.BlockSpec((WINDOW, LANES), lambda b, i: (b, 0))
    return pl.pallas_call(
        _mix_kernel,
        grid=(batch, nt),
        in_specs=[row] + [pl.BlockSpec(memory_space=pltpu.SMEM) if a.ndim == 1 else _const_spec(a) for a in consts],
        out_specs=[row, win_out, win_out, pl.BlockSpec((1, HG_HEADS, HG_DIM, HG_DIM), lambda b, i: (b, 0, 0, 0))],
        out_shape=[jax.ShapeDtypeStruct((t, d), F32),
                   jax.ShapeDtypeStruct((batch * WINDOW, LANES), F32),
                   jax.ShapeDtypeStruct((batch * WINDOW, LANES), F32),
                   jax.ShapeDtypeStruct((batch, HG_HEADS, HG_DIM, HG_DIM), F32)],
        scratch_shapes=[pltpu.VMEM((WINDOW, LANES), F32), pltpu.VMEM((WINDOW, LANES), F32),
                        pltpu.VMEM((HG_HEADS, HG_DIM, HG_DIM), F32)],
        compiler_params=pltpu.CompilerParams(dimension_semantics=("parallel", "arbitrary"),
                                             vmem_limit_bytes=VMEM_LIMIT),
        name="mix",
    )(h, *consts)


def _proj_kernel(h_ref, gm_ref, win_ref, qg_ref, kg_ref, seg_ref,
                 q_ref, k_ref, v_ref, qr_ref, fr_ref, ir_ref, gr_ref):
    u = _rms_rows(h_ref[...], gm_ref[...]).astype(BF16)
    q, k, v = _project_qkv(u, win_ref, qg_ref[...], kg_ref[...], seg_ref[...])
    q_ref[...] = q
    k_ref[...] = k
    v_ref[...] = v
    qr_ref[...] = _dot(u, win_ref[:, O_QR:O_FR])
    fr_ref[...] = _dot(u, win_ref[:, O_FR:O_IR])
    ir_ref[...] = _dot(u, win_ref[:, O_IR:O_GR]).astype(ir_ref.dtype)
    gr_ref[...] = _dot(u, win_ref[:, O_GR:O_GA])


def _proj(h, gm, w_in, qg, kg, seg, tm):
    t, d = h.shape
    row = lambda w: pl.BlockSpec((tm, w), lambda i: (i, 0))
    widths = SPLIT_SIZES[:7]
    dtypes = (BF16, F32, F32, F32, F32, BF16, F32)
    return pl.pallas_call(
        _proj_kernel,
        grid=(t // tm,),
        in_specs=[row(d)] + [_const_spec(a) for a in (gm, w_in, qg, kg, seg)],
        out_specs=[row(w) for w in widths],
        out_shape=[jax.ShapeDtypeStruct((t, w), dt) for w, dt in zip(widths, dtypes)],
        compiler_params=pltpu.CompilerParams(dimension_semantics=("parallel",), vmem_limit_bytes=VMEM_LIMIT),
        name="proj",
    )(h, gm, w_in, qg, kg, seg)


def _out_kernel(h_ref, att_ref, orr_ref, gm_ref, win_ref, wua_ref, wur_ref, wo_ref, h2_ref):
    h = h_ref[...]
    u = _rms_rows(h, gm_ref[...]).astype(BF16)
    sga = jax.nn.sigmoid(_dot(u, win_ref[:, O_GA:O_GB]))
    sgb = jax.nn.sigmoid(_dot(u, win_ref[:, O_GB:O_END]))
    h2_ref[...] = _merge_out(h, sga, sgb, att_ref[...], orr_ref[...], wua_ref, wur_ref, wo_ref)


def _out(h, att, orr, consts, tm):
    t, d = h.shape
    row = lambda a: pl.BlockSpec((tm, a.shape[1]), lambda i: (i, 0))
    return pl.pallas_call(
        _out_kernel,
        grid=(t // tm,),
        in_specs=[row(a) for a in (h, att, orr)] + [_const_spec(a) for a in consts],
        out_specs=pl.BlockSpec((tm, d), lambda i: (i, 0)),
        out_shape=jax.ShapeDtypeStruct((t, d), F32),
        compiler_params=pltpu.CompilerParams(dimension_semantics=("parallel",), vmem_limit_bytes=VMEM_LIMIT),
        name="out",
    )(h, att, orr, *consts)


def _bucket_map():
    dist = np.arange(WINDOW)[:, None] + WINDOW - np.arange(2 * WINDOW)[None, :]
    valid = (dist >= 0) & (dist <= WINDOW)
    return np.where(valid, _t5_bucket(dist), -1).astype(np.int32)


def _stacked(bias, rows):
    return bias.reshape(N_KV_HEADS, GROUP * rows, bias.shape[-1])


def kernel(x_prompt, x_sample, cache_win_k, cache_win_v, state_hgrn, ffn1_norm, ffn1_w1, ffn1_w3, ffn1_w2, mix_norm, w_in, q_norm, k_norm, sinks, rel_bias_table, hgrn_lb_logits, hg_norm, w_up_attn, w_up_hgrn, w_out, ffn2_norm, ffn2_w1, ffn2_w3, ffn2_w2):
    depth = ffn1_norm.shape[0]
    assert depth == 1 and hgrn_lb_logits.shape[0] == 2, "single-layer step only"
    batch, seq, d = x_prompt.shape
    bd, ld, _ = x_sample.shape

    bf = lambda w: w[0].astype(BF16)
    row = lambda g: g[0].reshape(1, -1).astype(F32)
    w1a, w3a, w2a, w_in_b = bf(ffn1_w1), bf(ffn1_w3), bf(ffn1_w2), bf(w_in)
    w1b, w3b, w2b = bf(ffn2_w1), bf(ffn2_w3), bf(ffn2_w2)
    wua, wur, wo = bf(w_up_attn), bf(w_up_hgrn), bf(w_out)
    g1, gm, g2 = row(ffn1_norm), row(mix_norm), row(ffn2_norm)
    qg = jnp.tile(row(q_norm), (1, LANES // HEAD_DIM)) * (HEAD_DIM ** -0.5)
    kg = jnp.tile(row(k_norm), (1, LANES // HEAD_DIM))
    head_of_lane = np.arange(2 * LANES) // HEAD_DIM
    seg2 = jnp.asarray((head_of_lane[:, None] == head_of_lane[None, :]).astype(np.float32), BF16)
    seg = seg2[:LANES, :LANES]
    qg2 = jnp.tile(qg, (1, 2))
    ones = jnp.ones((HG_DIM, HG_DIM), BF16)
    tri = jnp.asarray(np.tril(np.ones((HG_CHUNK, HG_CHUNK), np.float32)), BF16)
    lb_logits = hgrn_lb_logits.astype(F32)
    hg_gain = row(hg_norm)

    later = _rel_bias(rel_bias_table.astype(F32), np.ascontiguousarray(_bucket_map().T))
    first = jnp.where(np.arange(2 * WINDOW)[None, :, None] >= WINDOW, later, NEG_BIG)
    def paired(b):
        b = b.reshape(N_KV_HEADS, GROUP // 2, 2, 2 * WINDOW, WINDOW)
        return b.transpose(0, 2, 3, 1, 4).reshape(N_KV_HEADS, 2, 2 * WINDOW, 2 * WINDOW)
    bias_p = jnp.stack([paired(first), paired(later)])
    bias_s = _stacked(later[:, :WINDOW + ld, :ld].transpose(0, 2, 1), ld)
    sink_1d = sinks[0].astype(F32)
    sink = sink_1d.reshape(N_KV_HEADS, GROUP, 1)
    sink_s = jnp.repeat(sink, ld, axis=1).reshape(N_KV_HEADS, GROUP * ld, 1)

    h_p = _ffn(x_prompt.reshape(batch * seq, d), g1, w1a, w3a, w2a, TM_FFN)
    h_s = _ffn(x_sample.reshape(bd * ld, d), g1, w1a, w3a, w2a, TM_FFN)

    consts = (gm, w_in_b, qg2, kg, seg, seg2, bias_p, sink_1d, lb_logits, hg_gain, tri, ones, wua, wur, wo)
    h2_p, k_p, v_p, s_p = _mix(h_p, consts, batch, seq, TM_MIX)
    win = lambda a: a.reshape(1, batch, WINDOW, N_KV_HEADS, HEAD_DIM)

    q, k, v, qr, fr, ir, gr = _proj(h_s, gm, w_in_b, qg, kg, seg, TM_SAMPLE)
    ck = cache_win_k[0].reshape(bd, WINDOW, LANES)
    cv = cache_win_v[0].reshape(bd, WINDOW, LANES)
    att, nk, nv = _swa_sample(q, k, v, ck, cv, bias_s, sink_s, ld, SEQS_PER_STEP)
    orr, s_s = _hgrn_sample(qr, fr, ir, gr, state_hgrn[0], lb_logits, hg_gain, ld, SEQS_PER_STEP)
    h2_s = _out(h_s, att, orr, (gm, w_in_b, wua, wur, wo), TM_SAMPLE)

    y_p = _ffn(h2_p, g2, w1b, w3b, w2b, TM_FFN)
    y_s = _ffn(h2_s, g2, w1b, w3b, w2b, TM_FFN)
    unwin = lambda a: a.reshape(1, bd, WINDOW, N_KV_HEADS, HEAD_DIM)
    return (y_p.reshape(batch, seq, d), y_s.reshape(bd, ld, d), win(k_p), win(v_p), s_p[None],
            unwin(nk), unwin(nv), s_s[None])
```

```python
import functools

import numpy as np
import jax
import jax.numpy as jnp
from jax import lax
from jax.experimental import pallas as pl
from jax.experimental.pallas import tpu as pltpu

F32 = jnp.float32
BF16 = jnp.bfloat16

LANES = 128
HEAD_DIM = 64
N_Q_HEADS = 8
N_KV_HEADS = 2
GROUP = N_Q_HEADS // N_KV_HEADS
WINDOW = 128
N_BUCKETS = 32
MAX_DISTANCE = 128
HG_HEADS = 4
HG_DIM = 128
HG_CHUNK = 128
HG_SUB = 32
EPS = 1e-6
F_TINY = 1e-30
NEG_BIG = -1e30
EXP_CLAMP = 100.0
LOG2E = 1.4426950408889634
SPLIT_SIZES = (512, 128, 128, 512, 512, 512, 512, 1024, 1024)
SPLIT_OFFS = tuple(int(v) for v in np.cumsum((0,) + SPLIT_SIZES))
O_Q, O_K, O_V, O_QR, O_FR, O_IR, O_GR, O_GA, O_GB, O_END = SPLIT_OFFS
VMEM_LIMIT = 56 * 1024 * 1024
TM_FFN = 1024
TM_MIX = 512
TM_SAMPLE = 512
SEQS_PER_STEP = 16


def _dot(a, b):
    return jnp.dot(a, b, preferred_element_type=F32)


def _dot_nt(a, b):
    return lax.dot_general(a, b, (((1,), (1,)), ((), ())), preferred_element_type=F32)


def _split2_ldot(w, x):
    hi = x.astype(BF16)
    lo = (x - hi.astype(F32)).astype(BF16)
    return _dot(w, hi) + _dot(w, lo)


def _rms_rows(x, g):
    ms = jnp.mean(x * x, axis=-1, keepdims=True)
    return x * lax.rsqrt(ms + EPS) * g


def _seg_rms(x, seg, inv_n, g):
    ms = _dot((x * x).astype(BF16), seg) * inv_n
    return x * lax.rsqrt(ms + EPS) * g


class _Pool:
    def __init__(self):
        self.live = []
        self.out = {}

    def add(self, key, gen):
        self.live.append((key, gen))

    def round(self):
        still = []
        for key, gen in self.live:
            try:
                next(gen)
                still.append((key, gen))
            except StopIteration as done:
                self.out[key] = done.value
        self.live = still

    def has(self, keys):
        return all(k in self.out for k in keys)


def _const_spec(a):
    nd = a.ndim
    return pl.BlockSpec(a.shape, lambda *_: (0,) * nd, pipeline_mode=pl.Buffered(1))


def _ff_chunks(d_ff, step=1024):
    return tuple((lo, min(lo + step, d_ff)) for lo in range(0, d_ff, step))


def _swiglu(xn, w1_ref, w3_ref, w2_ref):
    acc = jnp.zeros((xn.shape[0], w2_ref.shape[1]), F32)
    for lo, hi in _ff_chunks(w1_ref.shape[1]):
        a = _dot(xn, w1_ref[:, lo:hi])
        b = _dot(xn, w3_ref[:, lo:hi])
        acc = acc + _dot((jax.nn.silu(a) * b).astype(BF16), w2_ref[lo:hi, :])
    return acc


def _ffn_kernel(x_ref, g1_ref, w1_ref, w3_ref, w2_ref, h_ref):
    x = x_ref[...]
    xn = _rms_rows(x, g1_ref[...]).astype(BF16)
    h_ref[...] = x + 0.5 * _swiglu(xn, w1_ref, w3_ref, w2_ref)


def _ffn(x, g1, w1, w3, w2, tm):
    t, d = x.shape
    row = pl.BlockSpec((tm, d), lambda i: (i, 0))
    return pl.pallas_call(
        _ffn_kernel,
        grid=(t // tm,),
        in_specs=[row] + [_const_spec(a) for a in (g1, w1, w3, w2)],
        out_specs=row,
        out_shape=jax.ShapeDtypeStruct((t, d), F32),
        compiler_params=pltpu.CompilerParams(dimension_semantics=("parallel",), vmem_limit_bytes=VMEM_LIMIT),
        name="ffn",
    )(x, g1, w1, w3, w2)


def _t5_bucket(dist):
    max_exact = N_BUCKETS // 2
    d = np.maximum(dist, 0)
    large = max_exact + (np.log(np.maximum(d, 1) / max_exact) / np.log(MAX_DISTANCE / max_exact)
                         * (N_BUCKETS - max_exact)).astype(np.int32)
    large = np.minimum(large, N_BUCKETS - 1)
    return np.where(d < max_exact, d, large).astype(np.int32)


def _bias_kernel(table_ref, bucket_ref, out_ref):
    bucket = bucket_ref[...]
    masked = jnp.where(bucket < 0, NEG_BIG, 0.0).astype(F32)
    for h in range(N_Q_HEADS):
        acc = masked
        for b in range(N_BUCKETS):
            acc = acc + jnp.where(bucket == b, table_ref[b, h], 0.0)
        out_ref[h] = acc


def _rel_bias(table, bucket_map):
    r, c = bucket_map.shape
    return pl.pallas_call(
        _bias_kernel,
        in_specs=[pl.BlockSpec(memory_space=pltpu.SMEM), pl.BlockSpec((r, c), lambda: (0, 0))],
        out_specs=pl.BlockSpec((N_Q_HEADS, r, c), lambda: (0, 0, 0)),
        out_shape=jax.ShapeDtypeStruct((N_Q_HEADS, r, c), F32),
        name="rel_bias",
    )(table, jnp.asarray(bucket_map))


def _project_qkv(u, win_ref, qg, kg, seg):
    qs = []
    for s in range(SPLIT_SIZES[0] // LANES):
        x = _dot(u, win_ref[:, O_Q + s * LANES:O_Q + (s + 1) * LANES])
        qs.append(_seg_rms(x, seg, 1.0 / HEAD_DIM, qg).astype(BF16))
    k = _seg_rms(_dot(u, win_ref[:, O_K:O_V]), seg, 1.0 / HEAD_DIM, kg)
    v = _dot(u, win_ref[:, O_V:O_QR])
    return jnp.concatenate(qs, axis=1), k, v


def _stack_group(q, kv):
    a = q[:, (2 * kv) * LANES:(2 * kv + 1) * LANES]
    b = q[:, (2 * kv + 1) * LANES:(2 * kv + 2) * LANES]
    ar, br = pltpu.roll(a, HEAD_DIM, 1), pltpu.roll(b, HEAD_DIM, 1)
    parts = (a, ar, b, br) if kv == 0 else (ar, a, br, b)
    return jnp.concatenate(parts, axis=0)


def _kv_half(x, kv, fill):
    lane = lax.broadcasted_iota(jnp.int32, x.shape, 1)
    keep = (lane < HEAD_DIM) if kv == 0 else (lane >= HEAD_DIM)
    return jnp.where(keep, x, fill)


def _attend_chain(q, kk, vv, bias_ref, sink_ref, r, kv):
    qs = _stack_group(q.astype(F32), kv).astype(BF16)
    s = _dot_nt(qs, _kv_half(kk, kv, 0.0).astype(BF16)) + bias_ref[kv]
    yield
    sink = sink_ref[kv]
    m = jnp.maximum(jnp.max(s, axis=-1, keepdims=True), sink)
    e = jnp.exp2(s - m).astype(BF16)
    yield
    o = _dot(e, _kv_half(vv, kv, 1.0).astype(BF16))
    yield
    o = o / (pltpu.roll(o, HEAD_DIM, 1) + jnp.exp2(sink - m))
    lane = lax.broadcasted_iota(jnp.int32, (r, LANES), 1)
    slabs = []
    for pair in range(GROUP // 2):
        a, b = o[(2 * pair) * r:(2 * pair + 1) * r], o[(2 * pair + 1) * r:(2 * pair + 2) * r]
        if kv == 0:
            slabs.append(jnp.where(lane < HEAD_DIM, a, pltpu.roll(b, HEAD_DIM, 1)))
        else:
            slabs.append(jnp.where(lane < HEAD_DIM, pltpu.roll(a, HEAD_DIM, 1), b))
    return slabs


def _swa_sample_kernel(q_ref, kn_ref, vn_ref, ck_ref, cv_ref, bias_ref, sink_ref, o_ref, nk_ref, nv_ref, *, ld):
    pool = _Pool()
    nseq = ck_ref.shape[0]
    for i in range(nseq):
        rows = slice(i * ld, (i + 1) * ld)
        kk = jnp.concatenate([ck_ref[i], kn_ref[rows, :]], axis=0)
        vv = jnp.concatenate([cv_ref[i], vn_ref[rows, :]], axis=0)
        nk_ref[i] = kk[ld:, :]
        nv_ref[i] = vv[ld:, :]
        for kv in range(N_KV_HEADS):
            pool.add((i, kv), _attend_chain(q_ref[rows, :], kk, vv, bias_ref, sink_ref, ld, kv))
    while pool.live:
        pool.round()
    for i in range(nseq):
        slabs = [s for kv in range(N_KV_HEADS) for s in pool.out[(i, kv)]]
        o_ref[i * ld:(i + 1) * ld, :] = jnp.concatenate(slabs, axis=1).astype(o_ref.dtype)


def _swa_sample(q, k, v, cache_k, cache_v, bias, sink_col, ld, sb):
    bd = cache_k.shape[0]
    tok = lambda w: pl.BlockSpec((sb * ld, w), lambda i: (i, 0))
    cache = pl.BlockSpec((sb, WINDOW, LANES), lambda i: (i, 0, 0))
    return pl.pallas_call(
        functools.partial(_swa_sample_kernel, ld=ld),
        grid=(bd // sb,),
        in_specs=[tok(4 * LANES), tok(LANES), tok(LANES), cache, cache,
                  pl.BlockSpec(bias.shape, lambda i: (0, 0, 0)), pl.BlockSpec(sink_col.shape, lambda i: (0, 0, 0))],
        out_specs=[tok(4 * LANES), cache, cache],
        out_shape=[jax.ShapeDtypeStruct(q.shape, BF16), jax.ShapeDtypeStruct(cache_k.shape, F32),
                   jax.ShapeDtypeStruct(cache_v.shape, F32)],
        compiler_params=pltpu.CompilerParams(dimension_semantics=("parallel",)),
        name="swa_sample",
    )(q, k, v, cache_k, cache_v, bias, sink_col)


def _lower_bound(lb_logits):
    z = lb_logits - jnp.max(lb_logits, axis=0, keepdims=True)
    e = jnp.exp(z)
    return e[0:1, :] / jnp.sum(e, axis=0, keepdims=True)


def _gates(qr, fr, lb):
    f = lb + (1.0 - lb) * jax.nn.sigmoid(fr)
    return jax.nn.silu(qr), 1.0 - f, jnp.log2(jnp.maximum(f, F_TINY))


def _head_out(o, gr, hg_gain, ones_seg):
    return _seg_rms(o, ones_seg, 1.0 / HG_DIM, hg_gain) * jax.nn.silu(gr)


def _hgrn_proj(u, win_ref, pair):
    cols = lambda off: slice(off + 2 * pair * HG_DIM, off + 2 * (pair + 1) * HG_DIM)
    qr = _dot(u, win_ref[:, cols(O_QR)])
    fr = _dot(u, win_ref[:, cols(O_FR)])
    yield
    ir = _dot(u, win_ref[:, cols(O_IR)]).astype(BF16)
    gr = _dot(u, win_ref[:, cols(O_GR)])
    yield
    return qr, fr, ir, gr


def _hgrn_head(qr, fr, vals, gr, lb, hd, s_ref, s_out_ref, hg_gain, ones, tri):
    qf, kk, g = _gates(qr, fr, lb)
    nchunk = qr.shape[0] // HG_CHUNK
    chunks = [slice(c * HG_CHUNK, (c + 1) * HG_CHUNK) for c in range(nchunk)]
    yield
    gcs = [_split2_ldot(tri, g[rows]) for rows in chunks]
    yield
    q_in, upd, decay, scores = [], [], [], []
    worst = jnp.zeros((1, HG_DIM), F32)
    for rows, gc in zip(chunks, gcs):
        g_last = gc[HG_CHUNK - 1:HG_CHUNK, :]
        q_in.append((qf[rows] * jnp.exp2(gc)).astype(BF16))
        k_end_t = (kk[rows] * jnp.exp2(g_last - gc)).T
        upd.append(_dot(k_end_t.astype(BF16), vals[rows]))
        decay.append(jnp.broadcast_to(jnp.exp2(g_last), (HG_DIM, HG_DIM)).T)
        qc, kc = qf[rows], kk[rows]
        for i in range(HG_CHUNK // HG_SUB):
            lo, hi = i * HG_SUB, (i + 1) * HG_SUB
            g_base = gc[lo - 1:lo, :] if i else jnp.zeros((1, HG_DIM), F32)
            worst = jnp.minimum(worst, gc[hi - 1:hi, :] - g_base)
            q_hat = qc[lo:hi] * jnp.exp2(gc[lo:hi] - g_base)
            k_hat = kc[:hi] * jnp.exp2(jnp.minimum(g_base - gc[:hi], EXP_CLAMP))
            a = _dot_nt(q_hat.astype(BF16), k_hat.astype(BF16))
            row = lax.broadcasted_iota(jnp.int32, (HG_SUB, hi), 0)
            col = lax.broadcasted_iota(jnp.int32, (HG_SUB, hi), 1)
            scores.append(jnp.where(col <= row + lo, a, 0.0).astype(BF16))
    yield
    intra = []
    nsub = HG_CHUNK // HG_SUB
    for c, rows in enumerate(chunks):
        vc = vals[rows]
        parts = [_dot(scores[c * nsub + i], vc[:(i + 1) * HG_SUB]) for i in range(nsub)]
        intra.append(jnp.concatenate(parts, axis=0))
    yield
    s = s_ref[hd]
    inter = []
    for c in range(nchunk):
        inter.append(_dot(q_in[c], s.astype(BF16)))
        s = decay[c] * s + upd[c]
    s_ref[hd] = s
    s_out_ref[0, hd] = s
    yield
    o = jnp.concatenate([a + b for a, b in zip(inter, intra)], axis=0)
    redo = dict(qf=qf, kk=kk, gcs=gcs, vals=vals, inter=inter, gr=gr)
    return _head_out(o, gr, hg_gain, ones).astype(BF16), worst, redo


def _exact_intra(qc, kc, gc, vc):
    c = qc.shape[0]
    group = 8
    row = lax.broadcasted_iota(jnp.int32, (c, c), 0)
    col = lax.broadcasted_iota(jnp.int32, (c, c), 1)
    col_g = lax.broadcasted_iota(jnp.int32, (group, c), 1)
    earlier = []
    for i in range(c // group):
        lo, hi = i * group, (i + 1) * group
        g_base = gc[lo - 1:lo, :] if i else jnp.zeros((1, HG_DIM), F32)
        q_hat = qc[lo:hi] * jnp.exp2(gc[lo:hi] - g_base)
        k_hat = kc * jnp.exp2(jnp.minimum(g_base - gc, 0.0))
        earlier.append(jnp.where(col_g < lo, _dot_nt(q_hat.astype(BF16), k_hat.astype(BF16)), 0.0))
    a = jnp.concatenate(earlier, axis=0)
    qb = qc.astype(BF16)
    g3 = gc.reshape(c // group, group, HG_DIM)
    own = (col <= row) & (col >= row - lax.rem(row, group))
    for p in range(group):
        g_p = jnp.broadcast_to(g3[:, p:p + 1, :], g3.shape).reshape(c, HG_DIM)
        k_p = kc * jnp.exp2(jnp.minimum(g_p - gc, 0.0))
        a = a + jnp.where(own & (lax.rem(row, group) == p), _dot_nt(qb, k_p.astype(BF16)), 0.0)
    return _dot(a.astype(BF16), vc)


def _hgrn_redo(redo, hg_gain, ones):
    outs = []
    for c, gc in enumerate(redo["gcs"]):
        rows = slice(c * HG_CHUNK, (c + 1) * HG_CHUNK)
        outs.append(redo["inter"][c] + _exact_intra(redo["qf"][rows], redo["kk"][rows], gc, redo["vals"][rows]))
    return _head_out(jnp.concatenate(outs, axis=0), redo["gr"], hg_gain, ones).astype(BF16)


def _attn_chain(q_pair, k_half, v_t, bias_ref, sink_a, sink_b, kv):
    s = _dot_nt(k_half, q_pair) + bias_ref[...]
    yield
    col = lax.broadcasted_iota(jnp.int32, (1, s.shape[1]), 1)
    sink = jnp.where(col < WINDOW, sink_a, sink_b)
    m = jnp.maximum(jnp.max(s, axis=0, keepdims=True), sink)
    e = jnp.exp2(s - m)
    yield
    denom = jnp.sum(e, axis=0, keepdims=True) + jnp.exp2(sink - m)
    o_t = _dot(v_t, e.astype(BF16))[kv * HEAD_DIM:(kv + 1) * HEAD_DIM]
    yield
    return o_t * (1.0 / denom)


def _gate_stream(u, win_ref, step=256):
    res = []
    for off in (O_GA, O_GB):
        cols = []
        for lo in range(0, SPLIT_SIZES[7], step):
            cols.append(jax.nn.sigmoid(_dot(u, win_ref[:, off + lo:off + lo + step])))
            yield
        res.append(jnp.concatenate(cols, axis=1))
    return res


def _q_cols(u, win_ref, kv, seg2, gain2):
    x = _dot(u, win_ref[:, O_Q + 2 * kv * LANES:O_Q + 2 * (kv + 1) * LANES])
    yield
    ms = _dot((x * x).astype(BF16), seg2) * (1.0 / HEAD_DIM)
    yield
    return (x * lax.rsqrt(ms + EPS) * gain2).astype(BF16)


def _kv_cols(u, win_ref, seg, gain):
    x = _dot(u, win_ref[:, O_K:O_QR])
    yield
    k, v = x[:, :LANES], x[:, LANES:]
    ms = _dot((k * k).astype(BF16), seg) * (1.0 / HEAD_DIM)
    yield
    return k * lax.rsqrt(ms + EPS) * gain, v


def _attn_up(parts, wua_ref):
    att = jnp.concatenate(parts, axis=0).T.astype(BF16)
    yield
    return _dot(att, wua_ref[...])


def _hgrn_sample_kernel(qr_ref, fr_ref, ir_ref, gr_ref, s0_ref, lb_ref, hg_ref, tri_ref, tot_ref, ones_ref,
                        o_ref, s_out_ref, *, ld):
    nseq = s0_ref.shape[0]
    r = nseq * ld
    lb = _lower_bound(lb_ref[...])
    qf_all, kk_all, g_all = _gates(qr_ref[...], fr_ref[...], lb)
    tri, tot = tri_ref[...], tot_ref[...]
    causal = tri.astype(F32) > 0
    lane = lax.broadcasted_iota(jnp.int32, (HG_DIM, r), 1)
    pos = lax.rem(lax.broadcasted_iota(jnp.int32, (r, r), 0), ld)

    def head(hd):
        lanes = slice(hd * HG_DIM, (hd + 1) * HG_DIM)
        qf, kk, v = qf_all[:, lanes], kk_all[:, lanes], ir_ref[:, lanes]
        g = _split2_ldot(tri, g_all[:, lanes])
        g_last = _split2_ldot(tot, g_all[:, lanes])
        yield
        q_t = (qf * jnp.exp2(g)).astype(BF16)
        qb = qf.astype(BF16)
        g3 = g.reshape(nseq, ld, HG_DIM)
        a = jnp.zeros((r, r), F32)
        for p in range(ld):
            g_p = jnp.broadcast_to(g3[:, p:p + 1, :], g3.shape).reshape(r, HG_DIM)
            k_p = kk * jnp.exp2(jnp.minimum(g_p - g, 0.0))
            a = a + jnp.where(causal & (pos == p), _dot_nt(qb, k_p.astype(BF16)), 0.0)
        k_end_t = (kk * jnp.exp2(g_last - g)).T
        decay_t = jnp.exp2(g_last).T
        yield
        intra = _dot(a.astype(BF16), v)
        outs = []
        for i in range(nseq):
            s_prev = s0_ref[i, hd]
            outs.append(_dot(q_t[i * ld:(i + 1) * ld], s_prev.astype(BF16)))
            own = (lane >= i * ld) & (lane < (i + 1) * ld)
            upd = _dot(jnp.where(own, k_end_t, 0.0).astype(BF16), v)
            s_out_ref[i, hd] = decay_t[:, i * ld:i * ld + 1] * s_prev + upd
            if i % 2:
                yield
        o = jnp.concatenate(outs, axis=0) + intra
        yield
        o_ref[:, lanes] = _head_out(o, gr_ref[:, lanes], hg_ref[...], ones_ref[...]).astype(o_ref.dtype)

    pool = _Pool()
    for hd in range(HG_HEADS):
        pool.add(hd, head(hd))
    while pool.live:
        pool.round()


def _hgrn_sample(qr, fr, ir, gr, s0, lb_logits, hg_gain, ld, sb):
    bd = s0.shape[0]
    r = sb * ld
    seq = np.arange(r) // ld
    same = seq[:, None] == seq[None, :]
    tri = jnp.asarray((same & (np.arange(r)[:, None] >= np.arange(r)[None, :])).astype(np.float32), BF16)
    tot = jnp.asarray(same.astype(np.float32), BF16)
    ones = jnp.ones((HG_DIM, HG_DIM), BF16)
    tok = pl.BlockSpec((r, HG_HEADS * HG_DIM), lambda i: (i, 0))
    state = pl.BlockSpec((sb, HG_HEADS, HG_DIM, HG_DIM), lambda i: (i, 0, 0, 0))
    const = lambda a: pl.BlockSpec(a.shape, lambda i: (0,) * a.ndim)
    return pl.pallas_call(
        functools.partial(_hgrn_sample_kernel, ld=ld),
        grid=(bd // sb,),
        in_specs=[tok, tok, tok, tok, state, const(lb_logits), const(hg_gain), const(tri), const(tot), const(ones)],
        out_specs=[tok, state],
        out_shape=[jax.ShapeDtypeStruct(qr.shape, BF16), jax.ShapeDtypeStruct(s0.shape, F32)],
        compiler_params=pltpu.CompilerParams(dimension_semantics=("parallel",)),
        name="hgrn_sample",
    )(qr, fr, ir, gr, s0, lb_logits, hg_gain, tri, tot, ones)


def _merge_out(h, sga, sgb, att, orr, wua_ref, wur_ref, wo_ref):
    merged = sga * _dot(att, wua_ref[...]) + sgb * _dot(orr, wur_ref[...])
    return h + _dot(merged.astype(BF16), wo_ref[...])


def _mix_kernel(h_ref, gm_ref, win_ref, qg_ref, kg_ref, seg_ref, seg2_ref, bias_ref, sink_ref, lb_ref, hg_ref,
                tri_ref, ones_ref, wua_ref, wur_ref, wo_ref,
                h2_ref, k_out_ref, v_out_ref, s_out_ref, kprev_ref, vprev_ref, s_ref):
    tm = h_ref.shape[0]
    first = pl.program_id(1) == 0

    @pl.when(first)
    def _():
        kprev_ref[...] = jnp.zeros(kprev_ref.shape, F32)
        vprev_ref[...] = jnp.zeros(vprev_ref.shape, F32)
        s_ref[...] = jnp.zeros(s_ref.shape, F32)

    h = h_ref[...]
    u = _rms_rows(h, gm_ref[...]).astype(BF16)

    pool = _Pool()
    lb = _lower_bound(lb_ref[...])
    for kv in range(N_KV_HEADS):
        pool.add(("q", kv), _q_cols(u, win_ref, kv, seg2_ref[...], qg_ref[...]))
    pool.add("kv", _kv_cols(u, win_ref, seg_ref[...], kg_ref[...]))
    for pair in range(HG_HEADS // 2):
        pool.add(("hproj", pair), _hgrn_proj(u, win_ref, pair))
    pool.add("gates", _gate_stream(u, win_ref))
    first_keys = [("q", kv) for kv in range(N_KV_HEADS)] + ["kv"] + [("hproj", p) for p in range(HG_HEADS // 2)]
    while not pool.has(first_keys):
        pool.round()
    for hd in range(HG_HEADS):
        qr, fr, ir, gr = (a[:, (hd % 2) * HG_DIM:(hd % 2 + 1) * HG_DIM] for a in pool.out[("hproj", hd // 2)])
        pool.add(("hgrn", hd), _hgrn_head(qr, fr, ir, gr, lb[:, hd * HG_DIM:(hd + 1) * HG_DIM], hd, s_ref, s_out_ref,
                                          hg_ref[...], ones_ref[...], tri_ref[...]))

    q = [pool.out[("q", j // 2)][:, (j % 2) * LANES:(j % 2 + 1) * LANES] for j in range(SPLIT_SIZES[0] // LANES)]
    k, v = pool.out["kv"]
    kk_all = jnp.concatenate([kprev_ref[...], k], axis=0)
    vv_all = jnp.concatenate([vprev_ref[...], v], axis=0)
    k_tail, v_tail = k[tm - WINDOW:], v[tm - WINDOW:]
    kprev_ref[...] = k_tail
    vprev_ref[...] = v_tail
    k_out_ref[...] = k_tail
    v_out_ref[...] = v_tail
    lane = lax.broadcasted_iota(jnp.int32, kk_all.shape, 1)
    k_lo0 = jnp.where(lane < HEAD_DIM, kk_all, 0.0)
    k_hi1 = jnp.where(lane >= HEAD_DIM, kk_all, 0.0)
    k_halves = ((k_lo0.astype(BF16), pltpu.roll(k_lo0, HEAD_DIM, 1).astype(BF16)),
                (pltpu.roll(k_hi1, HEAD_DIM, 1).astype(BF16), k_hi1.astype(BF16)))
    v_t = vv_all.T.astype(BF16)

    n_blk = tm // WINDOW
    chain_keys = lambda blk: [("att", blk, kv, half) for kv in range(N_KV_HEADS) for half in range(2)]
    for blk in range(n_blk):
        lo = blk * WINDOW
        variant = jnp.where(first, 0, 1) if blk == 0 else 1
        for kv in range(N_KV_HEADS):
            q_pair = jnp.concatenate([q[2 * kv + j][lo:lo + WINDOW] for j in range(GROUP // 2)], axis=0)
            for half in range(2):
                pool.add(("att", blk, kv, half),
                         _attn_chain(q_pair, k_halves[kv][half][lo:lo + 2 * WINDOW], v_t[:, lo:lo + 2 * WINDOW],
                                     bias_ref.at[variant, kv, half], sink_ref[GROUP * kv + half],
                                     sink_ref[GROUP * kv + 2 + half], kv))

    waiting = list(range(n_blk))
    while pool.live:
        pool.round()
        for blk in [b for b in waiting if pool.has(chain_keys(b))]:
            waiting.remove(blk)
            parts = []
            for hq in range(N_Q_HEADS):
                kv, j, half = hq // GROUP, (hq % GROUP) // 2, hq % 2
                parts.append(pool.out[("att", blk, kv, half)][:, j * WINDOW:(j + 1) * WINDOW])
            pool.add(("up", blk), _attn_up(parts, wua_ref))

    up_a = jnp.concatenate([pool.out[("up", blk)] for blk in range(n_blk)], axis=0)
    sga, sgb = pool.out["gates"]
    heads = [pool.out[("hgrn", hd)] for hd in range(HG_HEADS)]

    def finish(orr_heads):
        orr = jnp.concatenate(orr_heads, axis=1)
        merged = sga * up_a + sgb * _dot(orr, wur_ref[...])
        h2_ref[...] = h + _dot(merged.astype(BF16), wo_ref[...])

    finish([o for o, _, _ in heads])

    worst = functools.reduce(jnp.minimum, [w for _, w, _ in heads])

    @pl.when(jnp.min(worst) < -EXP_CLAMP)
    def _():
        finish([_hgrn_redo(redo, hg_ref[...], ones_ref[...]) for _, _, redo in heads])


def _mix(h, consts, batch, seq, tm):
    t, d = h.shape
    nt = seq // tm
    row = pl.BlockSpec((tm, d), lambda b, i: (b * nt + i, 0))
    win_out = pl.BlockSpec((WINDOW, LANES), lambda b, i: (b, 0))
    return pl.pallas_call(
        _mix_kernel,
        grid=(batch, nt),
        in_specs=[row] + [pl.BlockSpec(memory_space=pltpu.SMEM) if a.ndim == 1 else _const_spec(a) for a in consts],
        out_specs=[row, win_out, win_out, pl.BlockSpec((1, HG_HEADS, HG_DIM, HG_DIM), lambda b, i: (b, 0, 0, 0))],
        out_shape=[jax.ShapeDtypeStruct((t, d), F32),
                   jax.ShapeDtypeStruct((batch * WINDOW, LANES), F32),
                   jax.ShapeDtypeStruct((batch * WINDOW, LANES), F32),
                   jax.ShapeDtypeStruct((batch, HG_HEADS, HG_DIM, HG_DIM), F32)],
        scratch_shapes=[pltpu.VMEM((WINDOW, LANES), F32), pltpu.VMEM((WINDOW, LANES), F32),
                        pltpu.VMEM((HG_HEADS, HG_DIM, HG_DIM), F32)],
        compiler_params=pltpu.CompilerParams(dimension_semantics=("parallel", "arbitrary"),
                                             vmem_limit_bytes=VMEM_LIMIT),
        name="mix",
    )(h, *consts)


def _proj_kernel(h_ref, gm_ref, win_ref, qg_ref, kg_ref, seg_ref,
                 q_ref, k_ref, v_ref, qr_ref, fr_ref, ir_ref, gr_ref):
    u = _rms_rows(h_ref[...], gm_ref[...]).astype(BF16)
    q, k, v = _project_qkv(u, win_ref, qg_ref[...], kg_ref[...], seg_ref[...])
    q_ref[...] = q
    k_ref[...] = k
    v_ref[...] = v
    qr_ref[...] = _dot(u, win_ref[:, O_QR:O_FR])
    fr_ref[...] = _dot(u, win_ref[:, O_FR:O_IR])
    ir_ref[...] = _dot(u, win_ref[:, O_IR:O_GR]).astype(ir_ref.dtype)
    gr_ref[...] = _dot(u, win_ref[:, O_GR:O_GA])


def _proj(h, gm, w_in, qg, kg, seg, tm):
    t, d = h.shape
    row = lambda w: pl.BlockSpec((tm, w), lambda i: (i, 0))
    widths = SPLIT_SIZES[:7]
    dtypes = (BF16, F32, F32, F32, F32, BF16, F32)
    return pl.pallas_call(
        _proj_kernel,
        grid=(t // tm,),
        in_specs=[row(d)] + [_const_spec(a) for a in (gm, w_in, qg, kg, seg)],
        out_specs=[row(w) for w in widths],
        out_shape=[jax.ShapeDtypeStruct((t, w), dt) for w, dt in zip(widths, dtypes)],
        compiler_params=pltpu.CompilerParams(dimension_semantics=("parallel",), vmem_limit_bytes=VMEM_LIMIT),
        name="proj",
    )(h, gm, w_in, qg, kg, seg)


def _out_kernel(h_ref, att_ref, orr_ref, gm_ref, win_ref, wua_ref, wur_ref, wo_ref, h2_ref):
    h = h_ref[...]
    u = _rms_rows(h, gm_ref[...]).astype(BF16)
    sga = jax.nn.sigmoid(_dot(u, win_ref[:, O_GA:O_GB]))
    sgb = jax.nn.sigmoid(_dot(u, win_ref[:, O_GB:O_END]))
    h2_ref[...] = _merge_out(h, sga, sgb, att_ref[...], orr_ref[...], wua_ref, wur_ref, wo_ref)


def _out(h, att, orr, consts, tm):
    t, d = h.shape
    row = lambda a: pl.BlockSpec((tm, a.shape[1]), lambda i: (i, 0))
    return pl.pallas_call(
        _out_kernel,
        grid=(t // tm,),
        in_specs=[row(a) for a in (h, att, orr)] + [_const_spec(a) for a in consts],
        out_specs=pl.BlockSpec((tm, d), lambda i: (i, 0)),
        out_shape=jax.ShapeDtypeStruct((t, d), F32),
        compiler_params=pltpu.CompilerParams(dimension_semantics=("parallel",), vmem_limit_bytes=VMEM_LIMIT),
        name="out",
    )(h, att, orr, *consts)


def _bucket_map():
    dist = np.arange(WINDOW)[:, None] + WINDOW - np.arange(2 * WINDOW)[None, :]
    valid = (dist >= 0) & (dist <= WINDOW)
    return np.where(valid, _t5_bucket(dist), -1).astype(np.int32)


def _stacked(bias, rows):
    return bias.reshape(N_KV_HEADS, GROUP * rows, bias.shape[-1])


def kernel(x_prompt, x_sample, cache_win_k, cache_win_v, state_hgrn, ffn1_norm, ffn1_w1, ffn1_w3, ffn1_w2, mix_norm, w_in, q_norm, k_norm, sinks, rel_bias_table, hgrn_lb_logits, hg_norm, w_up_attn, w_up_hgrn, w_out, ffn2_norm, ffn2_w1, ffn2_w3, ffn2_w2):
    depth = ffn1_norm.shape[0]
    assert depth == 1 and hgrn_lb_logits.shape[0] == 2, "single-layer step only"
    batch, seq, d = x_prompt.shape
    bd, ld, _ = x_sample.shape

    bf = lambda w: w[0].astype(BF16)
    row = lambda g: g[0].reshape(1, -1).astype(F32)
    w1a, w3a, w2a, w_in_b = bf(ffn1_w1), bf(ffn1_w3), bf(ffn1_w2), bf(w_in)
    w1b, w3b, w2b = bf(ffn2_w1), bf(ffn2_w3), bf(ffn2_w2)
    wua, wur, wo = bf(w_up_attn), bf(w_up_hgrn), bf(w_out)
    g1, gm, g2 = row(ffn1_norm), row(mix_norm), row(ffn2_norm)
    qg = jnp.tile(row(q_norm), (1, LANES // HEAD_DIM)) * (HEAD_DIM ** -0.5 * LOG2E)
    kg = jnp.tile(row(k_norm), (1, LANES // HEAD_DIM))
    head_of_lane = np.arange(2 * LANES) // HEAD_DIM
    seg2 = jnp.asarray((head_of_lane[:, None] == head_of_lane[None, :]).astype(np.float32), BF16)
    seg = seg2[:LANES, :LANES]
    qg2 = jnp.tile(qg, (1, 2))
    ones = jnp.ones((HG_DIM, HG_DIM), BF16)
    tri = jnp.asarray(np.tril(np.ones((HG_CHUNK, HG_CHUNK), np.float32)), BF16)
    lb_logits = hgrn_lb_logits.astype(F32)
    hg_gain = row(hg_norm)

    later = _rel_bias(rel_bias_table.astype(F32) * LOG2E, np.ascontiguousarray(_bucket_map().T))
    first = jnp.where(np.arange(2 * WINDOW)[None, :, None] >= WINDOW, later, NEG_BIG)
    def paired(b):
        b = b.reshape(N_KV_HEADS, GROUP // 2, 2, 2 * WINDOW, WINDOW)
        return b.transpose(0, 2, 3, 1, 4).reshape(N_KV_HEADS, 2, 2 * WINDOW, 2 * WINDOW)
    bias_p = jnp.stack([paired(first), paired(later)])
    bias_s = _stacked(later[:, :WINDOW + ld, :ld].transpose(0, 2, 1), ld)
    sink_1d = sinks[0].astype(F32) * LOG2E
    sink = sink_1d.reshape(N_KV_HEADS, GROUP, 1)
    sink_s = jnp.repeat(sink, ld, axis=1).reshape(N_KV_HEADS, GROUP * ld, 1)

    h_p = _ffn(x_prompt.reshape(batch * seq, d), g1, w1a, w3a, w2a, TM_FFN)
    h_s = _ffn(x_sample.reshape(bd * ld, d), g1, w1a, w3a, w2a, TM_FFN)

    consts = (gm, w_in_b, qg2, kg, seg, seg2, bias_p, sink_1d, lb_logits, hg_gain, tri, ones, wua, wur, wo)
    h2_p, k_p, v_p, s_p = _mix(h_p, consts, batch, seq, TM_MIX)
    win = lambda a: a.reshape(1, batch, WINDOW, N_KV_HEADS, HEAD_DIM)

    q, k, v, qr, fr, ir, gr = _proj(h_s, gm, w_in_b, qg, kg, seg, TM_SAMPLE)
    ck = cache_win_k[0].reshape(bd, WINDOW, LANES)
    cv = cache_win_v[0].reshape(bd, WINDOW, LANES)
    att, nk, nv = _swa_sample(q, k, v, ck, cv, bias_s, sink_s, ld, SEQS_PER_STEP)
    orr, s_s = _hgrn_sample(qr, fr, ir, gr, state_hgrn[0], lb_logits, hg_gain, ld, SEQS_PER_STEP)
    h2_s = _out(h_s, att, orr, (gm, w_in_b, wua, wur, wo), TM_SAMPLE)

    y_p = _ffn(h2_p, g2, w1b, w3b, w2b, TM_FFN)
    y_s = _ffn(h2_s, g2, w1b, w3b, w2b, TM_FFN)
    unwin = lambda a: a.reshape(1, bd, WINDOW, N_KV_HEADS, HEAD_DIM)
    return (y_p.reshape(batch, seq, d), y_s.reshape(bd, ld, d), win(k_p), win(v_p), s_p[None],
            unwin(nk), unwin(nv), s_s[None])
```

```python
import functools

import numpy as np
import jax
import jax.numpy as jnp
from jax import lax
from jax.experimental import pallas as pl
from jax.experimental.pallas import tpu as pltpu

F32 = jnp.float32
BF16 = jnp.bfloat16

LANES = 128
HEAD_DIM = 64
N_Q_HEADS = 8
N_KV_HEADS = 2
GROUP = N_Q_HEADS // N_KV_HEADS
WINDOW = 128
N_BUCKETS = 32
MAX_DISTANCE = 128
HG_HEADS = 4
HG_DIM = 128
HG_CHUNK = 128
HG_SUB = 32
EPS = 1e-6
F_TINY = 1e-30
NEG_BIG = -1e30
EXP_CLAMP = 100.0
LOG2E = 1.4426950408889634
SPLIT_SIZES = (512, 128, 128, 512, 512, 512, 512, 1024, 1024)
SPLIT_OFFS = tuple(int(v) for v in np.cumsum((0,) + SPLIT_SIZES))
O_Q, O_K, O_V, O_QR, O_FR, O_IR, O_GR, O_GA, O_GB, O_END = SPLIT_OFFS
VMEM_LIMIT = 56 * 1024 * 1024
TM_FFN = 1024
TM_MIX = 512
TM_SAMPLE = 256
SEQS_PER_STEP = 16


def _dot(a, b):
    return jnp.dot(a, b, preferred_element_type=F32)


def _dot_nt(a, b):
    return lax.dot_general(a, b, (((1,), (1,)), ((), ())), preferred_element_type=F32)


def _split2_ldot(w, x):
    hi = x.astype(BF16)
    lo = (x - hi.astype(F32)).astype(BF16)
    return _dot(w, hi) + _dot(w, lo)


def _rms_rows(x, g):
    ms = jnp.mean(x * x, axis=-1, keepdims=True)
    return x * lax.rsqrt(ms + EPS) * g


def _seg_rms(x, seg, inv_n, g):
    ms = _dot((x * x).astype(BF16), seg) * inv_n
    return x * lax.rsqrt(ms + EPS) * g


class _Pool:
    def __init__(self):
        self.live = []
        self.out = {}

    def add(self, key, gen):
        self.live.append((key, gen))

    def round(self):
        still = []
        for key, gen in self.live:
            try:
                next(gen)
                still.append((key, gen))
            except StopIteration as done:
                self.out[key] = done.value
        self.live = still

    def has(self, keys):
        return all(k in self.out for k in keys)


def _const_spec(a):
    nd = a.ndim
    return pl.BlockSpec(a.shape, lambda *_: (0,) * nd, pipeline_mode=pl.Buffered(1))


def _ff_chunks(d_ff, step=1024):
    return tuple((lo, min(lo + step, d_ff)) for lo in range(0, d_ff, step))


def _swiglu(xn, w1_ref, w3_ref, w2_ref):
    acc = jnp.zeros((xn.shape[0], w2_ref.shape[1]), F32)
    for lo, hi in _ff_chunks(w1_ref.shape[1]):
        a = _dot(xn, w1_ref[:, lo:hi])
        b = _dot(xn, w3_ref[:, lo:hi])
        acc = acc + _dot((jax.nn.silu(a) * b).astype(BF16), w2_ref[lo:hi, :])
    return acc


def _ffn_kernel(x_ref, g1_ref, w1_ref, w3_ref, w2_ref, h_ref):
    x = x_ref[...]
    xn = _rms_rows(x, g1_ref[...]).astype(BF16)
    h_ref[...] = x + 0.5 * _swiglu(xn, w1_ref, w3_ref, w2_ref)


CAST_CHUNKS = 8


def _cast_copies(pairs, stage_refs, sem):
    out = []
    for (src, dst), stage in zip(pairs, stage_refs):
        rows = src.shape[0] // CAST_CHUNKS
        for c in range(CAST_CHUNKS):
            slot = len(out) % 2
            staged = stage.at[slot]
            out.append((pltpu.make_async_copy(src.at[pl.ds(c * rows, rows), :], staged, sem.at[slot]),
                        staged, dst.at[pl.ds(c * rows, rows), :]))
    return out


def _ffn_cast_kernel(x_ref, g1_ref, w1_hbm, w3_hbm, w2_hbm, h_ref, w1_out, w3_out, w2_out,
                     w1_ref, w3_ref, w2_ref, stage_in_ref, stage_out_ref, sem_in, sem_out):
    step = pl.program_id(0)
    resident = ((w1_ref, w1_out), (w3_ref, w3_out), (w2_ref, w2_out))
    writebacks = [pltpu.make_async_copy(src, dst, sem_out.at[k]) for k, (src, dst) in enumerate(resident)]

    @pl.when(step == 0)
    def _():
        copies = _cast_copies(((w1_hbm, w1_ref), (w3_hbm, w3_ref), (w2_hbm, w2_ref)),
                              (stage_in_ref, stage_in_ref, stage_out_ref), sem_in)
        copies[0][0].start()
        for j, (dma, staged, dst) in enumerate(copies):
            if j + 1 < len(copies):
                copies[j + 1][0].start()
            dma.wait()
            dst[...] = staged[...].astype(BF16)
        for wb in writebacks:
            wb.start()

    x = x_ref[...]
    xn = _rms_rows(x, g1_ref[...]).astype(BF16)
    h_ref[...] = x + 0.5 * _swiglu(xn, w1_ref, w3_ref, w2_ref)

    @pl.when(step == pl.num_programs(0) - 1)
    def _():
        for wb in writebacks:
            wb.wait()


def _ffn_cast(x, g1, w1, w3, w2, tm):
    t, d = x.shape
    d_ff = w1.shape[1]
    row = pl.BlockSpec((tm, d), lambda i: (i, 0))
    hbm = pl.BlockSpec(memory_space=pl.ANY)
    bf = lambda w: jax.ShapeDtypeStruct(w.shape, BF16)
    return pl.pallas_call(
        _ffn_cast_kernel,
        grid=(t // tm,),
        in_specs=[row, _const_spec(g1), hbm, hbm, hbm],
        out_specs=[row, hbm, hbm, hbm],
        out_shape=[jax.ShapeDtypeStruct((t, d), F32), bf(w1), bf(w3), bf(w2)],
        scratch_shapes=[pltpu.VMEM(w1.shape, BF16), pltpu.VMEM(w3.shape, BF16), pltpu.VMEM(w2.shape, BF16),
                        pltpu.VMEM((2, d // CAST_CHUNKS, d_ff), F32), pltpu.VMEM((2, d_ff // CAST_CHUNKS, d), F32),
                        pltpu.SemaphoreType.DMA((2,)), pltpu.SemaphoreType.DMA((3,))],
        compiler_params=pltpu.CompilerParams(dimension_semantics=("arbitrary",), vmem_limit_bytes=VMEM_LIMIT),
        name="ffn_cast",
    )(x, g1, w1, w3, w2)


def _ffn(x, g1, w1, w3, w2, tm):
    t, d = x.shape
    row = pl.BlockSpec((tm, d), lambda i: (i, 0))
    return pl.pallas_call(
        _ffn_kernel,
        grid=(t // tm,),
        in_specs=[row] + [_const_spec(a) for a in (g1, w1, w3, w2)],
        out_specs=row,
        out_shape=jax.ShapeDtypeStruct((t, d), F32),
        compiler_params=pltpu.CompilerParams(dimension_semantics=("parallel",), vmem_limit_bytes=VMEM_LIMIT),
        name="ffn",
    )(x, g1, w1, w3, w2)


def _t5_bucket(dist):
    max_exact = N_BUCKETS // 2
    d = np.maximum(dist, 0)
    large = max_exact + (np.log(np.maximum(d, 1) / max_exact) / np.log(MAX_DISTANCE / max_exact)
                         * (N_BUCKETS - max_exact)).astype(np.int32)
    large = np.minimum(large, N_BUCKETS - 1)
    return np.where(d < max_exact, d, large).astype(np.int32)


def _bias_kernel(table_ref, bucket_ref, out_ref):
    bucket = bucket_ref[...]
    masked = jnp.where(bucket < 0, NEG_BIG, 0.0).astype(F32)
    for h in range(N_Q_HEADS):
        acc = masked
        for b in range(N_BUCKETS):
            acc = acc + jnp.where(bucket == b, table_ref[b, h], 0.0)
        out_ref[h] = acc


def _rel_bias(table, bucket_map):
    r, c = bucket_map.shape
    return pl.pallas_call(
        _bias_kernel,
        in_specs=[pl.BlockSpec(memory_space=pltpu.SMEM), pl.BlockSpec((r, c), lambda: (0, 0))],
        out_specs=pl.BlockSpec((N_Q_HEADS, r, c), lambda: (0, 0, 0)),
        out_shape=jax.ShapeDtypeStruct((N_Q_HEADS, r, c), F32),
        name="rel_bias",
    )(table, jnp.asarray(bucket_map))


def _project_qkv(u, win_ref, qg, kg, seg):
    qs = []
    for s in range(SPLIT_SIZES[0] // LANES):
        x = _dot(u, win_ref[:, O_Q + s * LANES:O_Q + (s + 1) * LANES])
        qs.append(_seg_rms(x, seg, 1.0 / HEAD_DIM, qg).astype(BF16))
    k = _seg_rms(_dot(u, win_ref[:, O_K:O_V]), seg, 1.0 / HEAD_DIM, kg)
    v = _dot(u, win_ref[:, O_V:O_QR])
    return jnp.concatenate(qs, axis=1), k, v


def _stack_group(q, kv):
    a = q[:, (2 * kv) * LANES:(2 * kv + 1) * LANES]
    b = q[:, (2 * kv + 1) * LANES:(2 * kv + 2) * LANES]
    ar, br = pltpu.roll(a, HEAD_DIM, 1), pltpu.roll(b, HEAD_DIM, 1)
    parts = (a, ar, b, br) if kv == 0 else (ar, a, br, b)
    return jnp.concatenate(parts, axis=0)


def _kv_half(x, kv, fill):
    lane = lax.broadcasted_iota(jnp.int32, x.shape, 1)
    keep = (lane < HEAD_DIM) if kv == 0 else (lane >= HEAD_DIM)
    return jnp.where(keep, x, fill)


def _attend_chain(q, kk, vv, bias_ref, sink_ref, r, kv):
    qs = _stack_group(q.astype(F32), kv).astype(BF16)
    s = _dot_nt(qs, _kv_half(kk, kv, 0.0).astype(BF16)) + bias_ref[kv]
    yield
    sink = sink_ref[kv]
    m = jnp.maximum(jnp.max(s, axis=-1, keepdims=True), sink)
    e = jnp.exp2(s - m).astype(BF16)
    yield
    o = _dot(e, _kv_half(vv, kv, 1.0).astype(BF16))
    yield
    o = o / (pltpu.roll(o, HEAD_DIM, 1) + jnp.exp2(sink - m))
    lane = lax.broadcasted_iota(jnp.int32, (r, LANES), 1)
    slabs = []
    for pair in range(GROUP // 2):
        a, b = o[(2 * pair) * r:(2 * pair + 1) * r], o[(2 * pair + 1) * r:(2 * pair + 2) * r]
        if kv == 0:
            slabs.append(jnp.where(lane < HEAD_DIM, a, pltpu.roll(b, HEAD_DIM, 1)))
        else:
            slabs.append(jnp.where(lane < HEAD_DIM, pltpu.roll(a, HEAD_DIM, 1), b))
    return slabs


def _swa_sample_kernel(q_ref, kn_ref, vn_ref, ck_ref, cv_ref, bias_ref, sink_ref, o_ref, nk_ref, nv_ref, *, ld):
    pool = _Pool()
    nseq = ck_ref.shape[0]
    for i in range(nseq):
        rows = slice(i * ld, (i + 1) * ld)
        kk = jnp.concatenate([ck_ref[i], kn_ref[rows, :]], axis=0)
        vv = jnp.concatenate([cv_ref[i], vn_ref[rows, :]], axis=0)
        nk_ref[i] = kk[ld:, :]
        nv_ref[i] = vv[ld:, :]
        for kv in range(N_KV_HEADS):
            pool.add((i, kv), _attend_chain(q_ref[rows, :], kk, vv, bias_ref, sink_ref, ld, kv))
    while pool.live:
        pool.round()
    for i in range(nseq):
        slabs = [s for kv in range(N_KV_HEADS) for s in pool.out[(i, kv)]]
        o_ref[i * ld:(i + 1) * ld, :] = jnp.concatenate(slabs, axis=1).astype(o_ref.dtype)


def _swa_sample(q, k, v, cache_k, cache_v, bias, sink_col, ld, sb):
    bd = cache_k.shape[0]
    tok = lambda w: pl.BlockSpec((sb * ld, w), lambda i: (i, 0))
    cache = pl.BlockSpec((sb, WINDOW, LANES), lambda i: (i, 0, 0))
    return pl.pallas_call(
        functools.partial(_swa_sample_kernel, ld=ld),
        grid=(bd // sb,),
        in_specs=[tok(4 * LANES), tok(LANES), tok(LANES), cache, cache,
                  pl.BlockSpec(bias.shape, lambda i: (0, 0, 0)), pl.BlockSpec(sink_col.shape, lambda i: (0, 0, 0))],
        out_specs=[tok(4 * LANES), cache, cache],
        out_shape=[jax.ShapeDtypeStruct(q.shape, BF16), jax.ShapeDtypeStruct(cache_k.shape, F32),
                   jax.ShapeDtypeStruct(cache_v.shape, F32)],
        compiler_params=pltpu.CompilerParams(dimension_semantics=("parallel",)),
        name="swa_sample",
    )(q, k, v, cache_k, cache_v, bias, sink_col)


def _lower_bound(lb_logits):
    z = lb_logits - jnp.max(lb_logits, axis=0, keepdims=True)
    e = jnp.exp(z)
    return e[0:1, :] / jnp.sum(e, axis=0, keepdims=True)


def _gates(qr, fr, lb):
    f = lb + (1.0 - lb) * jax.nn.sigmoid(fr)
    return jax.nn.silu(qr), 1.0 - f, jnp.log2(jnp.maximum(f, F_TINY))


def _head_out(o, gr, hg_gain, ones_seg):
    return _seg_rms(o, ones_seg, 1.0 / HG_DIM, hg_gain) * jax.nn.silu(gr)


def _hgrn_proj(u, win_ref, pair):
    cols = lambda off: slice(off + 2 * pair * HG_DIM, off + 2 * (pair + 1) * HG_DIM)
    qr = _dot(u, win_ref[:, cols(O_QR)])
    fr = _dot(u, win_ref[:, cols(O_FR)])
    yield
    ir = _dot(u, win_ref[:, cols(O_IR)]).astype(BF16)
    gr = _dot(u, win_ref[:, cols(O_GR)])
    yield
    return qr, fr, ir, gr


def _hgrn_head(qr, fr, vals, gr, lb, hd, s_ref, s_out_ref, hg_gain, ones, tri):
    qf, kk, g = _gates(qr, fr, lb)
    nchunk = qr.shape[0] // HG_CHUNK
    chunks = [slice(c * HG_CHUNK, (c + 1) * HG_CHUNK) for c in range(nchunk)]
    yield
    gcs = [_split2_ldot(tri, g[rows]) for rows in chunks]
    yield
    q_in, upd, decay, scores = [], [], [], []
    worst = jnp.zeros((1, HG_DIM), F32)
    for rows, gc in zip(chunks, gcs):
        g_last = gc[HG_CHUNK - 1:HG_CHUNK, :]
        q_in.append((qf[rows] * jnp.exp2(gc)).astype(BF16))
        k_end_t = (kk[rows] * jnp.exp2(g_last - gc)).T
        upd.append(_dot(k_end_t.astype(BF16), vals[rows]))
        decay.append(jnp.broadcast_to(jnp.exp2(g_last), (HG_DIM, HG_DIM)).T)
        qc, kc = qf[rows], kk[rows]
        for i in range(HG_CHUNK // HG_SUB):
            lo, hi = i * HG_SUB, (i + 1) * HG_SUB
            g_base = gc[lo - 1:lo, :] if i else jnp.zeros((1, HG_DIM), F32)
            worst = jnp.minimum(worst, gc[hi - 1:hi, :] - g_base)
            q_hat = qc[lo:hi] * jnp.exp2(gc[lo:hi] - g_base)
            k_hat = kc[:hi] * jnp.exp2(jnp.minimum(g_base - gc[:hi], EXP_CLAMP))
            a = _dot_nt(q_hat.astype(BF16), k_hat.astype(BF16))
            row = lax.broadcasted_iota(jnp.int32, (HG_SUB, hi), 0)
            col = lax.broadcasted_iota(jnp.int32, (HG_SUB, hi), 1)
            scores.append(jnp.where(col <= row + lo, a, 0.0).astype(BF16))
    yield
    intra = []
    nsub = HG_CHUNK // HG_SUB
    for c, rows in enumerate(chunks):
        vc = vals[rows]
        parts = [_dot(scores[c * nsub + i], vc[:(i + 1) * HG_SUB]) for i in range(nsub)]
        intra.append(jnp.concatenate(parts, axis=0))
    yield
    s = s_ref[hd]
    inter = []
    for c in range(nchunk):
        inter.append(_dot(q_in[c], s.astype(BF16)))
        s = decay[c] * s + upd[c]
    s_ref[hd] = s
    s_out_ref[0, hd] = s
    yield
    o = jnp.concatenate([a + b for a, b in zip(inter, intra)], axis=0)
    redo = dict(qf=qf, kk=kk, gcs=gcs, vals=vals, inter=inter, gr=gr)
    return _head_out(o, gr, hg_gain, ones).astype(BF16), worst, redo


def _exact_intra(qc, kc, gc, vc):
    c = qc.shape[0]
    group = 8
    row = lax.broadcasted_iota(jnp.int32, (c, c), 0)
    col = lax.broadcasted_iota(jnp.int32, (c, c), 1)
    col_g = lax.broadcasted_iota(jnp.int32, (group, c), 1)
    earlier = []
    for i in range(c // group):
        lo, hi = i * group, (i + 1) * group
        g_base = gc[lo - 1:lo, :] if i else jnp.zeros((1, HG_DIM), F32)
        q_hat = qc[lo:hi] * jnp.exp2(gc[lo:hi] - g_base)
        k_hat = kc * jnp.exp2(jnp.minimum(g_base - gc, 0.0))
        earlier.append(jnp.where(col_g < lo, _dot_nt(q_hat.astype(BF16), k_hat.astype(BF16)), 0.0))
    a = jnp.concatenate(earlier, axis=0)
    qb = qc.astype(BF16)
    g3 = gc.reshape(c // group, group, HG_DIM)
    own = (col <= row) & (col >= row - lax.rem(row, group))
    for p in range(group):
        g_p = jnp.broadcast_to(g3[:, p:p + 1, :], g3.shape).reshape(c, HG_DIM)
        k_p = kc * jnp.exp2(jnp.minimum(g_p - gc, 0.0))
        a = a + jnp.where(own & (lax.rem(row, group) == p), _dot_nt(qb, k_p.astype(BF16)), 0.0)
    return _dot(a.astype(BF16), vc)


def _hgrn_redo(redo, hg_gain, ones):
    outs = []
    for c, gc in enumerate(redo["gcs"]):
        rows = slice(c * HG_CHUNK, (c + 1) * HG_CHUNK)
        outs.append(redo["inter"][c] + _exact_intra(redo["qf"][rows], redo["kk"][rows], gc, redo["vals"][rows]))
    return _head_out(jnp.concatenate(outs, axis=0), redo["gr"], hg_gain, ones).astype(BF16)


def _attn_chain(q_pair, k_half, v_t, bias_ref, sink_a, sink_b, kv):
    s = _dot_nt(k_half, q_pair) + bias_ref[...]
    yield
    col = lax.broadcasted_iota(jnp.int32, (1, s.shape[1]), 1)
    sink = jnp.where(col < WINDOW, sink_a, sink_b)
    m = jnp.maximum(jnp.max(s, axis=0, keepdims=True), sink)
    e = jnp.exp2(s - m)
    yield
    denom = jnp.sum(e, axis=0, keepdims=True) + jnp.exp2(sink - m)
    o_t = _dot(v_t, e.astype(BF16))[kv * HEAD_DIM:(kv + 1) * HEAD_DIM]
    yield
    return o_t * (1.0 / denom)


def _gate_stream(u, win_ref, step=256):
    res = []
    for off in (O_GA, O_GB):
        cols = []
        for lo in range(0, SPLIT_SIZES[7], step):
            cols.append(jax.nn.sigmoid(_dot(u, win_ref[:, off + lo:off + lo + step])))
            yield
        res.append(jnp.concatenate(cols, axis=1))
    return res


def _q_cols(u, win_ref, kv, seg2, gain2):
    x = _dot(u, win_ref[:, O_Q + 2 * kv * LANES:O_Q + 2 * (kv + 1) * LANES])
    yield
    ms = _dot((x * x).astype(BF16), seg2) * (1.0 / HEAD_DIM)
    yield
    return (x * lax.rsqrt(ms + EPS) * gain2).astype(BF16)


def _kv_cols(u, win_ref, seg, gain):
    x = _dot(u, win_ref[:, O_K:O_QR])
    yield
    k, v = x[:, :LANES], x[:, LANES:]
    ms = _dot((k * k).astype(BF16), seg) * (1.0 / HEAD_DIM)
    yield
    return k * lax.rsqrt(ms + EPS) * gain, v


def _attn_up(parts, wua_ref):
    att = jnp.concatenate(parts, axis=0).T.astype(BF16)
    yield
    return _dot(att, wua_ref[...])


def _hgrn_sample_kernel(qr_ref, fr_ref, ir_ref, gr_ref, s0_ref, lb_ref, hg_ref, tri_ref, tot_ref, ones_ref,
                        o_ref, s_out_ref, *, ld):
    nseq = s0_ref.shape[0]
    r = nseq * ld
    lb = _lower_bound(lb_ref[...])
    qf_all, kk_all, g_all = _gates(qr_ref[...], fr_ref[...], lb)
    tri, tot = tri_ref[...], tot_ref[...]
    causal = tri.astype(F32) > 0
    lane = lax.broadcasted_iota(jnp.int32, (HG_DIM, r), 1)
    pos = lax.rem(lax.broadcasted_iota(jnp.int32, (r, r), 0), ld)

    def head(hd):
        lanes = slice(hd * HG_DIM, (hd + 1) * HG_DIM)
        qf, kk, v = qf_all[:, lanes], kk_all[:, lanes], ir_ref[:, lanes]
        g = _split2_ldot(tri, g_all[:, lanes])
        g_last = _split2_ldot(tot, g_all[:, lanes])
        yield
        q_t = (qf * jnp.exp2(g)).astype(BF16)
        qb = qf.astype(BF16)
        g3 = g.reshape(nseq, ld, HG_DIM)
        a = jnp.zeros((r, r), F32)
        for p in range(ld):
            g_p = jnp.broadcast_to(g3[:, p:p + 1, :], g3.shape).reshape(r, HG_DIM)
            k_p = kk * jnp.exp2(jnp.minimum(g_p - g, 0.0))
            a = a + jnp.where(causal & (pos == p), _dot_nt(qb, k_p.astype(BF16)), 0.0)
        k_end_t = (kk * jnp.exp2(g_last - g)).T
        decay_t = jnp.exp2(g_last).T
        yield
        intra = _dot(a.astype(BF16), v)
        outs = []
        for i in range(nseq):
            s_prev = s0_ref[i, hd]
            outs.append(_dot(q_t[i * ld:(i + 1) * ld], s_prev.astype(BF16)))
            own = (lane >= i * ld) & (lane < (i + 1) * ld)
            upd = _dot(jnp.where(own, k_end_t, 0.0).astype(BF16), v)
            s_out_ref[i, hd] = decay_t[:, i * ld:i * ld + 1] * s_prev + upd
            if i % 2:
                yield
        o = jnp.concatenate(outs, axis=0) + intra
        yield
        o_ref[:, lanes] = _head_out(o, gr_ref[:, lanes], hg_ref[...], ones_ref[...]).astype(o_ref.dtype)

    pool = _Pool()
    for hd in range(HG_HEADS):
        pool.add(hd, head(hd))
    while pool.live:
        pool.round()


def _hgrn_sample(qr, fr, ir, gr, s0, lb_logits, hg_gain, ld, sb):
    bd = s0.shape[0]
    r = sb * ld
    seq = np.arange(r) // ld
    same = seq[:, None] == seq[None, :]
    tri = jnp.asarray((same & (np.arange(r)[:, None] >= np.arange(r)[None, :])).astype(np.float32), BF16)
    tot = jnp.asarray(same.astype(np.float32), BF16)
    ones = jnp.ones((HG_DIM, HG_DIM), BF16)
    tok = pl.BlockSpec((r, HG_HEADS * HG_DIM), lambda i: (i, 0))
    state = pl.BlockSpec((sb, HG_HEADS, HG_DIM, HG_DIM), lambda i: (i, 0, 0, 0))
    const = lambda a: pl.BlockSpec(a.shape, lambda i: (0,) * a.ndim)
    return pl.pallas_call(
        functools.partial(_hgrn_sample_kernel, ld=ld),
        grid=(bd // sb,),
        in_specs=[tok, tok, tok, tok, state, const(lb_logits), const(hg_gain), const(tri), const(tot), const(ones)],
        out_specs=[tok, state],
        out_shape=[jax.ShapeDtypeStruct(qr.shape, BF16), jax.ShapeDtypeStruct(s0.shape, F32)],
        compiler_params=pltpu.CompilerParams(dimension_semantics=("parallel",)),
        name="hgrn_sample",
    )(qr, fr, ir, gr, s0, lb_logits, hg_gain, tri, tot, ones)


def _merge_out(h, sga, sgb, att, orr, wua_ref, wur_ref, wo_ref):
    merged = sga * _dot(att, wua_ref[...]) + sgb * _dot(orr, wur_ref[...])
    return h + _dot(merged.astype(BF16), wo_ref[...])


def _mix_kernel(h_ref, gm_ref, win_ref, qg_ref, kg_ref, seg_ref, seg2_ref, bias_ref, sink_ref, lb_ref, hg_ref,
                tri_ref, ones_ref, wua_ref, wur_ref, wo_ref,
                h2_ref, k_out_ref, v_out_ref, s_out_ref, kprev_ref, vprev_ref, s_ref):
    tm = h_ref.shape[0]
    first = pl.program_id(1) == 0

    @pl.when(first)
    def _():
        kprev_ref[...] = jnp.zeros(kprev_ref.shape, F32)
        vprev_ref[...] = jnp.zeros(vprev_ref.shape, F32)
        s_ref[...] = jnp.zeros(s_ref.shape, F32)

    h = h_ref[...]
    u = _rms_rows(h, gm_ref[...]).astype(BF16)

    pool = _Pool()
    lb = _lower_bound(lb_ref[...])
    for kv in range(N_KV_HEADS):
        pool.add(("q", kv), _q_cols(u, win_ref, kv, seg2_ref[...], qg_ref[...]))
    pool.add("kv", _kv_cols(u, win_ref, seg_ref[...], kg_ref[...]))
    for pair in range(HG_HEADS // 2):
        pool.add(("hproj", pair), _hgrn_proj(u, win_ref, pair))
    pool.add("gates", _gate_stream(u, win_ref))
    first_keys = [("q", kv) for kv in range(N_KV_HEADS)] + ["kv"] + [("hproj", p) for p in range(HG_HEADS // 2)]
    while not pool.has(first_keys):
        pool.round()
    for hd in range(HG_HEADS):
        qr, fr, ir, gr = (a[:, (hd % 2) * HG_DIM:(hd % 2 + 1) * HG_DIM] for a in pool.out[("hproj", hd // 2)])
        pool.add(("hgrn", hd), _hgrn_head(qr, fr, ir, gr, lb[:, hd * HG_DIM:(hd + 1) * HG_DIM], hd, s_ref, s_out_ref,
                                          hg_ref[...], ones_ref[...], tri_ref[...]))

    q = [pool.out[("q", j // 2)][:, (j % 2) * LANES:(j % 2 + 1) * LANES] for j in range(SPLIT_SIZES[0] // LANES)]
    k, v = pool.out["kv"]
    kk_all = jnp.concatenate([kprev_ref[...], k], axis=0)
    vv_all = jnp.concatenate([vprev_ref[...], v], axis=0)
    k_tail, v_tail = k[tm - WINDOW:], v[tm - WINDOW:]
    kprev_ref[...] = k_tail
    vprev_ref[...] = v_tail
    k_out_ref[...] = k_tail
    v_out_ref[...] = v_tail
    lane = lax.broadcasted_iota(jnp.int32, kk_all.shape, 1)
    k_lo0 = jnp.where(lane < HEAD_DIM, kk_all, 0.0)
    k_hi1 = jnp.where(lane >= HEAD_DIM, kk_all, 0.0)
    k_halves = ((k_lo0.astype(BF16), pltpu.roll(k_lo0, HEAD_DIM, 1).astype(BF16)),
                (pltpu.roll(k_hi1, HEAD_DIM, 1).astype(BF16), k_hi1.astype(BF16)))
    v_t = vv_all.T.astype(BF16)

    n_blk = tm // WINDOW
    chain_keys = lambda blk: [("att", blk, kv, half) for kv in range(N_KV_HEADS) for half in range(2)]
    for blk in range(n_blk):
        lo = blk * WINDOW
        variant = jnp.where(first, 0, 1) if blk == 0 else 1
        for kv in range(N_KV_HEADS):
            q_pair = jnp.concatenate([q[2 * kv + j][lo:lo + WINDOW] for j in range(GROUP // 2)], axis=0)
            for half in range(2):
                pool.add(("att", blk, kv, half),
                         _attn_chain(q_pair, k_halves[kv][half][lo:lo + 2 * WINDOW], v_t[:, lo:lo + 2 * WINDOW],
                                     bias_ref.at[variant, kv, half], sink_ref[GROUP * kv + half],
                                     sink_ref[GROUP * kv + 2 + half], kv))

    waiting = list(range(n_blk))
    while pool.live:
        pool.round()
        for blk in [b for b in waiting if pool.has(chain_keys(b))]:
            waiting.remove(blk)
            parts = []
            for hq in range(N_Q_HEADS):
                kv, j, half = hq // GROUP, (hq % GROUP) // 2, hq % 2
                parts.append(pool.out[("att", blk, kv, half)][:, j * WINDOW:(j + 1) * WINDOW])
            pool.add(("up", blk), _attn_up(parts, wua_ref))

    up_a = jnp.concatenate([pool.out[("up", blk)] for blk in range(n_blk)], axis=0)
    sga, sgb = pool.out["gates"]
    heads = [pool.out[("hgrn", hd)] for hd in range(HG_HEADS)]

    def finish(orr_heads):
        orr = jnp.concatenate(orr_heads, axis=1)
        merged = sga * up_a + sgb * _dot(orr, wur_ref[...])
        h2_ref[...] = h + _dot(merged.astype(BF16), wo_ref[...])

    finish([o for o, _, _ in heads])

    worst = functools.reduce(jnp.minimum, [w for _, w, _ in heads])

    @pl.when(jnp.min(worst) < -EXP_CLAMP)
    def _():
        finish([_hgrn_redo(redo, hg_ref[...], ones_ref[...]) for _, _, redo in heads])


def _mix(h, consts, batch, seq, tm):
    t, d = h.shape
    nt = seq // tm
    row = pl.BlockSpec((tm, d), lambda b, i: (b * nt + i, 0))
    win_out = pl.BlockSpec((WINDOW, LANES), lambda b, i: (b, 0))
    return pl.pallas_call(
        _mix_kernel,
        grid=(batch, nt),
        in_specs=[row] + [pl.BlockSpec(memory_space=pltpu.SMEM) if a.ndim == 1 else _const_spec(a) for a in consts],
        out_specs=[row, win_out, win_out, pl.BlockSpec((1, HG_HEADS, HG_DIM, HG_DIM), lambda b, i: (b, 0, 0, 0))],
        out_shape=[jax.ShapeDtypeStruct((t, d), F32),
                   jax.ShapeDtypeStruct((batch * WINDOW, LANES), F32),
                   jax.ShapeDtypeStruct((batch * WINDOW, LANES), F32),
                   jax.ShapeDtypeStruct((batch, HG_HEADS, HG_DIM, HG_DIM), F32)],
        scratch_shapes=[pltpu.VMEM((WINDOW, LANES), F32), pltpu.VMEM((WINDOW, LANES), F32),
                        pltpu.VMEM((HG_HEADS, HG_DIM, HG_DIM), F32)],
        compiler_params=pltpu.CompilerParams(dimension_semantics=("parallel", "arbitrary"),
                                             vmem_limit_bytes=VMEM_LIMIT),
        name="mix",
    )(h, *consts)


def _proj_kernel(h_ref, gm_ref, win_ref, qg_ref, kg_ref, seg_ref,
                 q_ref, k_ref, v_ref, qr_ref, fr_ref, ir_ref, gr_ref):
    u = _rms_rows(h_ref[...], gm_ref[...]).astype(BF16)
    q, k, v = _project_qkv(u, win_ref, qg_ref[...], kg_ref[...], seg_ref[...])
    q_ref[...] = q
    k_ref[...] = k
    v_ref[...] = v
    qr_ref[...] = _dot(u, win_ref[:, O_QR:O_FR])
    fr_ref[...] = _dot(u, win_ref[:, O_FR:O_IR])
    ir_ref[...] = _dot(u, win_ref[:, O_IR:O_GR]).astype(ir_ref.dtype)
    gr_ref[...] = _dot(u, win_ref[:, O_GR:O_GA])


def _proj(h, gm, w_in, qg, kg, seg, tm):
    t, d = h.shape
    row = lambda w: pl.BlockSpec((tm, w), lambda i: (i, 0))
    widths = SPLIT_SIZES[:7]
    dtypes = (BF16, F32, F32, F32, F32, BF16, F32)
    return pl.pallas_call(
        _proj_kernel,
        grid=(t // tm,),
        in_specs=[row(d)] + [_const_spec(a) for a in (gm, w_in, qg, kg, seg)],
        out_specs=[row(w) for w in widths],
        out_shape=[jax.ShapeDtypeStruct((t, w), dt) for w, dt in zip(widths, dtypes)],
        compiler_params=pltpu.CompilerParams(dimension_semantics=("parallel",), vmem_limit_bytes=VMEM_LIMIT),
        name="proj",
    )(h, gm, w_in, qg, kg, seg)


def _out_kernel(h_ref, att_ref, orr_ref, gm_ref, win_ref, wua_ref, wur_ref, wo_ref, h2_ref):
    h = h_ref[...]
    u = _rms_rows(h, gm_ref[...]).astype(BF16)
    sga = jax.nn.sigmoid(_dot(u, win_ref[:, O_GA:O_GB]))
    sgb = jax.nn.sigmoid(_dot(u, win_ref[:, O_GB:O_END]))
    h2_ref[...] = _merge_out(h, sga, sgb, att_ref[...], orr_ref[...], wua_ref, wur_ref, wo_ref)


def _out(h, att, orr, consts, tm):
    t, d = h.shape
    row = lambda a: pl.BlockSpec((tm, a.shape[1]), lambda i: (i, 0))
    return pl.pallas_call(
        _out_kernel,
        grid=(t // tm,),
        in_specs=[row(a) for a in (h, att, orr)] + [_const_spec(a) for a in consts],
        out_specs=pl.BlockSpec((tm, d), lambda i: (i, 0)),
        out_shape=jax.ShapeDtypeStruct((t, d), F32),
        compiler_params=pltpu.CompilerParams(dimension_semantics=("parallel",), vmem_limit_bytes=VMEM_LIMIT),
        name="out",
    )(h, att, orr, *consts)


def _bucket_map():
    dist = np.arange(WINDOW)[:, None] + WINDOW - np.arange(2 * WINDOW)[None, :]
    valid = (dist >= 0) & (dist <= WINDOW)
    return np.where(valid, _t5_bucket(dist), -1).astype(np.int32)


def _stacked(bias, rows):
    return bias.reshape(N_KV_HEADS, GROUP * rows, bias.shape[-1])


def kernel(x_prompt, x_sample, cache_win_k, cache_win_v, state_hgrn, ffn1_norm, ffn1_w1, ffn1_w3, ffn1_w2, mix_norm, w_in, q_norm, k_norm, sinks, rel_bias_table, hgrn_lb_logits, hg_norm, w_up_attn, w_up_hgrn, w_out, ffn2_norm, ffn2_w1, ffn2_w3, ffn2_w2):
    depth = ffn1_norm.shape[0]
    assert depth == 1 and hgrn_lb_logits.shape[0] == 2, "single-layer step only"
    batch, seq, d = x_prompt.shape
    bd, ld, _ = x_sample.shape

    bf = lambda w: w[0].astype(BF16)
    row = lambda g: g[0].reshape(1, -1).astype(F32)
    w_in_b = bf(w_in)
    wua, wur, wo = bf(w_up_attn), bf(w_up_hgrn), bf(w_out)
    g1, gm, g2 = row(ffn1_norm), row(mix_norm), row(ffn2_norm)
    qg = jnp.tile(row(q_norm), (1, LANES // HEAD_DIM)) * (HEAD_DIM ** -0.5 * LOG2E)
    kg = jnp.tile(row(k_norm), (1, LANES // HEAD_DIM))
    head_of_lane = np.arange(2 * LANES) // HEAD_DIM
    seg2 = jnp.asarray((head_of_lane[:, None] == head_of_lane[None, :]).astype(np.float32), BF16)
    seg = seg2[:LANES, :LANES]
    qg2 = jnp.tile(qg, (1, 2))
    ones = jnp.ones((HG_DIM, HG_DIM), BF16)
    tri = jnp.asarray(np.tril(np.ones((HG_CHUNK, HG_CHUNK), np.float32)), BF16)
    lb_logits = hgrn_lb_logits.astype(F32)
    hg_gain = row(hg_norm)

    later = _rel_bias(rel_bias_table.astype(F32) * LOG2E, np.ascontiguousarray(_bucket_map().T))
    first = jnp.where(np.arange(2 * WINDOW)[None, :, None] >= WINDOW, later, NEG_BIG)
    def paired(b):
        b = b.reshape(N_KV_HEADS, GROUP // 2, 2, 2 * WINDOW, WINDOW)
        return b.transpose(0, 2, 3, 1, 4).reshape(N_KV_HEADS, 2, 2 * WINDOW, 2 * WINDOW)
    bias_p = jnp.stack([paired(first), paired(later)])
    bias_s = _stacked(later[:, :WINDOW + ld, :ld].transpose(0, 2, 1), ld)
    sink_1d = sinks[0].astype(F32) * LOG2E
    sink = sink_1d.reshape(N_KV_HEADS, GROUP, 1)
    sink_s = jnp.repeat(sink, ld, axis=1).reshape(N_KV_HEADS, GROUP * ld, 1)

    h_p, w1a, w3a, w2a = _ffn_cast(x_prompt.reshape(batch * seq, d), g1, ffn1_w1[0], ffn1_w3[0], ffn1_w2[0], TM_FFN)
    h_s = _ffn(x_sample.reshape(bd * ld, d), g1, w1a, w3a, w2a, TM_FFN)

    consts = (gm, w_in_b, qg2, kg, seg, seg2, bias_p, sink_1d, lb_logits, hg_gain, tri, ones, wua, wur, wo)
    h2_p, k_p, v_p, s_p = _mix(h_p, consts, batch, seq, TM_MIX)
    win = lambda a: a.reshape(1, batch, WINDOW, N_KV_HEADS, HEAD_DIM)

    q, k, v, qr, fr, ir, gr = _proj(h_s, gm, w_in_b, qg, kg, seg, TM_SAMPLE)
    ck = cache_win_k[0].reshape(bd, WINDOW, LANES)
    cv = cache_win_v[0].reshape(bd, WINDOW, LANES)
    att, nk, nv = _swa_sample(q, k, v, ck, cv, bias_s, sink_s, ld, SEQS_PER_STEP)
    orr, s_s = _hgrn_sample(qr, fr, ir, gr, state_hgrn[0], lb_logits, hg_gain, ld, SEQS_PER_STEP)
    h2_s = _out(h_s, att, orr, (gm, w_in_b, wua, wur, wo), TM_SAMPLE)

    y_p, w1b, w3b, w2b = _ffn_cast(h2_p, g2, ffn2_w1[0], ffn2_w3[0], ffn2_w2[0], TM_FFN)
    y_s = _ffn(h2_s, g2, w1b, w3b, w2b, TM_FFN)
    unwin = lambda a: a.reshape(1, bd, WINDOW, N_KV_HEADS, HEAD_DIM)
    return (y_p.reshape(batch, seq, d), y_s.reshape(bd, ld, d), win(k_p), win(v_p), s_p[None],
            unwin(nk), unwin(nv), s_s[None])
```

```python
import functools

import numpy as np
import jax
import jax.numpy as jnp
from jax import lax
from jax.experimental import pallas as pl
from jax.experimental.pallas import tpu as pltpu

F32 = jnp.float32
BF16 = jnp.bfloat16

LANES = 128
HEAD_DIM = 64
N_Q_HEADS = 8
N_KV_HEADS = 2
GROUP = N_Q_HEADS // N_KV_HEADS
WINDOW = 128
N_BUCKETS = 32
MAX_DISTANCE = 128
HG_HEADS = 4
HG_DIM = 128
HG_CHUNK = 128
HG_SUB = 32
EPS = 1e-6
F_TINY = 1e-30
NEG_BIG = -1e30
EXP_CLAMP = 100.0
LOG2E = 1.4426950408889634
SPLIT_SIZES = (512, 128, 128, 512, 512, 512, 512, 1024, 1024)
SPLIT_OFFS = tuple(int(v) for v in np.cumsum((0,) + SPLIT_SIZES))
O_Q, O_K, O_V, O_QR, O_FR, O_IR, O_GR, O_GA, O_GB, O_END = SPLIT_OFFS
VMEM_LIMIT = 56 * 1024 * 1024
TM_FFN = 1024
TM_MIX = 512
TM_SAMPLE = 256
SEQS_PER_STEP = 16


def _dot(a, b):
    return jnp.dot(a, b, preferred_element_type=F32)


def _dot_nt(a, b):
    return lax.dot_general(a, b, (((1,), (1,)), ((), ())), preferred_element_type=F32)


def _split2_ldot(w, x):
    hi = x.astype(BF16)
    lo = (x - hi.astype(F32)).astype(BF16)
    return _dot(w, hi) + _dot(w, lo)


def _rms_rows(x, g):
    ms = jnp.mean(x * x, axis=-1, keepdims=True)
    return x * lax.rsqrt(ms + EPS) * g


def _seg_rms(x, seg, inv_n, g):
    ms = _dot((x * x).astype(BF16), seg) * inv_n
    return x * lax.rsqrt(ms + EPS) * g


class _Pool:
    def __init__(self):
        self.live = []
        self.out = {}

    def add(self, key, gen):
        self.live.append((key, gen))

    def round(self):
        still = []
        for key, gen in self.live:
            try:
                next(gen)
                still.append((key, gen))
            except StopIteration as done:
                self.out[key] = done.value
        self.live = still

    def has(self, keys):
        return all(k in self.out for k in keys)


def _const_spec(a):
    nd = a.ndim
    return pl.BlockSpec(a.shape, lambda *_: (0,) * nd, pipeline_mode=pl.Buffered(1))


def _ff_chunks(d_ff, step=1024):
    return tuple((lo, min(lo + step, d_ff)) for lo in range(0, d_ff, step))


def _swiglu(xn, w1_ref, w3_ref, w2_ref):
    acc = jnp.zeros((xn.shape[0], w2_ref.shape[1]), F32)
    for lo, hi in _ff_chunks(w1_ref.shape[1]):
        a = _dot(xn, w1_ref[:, lo:hi])
        b = _dot(xn, w3_ref[:, lo:hi])
        acc = acc + _dot((jax.nn.silu(a) * b).astype(BF16), w2_ref[lo:hi, :])
    return acc


def _ffn_kernel(x_ref, g1_ref, w1_ref, w3_ref, w2_ref, h_ref):
    x = x_ref[...]
    xn = _rms_rows(x, g1_ref[...]).astype(BF16)
    h_ref[...] = x + 0.5 * _swiglu(xn, w1_ref, w3_ref, w2_ref)


CAST_CHUNKS = 8


def _cast_copies(pairs, stage_refs, sem):
    out = []
    for (src, dst), stage in zip(pairs, stage_refs):
        rows = src.shape[0] // CAST_CHUNKS
        for c in range(CAST_CHUNKS):
            slot = len(out) % 2
            staged = stage.at[slot]
            out.append((pltpu.make_async_copy(src.at[pl.ds(c * rows, rows), :], staged, sem.at[slot]),
                        staged, dst.at[pl.ds(c * rows, rows), :]))
    return out


def _ffn_cast_kernel(x_ref, g1_ref, w1_hbm, w3_hbm, w2_hbm, h_ref, w1_out, w3_out, w2_out,
                     w1_ref, w3_ref, w2_ref, stage_in_ref, stage_out_ref, sem_in, sem_out):
    step = pl.program_id(0)
    resident = ((w1_ref, w1_out), (w3_ref, w3_out), (w2_ref, w2_out))
    writebacks = [pltpu.make_async_copy(src, dst, sem_out.at[k]) for k, (src, dst) in enumerate(resident)]

    @pl.when(step == 0)
    def _():
        copies = _cast_copies(((w1_hbm, w1_ref), (w3_hbm, w3_ref), (w2_hbm, w2_ref)),
                              (stage_in_ref, stage_in_ref, stage_out_ref), sem_in)
        copies[0][0].start()
        for j, (dma, staged, dst) in enumerate(copies):
            if j + 1 < len(copies):
                copies[j + 1][0].start()
            dma.wait()
            dst[...] = staged[...].astype(BF16)
        for wb in writebacks:
            wb.start()

    x = x_ref[...]
    xn = _rms_rows(x, g1_ref[...]).astype(BF16)
    h_ref[...] = x + 0.5 * _swiglu(xn, w1_ref, w3_ref, w2_ref)

    @pl.when(step == pl.num_programs(0) - 1)
    def _():
        for wb in writebacks:
            wb.wait()


def _ffn_cast(x, g1, w1, w3, w2, tm):
    t, d = x.shape
    d_ff = w1.shape[1]
    row = pl.BlockSpec((tm, d), lambda i: (i, 0))
    hbm = pl.BlockSpec(memory_space=pl.ANY)
    bf = lambda w: jax.ShapeDtypeStruct(w.shape, BF16)
    return pl.pallas_call(
        _ffn_cast_kernel,
        grid=(t // tm,),
        in_specs=[row, _const_spec(g1), hbm, hbm, hbm],
        out_specs=[row, hbm, hbm, hbm],
        out_shape=[jax.ShapeDtypeStruct((t, d), F32), bf(w1), bf(w3), bf(w2)],
        scratch_shapes=[pltpu.VMEM(w1.shape, BF16), pltpu.VMEM(w3.shape, BF16), pltpu.VMEM(w2.shape, BF16),
                        pltpu.VMEM((2, d // CAST_CHUNKS, d_ff), F32), pltpu.VMEM((2, d_ff // CAST_CHUNKS, d), F32),
                        pltpu.SemaphoreType.DMA((2,)), pltpu.SemaphoreType.DMA((3,))],
        compiler_params=pltpu.CompilerParams(dimension_semantics=("arbitrary",), vmem_limit_bytes=VMEM_LIMIT),
        name="ffn_cast",
    )(x, g1, w1, w3, w2)


def _ffn(x, g1, w1, w3, w2, tm):
    t, d = x.shape
    row = pl.BlockSpec((tm, d), lambda i: (i, 0))
    return pl.pallas_call(
        _ffn_kernel,
        grid=(t // tm,),
        in_specs=[row] + [_const_spec(a) for a in (g1, w1, w3, w2)],
        out_specs=row,
        out_shape=jax.ShapeDtypeStruct((t, d), F32),
        compiler_params=pltpu.CompilerParams(dimension_semantics=("parallel",), vmem_limit_bytes=VMEM_LIMIT),
        name="ffn",
    )(x, g1, w1, w3, w2)


def _t5_bucket(dist):
    max_exact = N_BUCKETS // 2
    d = np.maximum(dist, 0)
    large = max_exact + (np.log(np.maximum(d, 1) / max_exact) / np.log(MAX_DISTANCE / max_exact)
                         * (N_BUCKETS - max_exact)).astype(np.int32)
    large = np.minimum(large, N_BUCKETS - 1)
    return np.where(d < max_exact, d, large).astype(np.int32)


def _bias_kernel(table_ref, bucket_ref, out_ref):
    bucket = bucket_ref[...]
    masked = jnp.where(bucket < 0, NEG_BIG, 0.0).astype(F32)
    for h in range(N_Q_HEADS):
        acc = masked
        for b in range(N_BUCKETS):
            acc = acc + jnp.where(bucket == b, table_ref[b, h], 0.0)
        out_ref[h] = acc


def _rel_bias(table, bucket_map):
    r, c = bucket_map.shape
    return pl.pallas_call(
        _bias_kernel,
        in_specs=[pl.BlockSpec(memory_space=pltpu.SMEM), pl.BlockSpec((r, c), lambda: (0, 0))],
        out_specs=pl.BlockSpec((N_Q_HEADS, r, c), lambda: (0, 0, 0)),
        out_shape=jax.ShapeDtypeStruct((N_Q_HEADS, r, c), F32),
        name="rel_bias",
    )(table, jnp.asarray(bucket_map))


def _project_qkv(u, win_ref, qg, kg, seg):
    qs = []
    for s in range(SPLIT_SIZES[0] // LANES):
        x = _dot(u, win_ref[:, O_Q + s * LANES:O_Q + (s + 1) * LANES])
        qs.append(_seg_rms(x, seg, 1.0 / HEAD_DIM, qg).astype(BF16))
    k = _seg_rms(_dot(u, win_ref[:, O_K:O_V]), seg, 1.0 / HEAD_DIM, kg)
    v = _dot(u, win_ref[:, O_V:O_QR])
    return jnp.concatenate(qs, axis=1), k, v


def _stack_group(q, kv):
    a = q[:, (2 * kv) * LANES:(2 * kv + 1) * LANES]
    b = q[:, (2 * kv + 1) * LANES:(2 * kv + 2) * LANES]
    ar, br = pltpu.roll(a, HEAD_DIM, 1), pltpu.roll(b, HEAD_DIM, 1)
    parts = (a, ar, b, br) if kv == 0 else (ar, a, br, b)
    return jnp.concatenate(parts, axis=0)


def _kv_half(x, kv, fill):
    lane = lax.broadcasted_iota(jnp.int32, x.shape, 1)
    keep = (lane < HEAD_DIM) if kv == 0 else (lane >= HEAD_DIM)
    return jnp.where(keep, x, fill)


def _attend_chain(q, kk, vv, bias_ref, sink_ref, r, kv):
    qs = _stack_group(q.astype(F32), kv).astype(BF16)
    s = _dot_nt(qs, _kv_half(kk, kv, 0.0).astype(BF16)) + bias_ref[kv]
    yield
    sink = sink_ref[kv]
    m = jnp.maximum(jnp.max(s, axis=-1, keepdims=True), sink)
    e = jnp.exp2(s - m).astype(BF16)
    yield
    o = _dot(e, _kv_half(vv, kv, 1.0).astype(BF16))
    yield
    o = o / (pltpu.roll(o, HEAD_DIM, 1) + jnp.exp2(sink - m))
    lane = lax.broadcasted_iota(jnp.int32, (r, LANES), 1)
    slabs = []
    for pair in range(GROUP // 2):
        a, b = o[(2 * pair) * r:(2 * pair + 1) * r], o[(2 * pair + 1) * r:(2 * pair + 2) * r]
        if kv == 0:
            slabs.append(jnp.where(lane < HEAD_DIM, a, pltpu.roll(b, HEAD_DIM, 1)))
        else:
            slabs.append(jnp.where(lane < HEAD_DIM, pltpu.roll(a, HEAD_DIM, 1), b))
    return slabs


def _swa_sample_kernel(q_ref, kn_ref, vn_ref, ck_ref, cv_ref, bias_ref, sink_ref, o_ref, nk_ref, nv_ref, *, ld):
    pool = _Pool()
    nseq = ck_ref.shape[0]
    for i in range(nseq):
        rows = slice(i * ld, (i + 1) * ld)
        kk = jnp.concatenate([ck_ref[i], kn_ref[rows, :]], axis=0)
        vv = jnp.concatenate([cv_ref[i], vn_ref[rows, :]], axis=0)
        nk_ref[i] = kk[ld:, :]
        nv_ref[i] = vv[ld:, :]
        for kv in range(N_KV_HEADS):
            pool.add((i, kv), _attend_chain(q_ref[rows, :], kk, vv, bias_ref, sink_ref, ld, kv))
    while pool.live:
        pool.round()
    for i in range(nseq):
        slabs = [s for kv in range(N_KV_HEADS) for s in pool.out[(i, kv)]]
        o_ref[i * ld:(i + 1) * ld, :] = jnp.concatenate(slabs, axis=1).astype(o_ref.dtype)


def _swa_sample(q, k, v, cache_k, cache_v, bias, sink_col, ld, sb):
    bd = cache_k.shape[0]
    tok = lambda w: pl.BlockSpec((sb * ld, w), lambda i: (i, 0))
    cache = pl.BlockSpec((sb, WINDOW, LANES), lambda i: (i, 0, 0))
    return pl.pallas_call(
        functools.partial(_swa_sample_kernel, ld=ld),
        grid=(bd // sb,),
        in_specs=[tok(4 * LANES), tok(LANES), tok(LANES), cache, cache,
                  pl.BlockSpec(bias.shape, lambda i: (0, 0, 0)), pl.BlockSpec(sink_col.shape, lambda i: (0, 0, 0))],
        out_specs=[tok(4 * LANES), cache, cache],
        out_shape=[jax.ShapeDtypeStruct(q.shape, BF16), jax.ShapeDtypeStruct(cache_k.shape, F32),
                   jax.ShapeDtypeStruct(cache_v.shape, F32)],
        compiler_params=pltpu.CompilerParams(dimension_semantics=("parallel",)),
        name="swa_sample",
    )(q, k, v, cache_k, cache_v, bias, sink_col)


def _lower_bound(lb_logits):
    z = lb_logits - jnp.max(lb_logits, axis=0, keepdims=True)
    e = jnp.exp(z)
    return e[0:1, :] / jnp.sum(e, axis=0, keepdims=True)


def _gates(qr, fr, lb):
    f = lb + (1.0 - lb) * jax.nn.sigmoid(fr)
    return jax.nn.silu(qr), 1.0 - f, jnp.log2(jnp.maximum(f, F_TINY))


def _head_out(o, gr, hg_gain, ones_seg):
    return _seg_rms(o, ones_seg, 1.0 / HG_DIM, hg_gain) * jax.nn.silu(gr)


def _hgrn_proj(u, win_ref, pair):
    cols = lambda off: slice(off + 2 * pair * HG_DIM, off + 2 * (pair + 1) * HG_DIM)
    qr = _dot(u, win_ref[:, cols(O_QR)])
    fr = _dot(u, win_ref[:, cols(O_FR)])
    yield
    ir = _dot(u, win_ref[:, cols(O_IR)]).astype(BF16)
    gr = _dot(u, win_ref[:, cols(O_GR)])
    yield
    return qr, fr, ir, gr


def _hgrn_head(qr, fr, vals, gr, lb, hd, s_ref, s_out_ref, hg_gain, ones, tri):
    qf, kk, g = _gates(qr, fr, lb)
    nchunk = qr.shape[0] // HG_CHUNK
    chunks = [slice(c * HG_CHUNK, (c + 1) * HG_CHUNK) for c in range(nchunk)]
    yield
    gcs = [_split2_ldot(tri, g[rows]) for rows in chunks]
    yield
    q_in, upd, decay, scores = [], [], [], []
    worst = jnp.zeros((1, HG_DIM), F32)
    for rows, gc in zip(chunks, gcs):
        g_last = gc[HG_CHUNK - 1:HG_CHUNK, :]
        q_in.append((qf[rows] * jnp.exp2(gc)).astype(BF16))
        k_end_t = (kk[rows] * jnp.exp2(g_last - gc)).T
        upd.append(_dot(k_end_t.astype(BF16), vals[rows]))
        decay.append(jnp.broadcast_to(jnp.exp2(g_last), (HG_DIM, HG_DIM)).T)
        qc, kc = qf[rows], kk[rows]
        for i in range(HG_CHUNK // HG_SUB):
            lo, hi = i * HG_SUB, (i + 1) * HG_SUB
            g_base = gc[lo - 1:lo, :] if i else jnp.zeros((1, HG_DIM), F32)
            worst = jnp.minimum(worst, gc[hi - 1:hi, :] - g_base)
            q_hat = qc[lo:hi] * jnp.exp2(gc[lo:hi] - g_base)
            k_hat = kc[:hi] * jnp.exp2(jnp.minimum(g_base - gc[:hi], EXP_CLAMP))
            a = _dot_nt(q_hat.astype(BF16), k_hat.astype(BF16))
            row = lax.broadcasted_iota(jnp.int32, (HG_SUB, hi), 0)
            col = lax.broadcasted_iota(jnp.int32, (HG_SUB, hi), 1)
            scores.append(jnp.where(col <= row + lo, a, 0.0).astype(BF16))
    yield
    intra = []
    nsub = HG_CHUNK // HG_SUB
    for c, rows in enumerate(chunks):
        vc = vals[rows]
        parts = [_dot(scores[c * nsub + i], vc[:(i + 1) * HG_SUB]) for i in range(nsub)]
        intra.append(jnp.concatenate(parts, axis=0))
    yield
    s = s_ref[hd]
    inter = []
    for c in range(nchunk):
        inter.append(_dot(q_in[c], s.astype(BF16)))
        s = decay[c] * s + upd[c]
    s_ref[hd] = s
    s_out_ref[0, hd] = s
    yield
    o = jnp.concatenate([a + b for a, b in zip(inter, intra)], axis=0)
    redo = dict(qf=qf, kk=kk, gcs=gcs, vals=vals, inter=inter, gr=gr)
    return _head_out(o, gr, hg_gain, ones).astype(BF16), worst, redo


def _exact_intra(qc, kc, gc, vc):
    c = qc.shape[0]
    group = 8
    row = lax.broadcasted_iota(jnp.int32, (c, c), 0)
    col = lax.broadcasted_iota(jnp.int32, (c, c), 1)
    col_g = lax.broadcasted_iota(jnp.int32, (group, c), 1)
    earlier = []
    for i in range(c // group):
        lo, hi = i * group, (i + 1) * group
        g_base = gc[lo - 1:lo, :] if i else jnp.zeros((1, HG_DIM), F32)
        q_hat = qc[lo:hi] * jnp.exp2(gc[lo:hi] - g_base)
        k_hat = kc * jnp.exp2(jnp.minimum(g_base - gc, 0.0))
        earlier.append(jnp.where(col_g < lo, _dot_nt(q_hat.astype(BF16), k_hat.astype(BF16)), 0.0))
    a = jnp.concatenate(earlier, axis=0)
    qb = qc.astype(BF16)
    g3 = gc.reshape(c // group, group, HG_DIM)
    own = (col <= row) & (col >= row - lax.rem(row, group))
    for p in range(group):
        g_p = jnp.broadcast_to(g3[:, p:p + 1, :], g3.shape).reshape(c, HG_DIM)
        k_p = kc * jnp.exp2(jnp.minimum(g_p - gc, 0.0))
        a = a + jnp.where(own & (lax.rem(row, group) == p), _dot_nt(qb, k_p.astype(BF16)), 0.0)
    return _dot(a.astype(BF16), vc)


def _hgrn_redo(redo, hg_gain, ones):
    outs = []
    for c, gc in enumerate(redo["gcs"]):
        rows = slice(c * HG_CHUNK, (c + 1) * HG_CHUNK)
        outs.append(redo["inter"][c] + _exact_intra(redo["qf"][rows], redo["kk"][rows], gc, redo["vals"][rows]))
    return _head_out(jnp.concatenate(outs, axis=0), redo["gr"], hg_gain, ones).astype(BF16)


def _attn_chain(q_pair, k_half, v_t, bias_ref, sink_a, sink_b, kv):
    s = _dot_nt(k_half, q_pair) + bias_ref[...]
    yield
    col = lax.broadcasted_iota(jnp.int32, (1, s.shape[1]), 1)
    sink = jnp.where(col < WINDOW, sink_a, sink_b)
    m = jnp.maximum(jnp.max(s, axis=0, keepdims=True), sink)
    e = jnp.exp2(s - m)
    yield
    denom = jnp.sum(e, axis=0, keepdims=True) + jnp.exp2(sink - m)
    o_t = _dot(v_t, e.astype(BF16))[kv * HEAD_DIM:(kv + 1) * HEAD_DIM]
    yield
    return o_t * (1.0 / denom)


def _gate_stream(u, win_ref, step=256):
    res = []
    for off in (O_GA, O_GB):
        cols = []
        for lo in range(0, SPLIT_SIZES[7], step):
            cols.append(jax.nn.sigmoid(_dot(u, win_ref[:, off + lo:off + lo + step])))
            yield
        res.append(jnp.concatenate(cols, axis=1))
    return res


def _q_cols(u, win_ref, kv, seg2, gain2):
    x = _dot(u, win_ref[:, O_Q + 2 * kv * LANES:O_Q + 2 * (kv + 1) * LANES])
    yield
    ms = _dot((x * x).astype(BF16), seg2) * (1.0 / HEAD_DIM)
    yield
    return (x * lax.rsqrt(ms + EPS) * gain2).astype(BF16)


def _kv_cols(u, win_ref, seg, gain):
    x = _dot(u, win_ref[:, O_K:O_QR])
    yield
    k, v = x[:, :LANES], x[:, LANES:]
    ms = _dot((k * k).astype(BF16), seg) * (1.0 / HEAD_DIM)
    yield
    return k * lax.rsqrt(ms + EPS) * gain, v


def _attn_up(parts, wua_ref):
    att = jnp.concatenate(parts, axis=0).T.astype(BF16)
    yield
    return _dot(att, wua_ref[...])


def _hgrn_sample_kernel(qr_ref, fr_ref, ir_ref, gr_ref, s0_ref, lb_ref, hg_ref, tri_ref, tot_ref, ones_ref,
                        o_ref, s_out_ref, *, ld):
    nseq = s0_ref.shape[0]
    r = nseq * ld
    lb = _lower_bound(lb_ref[...])
    qf_all, kk_all, g_all = _gates(qr_ref[...], fr_ref[...], lb)
    tri, tot = tri_ref[...], tot_ref[...]
    causal = tri.astype(F32) > 0
    lane = lax.broadcasted_iota(jnp.int32, (HG_DIM, r), 1)
    pos = lax.rem(lax.broadcasted_iota(jnp.int32, (r, r), 0), ld)

    def head(hd):
        lanes = slice(hd * HG_DIM, (hd + 1) * HG_DIM)
        qf, kk, v = qf_all[:, lanes], kk_all[:, lanes], ir_ref[:, lanes]
        g = _split2_ldot(tri, g_all[:, lanes])
        g_last = _split2_ldot(tot, g_all[:, lanes])
        yield
        q_t = (qf * jnp.exp2(g)).astype(BF16)
        qb = qf.astype(BF16)
        g3 = g.reshape(nseq, ld, HG_DIM)
        a = jnp.zeros((r, r), F32)
        for p in range(ld):
            g_p = jnp.broadcast_to(g3[:, p:p + 1, :], g3.shape).reshape(r, HG_DIM)
            k_p = kk * jnp.exp2(jnp.minimum(g_p - g, 0.0))
            a = a + jnp.where(causal & (pos == p), _dot_nt(qb, k_p.astype(BF16)), 0.0)
        k_end_t = (kk * jnp.exp2(g_last - g)).T
        decay_t = jnp.exp2(g_last).T
        yield
        intra = _dot(a.astype(BF16), v)
        outs = []
        for i in range(nseq):
            s_prev = s0_ref[i, hd]
            outs.append(_dot(q_t[i * ld:(i + 1) * ld], s_prev.astype(BF16)))
            own = (lane >= i * ld) & (lane < (i + 1) * ld)
            upd = _dot(jnp.where(own, k_end_t, 0.0).astype(BF16), v)
            s_out_ref[i, hd] = decay_t[:, i * ld:i * ld + 1] * s_prev + upd
            if i % 2:
                yield
        o = jnp.concatenate(outs, axis=0) + intra
        yield
        o_ref[:, lanes] = _head_out(o, gr_ref[:, lanes], hg_ref[...], ones_ref[...]).astype(o_ref.dtype)

    pool = _Pool()
    for hd in range(HG_HEADS):
        pool.add(hd, head(hd))
    while pool.live:
        pool.round()


def _hgrn_sample(qr, fr, ir, gr, s0, lb_logits, hg_gain, ld, sb):
    bd = s0.shape[0]
    r = sb * ld
    seq = np.arange(r) // ld
    same = seq[:, None] == seq[None, :]
    tri = jnp.asarray((same & (np.arange(r)[:, None] >= np.arange(r)[None, :])).astype(np.float32), BF16)
    tot = jnp.asarray(same.astype(np.float32), BF16)
    ones = jnp.ones((HG_DIM, HG_DIM), BF16)
    tok = pl.BlockSpec((r, HG_HEADS * HG_DIM), lambda i: (i, 0))
    state = pl.BlockSpec((sb, HG_HEADS, HG_DIM, HG_DIM), lambda i: (i, 0, 0, 0))
    const = lambda a: pl.BlockSpec(a.shape, lambda i: (0,) * a.ndim)
    return pl.pallas_call(
        functools.partial(_hgrn_sample_kernel, ld=ld),
        grid=(bd // sb,),
        in_specs=[tok, tok, tok, tok, state, const(lb_logits), const(hg_gain), const(tri), const(tot), const(ones)],
        out_specs=[tok, state],
        out_shape=[jax.ShapeDtypeStruct(qr.shape, BF16), jax.ShapeDtypeStruct(s0.shape, F32)],
        compiler_params=pltpu.CompilerParams(dimension_semantics=("parallel",)),
        name="hgrn_sample",
    )(qr, fr, ir, gr, s0, lb_logits, hg_gain, tri, tot, ones)


def _merge_out(h, sga, sgb, att, orr, wua_ref, wur_ref, wo_ref):
    merged = sga * _dot(att, wua_ref[...]) + sgb * _dot(orr, wur_ref[...])
    return h + _dot(merged.astype(BF16), wo_ref[...])


def _mix_kernel(h_ref, gm_ref, qg_ref, kg_ref, seg_ref, seg2_ref, bias_ref, sink_ref, lb_ref, hg_ref, tri_ref, ones_ref,
                win_hbm, wua_hbm, wur_hbm, wo_hbm,
                h2_ref, k_out_ref, v_out_ref, s_out_ref, win_out, wua_out, wur_out, wo_out,
                kprev_ref, vprev_ref, s_ref, win_ref, wua_ref, wur_ref, wo_ref,
                stage_in_ref, stage_up_ref, stage_o_ref, sem_in, sem_out):
    tm = h_ref.shape[0]
    first = pl.program_id(1) == 0
    flat_step = pl.program_id(0) * pl.num_programs(1) + pl.program_id(1)
    resident = ((win_ref, win_out), (wua_ref, wua_out), (wur_ref, wur_out), (wo_ref, wo_out))
    writebacks = [pltpu.make_async_copy(src, dst, sem_out.at[k]) for k, (src, dst) in enumerate(resident)]

    @pl.when(flat_step == 0)
    def _():
        copies = _cast_copies(((win_hbm, win_ref), (wua_hbm, wua_ref), (wur_hbm, wur_ref), (wo_hbm, wo_ref)),
                              (stage_in_ref, stage_up_ref, stage_up_ref, stage_o_ref), sem_in)
        copies[0][0].start()
        for j, (dma, staged, dst) in enumerate(copies):
            if j + 1 < len(copies):
                copies[j + 1][0].start()
            dma.wait()
            dst[...] = staged[...].astype(BF16)
        for wb in writebacks:
            wb.start()

    @pl.when(first)
    def _():
        kprev_ref[...] = jnp.zeros(kprev_ref.shape, F32)
        vprev_ref[...] = jnp.zeros(vprev_ref.shape, F32)
        s_ref[...] = jnp.zeros(s_ref.shape, F32)

    h = h_ref[...]
    u = _rms_rows(h, gm_ref[...]).astype(BF16)

    pool = _Pool()
    lb = _lower_bound(lb_ref[...])
    for kv in range(N_KV_HEADS):
        pool.add(("q", kv), _q_cols(u, win_ref, kv, seg2_ref[...], qg_ref[...]))
    pool.add("kv", _kv_cols(u, win_ref, seg_ref[...], kg_ref[...]))
    for pair in range(HG_HEADS // 2):
        pool.add(("hproj", pair), _hgrn_proj(u, win_ref, pair))
    pool.add("gates", _gate_stream(u, win_ref))
    first_keys = [("q", kv) for kv in range(N_KV_HEADS)] + ["kv"] + [("hproj", p) for p in range(HG_HEADS // 2)]
    while not pool.has(first_keys):
        pool.round()
    for hd in range(HG_HEADS):
        qr, fr, ir, gr = (a[:, (hd % 2) * HG_DIM:(hd % 2 + 1) * HG_DIM] for a in pool.out[("hproj", hd // 2)])
        pool.add(("hgrn", hd), _hgrn_head(qr, fr, ir, gr, lb[:, hd * HG_DIM:(hd + 1) * HG_DIM], hd, s_ref, s_out_ref,
                                          hg_ref[...], ones_ref[...], tri_ref[...]))

    q = [pool.out[("q", j // 2)][:, (j % 2) * LANES:(j % 2 + 1) * LANES] for j in range(SPLIT_SIZES[0] // LANES)]
    k, v = pool.out["kv"]
    kk_all = jnp.concatenate([kprev_ref[...], k], axis=0)
    vv_all = jnp.concatenate([vprev_ref[...], v], axis=0)
    k_tail, v_tail = k[tm - WINDOW:], v[tm - WINDOW:]
    kprev_ref[...] = k_tail
    vprev_ref[...] = v_tail
    k_out_ref[...] = k_tail
    v_out_ref[...] = v_tail
    lane = lax.broadcasted_iota(jnp.int32, kk_all.shape, 1)
    k_lo0 = jnp.where(lane < HEAD_DIM, kk_all, 0.0)
    k_hi1 = jnp.where(lane >= HEAD_DIM, kk_all, 0.0)
    k_halves = ((k_lo0.astype(BF16), pltpu.roll(k_lo0, HEAD_DIM, 1).astype(BF16)),
                (pltpu.roll(k_hi1, HEAD_DIM, 1).astype(BF16), k_hi1.astype(BF16)))
    v_t = vv_all.T.astype(BF16)

    n_blk = tm // WINDOW
    chain_keys = lambda blk: [("att", blk, kv, half) for kv in range(N_KV_HEADS) for half in range(2)]
    for blk in range(n_blk):
        lo = blk * WINDOW
        variant = jnp.where(first, 0, 1) if blk == 0 else 1
        for kv in range(N_KV_HEADS):
            q_pair = jnp.concatenate([q[2 * kv + j][lo:lo + WINDOW] for j in range(GROUP // 2)], axis=0)
            for half in range(2):
                pool.add(("att", blk, kv, half),
                         _attn_chain(q_pair, k_halves[kv][half][lo:lo + 2 * WINDOW], v_t[:, lo:lo + 2 * WINDOW],
                                     bias_ref.at[variant, kv, half], sink_ref[GROUP * kv + half],
                                     sink_ref[GROUP * kv + 2 + half], kv))

    waiting = list(range(n_blk))
    while pool.live:
        pool.round()
        for blk in [b for b in waiting if pool.has(chain_keys(b))]:
            waiting.remove(blk)
            parts = []
            for hq in range(N_Q_HEADS):
                kv, j, half = hq // GROUP, (hq % GROUP) // 2, hq % 2
                parts.append(pool.out[("att", blk, kv, half)][:, j * WINDOW:(j + 1) * WINDOW])
            pool.add(("up", blk), _attn_up(parts, wua_ref))

    up_a = jnp.concatenate([pool.out[("up", blk)] for blk in range(n_blk)], axis=0)
    sga, sgb = pool.out["gates"]
    heads = [pool.out[("hgrn", hd)] for hd in range(HG_HEADS)]

    def finish(orr_heads):
        orr = jnp.concatenate(orr_heads, axis=1)
        merged = sga * up_a + sgb * _dot(orr, wur_ref[...])
        h2_ref[...] = h + _dot(merged.astype(BF16), wo_ref[...])

    finish([o for o, _, _ in heads])

    worst = functools.reduce(jnp.minimum, [w for _, w, _ in heads])

    @pl.when(jnp.min(worst) < -EXP_CLAMP)
    def _():
        finish([_hgrn_redo(redo, hg_ref[...], ones_ref[...]) for _, _, redo in heads])

    @pl.when(flat_step == pl.num_programs(0) * pl.num_programs(1) - 1)
    def _():
        for wb in writebacks:
            wb.wait()


def _mix(h, consts, weights, batch, seq, tm):
    t, d = h.shape
    nt = seq // tm
    row = pl.BlockSpec((tm, d), lambda b, i: (b * nt + i, 0))
    win_out = pl.BlockSpec((WINDOW, LANES), lambda b, i: (b, 0))
    hbm = pl.BlockSpec(memory_space=pl.ANY)
    w_in, wua, wur, wo = weights
    assert wua.shape == wur.shape
    stage = lambda w: pltpu.VMEM((2, w.shape[0] // CAST_CHUNKS, w.shape[1]), F32)
    return pl.pallas_call(
        _mix_kernel,
        grid=(batch, nt),
        in_specs=[row] + [pl.BlockSpec(memory_space=pltpu.SMEM) if a.ndim == 1 else _const_spec(a) for a in consts]
                 + [hbm] * len(weights),
        out_specs=[row, win_out, win_out, pl.BlockSpec((1, HG_HEADS, HG_DIM, HG_DIM), lambda b, i: (b, 0, 0, 0))]
                  + [hbm] * len(weights),
        out_shape=[jax.ShapeDtypeStruct((t, d), F32),
                   jax.ShapeDtypeStruct((batch * WINDOW, LANES), F32),
                   jax.ShapeDtypeStruct((batch * WINDOW, LANES), F32),
                   jax.ShapeDtypeStruct((batch, HG_HEADS, HG_DIM, HG_DIM), F32)]
                  + [jax.ShapeDtypeStruct(w.shape, BF16) for w in weights],
        scratch_shapes=[pltpu.VMEM((WINDOW, LANES), F32), pltpu.VMEM((WINDOW, LANES), F32),
                        pltpu.VMEM((HG_HEADS, HG_DIM, HG_DIM), F32)]
                       + [pltpu.VMEM(w.shape, BF16) for w in weights]
                       + [stage(w_in), stage(wua), stage(wo),
                          pltpu.SemaphoreType.DMA((2,)), pltpu.SemaphoreType.DMA((len(weights),))],
        compiler_params=pltpu.CompilerParams(dimension_semantics=("arbitrary", "arbitrary"),
                                             vmem_limit_bytes=VMEM_LIMIT),
        name="mix",
    )(h, *consts, *weights)


def _proj_kernel(h_ref, gm_ref, win_ref, qg_ref, kg_ref, seg_ref,
                 q_ref, k_ref, v_ref, qr_ref, fr_ref, ir_ref, gr_ref):
    u = _rms_rows(h_ref[...], gm_ref[...]).astype(BF16)
    q, k, v = _project_qkv(u, win_ref, qg_ref[...], kg_ref[...], seg_ref[...])
    q_ref[...] = q
    k_ref[...] = k
    v_ref[...] = v
    qr_ref[...] = _dot(u, win_ref[:, O_QR:O_FR])
    fr_ref[...] = _dot(u, win_ref[:, O_FR:O_IR])
    ir_ref[...] = _dot(u, win_ref[:, O_IR:O_GR]).astype(ir_ref.dtype)
    gr_ref[...] = _dot(u, win_ref[:, O_GR:O_GA])


def _proj(h, gm, w_in, qg, kg, seg, tm):
    t, d = h.shape
    row = lambda w: pl.BlockSpec((tm, w), lambda i: (i, 0))
    widths = SPLIT_SIZES[:7]
    dtypes = (BF16, F32, F32, F32, F32, BF16, F32)
    return pl.pallas_call(
        _proj_kernel,
        grid=(t // tm,),
        in_specs=[row(d)] + [_const_spec(a) for a in (gm, w_in, qg, kg, seg)],
        out_specs=[row(w) for w in widths],
        out_shape=[jax.ShapeDtypeStruct((t, w), dt) for w, dt in zip(widths, dtypes)],
        compiler_params=pltpu.CompilerParams(dimension_semantics=("parallel",), vmem_limit_bytes=VMEM_LIMIT),
        name="proj",
    )(h, gm, w_in, qg, kg, seg)


def _out_kernel(h_ref, att_ref, orr_ref, gm_ref, win_ref, wua_ref, wur_ref, wo_ref, h2_ref):
    h = h_ref[...]
    u = _rms_rows(h, gm_ref[...]).astype(BF16)
    sga = jax.nn.sigmoid(_dot(u, win_ref[:, O_GA:O_GB]))
    sgb = jax.nn.sigmoid(_dot(u, win_ref[:, O_GB:O_END]))
    h2_ref[...] = _merge_out(h, sga, sgb, att_ref[...], orr_ref[...], wua_ref, wur_ref, wo_ref)


def _out(h, att, orr, consts, tm):
    t, d = h.shape
    row = lambda a: pl.BlockSpec((tm, a.shape[1]), lambda i: (i, 0))
    return pl.pallas_call(
        _out_kernel,
        grid=(t // tm,),
        in_specs=[row(a) for a in (h, att, orr)] + [_const_spec(a) for a in consts],
        out_specs=pl.BlockSpec((tm, d), lambda i: (i, 0)),
        out_shape=jax.ShapeDtypeStruct((t, d), F32),
        compiler_params=pltpu.CompilerParams(dimension_semantics=("parallel",), vmem_limit_bytes=VMEM_LIMIT),
        name="out",
    )(h, att, orr, *consts)


def _bucket_map():
    dist = np.arange(WINDOW)[:, None] + WINDOW - np.arange(2 * WINDOW)[None, :]
    valid = (dist >= 0) & (dist <= WINDOW)
    return np.where(valid, _t5_bucket(dist), -1).astype(np.int32)


def _stacked(bias, rows):
    return bias.reshape(N_KV_HEADS, GROUP * rows, bias.shape[-1])


def kernel(x_prompt, x_sample, cache_win_k, cache_win_v, state_hgrn, ffn1_norm, ffn1_w1, ffn1_w3, ffn1_w2, mix_norm, w_in, q_norm, k_norm, sinks, rel_bias_table, hgrn_lb_logits, hg_norm, w_up_attn, w_up_hgrn, w_out, ffn2_norm, ffn2_w1, ffn2_w3, ffn2_w2):
    depth = ffn1_norm.shape[0]
    assert depth == 1 and hgrn_lb_logits.shape[0] == 2, "single-layer step only"
    batch, seq, d = x_prompt.shape
    bd, ld, _ = x_sample.shape

    row = lambda g: g[0].reshape(1, -1).astype(F32)
    g1, gm, g2 = row(ffn1_norm), row(mix_norm), row(ffn2_norm)
    qg = jnp.tile(row(q_norm), (1, LANES // HEAD_DIM)) * (HEAD_DIM ** -0.5 * LOG2E)
    kg = jnp.tile(row(k_norm), (1, LANES // HEAD_DIM))
    head_of_lane = np.arange(2 * LANES) // HEAD_DIM
    seg2 = jnp.asarray((head_of_lane[:, None] == head_of_lane[None, :]).astype(np.float32), BF16)
    seg = seg2[:LANES, :LANES]
    qg2 = jnp.tile(qg, (1, 2))
    ones = jnp.ones((HG_DIM, HG_DIM), BF16)
    tri = jnp.asarray(np.tril(np.ones((HG_CHUNK, HG_CHUNK), np.float32)), BF16)
    lb_logits = hgrn_lb_logits.astype(F32)
    hg_gain = row(hg_norm)

    later = _rel_bias(rel_bias_table.astype(F32) * LOG2E, np.ascontiguousarray(_bucket_map().T))
    first = jnp.where(np.arange(2 * WINDOW)[None, :, None] >= WINDOW, later, NEG_BIG)
    def paired(b):
        b = b.reshape(N_KV_HEADS, GROUP // 2, 2, 2 * WINDOW, WINDOW)
        return b.transpose(0, 2, 3, 1, 4).reshape(N_KV_HEADS, 2, 2 * WINDOW, 2 * WINDOW)
    bias_p = jnp.stack([paired(first), paired(later)])
    bias_s = _stacked(later[:, :WINDOW + ld, :ld].transpose(0, 2, 1), ld)
    sink_1d = sinks[0].astype(F32) * LOG2E
    sink = sink_1d.reshape(N_KV_HEADS, GROUP, 1)
    sink_s = jnp.repeat(sink, ld, axis=1).reshape(N_KV_HEADS, GROUP * ld, 1)

    h_p, w1a, w3a, w2a = _ffn_cast(x_prompt.reshape(batch * seq, d), g1, ffn1_w1[0], ffn1_w3[0], ffn1_w2[0], TM_FFN)
    h_s = _ffn(x_sample.reshape(bd * ld, d), g1, w1a, w3a, w2a, TM_FFN)

    consts = (gm, qg2, kg, seg, seg2, bias_p, sink_1d, lb_logits, hg_gain, tri, ones)
    h2_p, k_p, v_p, s_p, w_in_b, wua, wur, wo = _mix(h_p, consts, (w_in[0], w_up_attn[0], w_up_hgrn[0], w_out[0]),
                                                     batch, seq, TM_MIX)
    win = lambda a: a.reshape(1, batch, WINDOW, N_KV_HEADS, HEAD_DIM)

    q, k, v, qr, fr, ir, gr = _proj(h_s, gm, w_in_b, qg, kg, seg, TM_SAMPLE)
    ck = cache_win_k[0].reshape(bd, WINDOW, LANES)
    cv = cache_win_v[0].reshape(bd, WINDOW, LANES)
    att, nk, nv = _swa_sample(q, k, v, ck, cv, bias_s, sink_s, ld, SEQS_PER_STEP)
    orr, s_s = _hgrn_sample(qr, fr, ir, gr, state_hgrn[0], lb_logits, hg_gain, ld, SEQS_PER_STEP)
    h2_s = _out(h_s, att, orr, (gm, w_in_b, wua, wur, wo), TM_SAMPLE)

    y_p, w1b, w3b, w2b = _ffn_cast(h2_p, g2, ffn2_w1[0], ffn2_w3[0], ffn2_w2[0], TM_FFN)
    y_s = _ffn(h2_s, g2, w1b, w3b, w2b, TM_FFN)
    unwin = lambda a: a.reshape(1, bd, WINDOW, N_KV_HEADS, HEAD_DIM)
    return (y_p.reshape(batch, seq, d), y_s.reshape(bd, ld, d), win(k_p), win(v_p), s_p[None],
            unwin(nk), unwin(nv), s_s[None])
```

```python
import functools

import numpy as np
import jax
import jax.numpy as jnp
from jax import lax
from jax.experimental import pallas as pl
from jax.experimental.pallas import tpu as pltpu

F32 = jnp.float32
BF16 = jnp.bfloat16

LANES = 128
HEAD_DIM = 64
N_Q_HEADS = 8
N_KV_HEADS = 2
GROUP = N_Q_HEADS // N_KV_HEADS
WINDOW = 128
N_BUCKETS = 32
MAX_DISTANCE = 128
HG_HEADS = 4
HG_DIM = 128
HG_CHUNK = 128
HG_SUB = 32
EPS = 1e-6
F_TINY = 1e-30
NEG_BIG = -1e30
EXP_CLAMP = 100.0
LOG2E = 1.4426950408889634
SPLIT_SIZES = (512, 128, 128, 512, 512, 512, 512, 1024, 1024)
SPLIT_OFFS = tuple(int(v) for v in np.cumsum((0,) + SPLIT_SIZES))
O_Q, O_K, O_V, O_QR, O_FR, O_IR, O_GR, O_GA, O_GB, O_END = SPLIT_OFFS
VMEM_LIMIT = 56 * 1024 * 1024
TM_FFN = 1024
NORM_BLOCKS = 4
TM_MIX = 512
TM_SAMPLE = 256
SEQS_PER_STEP = 16


def _dot(a, b):
    return jnp.dot(a, b, preferred_element_type=F32)


def _dot_nt(a, b):
    return lax.dot_general(a, b, (((1,), (1,)), ((), ())), preferred_element_type=F32)


def _split2_ldot(w, x):
    hi = x.astype(BF16)
    lo = (x - hi.astype(F32)).astype(BF16)
    return _dot(w, hi) + _dot(w, lo)


def _rms_rows(x, g):
    ms = jnp.mean(x * x, axis=-1, keepdims=True)
    return x * lax.rsqrt(ms + EPS) * g


def _seg_rms(x, seg, inv_n, g):
    ms = _dot((x * x).astype(BF16), seg) * inv_n
    return x * lax.rsqrt(ms + EPS) * g


class _Pool:
    def __init__(self):
        self.live = []
        self.out = {}

    def add(self, key, gen):
        self.live.append((key, gen))

    def round(self):
        still = []
        for key, gen in self.live:
            try:
                next(gen)
                still.append((key, gen))
            except StopIteration as done:
                self.out[key] = done.value
        self.live = still

    def has(self, keys):
        return all(k in self.out for k in keys)


def _const_spec(a):
    nd = a.ndim
    return pl.BlockSpec(a.shape, lambda *_: (0,) * nd, pipeline_mode=pl.Buffered(1))


def _ff_chunks(d_ff, step=1024):
    return tuple((lo, min(lo + step, d_ff)) for lo in range(0, d_ff, step))


def _swiglu(xn_blocks, w1_ref, w3_ref, w2_ref):
    xn = None
    acc = None
    for c, (lo, hi) in enumerate(_ff_chunks(w1_ref.shape[1])):
        if c == 0:
            a = jnp.concatenate([_dot(blk, w1_ref[:, lo:hi]) for blk in xn_blocks], axis=0)
            b = jnp.concatenate([_dot(blk, w3_ref[:, lo:hi]) for blk in xn_blocks], axis=0)
            xn = jnp.concatenate(xn_blocks, axis=0)
        else:
            a = _dot(xn, w1_ref[:, lo:hi])
            b = _dot(xn, w3_ref[:, lo:hi])
        part = _dot((jax.nn.silu(a) * b).astype(BF16), w2_ref[lo:hi, :])
        acc = part if acc is None else acc + part
    return acc


def _normed_blocks(x_ref, g, blocks):
    rows = x_ref.shape[0] // blocks
    return [_rms_rows(x_ref[r * rows:(r + 1) * rows, :], g).astype(BF16) for r in range(blocks)]


def _ffn_kernel(x_ref, g1_ref, w1_ref, w3_ref, w2_ref, h_ref):
    xn = _normed_blocks(x_ref, g1_ref[...], NORM_BLOCKS)
    h_ref[...] = x_ref[...] + 0.5 * _swiglu(xn, w1_ref, w3_ref, w2_ref)


CAST_CHUNKS = 8


def _cast_copies(pairs, stage_refs, sem):
    out = []
    for (src, dst), stage in zip(pairs, stage_refs):
        rows = src.shape[0] // CAST_CHUNKS
        for c in range(CAST_CHUNKS):
            slot = len(out) % 2
            staged = stage.at[slot]
            out.append((pltpu.make_async_copy(src.at[pl.ds(c * rows, rows), :], staged, sem.at[slot]),
                        staged, dst.at[pl.ds(c * rows, rows), :]))
    return out


def _ffn_cast_kernel(x_ref, g1_ref, w1_hbm, w3_hbm, w2_hbm, h_ref, w1_out, w3_out, w2_out,
                     w1_ref, w3_ref, w2_ref, stage_in_ref, stage_out_ref, sem_in, sem_out):
    step = pl.program_id(0)
    resident = ((w1_ref, w1_out), (w3_ref, w3_out), (w2_ref, w2_out))
    writebacks = [pltpu.make_async_copy(src, dst, sem_out.at[k]) for k, (src, dst) in enumerate(resident)]

    @pl.when(step == 0)
    def _():
        copies = _cast_copies(((w1_hbm, w1_ref), (w3_hbm, w3_ref), (w2_hbm, w2_ref)),
                              (stage_in_ref, stage_in_ref, stage_out_ref), sem_in)
        copies[0][0].start()
        for j, (dma, staged, dst) in enumerate(copies):
            if j + 1 < len(copies):
                copies[j + 1][0].start()
            dma.wait()
            dst[...] = staged[...].astype(BF16)
        for wb in writebacks:
            wb.start()

    xn = _normed_blocks(x_ref, g1_ref[...], NORM_BLOCKS)
    h_ref[...] = x_ref[...] + 0.5 * _swiglu(xn, w1_ref, w3_ref, w2_ref)

    @pl.when(step == pl.num_programs(0) - 1)
    def _():
        for wb in writebacks:
            wb.wait()


def _ffn_cast(x, g1, w1, w3, w2, tm):
    t, d = x.shape
    d_ff = w1.shape[1]
    row = pl.BlockSpec((tm, d), lambda i: (i, 0))
    hbm = pl.BlockSpec(memory_space=pl.ANY)
    bf = lambda w: jax.ShapeDtypeStruct(w.shape, BF16)
    return pl.pallas_call(
        _ffn_cast_kernel,
        grid=(t // tm,),
        in_specs=[row, _const_spec(g1), hbm, hbm, hbm],
        out_specs=[row, hbm, hbm, hbm],
        out_shape=[jax.ShapeDtypeStruct((t, d), F32), bf(w1), bf(w3), bf(w2)],
        scratch_shapes=[pltpu.VMEM(w1.shape, BF16), pltpu.VMEM(w3.shape, BF16), pltpu.VMEM(w2.shape, BF16),
                        pltpu.VMEM((2, d // CAST_CHUNKS, d_ff), F32), pltpu.VMEM((2, d_ff // CAST_CHUNKS, d), F32),
                        pltpu.SemaphoreType.DMA((2,)), pltpu.SemaphoreType.DMA((3,))],
        compiler_params=pltpu.CompilerParams(dimension_semantics=("arbitrary",), vmem_limit_bytes=VMEM_LIMIT),
        name="ffn_cast",
    )(x, g1, w1, w3, w2)


def _ffn(x, g1, w1, w3, w2, tm):
    t, d = x.shape
    row = pl.BlockSpec((tm, d), lambda i: (i, 0))
    return pl.pallas_call(
        _ffn_kernel,
        grid=(t // tm,),
        in_specs=[row] + [_const_spec(a) for a in (g1, w1, w3, w2)],
        out_specs=row,
        out_shape=jax.ShapeDtypeStruct((t, d), F32),
        compiler_params=pltpu.CompilerParams(dimension_semantics=("parallel",), vmem_limit_bytes=VMEM_LIMIT),
        name="ffn",
    )(x, g1, w1, w3, w2)


def _t5_bucket(dist):
    max_exact = N_BUCKETS // 2
    d = np.maximum(dist, 0)
    large = max_exact + (np.log(np.maximum(d, 1) / max_exact) / np.log(MAX_DISTANCE / max_exact)
                         * (N_BUCKETS - max_exact)).astype(np.int32)
    large = np.minimum(large, N_BUCKETS - 1)
    return np.where(d < max_exact, d, large).astype(np.int32)


def _bias_kernel(table_ref, bucket_ref, out_ref):
    bucket = bucket_ref[...]
    masked = jnp.where(bucket < 0, NEG_BIG, 0.0).astype(F32)
    for h in range(N_Q_HEADS):
        acc = masked
        for b in range(N_BUCKETS):
            acc = acc + jnp.where(bucket == b, table_ref[b, h], 0.0)
        out_ref[h] = acc


def _rel_bias(table, bucket_map):
    r, c = bucket_map.shape
    return pl.pallas_call(
        _bias_kernel,
        in_specs=[pl.BlockSpec(memory_space=pltpu.SMEM), pl.BlockSpec((r, c), lambda: (0, 0))],
        out_specs=pl.BlockSpec((N_Q_HEADS, r, c), lambda: (0, 0, 0)),
        out_shape=jax.ShapeDtypeStruct((N_Q_HEADS, r, c), F32),
        name="rel_bias",
    )(table, jnp.asarray(bucket_map))


def _project_qkv(u, win_ref, qg, kg, seg):
    qs = []
    for s in range(SPLIT_SIZES[0] // LANES):
        x = _dot(u, win_ref[:, O_Q + s * LANES:O_Q + (s + 1) * LANES])
        qs.append(_seg_rms(x, seg, 1.0 / HEAD_DIM, qg).astype(BF16))
    k = _seg_rms(_dot(u, win_ref[:, O_K:O_V]), seg, 1.0 / HEAD_DIM, kg)
    v = _dot(u, win_ref[:, O_V:O_QR])
    return jnp.concatenate(qs, axis=1), k, v


def _stack_group(q, kv):
    a = q[:, (2 * kv) * LANES:(2 * kv + 1) * LANES]
    b = q[:, (2 * kv + 1) * LANES:(2 * kv + 2) * LANES]
    ar, br = pltpu.roll(a, HEAD_DIM, 1), pltpu.roll(b, HEAD_DIM, 1)
    parts = (a, ar, b, br) if kv == 0 else (ar, a, br, b)
    return jnp.concatenate(parts, axis=0)


def _kv_half(x, kv, fill):
    lane = lax.broadcasted_iota(jnp.int32, x.shape, 1)
    keep = (lane < HEAD_DIM) if kv == 0 else (lane >= HEAD_DIM)
    return jnp.where(keep, x, fill)


def _attend_chain(q, kk, vv, bias_ref, sink_ref, r, kv):
    qs = _stack_group(q.astype(F32), kv).astype(BF16)
    s = _dot_nt(qs, _kv_half(kk, kv, 0.0).astype(BF16)) + bias_ref[kv]
    yield
    sink = sink_ref[kv]
    m = jnp.maximum(jnp.max(s, axis=-1, keepdims=True), sink)
    e = jnp.exp2(s - m).astype(BF16)
    yield
    o = _dot(e, _kv_half(vv, kv, 1.0).astype(BF16))
    yield
    o = o / (pltpu.roll(o, HEAD_DIM, 1) + jnp.exp2(sink - m))
    lane = lax.broadcasted_iota(jnp.int32, (r, LANES), 1)
    slabs = []
    for pair in range(GROUP // 2):
        a, b = o[(2 * pair) * r:(2 * pair + 1) * r], o[(2 * pair + 1) * r:(2 * pair + 2) * r]
        if kv == 0:
            slabs.append(jnp.where(lane < HEAD_DIM, a, pltpu.roll(b, HEAD_DIM, 1)))
        else:
            slabs.append(jnp.where(lane < HEAD_DIM, pltpu.roll(a, HEAD_DIM, 1), b))
    return slabs


def _swa_sample_kernel(q_ref, kn_ref, vn_ref, ck_ref, cv_ref, bias_ref, sink_ref, o_ref, nk_ref, nv_ref, *, ld):
    pool = _Pool()
    nseq = ck_ref.shape[0]
    for i in range(nseq):
        rows = slice(i * ld, (i + 1) * ld)
        kk = jnp.concatenate([ck_ref[i], kn_ref[rows, :]], axis=0)
        vv = jnp.concatenate([cv_ref[i], vn_ref[rows, :]], axis=0)
        nk_ref[i] = kk[ld:, :]
        nv_ref[i] = vv[ld:, :]
        for kv in range(N_KV_HEADS):
            pool.add((i, kv), _attend_chain(q_ref[rows, :], kk, vv, bias_ref, sink_ref, ld, kv))
    while pool.live:
        pool.round()
    for i in range(nseq):
        slabs = [s for kv in range(N_KV_HEADS) for s in pool.out[(i, kv)]]
        o_ref[i * ld:(i + 1) * ld, :] = jnp.concatenate(slabs, axis=1).astype(o_ref.dtype)


def _swa_sample(q, k, v, cache_k, cache_v, bias, sink_col, ld, sb):
    bd = cache_k.shape[0]
    tok = lambda w: pl.BlockSpec((sb * ld, w), lambda i: (i, 0))
    cache = pl.BlockSpec((sb, WINDOW, LANES), lambda i: (i, 0, 0))
    return pl.pallas_call(
        functools.partial(_swa_sample_kernel, ld=ld),
        grid=(bd // sb,),
        in_specs=[tok(4 * LANES), tok(LANES), tok(LANES), cache, cache,
                  pl.BlockSpec(bias.shape, lambda i: (0, 0, 0)), pl.BlockSpec(sink_col.shape, lambda i: (0, 0, 0))],
        out_specs=[tok(4 * LANES), cache, cache],
        out_shape=[jax.ShapeDtypeStruct(q.shape, BF16), jax.ShapeDtypeStruct(cache_k.shape, F32),
                   jax.ShapeDtypeStruct(cache_v.shape, F32)],
        compiler_params=pltpu.CompilerParams(dimension_semantics=("parallel",)),
        name="swa_sample",
    )(q, k, v, cache_k, cache_v, bias, sink_col)


def _lower_bound(lb_logits):
    z = lb_logits - jnp.max(lb_logits, axis=0, keepdims=True)
    e = jnp.exp(z)
    return e[0:1, :] / jnp.sum(e, axis=0, keepdims=True)


def _gates(qr, fr, lb):
    f = lb + (1.0 - lb) * jax.nn.sigmoid(fr)
    return jax.nn.silu(qr), 1.0 - f, jnp.log2(jnp.maximum(f, F_TINY))


def _head_out(o, gr, hg_gain, ones_seg):
    return _seg_rms(o, ones_seg, 1.0 / HG_DIM, hg_gain) * jax.nn.silu(gr)


def _hgrn_proj(u, win_ref, pair):
    cols = lambda off: slice(off + 2 * pair * HG_DIM, off + 2 * (pair + 1) * HG_DIM)
    qr = _dot(u, win_ref[:, cols(O_QR)])
    fr = _dot(u, win_ref[:, cols(O_FR)])
    yield
    ir = _dot(u, win_ref[:, cols(O_IR)]).astype(BF16)
    gr = _dot(u, win_ref[:, cols(O_GR)])
    yield
    return qr, fr, ir, gr


def _hgrn_head(qr, fr, vals, gr, lb, hd, s_ref, s_out_ref, hg_gain, ones, tri):
    qf, kk, g = _gates(qr, fr, lb)
    nchunk = qr.shape[0] // HG_CHUNK
    chunks = [slice(c * HG_CHUNK, (c + 1) * HG_CHUNK) for c in range(nchunk)]
    yield
    gcs = [_split2_ldot(tri, g[rows]) for rows in chunks]
    yield
    q_in, upd, decay, scores = [], [], [], []
    worst = jnp.zeros((1, HG_DIM), F32)
    for rows, gc in zip(chunks, gcs):
        g_last = gc[HG_CHUNK - 1:HG_CHUNK, :]
        q_in.append((qf[rows] * jnp.exp2(gc)).astype(BF16))
        k_end_t = (kk[rows] * jnp.exp2(g_last - gc)).T
        upd.append(_dot(k_end_t.astype(BF16), vals[rows]))
        decay.append(jnp.broadcast_to(jnp.exp2(g_last), (HG_DIM, HG_DIM)).T)
        qc, kc = qf[rows], kk[rows]
        for i in range(HG_CHUNK // HG_SUB):
            lo, hi = i * HG_SUB, (i + 1) * HG_SUB
            g_base = gc[lo - 1:lo, :] if i else jnp.zeros((1, HG_DIM), F32)
            worst = jnp.minimum(worst, gc[hi - 1:hi, :] - g_base)
            q_hat = qc[lo:hi] * jnp.exp2(gc[lo:hi] - g_base)
            k_hat = kc[:hi] * jnp.exp2(jnp.minimum(g_base - gc[:hi], EXP_CLAMP))
            a = _dot_nt(q_hat.astype(BF16), k_hat.astype(BF16))
            row = lax.broadcasted_iota(jnp.int32, (HG_SUB, hi), 0)
            col = lax.broadcasted_iota(jnp.int32, (HG_SUB, hi), 1)
            scores.append(jnp.where(col <= row + lo, a, 0.0).astype(BF16))
    yield
    intra = []
    nsub = HG_CHUNK // HG_SUB
    for c, rows in enumerate(chunks):
        vc = vals[rows]
        parts = [_dot(scores[c * nsub + i], vc[:(i + 1) * HG_SUB]) for i in range(nsub)]
        intra.append(jnp.concatenate(parts, axis=0))
    yield
    s = s_ref[hd]
    inter = []
    for c in range(nchunk):
        inter.append(_dot(q_in[c], s.astype(BF16)))
        s = decay[c] * s + upd[c]
    s_ref[hd] = s
    s_out_ref[0, hd] = s
    yield
    o = jnp.concatenate([a + b for a, b in zip(inter, intra)], axis=0)
    redo = dict(qf=qf, kk=kk, gcs=gcs, vals=vals, inter=inter, gr=gr)
    return _head_out(o, gr, hg_gain, ones).astype(BF16), worst, redo


def _exact_intra(qc, kc, gc, vc):
    c = qc.shape[0]
    group = 8
    row = lax.broadcasted_iota(jnp.int32, (c, c), 0)
    col = lax.broadcasted_iota(jnp.int32, (c, c), 1)
    col_g = lax.broadcasted_iota(jnp.int32, (group, c), 1)
    earlier = []
    for i in range(c // group):
        lo, hi = i * group, (i + 1) * group
        g_base = gc[lo - 1:lo, :] if i else jnp.zeros((1, HG_DIM), F32)
        q_hat = qc[lo:hi] * jnp.exp2(gc[lo:hi] - g_base)
        k_hat = kc * jnp.exp2(jnp.minimum(g_base - gc, 0.0))
        earlier.append(jnp.where(col_g < lo, _dot_nt(q_hat.astype(BF16), k_hat.astype(BF16)), 0.0))
    a = jnp.concatenate(earlier, axis=0)
    qb = qc.astype(BF16)
    g3 = gc.reshape(c // group, group, HG_DIM)
    own = (col <= row) & (col >= row - lax.rem(row, group))
    for p in range(group):
        g_p = jnp.broadcast_to(g3[:, p:p + 1, :], g3.shape).reshape(c, HG_DIM)
        k_p = kc * jnp.exp2(jnp.minimum(g_p - gc, 0.0))
        a = a + jnp.where(own & (lax.rem(row, group) == p), _dot_nt(qb, k_p.astype(BF16)), 0.0)
    return _dot(a.astype(BF16), vc)


def _hgrn_redo(redo, hg_gain, ones):
    outs = []
    for c, gc in enumerate(redo["gcs"]):
        rows = slice(c * HG_CHUNK, (c + 1) * HG_CHUNK)
        outs.append(redo["inter"][c] + _exact_intra(redo["qf"][rows], redo["kk"][rows], gc, redo["vals"][rows]))
    return _head_out(jnp.concatenate(outs, axis=0), redo["gr"], hg_gain, ones).astype(BF16)


def _attn_chain(q_pair, k_half, v_t, bias_ref, sink_a, sink_b, kv):
    s = _dot_nt(k_half, q_pair) + bias_ref[...]
    yield
    col = lax.broadcasted_iota(jnp.int32, (1, s.shape[1]), 1)
    sink = jnp.where(col < WINDOW, sink_a, sink_b)
    m = jnp.maximum(jnp.max(s, axis=0, keepdims=True), sink)
    e = jnp.exp2(s - m)
    yield
    denom = jnp.sum(e, axis=0, keepdims=True) + jnp.exp2(sink - m)
    o_t = _dot(v_t, e.astype(BF16))[kv * HEAD_DIM:(kv + 1) * HEAD_DIM]
    yield
    return o_t * (1.0 / denom)


def _gate_stream(u, win_ref, step=256):
    res = []
    for off in (O_GA, O_GB):
        cols = []
        for lo in range(0, SPLIT_SIZES[7], step):
            cols.append(jax.nn.sigmoid(_dot(u, win_ref[:, off + lo:off + lo + step])))
            yield
        res.append(jnp.concatenate(cols, axis=1))
    return res


def _q_cols(u, win_ref, kv, seg2, gain2):
    w = win_ref[:, O_Q + 2 * kv * LANES:O_Q + 2 * (kv + 1) * LANES]
    x = jnp.concatenate([_dot(blk, w) for blk in u], axis=0) if isinstance(u, list) else _dot(u, w)
    yield
    ms = _dot((x * x).astype(BF16), seg2) * (1.0 / HEAD_DIM)
    yield
    return (x * lax.rsqrt(ms + EPS) * gain2).astype(BF16)


def _kv_cols(u, win_ref, seg, gain):
    x = _dot(u, win_ref[:, O_K:O_QR])
    yield
    k, v = x[:, :LANES], x[:, LANES:]
    ms = _dot((k * k).astype(BF16), seg) * (1.0 / HEAD_DIM)
    yield
    return k * lax.rsqrt(ms + EPS) * gain, v


def _attn_up(parts, wua_ref):
    att = jnp.concatenate(parts, axis=0).T.astype(BF16)
    yield
    return _dot(att, wua_ref[...])


def _hgrn_sample_kernel(qr_ref, fr_ref, ir_ref, gr_ref, s0_ref, lb_ref, hg_ref, tri_ref, tot_ref, ones_ref,
                        o_ref, s_out_ref, *, ld):
    nseq = s0_ref.shape[0]
    r = nseq * ld
    lb = _lower_bound(lb_ref[...])
    qf_all, kk_all, g_all = _gates(qr_ref[...], fr_ref[...], lb)
    tri, tot = tri_ref[...], tot_ref[...]
    causal = tri.astype(F32) > 0
    lane = lax.broadcasted_iota(jnp.int32, (HG_DIM, r), 1)
    pos = lax.rem(lax.broadcasted_iota(jnp.int32, (r, r), 0), ld)

    def head(hd):
        lanes = slice(hd * HG_DIM, (hd + 1) * HG_DIM)
        qf, kk, v = qf_all[:, lanes], kk_all[:, lanes], ir_ref[:, lanes]
        g = _split2_ldot(tri, g_all[:, lanes])
        g_last = _split2_ldot(tot, g_all[:, lanes])
        yield
        q_t = (qf * jnp.exp2(g)).astype(BF16)
        qb = qf.astype(BF16)
        g3 = g.reshape(nseq, ld, HG_DIM)
        a = jnp.zeros((r, r), F32)
        for p in range(ld):
            g_p = jnp.broadcast_to(g3[:, p:p + 1, :], g3.shape).reshape(r, HG_DIM)
            k_p = kk * jnp.exp2(jnp.minimum(g_p - g, 0.0))
            a = a + jnp.where(causal & (pos == p), _dot_nt(qb, k_p.astype(BF16)), 0.0)
        k_end_t = (kk * jnp.exp2(g_last - g)).T
        decay_t = jnp.exp2(g_last).T
        yield
        intra = _dot(a.astype(BF16), v)
        outs = []
        for i in range(nseq):
            s_prev = s0_ref[i, hd]
            outs.append(_dot(q_t[i * ld:(i + 1) * ld], s_prev.astype(BF16)))
            own = (lane >= i * ld) & (lane < (i + 1) * ld)
            upd = _dot(jnp.where(own, k_end_t, 0.0).astype(BF16), v)
            s_out_ref[i, hd] = decay_t[:, i * ld:i * ld + 1] * s_prev + upd
            if i % 2:
                yield
        o = jnp.concatenate(outs, axis=0) + intra
        yield
        o_ref[:, lanes] = _head_out(o, gr_ref[:, lanes], hg_ref[...], ones_ref[...]).astype(o_ref.dtype)

    pool = _Pool()
    for hd in range(HG_HEADS):
        pool.add(hd, head(hd))
    while pool.live:
        pool.round()


def _hgrn_sample(qr, fr, ir, gr, s0, lb_logits, hg_gain, ld, sb):
    bd = s0.shape[0]
    r = sb * ld
    seq = np.arange(r) // ld
    same = seq[:, None] == seq[None, :]
    tri = jnp.asarray((same & (np.arange(r)[:, None] >= np.arange(r)[None, :])).astype(np.float32), BF16)
    tot = jnp.asarray(same.astype(np.float32), BF16)
    ones = jnp.ones((HG_DIM, HG_DIM), BF16)
    tok = pl.BlockSpec((r, HG_HEADS * HG_DIM), lambda i: (i, 0))
    state = pl.BlockSpec((sb, HG_HEADS, HG_DIM, HG_DIM), lambda i: (i, 0, 0, 0))
    const = lambda a: pl.BlockSpec(a.shape, lambda i: (0,) * a.ndim)
    return pl.pallas_call(
        functools.partial(_hgrn_sample_kernel, ld=ld),
        grid=(bd // sb,),
        in_specs=[tok, tok, tok, tok, state, const(lb_logits), const(hg_gain), const(tri), const(tot), const(ones)],
        out_specs=[tok, state],
        out_shape=[jax.ShapeDtypeStruct(qr.shape, BF16), jax.ShapeDtypeStruct(s0.shape, F32)],
        compiler_params=pltpu.CompilerParams(dimension_semantics=("parallel",)),
        name="hgrn_sample",
    )(qr, fr, ir, gr, s0, lb_logits, hg_gain, tri, tot, ones)


def _merge_out(h, sga, sgb, att, orr, wua_ref, wur_ref, wo_ref):
    merged = sga * _dot(att, wua_ref[...]) + sgb * _dot(orr, wur_ref[...])
    return h + _dot(merged.astype(BF16), wo_ref[...])


def _mix_kernel(h_ref, gm_ref, win_ref, qg_ref, kg_ref, seg_ref, seg2_ref, bias_ref, sink_ref, lb_ref, hg_ref,
                tri_ref, ones_ref, wua_ref, wur_ref, wo_ref,
                h2_ref, k_out_ref, v_out_ref, s_out_ref, kprev_ref, vprev_ref, s_ref):
    tm = h_ref.shape[0]
    first = pl.program_id(1) == 0

    @pl.when(first)
    def _():
        kprev_ref[...] = jnp.zeros(kprev_ref.shape, F32)
        vprev_ref[...] = jnp.zeros(vprev_ref.shape, F32)
        s_ref[...] = jnp.zeros(s_ref.shape, F32)

    h = h_ref[...]
    u_blocks = _normed_blocks(h_ref, gm_ref[...], NORM_BLOCKS)
    u = jnp.concatenate(u_blocks, axis=0)

    pool = _Pool()
    lb = _lower_bound(lb_ref[...])
    for kv in range(N_KV_HEADS):
        pool.add(("q", kv), _q_cols(u if kv else u_blocks, win_ref, kv, seg2_ref[...], qg_ref[...]))
    pool.add("kv", _kv_cols(u, win_ref, seg_ref[...], kg_ref[...]))
    for pair in range(HG_HEADS // 2):
        pool.add(("hproj", pair), _hgrn_proj(u, win_ref, pair))
    pool.add("gates", _gate_stream(u, win_ref))
    first_keys = [("q", kv) for kv in range(N_KV_HEADS)] + ["kv"] + [("hproj", p) for p in range(HG_HEADS // 2)]
    while not pool.has(first_keys):
        pool.round()
    for hd in range(HG_HEADS):
        qr, fr, ir, gr = (a[:, (hd % 2) * HG_DIM:(hd % 2 + 1) * HG_DIM] for a in pool.out[("hproj", hd // 2)])
        pool.add(("hgrn", hd), _hgrn_head(qr, fr, ir, gr, lb[:, hd * HG_DIM:(hd + 1) * HG_DIM], hd, s_ref, s_out_ref,
                                          hg_ref[...], ones_ref[...], tri_ref[...]))

    q = [pool.out[("q", j // 2)][:, (j % 2) * LANES:(j % 2 + 1) * LANES] for j in range(SPLIT_SIZES[0] // LANES)]
    k, v = pool.out["kv"]
    kk_all = jnp.concatenate([kprev_ref[...], k], axis=0)
    vv_all = jnp.concatenate([vprev_ref[...], v], axis=0)
    k_tail, v_tail = k[tm - WINDOW:], v[tm - WINDOW:]
    kprev_ref[...] = k_tail
    vprev_ref[...] = v_tail
    k_out_ref[...] = k_tail
    v_out_ref[...] = v_tail
    lane = lax.broadcasted_iota(jnp.int32, kk_all.shape, 1)
    k_lo0 = jnp.where(lane < HEAD_DIM, kk_all, 0.0)
    k_hi1 = jnp.where(lane >= HEAD_DIM, kk_all, 0.0)
    k_halves = ((k_lo0.astype(BF16), pltpu.roll(k_lo0, HEAD_DIM, 1).astype(BF16)),
                (pltpu.roll(k_hi1, HEAD_DIM, 1).astype(BF16), k_hi1.astype(BF16)))
    v_t = vv_all.T.astype(BF16)

    n_blk = tm // WINDOW
    chain_keys = lambda blk: [("att", blk, kv, half) for kv in range(N_KV_HEADS) for half in range(2)]
    for blk in range(n_blk):
        lo = blk * WINDOW
        variant = jnp.where(first, 0, 1) if blk == 0 else 1
        for kv in range(N_KV_HEADS):
            q_pair = jnp.concatenate([q[2 * kv + j][lo:lo + WINDOW] for j in range(GROUP // 2)], axis=0)
            for half in range(2):
                pool.add(("att", blk, kv, half),
                         _attn_chain(q_pair, k_halves[kv][half][lo:lo + 2 * WINDOW], v_t[:, lo:lo + 2 * WINDOW],
                                     bias_ref.at[variant, kv, half], sink_ref[GROUP * kv + half],
                                     sink_ref[GROUP * kv + 2 + half], kv))

    waiting = list(range(n_blk))
    while pool.live:
        pool.round()
        for blk in [b for b in waiting if pool.has(chain_keys(b))]:
            waiting.remove(blk)
            parts = []
            for hq in range(N_Q_HEADS):
                kv, j, half = hq // GROUP, (hq % GROUP) // 2, hq % 2
                parts.append(pool.out[("att", blk, kv, half)][:, j * WINDOW:(j + 1) * WINDOW])
            pool.add(("up", blk), _attn_up(parts, wua_ref))

    up_a = jnp.concatenate([pool.out[("up", blk)] for blk in range(n_blk)], axis=0)
    sga, sgb = pool.out["gates"]
    heads = [pool.out[("hgrn", hd)] for hd in range(HG_HEADS)]

    def finish(orr_heads):
        orr = jnp.concatenate(orr_heads, axis=1)
        merged = sga * up_a + sgb * _dot(orr, wur_ref[...])
        h2_ref[...] = h + _dot(merged.astype(BF16), wo_ref[...])

    finish([o for o, _, _ in heads])

    worst = functools.reduce(jnp.minimum, [w for _, w, _ in heads])

    @pl.when(jnp.min(worst) < -EXP_CLAMP)
    def _():
        finish([_hgrn_redo(redo, hg_ref[...], ones_ref[...]) for _, _, redo in heads])


def _mix(h, consts, batch, seq, tm):
    t, d = h.shape
    nt = seq // tm
    row = pl.BlockSpec((tm, d), lambda b, i: (b * nt + i, 0))
    win_out = pl.BlockSpec((WINDOW, LANES), lambda b, i: (b, 0))
    return pl.pallas_call(
        _mix_kernel,
        grid=(batch, nt),
        in_specs=[row] + [pl.BlockSpec(memory_space=pltpu.SMEM) if a.ndim == 1 else _const_spec(a) for a in consts],
        out_specs=[row, win_out, win_out, pl.BlockSpec((1, HG_HEADS, HG_DIM, HG_DIM), lambda b, i: (b, 0, 0, 0))],
        out_shape=[jax.ShapeDtypeStruct((t, d), F32),
                   jax.ShapeDtypeStruct((batch * WINDOW, LANES), F32),
                   jax.ShapeDtypeStruct((batch * WINDOW, LANES), F32),
                   jax.ShapeDtypeStruct((batch, HG_HEADS, HG_DIM, HG_DIM), F32)],
        scratch_shapes=[pltpu.VMEM((WINDOW, LANES), F32), pltpu.VMEM((WINDOW, LANES), F32),
                        pltpu.VMEM((HG_HEADS, HG_DIM, HG_DIM), F32)],
        compiler_params=pltpu.CompilerParams(dimension_semantics=("parallel", "arbitrary"),
                                             vmem_limit_bytes=VMEM_LIMIT),
        name="mix",
    )(h, *consts)


def _proj_kernel(h_ref, gm_ref, win_ref, qg_ref, kg_ref, seg_ref,
                 q_ref, k_ref, v_ref, qr_ref, fr_ref, ir_ref, gr_ref):
    u = _rms_rows(h_ref[...], gm_ref[...]).astype(BF16)
    q, k, v = _project_qkv(u, win_ref, qg_ref[...], kg_ref[...], seg_ref[...])
    q_ref[...] = q
    k_ref[...] = k
    v_ref[...] = v
    qr_ref[...] = _dot(u, win_ref[:, O_QR:O_FR])
    fr_ref[...] = _dot(u, win_ref[:, O_FR:O_IR])
    ir_ref[...] = _dot(u, win_ref[:, O_IR:O_GR]).astype(ir_ref.dtype)
    gr_ref[...] = _dot(u, win_ref[:, O_GR:O_GA])


def _proj(h, gm, w_in, qg, kg, seg, tm):
    t, d = h.shape
    row = lambda w: pl.BlockSpec((tm, w), lambda i: (i, 0))
    widths = SPLIT_SIZES[:7]
    dtypes = (BF16, F32, F32, F32, F32, BF16, F32)
    return pl.pallas_call(
        _proj_kernel,
        grid=(t // tm,),
        in_specs=[row(d)] + [_const_spec(a) for a in (gm, w_in, qg, kg, seg)],
        out_specs=[row(w) for w in widths],
        out_shape=[jax.ShapeDtypeStruct((t, w), dt) for w, dt in zip(widths, dtypes)],
        compiler_params=pltpu.CompilerParams(dimension_semantics=("parallel",), vmem_limit_bytes=VMEM_LIMIT),
        name="proj",
    )(h, gm, w_in, qg, kg, seg)


def _out_kernel(h_ref, att_ref, orr_ref, gm_ref, win_ref, wua_ref, wur_ref, wo_ref, h2_ref):
    h = h_ref[...]
    u = _rms_rows(h, gm_ref[...]).astype(BF16)
    sga = jax.nn.sigmoid(_dot(u, win_ref[:, O_GA:O_GB]))
    sgb = jax.nn.sigmoid(_dot(u, win_ref[:, O_GB:O_END]))
    h2_ref[...] = _merge_out(h, sga, sgb, att_ref[...], orr_ref[...], wua_ref, wur_ref, wo_ref)


def _out(h, att, orr, consts, tm):
    t, d = h.shape
    row = lambda a: pl.BlockSpec((tm, a.shape[1]), lambda i: (i, 0))
    return pl.pallas_call(
        _out_kernel,
        grid=(t // tm,),
        in_specs=[row(a) for a in (h, att, orr)] + [_const_spec(a) for a in consts],
        out_specs=pl.BlockSpec((tm, d), lambda i: (i, 0)),
        out_shape=jax.ShapeDtypeStruct((t, d), F32),
        compiler_params=pltpu.CompilerParams(dimension_semantics=("parallel",), vmem_limit_bytes=VMEM_LIMIT),
        name="out",
    )(h, att, orr, *consts)


def _bucket_map():
    dist = np.arange(WINDOW)[:, None] + WINDOW - np.arange(2 * WINDOW)[None, :]
    valid = (dist >= 0) & (dist <= WINDOW)
    return np.where(valid, _t5_bucket(dist), -1).astype(np.int32)


def _stacked(bias, rows):
    return bias.reshape(N_KV_HEADS, GROUP * rows, bias.shape[-1])


def kernel(x_prompt, x_sample, cache_win_k, cache_win_v, state_hgrn, ffn1_norm, ffn1_w1, ffn1_w3, ffn1_w2, mix_norm, w_in, q_norm, k_norm, sinks, rel_bias_table, hgrn_lb_logits, hg_norm, w_up_attn, w_up_hgrn, w_out, ffn2_norm, ffn2_w1, ffn2_w3, ffn2_w2):
    depth = ffn1_norm.shape[0]
    assert depth == 1 and hgrn_lb_logits.shape[0] == 2, "single-layer step only"
    batch, seq, d = x_prompt.shape
    bd, ld, _ = x_sample.shape

    bf = lambda w: w[0].astype(BF16)
    row = lambda g: g[0].reshape(1, -1).astype(F32)
    w_in_b = bf(w_in)
    wua, wur, wo = bf(w_up_attn), bf(w_up_hgrn), bf(w_out)
    g1, gm, g2 = row(ffn1_norm), row(mix_norm), row(ffn2_norm)
    qg = jnp.tile(row(q_norm), (1, LANES // HEAD_DIM)) * (HEAD_DIM ** -0.5 * LOG2E)
    kg = jnp.tile(row(k_norm), (1, LANES // HEAD_DIM))
    head_of_lane = np.arange(2 * LANES) // HEAD_DIM
    seg2 = jnp.asarray((head_of_lane[:, None] == head_of_lane[None, :]).astype(np.float32), BF16)
    seg = seg2[:LANES, :LANES]
    qg2 = jnp.tile(qg, (1, 2))
    ones = jnp.ones((HG_DIM, HG_DIM), BF16)
    tri = jnp.asarray(np.tril(np.ones((HG_CHUNK, HG_CHUNK), np.float32)), BF16)
    lb_logits = hgrn_lb_logits.astype(F32)
    hg_gain = row(hg_norm)

    later = _rel_bias(rel_bias_table.astype(F32) * LOG2E, np.ascontiguousarray(_bucket_map().T))
    first = jnp.where(np.arange(2 * WINDOW)[None, :, None] >= WINDOW, later, NEG_BIG)
    def paired(b):
        b = b.reshape(N_KV_HEADS, GROUP // 2, 2, 2 * WINDOW, WINDOW)
        return b.transpose(0, 2, 3, 1, 4).reshape(N_KV_HEADS, 2, 2 * WINDOW, 2 * WINDOW)
    bias_p = jnp.stack([paired(first), paired(later)])
    bias_s = _stacked(later[:, :WINDOW + ld, :ld].transpose(0, 2, 1), ld)
    sink_1d = sinks[0].astype(F32) * LOG2E
    sink = sink_1d.reshape(N_KV_HEADS, GROUP, 1)
    sink_s = jnp.repeat(sink, ld, axis=1).reshape(N_KV_HEADS, GROUP * ld, 1)

    h_p, w1a, w3a, w2a = _ffn_cast(x_prompt.reshape(batch * seq, d), g1, ffn1_w1[0], ffn1_w3[0], ffn1_w2[0], TM_FFN)
    h_s = _ffn(x_sample.reshape(bd * ld, d), g1, w1a, w3a, w2a, TM_FFN)

    consts = (gm, w_in_b, qg2, kg, seg, seg2, bias_p, sink_1d, lb_logits, hg_gain, tri, ones, wua, wur, wo)
    h2_p, k_p, v_p, s_p = _mix(h_p, consts, batch, seq, TM_MIX)
    win = lambda a: a.reshape(1, batch, WINDOW, N_KV_HEADS, HEAD_DIM)

    q, k, v, qr, fr, ir, gr = _proj(h_s, gm, w_in_b, qg, kg, seg, TM_SAMPLE)
    ck = cache_win_k[0].reshape(bd, WINDOW, LANES)
    cv = cache_win_v[0].reshape(bd, WINDOW, LANES)
    att, nk, nv = _swa_sample(q, k, v, ck, cv, bias_s, sink_s, ld, SEQS_PER_STEP)
    orr, s_s = _hgrn_sample(qr, fr, ir, gr, state_hgrn[0], lb_logits, hg_gain, ld, SEQS_PER_STEP)
    h2_s = _out(h_s, att, orr, (gm, w_in_b, wua, wur, wo), TM_SAMPLE)

    y_p, w1b, w3b, w2b = _ffn_cast(h2_p, g2, ffn2_w1[0], ffn2_w3[0], ffn2_w2[0], TM_FFN)
    y_s = _ffn(h2_s, g2, w1b, w3b, w2b, TM_FFN)
    unwin = lambda a: a.reshape(1, bd, WINDOW, N_KV_HEADS, HEAD_DIM)
    return (y_p.reshape(batch, seq, d), y_s.reshape(bd, ld, d), win(k_p), win(v_p), s_p[None],
            unwin(nk), unwin(nv), s_s[None])
```

```python
import functools

import numpy as np
import jax
import jax.numpy as jnp
from jax import lax
from jax.experimental import pallas as pl
from jax.experimental.pallas import tpu as pltpu

F32 = jnp.float32
BF16 = jnp.bfloat16

LANES = 128
HEAD_DIM = 64
N_Q_HEADS = 8
N_KV_HEADS = 2
GROUP = N_Q_HEADS // N_KV_HEADS
WINDOW = 128
N_BUCKETS = 32
MAX_DISTANCE = 128
HG_HEADS = 4
HG_DIM = 128
HG_CHUNK = 128
HG_SUB = 64
EPS = 1e-6
F_TINY = 1e-30
NEG_BIG = -1e30
EXP_CLAMP = 120.0
LOG2E = 1.4426950408889634
SPLIT_SIZES = (512, 128, 128, 512, 512, 512, 512, 1024, 1024)
SPLIT_OFFS = tuple(int(v) for v in np.cumsum((0,) + SPLIT_SIZES))
O_Q, O_K, O_V, O_QR, O_FR, O_IR, O_GR, O_GA, O_GB, O_END = SPLIT_OFFS
VMEM_LIMIT = 56 * 1024 * 1024
TM_FFN = 1024
NORM_BLOCKS = 4
TM_MIX = 512
TM_SAMPLE = 256
SEQS_PER_STEP = 16


def _dot(a, b):
    return jnp.dot(a, b, preferred_element_type=F32)


def _dot_nt(a, b):
    return lax.dot_general(a, b, (((1,), (1,)), ((), ())), preferred_element_type=F32)


def _split2_ldot(w, x):
    hi = x.astype(BF16)
    lo = (x - hi.astype(F32)).astype(BF16)
    return _dot(w, hi) + _dot(w, lo)


def _rms_rows(x, g):
    ms = jnp.mean(x * x, axis=-1, keepdims=True)
    return x * lax.rsqrt(ms + EPS) * g


def _seg_rms(x, seg, inv_n, g):
    ms = _dot((x * x).astype(BF16), seg) * inv_n
    return x * lax.rsqrt(ms + EPS) * g


class _Pool:
    def __init__(self):
        self.live = []
        self.out = {}

    def add(self, key, gen):
        self.live.append((key, gen))

    def round(self):
        still = []
        for key, gen in self.live:
            try:
                next(gen)
                still.append((key, gen))
            except StopIteration as done:
                self.out[key] = done.value
        self.live = still

    def has(self, keys):
        return all(k in self.out for k in keys)


def _const_spec(a):
    nd = a.ndim
    return pl.BlockSpec(a.shape, lambda *_: (0,) * nd, pipeline_mode=pl.Buffered(1))


def _ff_chunks(d_ff, step=1024):
    return tuple((lo, min(lo + step, d_ff)) for lo in range(0, d_ff, step))


def _swiglu(xn_blocks, w1_ref, w3_ref, w2_ref):
    xn = None
    acc = None
    for c, (lo, hi) in enumerate(_ff_chunks(w1_ref.shape[1])):
        if c == 0:
            a = jnp.concatenate([_dot(blk, w1_ref[:, lo:hi]) for blk in xn_blocks], axis=0)
            b = jnp.concatenate([_dot(blk, w3_ref[:, lo:hi]) for blk in xn_blocks], axis=0)
            xn = jnp.concatenate(xn_blocks, axis=0)
        else:
            a = _dot(xn, w1_ref[:, lo:hi])
            b = _dot(xn, w3_ref[:, lo:hi])
        part = _dot((jax.nn.silu(a) * b).astype(BF16), w2_ref[lo:hi, :])
        acc = part if acc is None else acc + part
    return acc


def _normed_blocks(x_ref, g, blocks):
    rows = x_ref.shape[0] // blocks
    return [_rms_rows(x_ref[r * rows:(r + 1) * rows, :], g).astype(BF16) for r in range(blocks)]


def _ffn_kernel(x_ref, g1_ref, w1_ref, w3_ref, w2_ref, h_ref):
    xn = _normed_blocks(x_ref, g1_ref[...], NORM_BLOCKS)
    h_ref[...] = x_ref[...] + 0.5 * _swiglu(xn, w1_ref, w3_ref, w2_ref)


CAST_CHUNKS = 8


def _cast_copies(pairs, stage_refs, sem):
    out = []
    for (src, dst), stage in zip(pairs, stage_refs):
        rows = src.shape[0] // CAST_CHUNKS
        for c in range(CAST_CHUNKS):
            slot = len(out) % 2
            staged = stage.at[slot]
            out.append((pltpu.make_async_copy(src.at[pl.ds(c * rows, rows), :], staged, sem.at[slot]),
                        staged, dst.at[pl.ds(c * rows, rows), :]))
    return out


def _ffn_cast_kernel(x_ref, g1_ref, w1_hbm, w3_hbm, w2_hbm, h_ref, w1_out, w3_out, w2_out,
                     w1_ref, w3_ref, w2_ref, stage_in_ref, stage_out_ref, sem_in, sem_out):
    step = pl.program_id(0)
    resident = ((w1_ref, w1_out), (w3_ref, w3_out), (w2_ref, w2_out))
    writebacks = [pltpu.make_async_copy(src, dst, sem_out.at[k]) for k, (src, dst) in enumerate(resident)]

    @pl.when(step == 0)
    def _():
        copies = _cast_copies(((w1_hbm, w1_ref), (w3_hbm, w3_ref), (w2_hbm, w2_ref)),
                              (stage_in_ref, stage_in_ref, stage_out_ref), sem_in)
        copies[0][0].start()
        for j, (dma, staged, dst) in enumerate(copies):
            if j + 1 < len(copies):
                copies[j + 1][0].start()
            dma.wait()
            dst[...] = staged[...].astype(BF16)
        for wb in writebacks:
            wb.start()

    xn = _normed_blocks(x_ref, g1_ref[...], NORM_BLOCKS)
    h_ref[...] = x_ref[...] + 0.5 * _swiglu(xn, w1_ref, w3_ref, w2_ref)

    @pl.when(step == pl.num_programs(0) - 1)
    def _():
        for wb in writebacks:
            wb.wait()


def _ffn_cast(x, g1, w1, w3, w2, tm):
    t, d = x.shape
    d_ff = w1.shape[1]
    row = pl.BlockSpec((tm, d), lambda i: (i, 0))
    hbm = pl.BlockSpec(memory_space=pl.ANY)
    bf = lambda w: jax.ShapeDtypeStruct(w.shape, BF16)
    return pl.pallas_call(
        _ffn_cast_kernel,
        grid=(t // tm,),
        in_specs=[row, _const_spec(g1), hbm, hbm, hbm],
        out_specs=[row, hbm, hbm, hbm],
        out_shape=[jax.ShapeDtypeStruct((t, d), F32), bf(w1), bf(w3), bf(w2)],
        scratch_shapes=[pltpu.VMEM(w1.shape, BF16), pltpu.VMEM(w3.shape, BF16), pltpu.VMEM(w2.shape, BF16),
                        pltpu.VMEM((2, d // CAST_CHUNKS, d_ff), F32), pltpu.VMEM((2, d_ff // CAST_CHUNKS, d), F32),
                        pltpu.SemaphoreType.DMA((2,)), pltpu.SemaphoreType.DMA((3,))],
        compiler_params=pltpu.CompilerParams(dimension_semantics=("arbitrary",), vmem_limit_bytes=VMEM_LIMIT),
        name="ffn_cast",
    )(x, g1, w1, w3, w2)


def _ffn(x, g1, w1, w3, w2, tm):
    t, d = x.shape
    row = pl.BlockSpec((tm, d), lambda i: (i, 0))
    return pl.pallas_call(
        _ffn_kernel,
        grid=(t // tm,),
        in_specs=[row] + [_const_spec(a) for a in (g1, w1, w3, w2)],
        out_specs=row,
        out_shape=jax.ShapeDtypeStruct((t, d), F32),
        compiler_params=pltpu.CompilerParams(dimension_semantics=("parallel",), vmem_limit_bytes=VMEM_LIMIT),
        name="ffn",
    )(x, g1, w1, w3, w2)


def _t5_bucket(dist):
    max_exact = N_BUCKETS // 2
    d = np.maximum(dist, 0)
    large = max_exact + (np.log(np.maximum(d, 1) / max_exact) / np.log(MAX_DISTANCE / max_exact)
                         * (N_BUCKETS - max_exact)).astype(np.int32)
    large = np.minimum(large, N_BUCKETS - 1)
    return np.where(d < max_exact, d, large).astype(np.int32)


def _bias_kernel(table_ref, bucket_ref, out_ref):
    bucket = bucket_ref[...]
    masked = jnp.where(bucket < 0, NEG_BIG, 0.0).astype(F32)
    for h in range(N_Q_HEADS):
        acc = masked
        for b in range(N_BUCKETS):
            acc = acc + jnp.where(bucket == b, table_ref[b, h], 0.0)
        out_ref[h] = acc


def _rel_bias(table, bucket_map):
    r, c = bucket_map.shape
    return pl.pallas_call(
        _bias_kernel,
        in_specs=[pl.BlockSpec(memory_space=pltpu.SMEM), pl.BlockSpec((r, c), lambda: (0, 0))],
        out_specs=pl.BlockSpec((N_Q_HEADS, r, c), lambda: (0, 0, 0)),
        out_shape=jax.ShapeDtypeStruct((N_Q_HEADS, r, c), F32),
        name="rel_bias",
    )(table, jnp.asarray(bucket_map))


def _project_qkv(u, win_ref, qg, kg, seg):
    qs = []
    for s in range(SPLIT_SIZES[0] // LANES):
        x = _dot(u, win_ref[:, O_Q + s * LANES:O_Q + (s + 1) * LANES])
        qs.append(_seg_rms(x, seg, 1.0 / HEAD_DIM, qg).astype(BF16))
    k = _seg_rms(_dot(u, win_ref[:, O_K:O_V]), seg, 1.0 / HEAD_DIM, kg)
    v = _dot(u, win_ref[:, O_V:O_QR])
    return jnp.concatenate(qs, axis=1), k, v


def _stack_group(q, kv):
    a = q[:, (2 * kv) * LANES:(2 * kv + 1) * LANES]
    b = q[:, (2 * kv + 1) * LANES:(2 * kv + 2) * LANES]
    ar, br = pltpu.roll(a, HEAD_DIM, 1), pltpu.roll(b, HEAD_DIM, 1)
    parts = (a, ar, b, br) if kv == 0 else (ar, a, br, b)
    return jnp.concatenate(parts, axis=0)


def _kv_half(x, kv, fill):
    lane = lax.broadcasted_iota(jnp.int32, x.shape, 1)
    keep = (lane < HEAD_DIM) if kv == 0 else (lane >= HEAD_DIM)
    return jnp.where(keep, x, fill)


def _attend_chain(q, kk, vv, bias_ref, sink_ref, r, kv):
    qs = _stack_group(q.astype(F32), kv).astype(BF16)
    s = _dot_nt(qs, _kv_half(kk, kv, 0.0).astype(BF16)) + bias_ref[kv]
    yield
    sink = sink_ref[kv]
    m = jnp.maximum(jnp.max(s, axis=-1, keepdims=True), sink)
    e = jnp.exp2(s - m).astype(BF16)
    yield
    o = _dot(e, _kv_half(vv, kv, 1.0).astype(BF16))
    yield
    o = o / (pltpu.roll(o, HEAD_DIM, 1) + jnp.exp2(sink - m))
    lane = lax.broadcasted_iota(jnp.int32, (r, LANES), 1)
    slabs = []
    for pair in range(GROUP // 2):
        a, b = o[(2 * pair) * r:(2 * pair + 1) * r], o[(2 * pair + 1) * r:(2 * pair + 2) * r]
        if kv == 0:
            slabs.append(jnp.where(lane < HEAD_DIM, a, pltpu.roll(b, HEAD_DIM, 1)))
        else:
            slabs.append(jnp.where(lane < HEAD_DIM, pltpu.roll(a, HEAD_DIM, 1), b))
    return slabs


def _swa_sample_kernel(q_ref, kn_ref, vn_ref, ck_ref, cv_ref, bias_ref, sink_ref, o_ref, nk_ref, nv_ref, *, ld):
    pool = _Pool()
    nseq = ck_ref.shape[0]
    for i in range(nseq):
        rows = slice(i * ld, (i + 1) * ld)
        kk = jnp.concatenate([ck_ref[i], kn_ref[rows, :]], axis=0)
        vv = jnp.concatenate([cv_ref[i], vn_ref[rows, :]], axis=0)
        nk_ref[i] = kk[ld:, :]
        nv_ref[i] = vv[ld:, :]
        for kv in range(N_KV_HEADS):
            pool.add((i, kv), _attend_chain(q_ref[rows, :], kk, vv, bias_ref, sink_ref, ld, kv))
    while pool.live:
        pool.round()
    for i in range(nseq):
        slabs = [s for kv in range(N_KV_HEADS) for s in pool.out[(i, kv)]]
        o_ref[i * ld:(i + 1) * ld, :] = jnp.concatenate(slabs, axis=1).astype(o_ref.dtype)


def _swa_sample(q, k, v, cache_k, cache_v, bias, sink_col, ld, sb):
    bd = cache_k.shape[0]
    tok = lambda w: pl.BlockSpec((sb * ld, w), lambda i: (i, 0))
    cache = pl.BlockSpec((sb, WINDOW, LANES), lambda i: (i, 0, 0))
    return pl.pallas_call(
        functools.partial(_swa_sample_kernel, ld=ld),
        grid=(bd // sb,),
        in_specs=[tok(4 * LANES), tok(LANES), tok(LANES), cache, cache,
                  pl.BlockSpec(bias.shape, lambda i: (0, 0, 0)), pl.BlockSpec(sink_col.shape, lambda i: (0, 0, 0))],
        out_specs=[tok(4 * LANES), cache, cache],
        out_shape=[jax.ShapeDtypeStruct(q.shape, BF16), jax.ShapeDtypeStruct(cache_k.shape, F32),
                   jax.ShapeDtypeStruct(cache_v.shape, F32)],
        compiler_params=pltpu.CompilerParams(dimension_semantics=("parallel",)),
        name="swa_sample",
    )(q, k, v, cache_k, cache_v, bias, sink_col)


def _lower_bound(lb_logits):
    z = lb_logits - jnp.max(lb_logits, axis=0, keepdims=True)
    e = jnp.exp(z)
    return e[0:1, :] / jnp.sum(e, axis=0, keepdims=True)


def _gates(qr, fr, lb):
    f = lb + (1.0 - lb) * jax.nn.sigmoid(fr)
    return jax.nn.silu(qr), 1.0 - f, jnp.log2(jnp.maximum(f, F_TINY))


def _head_out(o, gr, hg_gain, ones_seg):
    return _seg_rms(o, ones_seg, 1.0 / HG_DIM, hg_gain) * jax.nn.silu(gr)


def _hgrn_proj(u, win_ref, pair):
    cols = lambda off: slice(off + 2 * pair * HG_DIM, off + 2 * (pair + 1) * HG_DIM)
    qr = _dot(u, win_ref[:, cols(O_QR)])
    fr = _dot(u, win_ref[:, cols(O_FR)])
    yield
    ir = _dot(u, win_ref[:, cols(O_IR)]).astype(BF16)
    gr = _dot(u, win_ref[:, cols(O_GR)])
    yield
    return qr, fr, ir, gr


def _hgrn_head(qr, fr, vals, gr, lb, hd, s_ref, s_out_ref, hg_gain, ones, tri):
    qf, kk, g = _gates(qr, fr, lb)
    nchunk = qr.shape[0] // HG_CHUNK
    chunks = [slice(c * HG_CHUNK, (c + 1) * HG_CHUNK) for c in range(nchunk)]
    yield
    gcs = [_split2_ldot(tri, g[rows]) for rows in chunks]
    yield
    q_in, upd, decay, scores = [], [], [], []
    worst = jnp.zeros((1, HG_DIM), F32)
    for rows, gc in zip(chunks, gcs):
        g_last = gc[HG_CHUNK - 1:HG_CHUNK, :]
        q_in.append((qf[rows] * jnp.exp2(gc)).astype(BF16))
        k_end_t = (kk[rows] * jnp.exp2(g_last - gc)).T
        upd.append(_dot(k_end_t.astype(BF16), vals[rows]))
        decay.append(jnp.broadcast_to(jnp.exp2(g_last), (HG_DIM, HG_DIM)).T)
        qc, kc = qf[rows], kk[rows]
        for i in range(HG_CHUNK // HG_SUB):
            lo, hi = i * HG_SUB, (i + 1) * HG_SUB
            g_base = gc[lo - 1:lo, :] if i else jnp.zeros((1, HG_DIM), F32)
            worst = jnp.minimum(worst, gc[hi - 1:hi, :] - g_base)
            q_hat = qc[lo:hi] * jnp.exp2(gc[lo:hi] - g_base)
            k_hat = kc[:hi] * jnp.exp2(jnp.minimum(g_base - gc[:hi], EXP_CLAMP))
            a = _dot_nt(q_hat.astype(BF16), k_hat.astype(BF16))
            row = lax.broadcasted_iota(jnp.int32, (HG_SUB, hi), 0)
            col = lax.broadcasted_iota(jnp.int32, (HG_SUB, hi), 1)
            scores.append(jnp.where(col <= row + lo, a, 0.0).astype(BF16))
    yield
    intra = []
    nsub = HG_CHUNK // HG_SUB
    for c, rows in enumerate(chunks):
        vc = vals[rows]
        parts = [_dot(scores[c * nsub + i], vc[:(i + 1) * HG_SUB]) for i in range(nsub)]
        intra.append(jnp.concatenate(parts, axis=0))
    yield
    s = s_ref[hd]
    inter = []
    for c in range(nchunk):
        inter.append(_dot(q_in[c], s.astype(BF16)))
        s = decay[c] * s + upd[c]
    s_ref[hd] = s
    s_out_ref[0, hd] = s
    yield
    o = jnp.concatenate([a + b for a, b in zip(inter, intra)], axis=0)
    redo = dict(qf=qf, kk=kk, gcs=gcs, vals=vals, inter=inter, gr=gr)
    return _head_out(o, gr, hg_gain, ones).astype(BF16), worst, redo


def _exact_intra(qc, kc, gc, vc):
    c = qc.shape[0]
    group = 8
    row = lax.broadcasted_iota(jnp.int32, (c, c), 0)
    col = lax.broadcasted_iota(jnp.int32, (c, c), 1)
    col_g = lax.broadcasted_iota(jnp.int32, (group, c), 1)
    earlier = []
    for i in range(c // group):
        lo, hi = i * group, (i + 1) * group
        g_base = gc[lo - 1:lo, :] if i else jnp.zeros((1, HG_DIM), F32)
        q_hat = qc[lo:hi] * jnp.exp2(gc[lo:hi] - g_base)
        k_hat = kc * jnp.exp2(jnp.minimum(g_base - gc, 0.0))
        earlier.append(jnp.where(col_g < lo, _dot_nt(q_hat.astype(BF16), k_hat.astype(BF16)), 0.0))
    a = jnp.concatenate(earlier, axis=0)
    qb = qc.astype(BF16)
    g3 = gc.reshape(c // group, group, HG_DIM)
    own = (col <= row) & (col >= row - lax.rem(row, group))
    for p in range(group):
        g_p = jnp.broadcast_to(g3[:, p:p + 1, :], g3.shape).reshape(c, HG_DIM)
        k_p = kc * jnp.exp2(jnp.minimum(g_p - gc, 0.0))
        a = a + jnp.where(own & (lax.rem(row, group) == p), _dot_nt(qb, k_p.astype(BF16)), 0.0)
    return _dot(a.astype(BF16), vc)


def _hgrn_redo(redo, hg_gain, ones):
    outs = []
    for c, gc in enumerate(redo["gcs"]):
        rows = slice(c * HG_CHUNK, (c + 1) * HG_CHUNK)
        outs.append(redo["inter"][c] + _exact_intra(redo["qf"][rows], redo["kk"][rows], gc, redo["vals"][rows]))
    return _head_out(jnp.concatenate(outs, axis=0), redo["gr"], hg_gain, ones).astype(BF16)


def _attn_chain(q_pair, k_half, v_t, bias_ref, sink_a, sink_b, kv):
    s = _dot_nt(k_half, q_pair) + bias_ref[...]
    yield
    col = lax.broadcasted_iota(jnp.int32, (1, s.shape[1]), 1)
    sink = jnp.where(col < WINDOW, sink_a, sink_b)
    m = jnp.maximum(jnp.max(s, axis=0, keepdims=True), sink)
    e = jnp.exp2(s - m)
    yield
    denom = jnp.sum(e, axis=0, keepdims=True) + jnp.exp2(sink - m)
    o_t = _dot(v_t, e.astype(BF16))[kv * HEAD_DIM:(kv + 1) * HEAD_DIM]
    yield
    return o_t * (1.0 / denom)


def _gate_stream(u, win_ref, step=256):
    res = []
    for off in (O_GA, O_GB):
        cols = []
        for lo in range(0, SPLIT_SIZES[7], step):
            cols.append(jax.nn.sigmoid(_dot(u, win_ref[:, off + lo:off + lo + step])))
            yield
        res.append(jnp.concatenate(cols, axis=1))
    return res


def _q_cols(u, win_ref, kv, seg2, gain2):
    w = win_ref[:, O_Q + 2 * kv * LANES:O_Q + 2 * (kv + 1) * LANES]
    x = jnp.concatenate([_dot(blk, w) for blk in u], axis=0) if isinstance(u, list) else _dot(u, w)
    yield
    ms = _dot((x * x).astype(BF16), seg2) * (1.0 / HEAD_DIM)
    yield
    return (x * lax.rsqrt(ms + EPS) * gain2).astype(BF16)


def _kv_cols(u, win_ref, seg, gain):
    x = _dot(u, win_ref[:, O_K:O_QR])
    yield
    k, v = x[:, :LANES], x[:, LANES:]
    ms = _dot((k * k).astype(BF16), seg) * (1.0 / HEAD_DIM)
    yield
    return k * lax.rsqrt(ms + EPS) * gain, v


def _attn_up(parts, wua_ref):
    att = jnp.concatenate(parts, axis=0).T.astype(BF16)
    yield
    return _dot(att, wua_ref[...])


def _hgrn_sample_kernel(qr_ref, fr_ref, ir_ref, gr_ref, s0_ref, lb_ref, hg_ref, tri_ref, tot_ref, ones_ref,
                        o_ref, s_out_ref, *, ld):
    nseq = s0_ref.shape[0]
    r = nseq * ld
    lb = _lower_bound(lb_ref[...])
    qf_all, kk_all, g_all = _gates(qr_ref[...], fr_ref[...], lb)
    tri, tot = tri_ref[...], tot_ref[...]
    causal = tri.astype(F32) > 0
    lane = lax.broadcasted_iota(jnp.int32, (HG_DIM, r), 1)
    pos = lax.rem(lax.broadcasted_iota(jnp.int32, (r, r), 0), ld)

    def head(hd):
        lanes = slice(hd * HG_DIM, (hd + 1) * HG_DIM)
        qf, kk, v = qf_all[:, lanes], kk_all[:, lanes], ir_ref[:, lanes]
        g = _split2_ldot(tri, g_all[:, lanes])
        g_last = _split2_ldot(tot, g_all[:, lanes])
        yield
        q_t = (qf * jnp.exp2(g)).astype(BF16)
        qb = qf.astype(BF16)
        g3 = g.reshape(nseq, ld, HG_DIM)
        a = jnp.zeros((r, r), F32)
        for p in range(ld):
            g_p = jnp.broadcast_to(g3[:, p:p + 1, :], g3.shape).reshape(r, HG_DIM)
            k_p = kk * jnp.exp2(jnp.minimum(g_p - g, 0.0))
            a = a + jnp.where(causal & (pos == p), _dot_nt(qb, k_p.astype(BF16)), 0.0)
        k_end_t = (kk * jnp.exp2(g_last - g)).T
        decay_t = jnp.exp2(g_last).T
        yield
        intra = _dot(a.astype(BF16), v)
        outs = []
        for i in range(nseq):
            s_prev = s0_ref[i, hd]
            outs.append(_dot(q_t[i * ld:(i + 1) * ld], s_prev.astype(BF16)))
            own = (lane >= i * ld) & (lane < (i + 1) * ld)
            upd = _dot(jnp.where(own, k_end_t, 0.0).astype(BF16), v)
            s_out_ref[i, hd] = decay_t[:, i * ld:i * ld + 1] * s_prev + upd
            if i % 2:
                yield
        o = jnp.concatenate(outs, axis=0) + intra
        yield
        o_ref[:, lanes] = _head_out(o, gr_ref[:, lanes], hg_ref[...], ones_ref[...]).astype(o_ref.dtype)

    pool = _Pool()
    for hd in range(HG_HEADS):
        pool.add(hd, head(hd))
    while pool.live:
        pool.round()


def _hgrn_sample(qr, fr, ir, gr, s0, lb_logits, hg_gain, ld, sb):
    bd = s0.shape[0]
    r = sb * ld
    seq = np.arange(r) // ld
    same = seq[:, None] == seq[None, :]
    tri = jnp.asarray((same & (np.arange(r)[:, None] >= np.arange(r)[None, :])).astype(np.float32), BF16)
    tot = jnp.asarray(same.astype(np.float32), BF16)
    ones = jnp.ones((HG_DIM, HG_DIM), BF16)
    tok = pl.BlockSpec((r, HG_HEADS * HG_DIM), lambda i: (i, 0))
    state = pl.BlockSpec((sb, HG_HEADS, HG_DIM, HG_DIM), lambda i: (i, 0, 0, 0))
    const = lambda a: pl.BlockSpec(a.shape, lambda i: (0,) * a.ndim)
    return pl.pallas_call(
        functools.partial(_hgrn_sample_kernel, ld=ld),
        grid=(bd // sb,),
        in_specs=[tok, tok, tok, tok, state, const(lb_logits), const(hg_gain), const(tri), const(tot), const(ones)],
        out_specs=[tok, state],
        out_shape=[jax.ShapeDtypeStruct(qr.shape, BF16), jax.ShapeDtypeStruct(s0.shape, F32)],
        compiler_params=pltpu.CompilerParams(dimension_semantics=("parallel",)),
        name="hgrn_sample",
    )(qr, fr, ir, gr, s0, lb_logits, hg_gain, tri, tot, ones)


def _merge_out(h, sga, sgb, att, orr, wua_ref, wur_ref, wo_ref):
    merged = sga * _dot(att, wua_ref[...]) + sgb * _dot(orr, wur_ref[...])
    return h + _dot(merged.astype(BF16), wo_ref[...])


def _mix_kernel(h_ref, gm_ref, win_ref, qg_ref, kg_ref, seg_ref, seg2_ref, bias_ref, sink_ref, lb_ref, hg_ref,
                tri_ref, ones_ref, wua_ref, wur_ref, wo_ref,
                h2_ref, k_out_ref, v_out_ref, s_out_ref, kprev_ref, vprev_ref, s_ref):
    tm = h_ref.shape[0]
    first = pl.program_id(1) == 0

    @pl.when(first)
    def _():
        kprev_ref[...] = jnp.zeros(kprev_ref.shape, F32)
        vprev_ref[...] = jnp.zeros(vprev_ref.shape, F32)
        s_ref[...] = jnp.zeros(s_ref.shape, F32)

    h = h_ref[...]
    u_blocks = _normed_blocks(h_ref, gm_ref[...], NORM_BLOCKS)
    u = jnp.concatenate(u_blocks, axis=0)

    pool = _Pool()
    lb = _lower_bound(lb_ref[...])
    for kv in range(N_KV_HEADS):
        pool.add(("q", kv), _q_cols(u if kv else u_blocks, win_ref, kv, seg2_ref[...], qg_ref[...]))
    pool.add("kv", _kv_cols(u, win_ref, seg_ref[...], kg_ref[...]))
    for pair in range(HG_HEADS // 2):
        pool.add(("hproj", pair), _hgrn_proj(u, win_ref, pair))
    pool.add("gates", _gate_stream(u, win_ref))
    first_keys = [("q", kv) for kv in range(N_KV_HEADS)] + ["kv"] + [("hproj", p) for p in range(HG_HEADS // 2)]
    while not pool.has(first_keys):
        pool.round()
    for hd in range(HG_HEADS):
        qr, fr, ir, gr = (a[:, (hd % 2) * HG_DIM:(hd % 2 + 1) * HG_DIM] for a in pool.out[("hproj", hd // 2)])
        pool.add(("hgrn", hd), _hgrn_head(qr, fr, ir, gr, lb[:, hd * HG_DIM:(hd + 1) * HG_DIM], hd, s_ref, s_out_ref,
                                          hg_ref[...], ones_ref[...], tri_ref[...]))

    q = [pool.out[("q", j // 2)][:, (j % 2) * LANES:(j % 2 + 1) * LANES] for j in range(SPLIT_SIZES[0] // LANES)]
    k, v = pool.out["kv"]
    kk_all = jnp.concatenate([kprev_ref[...], k], axis=0)
    vv_all = jnp.concatenate([vprev_ref[...], v], axis=0)
    k_tail, v_tail = k[tm - WINDOW:], v[tm - WINDOW:]
    kprev_ref[...] = k_tail
    vprev_ref[...] = v_tail
    k_out_ref[...] = k_tail
    v_out_ref[...] = v_tail
    lane = lax.broadcasted_iota(jnp.int32, kk_all.shape, 1)
    k_lo0 = jnp.where(lane < HEAD_DIM, kk_all, 0.0)
    k_hi1 = jnp.where(lane >= HEAD_DIM, kk_all, 0.0)
    k_halves = ((k_lo0.astype(BF16), pltpu.roll(k_lo0, HEAD_DIM, 1).astype(BF16)),
                (pltpu.roll(k_hi1, HEAD_DIM, 1).astype(BF16), k_hi1.astype(BF16)))
    v_t = vv_all.T.astype(BF16)

    n_blk = tm // WINDOW
    chain_keys = lambda blk: [("att", blk, kv, half) for kv in range(N_KV_HEADS) for half in range(2)]
    for blk in range(n_blk):
        lo = blk * WINDOW
        variant = jnp.where(first, 0, 1) if blk == 0 else 1
        for kv in range(N_KV_HEADS):
            q_pair = jnp.concatenate([q[2 * kv + j][lo:lo + WINDOW] for j in range(GROUP // 2)], axis=0)
            for half in range(2):
                pool.add(("att", blk, kv, half),
                         _attn_chain(q_pair, k_halves[kv][half][lo:lo + 2 * WINDOW], v_t[:, lo:lo + 2 * WINDOW],
                                     bias_ref.at[variant, kv, half], sink_ref[GROUP * kv + half],
                                     sink_ref[GROUP * kv + 2 + half], kv))

    waiting = list(range(n_blk))
    while pool.live:
        pool.round()
        for blk in [b for b in waiting if pool.has(chain_keys(b))]:
            waiting.remove(blk)
            parts = []
            for hq in range(N_Q_HEADS):
                kv, j, half = hq // GROUP, (hq % GROUP) // 2, hq % 2
                parts.append(pool.out[("att", blk, kv, half)][:, j * WINDOW:(j + 1) * WINDOW])
            pool.add(("up", blk), _attn_up(parts, wua_ref))

    up_a = jnp.concatenate([pool.out[("up", blk)] for blk in range(n_blk)], axis=0)
    sga, sgb = pool.out["gates"]
    heads = [pool.out[("hgrn", hd)] for hd in range(HG_HEADS)]

    def finish(orr_heads):
        orr = jnp.concatenate(orr_heads, axis=1)
        merged = sga * up_a + sgb * _dot(orr, wur_ref[...])
        h2_ref[...] = h + _dot(merged.astype(BF16), wo_ref[...])

    finish([o for o, _, _ in heads])

    worst = functools.reduce(jnp.minimum, [w for _, w, _ in heads])

    @pl.when(jnp.min(worst) < -EXP_CLAMP)
    def _():
        finish([_hgrn_redo(redo, hg_ref[...], ones_ref[...]) for _, _, redo in heads])


def _mix(h, consts, batch, seq, tm):
    t, d = h.shape
    nt = seq // tm
    row = pl.BlockSpec((tm, d), lambda b, i: (b * nt + i, 0))
    win_out = pl.BlockSpec((WINDOW, LANES), lambda b, i: (b, 0))
    return pl.pallas_call(
        _mix_kernel,
        grid=(batch, nt),
        in_specs=[row] + [pl.BlockSpec(memory_space=pltpu.SMEM) if a.ndim == 1 else _const_spec(a) for a in consts],
        out_specs=[row, win_out, win_out, pl.BlockSpec((1, HG_HEADS, HG_DIM, HG_DIM), lambda b, i: (b, 0, 0, 0))],
        out_shape=[jax.ShapeDtypeStruct((t, d), F32),
                   jax.ShapeDtypeStruct((batch * WINDOW, LANES), F32),
                   jax.ShapeDtypeStruct((batch * WINDOW, LANES), F32),
                   jax.ShapeDtypeStruct((batch, HG_HEADS, HG_DIM, HG_DIM), F32)],
        scratch_shapes=[pltpu.VMEM((WINDOW, LANES), F32), pltpu.VMEM((WINDOW, LANES), F32),
                        pltpu.VMEM((HG_HEADS, HG_DIM, HG_DIM), F32)],
        compiler_params=pltpu.CompilerParams(dimension_semantics=("parallel", "arbitrary"),
                                             vmem_limit_bytes=VMEM_LIMIT),
        name="mix",
    )(h, *consts)


def _proj_kernel(h_ref, gm_ref, win_ref, qg_ref, kg_ref, seg_ref,
                 q_ref, k_ref, v_ref, qr_ref, fr_ref, ir_ref, gr_ref):
    u = _rms_rows(h_ref[...], gm_ref[...]).astype(BF16)
    q, k, v = _project_qkv(u, win_ref, qg_ref[...], kg_ref[...], seg_ref[...])
    q_ref[...] = q
    k_ref[...] = k
    v_ref[...] = v
    qr_ref[...] = _dot(u, win_ref[:, O_QR:O_FR])
    fr_ref[...] = _dot(u, win_ref[:, O_FR:O_IR])
    ir_ref[...] = _dot(u, win_ref[:, O_IR:O_GR]).astype(ir_ref.dtype)
    gr_ref[...] = _dot(u, win_ref[:, O_GR:O_GA])


def _proj(h, gm, w_in, qg, kg, seg, tm):
    t, d = h.shape
    row = lambda w: pl.BlockSpec((tm, w), lambda i: (i, 0))
    widths = SPLIT_SIZES[:7]
    dtypes = (BF16, F32, F32, F32, F32, BF16, F32)
    return pl.pallas_call(
        _proj_kernel,
        grid=(t // tm,),
        in_specs=[row(d)] + [_const_spec(a) for a in (gm, w_in, qg, kg, seg)],
        out_specs=[row(w) for w in widths],
        out_shape=[jax.ShapeDtypeStruct((t, w), dt) for w, dt in zip(widths, dtypes)],
        compiler_params=pltpu.CompilerParams(dimension_semantics=("parallel",), vmem_limit_bytes=VMEM_LIMIT),
        name="proj",
    )(h, gm, w_in, qg, kg, seg)


def _out_kernel(h_ref, att_ref, orr_ref, gm_ref, win_ref, wua_ref, wur_ref, wo_ref, h2_ref):
    h = h_ref[...]
    u = _rms_rows(h, gm_ref[...]).astype(BF16)
    sga = jax.nn.sigmoid(_dot(u, win_ref[:, O_GA:O_GB]))
    sgb = jax.nn.sigmoid(_dot(u, win_ref[:, O_GB:O_END]))
    h2_ref[...] = _merge_out(h, sga, sgb, att_ref[...], orr_ref[...], wua_ref, wur_ref, wo_ref)


def _out(h, att, orr, consts, tm):
    t, d = h.shape
    row = lambda a: pl.BlockSpec((tm, a.shape[1]), lambda i: (i, 0))
    return pl.pallas_call(
        _out_kernel,
        grid=(t // tm,),
        in_specs=[row(a) for a in (h, att, orr)] + [_const_spec(a) for a in consts],
        out_specs=pl.BlockSpec((tm, d), lambda i: (i, 0)),
        out_shape=jax.ShapeDtypeStruct((t, d), F32),
        compiler_params=pltpu.CompilerParams(dimension_semantics=("parallel",), vmem_limit_bytes=VMEM_LIMIT),
        name="out",
    )(h, att, orr, *consts)


def _bucket_map():
    dist = np.arange(WINDOW)[:, None] + WINDOW - np.arange(2 * WINDOW)[None, :]
    valid = (dist >= 0) & (dist <= WINDOW)
    return np.where(valid, _t5_bucket(dist), -1).astype(np.int32)


def _stacked(bias, rows):
    return bias.reshape(N_KV_HEADS, GROUP * rows, bias.shape[-1])


def kernel(x_prompt, x_sample, cache_win_k, cache_win_v, state_hgrn, ffn1_norm, ffn1_w1, ffn1_w3, ffn1_w2, mix_norm, w_in, q_norm, k_norm, sinks, rel_bias_table, hgrn_lb_logits, hg_norm, w_up_attn, w_up_hgrn, w_out, ffn2_norm, ffn2_w1, ffn2_w3, ffn2_w2):
    depth = ffn1_norm.shape[0]
    assert depth == 1 and hgrn_lb_logits.shape[0] == 2, "single-layer step only"
    batch, seq, d = x_prompt.shape
    bd, ld, _ = x_sample.shape

    bf = lambda w: w[0].astype(BF16)
    row = lambda g: g[0].reshape(1, -1).astype(F32)
    w_in_b = bf(w_in)
    wua, wur, wo = bf(w_up_attn), bf(w_up_hgrn), bf(w_out)
    g1, gm, g2 = row(ffn1_norm), row(mix_norm), row(ffn2_norm)
    qg = jnp.tile(row(q_norm), (1, LANES // HEAD_DIM)) * (HEAD_DIM ** -0.5 * LOG2E)
    kg = jnp.tile(row(k_norm), (1, LANES // HEAD_DIM))
    head_of_lane = np.arange(2 * LANES) // HEAD_DIM
    seg2 = jnp.asarray((head_of_lane[:, None] == head_of_lane[None, :]).astype(np.float32), BF16)
    seg = seg2[:LANES, :LANES]
    qg2 = jnp.tile(qg, (1, 2))
    ones = jnp.ones((HG_DIM, HG_DIM), BF16)
    tri = jnp.asarray(np.tril(np.ones((HG_CHUNK, HG_CHUNK), np.float32)), BF16)
    lb_logits = hgrn_lb_logits.astype(F32)
    hg_gain = row(hg_norm)

    later = _rel_bias(rel_bias_table.astype(F32) * LOG2E, np.ascontiguousarray(_bucket_map().T))
    first = jnp.where(np.arange(2 * WINDOW)[None, :, None] >= WINDOW, later, NEG_BIG)
    def paired(b):
        b = b.reshape(N_KV_HEADS, GROUP // 2, 2, 2 * WINDOW, WINDOW)
        return b.transpose(0, 2, 3, 1, 4).reshape(N_KV_HEADS, 2, 2 * WINDOW, 2 * WINDOW)
    bias_p = jnp.stack([paired(first), paired(later)])
    bias_s = _stacked(later[:, :WINDOW + ld, :ld].transpose(0, 2, 1), ld)
    sink_1d = sinks[0].astype(F32) * LOG2E
    sink = sink_1d.reshape(N_KV_HEADS, GROUP, 1)
    sink_s = jnp.repeat(sink, ld, axis=1).reshape(N_KV_HEADS, GROUP * ld, 1)

    h_p, w1a, w3a, w2a = _ffn_cast(x_prompt.reshape(batch * seq, d), g1, ffn1_w1[0], ffn1_w3[0], ffn1_w2[0], TM_FFN)
    h_s = _ffn(x_sample.reshape(bd * ld, d), g1, w1a, w3a, w2a, TM_FFN)

    consts = (gm, w_in_b, qg2, kg, seg, seg2, bias_p, sink_1d, lb_logits, hg_gain, tri, ones, wua, wur, wo)
    h2_p, k_p, v_p, s_p = _mix(h_p, consts, batch, seq, TM_MIX)
    win = lambda a: a.reshape(1, batch, WINDOW, N_KV_HEADS, HEAD_DIM)

    q, k, v, qr, fr, ir, gr = _proj(h_s, gm, w_in_b, qg, kg, seg, TM_SAMPLE)
    ck = cache_win_k[0].reshape(bd, WINDOW, LANES)
    cv = cache_win_v[0].reshape(bd, WINDOW, LANES)
    att, nk, nv = _swa_sample(q, k, v, ck, cv, bias_s, sink_s, ld, SEQS_PER_STEP)
    orr, s_s = _hgrn_sample(qr, fr, ir, gr, state_hgrn[0], lb_logits, hg_gain, ld, SEQS_PER_STEP)
    h2_s = _out(h_s, att, orr, (gm, w_in_b, wua, wur, wo), TM_SAMPLE)

    y_p, w1b, w3b, w2b = _ffn_cast(h2_p, g2, ffn2_w1[0], ffn2_w3[0], ffn2_w2[0], TM_FFN)
    y_s = _ffn(h2_s, g2, w1b, w3b, w2b, TM_FFN)
    unwin = lambda a: a.reshape(1, bd, WINDOW, N_KV_HEADS, HEAD_DIM)
    return (y_p.reshape(batch, seq, d), y_s.reshape(bd, ld, d), win(k_p), win(v_p), s_p[None],
            unwin(nk), unwin(nv), s_s[None])
```

```python
import functools

import numpy as np
import jax
import jax.numpy as jnp
from jax import lax
from jax.experimental import pallas as pl
from jax.experimental.pallas import tpu as pltpu

F32 = jnp.float32
BF16 = jnp.bfloat16

LANES = 128
HEAD_DIM = 64
N_Q_HEADS = 8
N_KV_HEADS = 2
GROUP = N_Q_HEADS // N_KV_HEADS
WINDOW = 128
N_BUCKETS = 32
MAX_DISTANCE = 128
HG_HEADS = 4
HG_DIM = 128
HG_CHUNK = 128
HG_SUB = 128
EPS = 1e-6
F_TINY = 1e-30
NEG_BIG = -1e30
EXP_CLAMP = 120.0
LOG2E = 1.4426950408889634
SPLIT_SIZES = (512, 128, 128, 512, 512, 512, 512, 1024, 1024)
SPLIT_OFFS = tuple(int(v) for v in np.cumsum((0,) + SPLIT_SIZES))
O_Q, O_K, O_V, O_QR, O_FR, O_IR, O_GR, O_GA, O_GB, O_END = SPLIT_OFFS
VMEM_LIMIT = 56 * 1024 * 1024
TM_FFN = 1024
NORM_BLOCKS = 4
TM_MIX = 512
TM_SAMPLE = 256
SEQS_PER_STEP = 16


def _dot(a, b):
    return jnp.dot(a, b, preferred_element_type=F32)


def _dot_nt(a, b):
    return lax.dot_general(a, b, (((1,), (1,)), ((), ())), preferred_element_type=F32)


def _split2_ldot(w, x):
    hi = x.astype(BF16)
    lo = (x - hi.astype(F32)).astype(BF16)
    return _dot(w, hi) + _dot(w, lo)


def _rms_rows(x, g):
    ms = jnp.mean(x * x, axis=-1, keepdims=True)
    return x * lax.rsqrt(ms + EPS) * g


def _seg_rms(x, seg, inv_n, g):
    ms = _dot((x * x).astype(BF16), seg) * inv_n
    return x * lax.rsqrt(ms + EPS) * g


class _Pool:
    def __init__(self):
        self.live = []
        self.out = {}

    def add(self, key, gen):
        self.live.append((key, gen))

    def round(self):
        still = []
        for key, gen in self.live:
            try:
                next(gen)
                still.append((key, gen))
            except StopIteration as done:
                self.out[key] = done.value
        self.live = still

    def has(self, keys):
        return all(k in self.out for k in keys)


def _const_spec(a):
    nd = a.ndim
    return pl.BlockSpec(a.shape, lambda *_: (0,) * nd, pipeline_mode=pl.Buffered(1))


def _ff_chunks(d_ff, step=1024):
    return tuple((lo, min(lo + step, d_ff)) for lo in range(0, d_ff, step))


def _swiglu(xn_blocks, w1_ref, w3_ref, w2_ref):
    xn = None
    acc = None
    for c, (lo, hi) in enumerate(_ff_chunks(w1_ref.shape[1])):
        if c == 0:
            a = jnp.concatenate([_dot(blk, w1_ref[:, lo:hi]) for blk in xn_blocks], axis=0)
            b = jnp.concatenate([_dot(blk, w3_ref[:, lo:hi]) for blk in xn_blocks], axis=0)
            xn = jnp.concatenate(xn_blocks, axis=0)
        else:
            a = _dot(xn, w1_ref[:, lo:hi])
            b = _dot(xn, w3_ref[:, lo:hi])
        part = _dot((jax.nn.silu(a) * b).astype(BF16), w2_ref[lo:hi, :])
        acc = part if acc is None else acc + part
    return acc


def _normed_blocks(x_ref, g, blocks):
    rows = x_ref.shape[0] // blocks
    return [_rms_rows(x_ref[r * rows:(r + 1) * rows, :], g).astype(BF16) for r in range(blocks)]


def _ffn_kernel(x_ref, g1_ref, w1_ref, w3_ref, w2_ref, h_ref):
    xn = _normed_blocks(x_ref, g1_ref[...], NORM_BLOCKS)
    h_ref[...] = x_ref[...] + 0.5 * _swiglu(xn, w1_ref, w3_ref, w2_ref)


CAST_CHUNKS = 8


def _cast_copies(pairs, stage_refs, sem):
    out = []
    for (src, dst), stage in zip(pairs, stage_refs):
        rows = src.shape[0] // CAST_CHUNKS
        for c in range(CAST_CHUNKS):
            slot = len(out) % 2
            staged = stage.at[slot]
            out.append((pltpu.make_async_copy(src.at[pl.ds(c * rows, rows), :], staged, sem.at[slot]),
                        staged, dst.at[pl.ds(c * rows, rows), :]))
    return out


def _ffn_cast_kernel(x_ref, g1_ref, w1_hbm, w3_hbm, w2_hbm, h_ref, w1_out, w3_out, w2_out,
                     w1_ref, w3_ref, w2_ref, stage_in_ref, stage_out_ref, sem_in, sem_out):
    step = pl.program_id(0)
    resident = ((w1_ref, w1_out), (w3_ref, w3_out), (w2_ref, w2_out))
    writebacks = [pltpu.make_async_copy(src, dst, sem_out.at[k]) for k, (src, dst) in enumerate(resident)]

    @pl.when(step == 0)
    def _():
        copies = _cast_copies(((w1_hbm, w1_ref), (w3_hbm, w3_ref), (w2_hbm, w2_ref)),
                              (stage_in_ref, stage_in_ref, stage_out_ref), sem_in)
        copies[0][0].start()
        for j, (dma, staged, dst) in enumerate(copies):
            if j + 1 < len(copies):
                copies[j + 1][0].start()
            dma.wait()
            dst[...] = staged[...].astype(BF16)
        for wb in writebacks:
            wb.start()

    xn = _normed_blocks(x_ref, g1_ref[...], NORM_BLOCKS)
    h_ref[...] = x_ref[...] + 0.5 * _swiglu(xn, w1_ref, w3_ref, w2_ref)

    @pl.when(step == pl.num_programs(0) - 1)
    def _():
        for wb in writebacks:
            wb.wait()


def _ffn_cast(x, g1, w1, w3, w2, tm):
    t, d = x.shape
    d_ff = w1.shape[1]
    row = pl.BlockSpec((tm, d), lambda i: (i, 0))
    hbm = pl.BlockSpec(memory_space=pl.ANY)
    bf = lambda w: jax.ShapeDtypeStruct(w.shape, BF16)
    return pl.pallas_call(
        _ffn_cast_kernel,
        grid=(t // tm,),
        in_specs=[row, _const_spec(g1), hbm, hbm, hbm],
        out_specs=[row, hbm, hbm, hbm],
        out_shape=[jax.ShapeDtypeStruct((t, d), F32), bf(w1), bf(w3), bf(w2)],
        scratch_shapes=[pltpu.VMEM(w1.shape, BF16), pltpu.VMEM(w3.shape, BF16), pltpu.VMEM(w2.shape, BF16),
                        pltpu.VMEM((2, d // CAST_CHUNKS, d_ff), F32), pltpu.VMEM((2, d_ff // CAST_CHUNKS, d), F32),
                        pltpu.SemaphoreType.DMA((2,)), pltpu.SemaphoreType.DMA((3,))],
        compiler_params=pltpu.CompilerParams(dimension_semantics=("arbitrary",), vmem_limit_bytes=VMEM_LIMIT),
        name="ffn_cast",
    )(x, g1, w1, w3, w2)


def _ffn(x, g1, w1, w3, w2, tm):
    t, d = x.shape
    row = pl.BlockSpec((tm, d), lambda i: (i, 0))
    return pl.pallas_call(
        _ffn_kernel,
        grid=(t // tm,),
        in_specs=[row] + [_const_spec(a) for a in (g1, w1, w3, w2)],
        out_specs=row,
        out_shape=jax.ShapeDtypeStruct((t, d), F32),
        compiler_params=pltpu.CompilerParams(dimension_semantics=("parallel",), vmem_limit_bytes=VMEM_LIMIT),
        name="ffn",
    )(x, g1, w1, w3, w2)


def _t5_bucket(dist):
    max_exact = N_BUCKETS // 2
    d = np.maximum(dist, 0)
    large = max_exact + (np.log(np.maximum(d, 1) / max_exact) / np.log(MAX_DISTANCE / max_exact)
                         * (N_BUCKETS - max_exact)).astype(np.int32)
    large = np.minimum(large, N_BUCKETS - 1)
    return np.where(d < max_exact, d, large).astype(np.int32)


def _bias_kernel(table_ref, bucket_ref, out_ref):
    bucket = bucket_ref[...]
    masked = jnp.where(bucket < 0, NEG_BIG, 0.0).astype(F32)
    for h in range(N_Q_HEADS):
        acc = masked
        for b in range(N_BUCKETS):
            acc = acc + jnp.where(bucket == b, table_ref[b, h], 0.0)
        out_ref[h] = acc


def _rel_bias(table, bucket_map):
    r, c = bucket_map.shape
    return pl.pallas_call(
        _bias_kernel,
        in_specs=[pl.BlockSpec(memory_space=pltpu.SMEM), pl.BlockSpec((r, c), lambda: (0, 0))],
        out_specs=pl.BlockSpec((N_Q_HEADS, r, c), lambda: (0, 0, 0)),
        out_shape=jax.ShapeDtypeStruct((N_Q_HEADS, r, c), F32),
        name="rel_bias",
    )(table, jnp.asarray(bucket_map))


def _project_qkv(u, win_ref, qg, kg, seg):
    qs = []
    for s in range(SPLIT_SIZES[0] // LANES):
        x = _dot(u, win_ref[:, O_Q + s * LANES:O_Q + (s + 1) * LANES])
        qs.append(_seg_rms(x, seg, 1.0 / HEAD_DIM, qg).astype(BF16))
    k = _seg_rms(_dot(u, win_ref[:, O_K:O_V]), seg, 1.0 / HEAD_DIM, kg)
    v = _dot(u, win_ref[:, O_V:O_QR])
    return jnp.concatenate(qs, axis=1), k, v


def _stack_group(q, kv):
    a = q[:, (2 * kv) * LANES:(2 * kv + 1) * LANES]
    b = q[:, (2 * kv + 1) * LANES:(2 * kv + 2) * LANES]
    ar, br = pltpu.roll(a, HEAD_DIM, 1), pltpu.roll(b, HEAD_DIM, 1)
    parts = (a, ar, b, br) if kv == 0 else (ar, a, br, b)
    return jnp.concatenate(parts, axis=0)


def _kv_half(x, kv, fill):
    lane = lax.broadcasted_iota(jnp.int32, x.shape, 1)
    keep = (lane < HEAD_DIM) if kv == 0 else (lane >= HEAD_DIM)
    return jnp.where(keep, x, fill)


def _attend_chain(q, kk, vv, bias_ref, sink_ref, r, kv):
    qs = _stack_group(q.astype(F32), kv).astype(BF16)
    s = _dot_nt(qs, _kv_half(kk, kv, 0.0).astype(BF16)) + bias_ref[kv]
    yield
    sink = sink_ref[kv]
    m = jnp.maximum(jnp.max(s, axis=-1, keepdims=True), sink)
    e = jnp.exp2(s - m).astype(BF16)
    yield
    o = _dot(e, _kv_half(vv, kv, 1.0).astype(BF16))
    yield
    o = o / (pltpu.roll(o, HEAD_DIM, 1) + jnp.exp2(sink - m))
    lane = lax.broadcasted_iota(jnp.int32, (r, LANES), 1)
    slabs = []
    for pair in range(GROUP // 2):
        a, b = o[(2 * pair) * r:(2 * pair + 1) * r], o[(2 * pair + 1) * r:(2 * pair + 2) * r]
        if kv == 0:
            slabs.append(jnp.where(lane < HEAD_DIM, a, pltpu.roll(b, HEAD_DIM, 1)))
        else:
            slabs.append(jnp.where(lane < HEAD_DIM, pltpu.roll(a, HEAD_DIM, 1), b))
    return slabs


def _swa_sample_kernel(q_ref, kn_ref, vn_ref, ck_ref, cv_ref, bias_ref, sink_ref, o_ref, nk_ref, nv_ref, *, ld):
    pool = _Pool()
    nseq = ck_ref.shape[0]
    for i in range(nseq):
        rows = slice(i * ld, (i + 1) * ld)
        kk = jnp.concatenate([ck_ref[i], kn_ref[rows, :]], axis=0)
        vv = jnp.concatenate([cv_ref[i], vn_ref[rows, :]], axis=0)
        nk_ref[i] = kk[ld:, :]
        nv_ref[i] = vv[ld:, :]
        for kv in range(N_KV_HEADS):
            pool.add((i, kv), _attend_chain(q_ref[rows, :], kk, vv, bias_ref, sink_ref, ld, kv))
    while pool.live:
        pool.round()
    for i in range(nseq):
        slabs = [s for kv in range(N_KV_HEADS) for s in pool.out[(i, kv)]]
        o_ref[i * ld:(i + 1) * ld, :] = jnp.concatenate(slabs, axis=1).astype(o_ref.dtype)


def _swa_sample(q, k, v, cache_k, cache_v, bias, sink_col, ld, sb):
    bd = cache_k.shape[0]
    tok = lambda w: pl.BlockSpec((sb * ld, w), lambda i: (i, 0))
    cache = pl.BlockSpec((sb, WINDOW, LANES), lambda i: (i, 0, 0))
    return pl.pallas_call(
        functools.partial(_swa_sample_kernel, ld=ld),
        grid=(bd // sb,),
        in_specs=[tok(4 * LANES), tok(LANES), tok(LANES), cache, cache,
                  pl.BlockSpec(bias.shape, lambda i: (0, 0, 0)), pl.BlockSpec(sink_col.shape, lambda i: (0, 0, 0))],
        out_specs=[tok(4 * LANES), cache, cache],
        out_shape=[jax.ShapeDtypeStruct(q.shape, BF16), jax.ShapeDtypeStruct(cache_k.shape, F32),
                   jax.ShapeDtypeStruct(cache_v.shape, F32)],
        compiler_params=pltpu.CompilerParams(dimension_semantics=("parallel",)),
        name="swa_sample",
    )(q, k, v, cache_k, cache_v, bias, sink_col)


def _lower_bound(lb_logits):
    z = lb_logits - jnp.max(lb_logits, axis=0, keepdims=True)
    e = jnp.exp(z)
    return e[0:1, :] / jnp.sum(e, axis=0, keepdims=True)


def _gates(qr, fr, lb):
    f = lb + (1.0 - lb) * jax.nn.sigmoid(fr)
    return jax.nn.silu(qr), 1.0 - f, jnp.log2(jnp.maximum(f, F_TINY))


def _head_out(o, gr, hg_gain, ones_seg):
    return _seg_rms(o, ones_seg, 1.0 / HG_DIM, hg_gain) * jax.nn.silu(gr)


def _hgrn_proj(u, win_ref, pair):
    cols = lambda off: slice(off + 2 * pair * HG_DIM, off + 2 * (pair + 1) * HG_DIM)
    qr = _dot(u, win_ref[:, cols(O_QR)])
    fr = _dot(u, win_ref[:, cols(O_FR)])
    yield
    ir = _dot(u, win_ref[:, cols(O_IR)]).astype(BF16)
    gr = _dot(u, win_ref[:, cols(O_GR)])
    yield
    return qr, fr, ir, gr


def _hgrn_head(qr, fr, vals, gr, lb, hd, s_ref, s_out_ref, hg_gain, ones, tri):
    qf, kk, g = _gates(qr, fr, lb)
    nchunk = qr.shape[0] // HG_CHUNK
    chunks = [slice(c * HG_CHUNK, (c + 1) * HG_CHUNK) for c in range(nchunk)]
    yield
    gcs = [_split2_ldot(tri, g[rows]) for rows in chunks]
    yield
    q_in, upd, decay, scores = [], [], [], []
    worst = jnp.zeros((1, HG_DIM), F32)
    for rows, gc in zip(chunks, gcs):
        g_last = gc[HG_CHUNK - 1:HG_CHUNK, :]
        q_in.append((qf[rows] * jnp.exp2(gc)).astype(BF16))
        k_end_t = (kk[rows] * jnp.exp2(g_last - gc)).T
        upd.append(_dot(k_end_t.astype(BF16), vals[rows]))
        decay.append(jnp.broadcast_to(jnp.exp2(g_last), (HG_DIM, HG_DIM)).T)
        qc, kc = qf[rows], kk[rows]
        for i in range(HG_CHUNK // HG_SUB):
            lo, hi = i * HG_SUB, (i + 1) * HG_SUB
            mid = lo + HG_SUB // 2
            g_start = gc[lo - 1:lo, :] if i else jnp.zeros((1, HG_DIM), F32)
            g_mid = gc[mid - 1:mid, :]
            worst = jnp.minimum(worst, jnp.minimum(gc[hi - 1:hi, :] - g_mid, g_mid - g_start))
            q_hat = qc[lo:hi] * jnp.exp2(jnp.minimum(gc[lo:hi] - g_mid, EXP_CLAMP))
            k_hat = kc[:hi] * jnp.exp2(jnp.minimum(g_mid - gc[:hi], EXP_CLAMP))
            a = _dot_nt(q_hat.astype(BF16), k_hat.astype(BF16))
            row = lax.broadcasted_iota(jnp.int32, (HG_SUB, hi), 0)
            col = lax.broadcasted_iota(jnp.int32, (HG_SUB, hi), 1)
            scores.append(jnp.where(col <= row + lo, a, 0.0).astype(BF16))
    yield
    intra = []
    nsub = HG_CHUNK // HG_SUB
    for c, rows in enumerate(chunks):
        vc = vals[rows]
        parts = [_dot(scores[c * nsub + i], vc[:(i + 1) * HG_SUB]) for i in range(nsub)]
        intra.append(jnp.concatenate(parts, axis=0))
    yield
    s = s_ref[hd]
    inter = []
    for c in range(nchunk):
        inter.append(_dot(q_in[c], s.astype(BF16)))
        s = decay[c] * s + upd[c]
    s_ref[hd] = s
    s_out_ref[0, hd] = s
    yield
    o = jnp.concatenate([a + b for a, b in zip(inter, intra)], axis=0)
    redo = dict(qf=qf, kk=kk, gcs=gcs, vals=vals, inter=inter, gr=gr)
    return _head_out(o, gr, hg_gain, ones).astype(BF16), worst, redo


def _exact_intra(qc, kc, gc, vc):
    c = qc.shape[0]
    group = 8
    row = lax.broadcasted_iota(jnp.int32, (c, c), 0)
    col = lax.broadcasted_iota(jnp.int32, (c, c), 1)
    col_g = lax.broadcasted_iota(jnp.int32, (group, c), 1)
    earlier = []
    for i in range(c // group):
        lo, hi = i * group, (i + 1) * group
        g_base = gc[lo - 1:lo, :] if i else jnp.zeros((1, HG_DIM), F32)
        q_hat = qc[lo:hi] * jnp.exp2(gc[lo:hi] - g_base)
        k_hat = kc * jnp.exp2(jnp.minimum(g_base - gc, 0.0))
        earlier.append(jnp.where(col_g < lo, _dot_nt(q_hat.astype(BF16), k_hat.astype(BF16)), 0.0))
    a = jnp.concatenate(earlier, axis=0)
    qb = qc.astype(BF16)
    g3 = gc.reshape(c // group, group, HG_DIM)
    own = (col <= row) & (col >= row - lax.rem(row, group))
    for p in range(group):
        g_p = jnp.broadcast_to(g3[:, p:p + 1, :], g3.shape).reshape(c, HG_DIM)
        k_p = kc * jnp.exp2(jnp.minimum(g_p - gc, 0.0))
        a = a + jnp.where(own & (lax.rem(row, group) == p), _dot_nt(qb, k_p.astype(BF16)), 0.0)
    return _dot(a.astype(BF16), vc)


def _hgrn_redo(redo, hg_gain, ones):
    outs = []
    for c, gc in enumerate(redo["gcs"]):
        rows = slice(c * HG_CHUNK, (c + 1) * HG_CHUNK)
        outs.append(redo["inter"][c] + _exact_intra(redo["qf"][rows], redo["kk"][rows], gc, redo["vals"][rows]))
    return _head_out(jnp.concatenate(outs, axis=0), redo["gr"], hg_gain, ones).astype(BF16)


def _attn_chain(q_pair, k_half, v_t, bias_ref, sink_a, sink_b, kv):
    s = _dot_nt(k_half, q_pair) + bias_ref[...]
    yield
    col = lax.broadcasted_iota(jnp.int32, (1, s.shape[1]), 1)
    sink = jnp.where(col < WINDOW, sink_a, sink_b)
    m = jnp.maximum(jnp.max(s, axis=0, keepdims=True), sink)
    e = jnp.exp2(s - m)
    yield
    denom = jnp.sum(e, axis=0, keepdims=True) + jnp.exp2(sink - m)
    o_t = _dot(v_t, e.astype(BF16))[kv * HEAD_DIM:(kv + 1) * HEAD_DIM]
    yield
    return o_t * (1.0 / denom)


def _gate_stream(u, win_ref, step=256):
    res = []
    for off in (O_GA, O_GB):
        cols = []
        for lo in range(0, SPLIT_SIZES[7], step):
            cols.append(jax.nn.sigmoid(_dot(u, win_ref[:, off + lo:off + lo + step])))
            yield
        res.append(jnp.concatenate(cols, axis=1))
    return res


def _q_cols(u, win_ref, kv, seg2, gain2):
    w = win_ref[:, O_Q + 2 * kv * LANES:O_Q + 2 * (kv + 1) * LANES]
    x = jnp.concatenate([_dot(blk, w) for blk in u], axis=0) if isinstance(u, list) else _dot(u, w)
    yield
    ms = _dot((x * x).astype(BF16), seg2) * (1.0 / HEAD_DIM)
    yield
    return (x * lax.rsqrt(ms + EPS) * gain2).astype(BF16)


def _kv_cols(u, win_ref, seg, gain):
    x = _dot(u, win_ref[:, O_K:O_QR])
    yield
    k, v = x[:, :LANES], x[:, LANES:]
    ms = _dot((k * k).astype(BF16), seg) * (1.0 / HEAD_DIM)
    yield
    return k * lax.rsqrt(ms + EPS) * gain, v


def _attn_up(parts, wua_ref):
    att = jnp.concatenate(parts, axis=0).T.astype(BF16)
    yield
    return _dot(att, wua_ref[...])


def _hgrn_sample_kernel(qr_ref, fr_ref, ir_ref, gr_ref, s0_ref, lb_ref, hg_ref, tri_ref, tot_ref, ones_ref,
                        o_ref, s_out_ref, *, ld):
    nseq = s0_ref.shape[0]
    r = nseq * ld
    lb = _lower_bound(lb_ref[...])
    qf_all, kk_all, g_all = _gates(qr_ref[...], fr_ref[...], lb)
    tri, tot = tri_ref[...], tot_ref[...]
    causal = tri.astype(F32) > 0
    lane = lax.broadcasted_iota(jnp.int32, (HG_DIM, r), 1)
    pos = lax.rem(lax.broadcasted_iota(jnp.int32, (r, r), 0), ld)

    def head(hd):
        lanes = slice(hd * HG_DIM, (hd + 1) * HG_DIM)
        qf, kk, v = qf_all[:, lanes], kk_all[:, lanes], ir_ref[:, lanes]
        g = _split2_ldot(tri, g_all[:, lanes])
        g_last = _split2_ldot(tot, g_all[:, lanes])
        yield
        q_t = (qf * jnp.exp2(g)).astype(BF16)
        qb = qf.astype(BF16)
        g3 = g.reshape(nseq, ld, HG_DIM)
        a = jnp.zeros((r, r), F32)
        for p in range(ld):
            g_p = jnp.broadcast_to(g3[:, p:p + 1, :], g3.shape).reshape(r, HG_DIM)
            k_p = kk * jnp.exp2(jnp.minimum(g_p - g, 0.0))
            a = a + jnp.where(causal & (pos == p), _dot_nt(qb, k_p.astype(BF16)), 0.0)
        k_end_t = (kk * jnp.exp2(g_last - g)).T
        decay_t = jnp.exp2(g_last).T
        yield
        intra = _dot(a.astype(BF16), v)
        outs = []
        for i in range(nseq):
            s_prev = s0_ref[i, hd]
            outs.append(_dot(q_t[i * ld:(i + 1) * ld], s_prev.astype(BF16)))
            own = (lane >= i * ld) & (lane < (i + 1) * ld)
            upd = _dot(jnp.where(own, k_end_t, 0.0).astype(BF16), v)
            s_out_ref[i, hd] = decay_t[:, i * ld:i * ld + 1] * s_prev + upd
            if i % 2:
                yield
        o = jnp.concatenate(outs, axis=0) + intra
        yield
        o_ref[:, lanes] = _head_out(o, gr_ref[:, lanes], hg_ref[...], ones_ref[...]).astype(o_ref.dtype)

    pool = _Pool()
    for hd in range(HG_HEADS):
        pool.add(hd, head(hd))
    while pool.live:
        pool.round()


def _hgrn_sample(qr, fr, ir, gr, s0, lb_logits, hg_gain, ld, sb):
    bd = s0.shape[0]
    r = sb * ld
    seq = np.arange(r) // ld
    same = seq[:, None] == seq[None, :]
    tri = jnp.asarray((same & (np.arange(r)[:, None] >= np.arange(r)[None, :])).astype(np.float32), BF16)
    tot = jnp.asarray(same.astype(np.float32), BF16)
    ones = jnp.ones((HG_DIM, HG_DIM), BF16)
    tok = pl.BlockSpec((r, HG_HEADS * HG_DIM), lambda i: (i, 0))
    state = pl.BlockSpec((sb, HG_HEADS, HG_DIM, HG_DIM), lambda i: (i, 0, 0, 0))
    const = lambda a: pl.BlockSpec(a.shape, lambda i: (0,) * a.ndim)
    return pl.pallas_call(
        functools.partial(_hgrn_sample_kernel, ld=ld),
        grid=(bd // sb,),
        in_specs=[tok, tok, tok, tok, state, const(lb_logits), const(hg_gain), const(tri), const(tot), const(ones)],
        out_specs=[tok, state],
        out_shape=[jax.ShapeDtypeStruct(qr.shape, BF16), jax.ShapeDtypeStruct(s0.shape, F32)],
        compiler_params=pltpu.CompilerParams(dimension_semantics=("parallel",)),
        name="hgrn_sample",
    )(qr, fr, ir, gr, s0, lb_logits, hg_gain, tri, tot, ones)


def _merge_out(h, sga, sgb, att, orr, wua_ref, wur_ref, wo_ref):
    merged = sga * _dot(att, wua_ref[...]) + sgb * _dot(orr, wur_ref[...])
    return h + _dot(merged.astype(BF16), wo_ref[...])


def _mix_kernel(h_ref, gm_ref, win_ref, qg_ref, kg_ref, seg_ref, seg2_ref, bias_ref, sink_ref, lb_ref, hg_ref,
                tri_ref, ones_ref, wua_ref, wur_ref, wo_ref,
                h2_ref, k_out_ref, v_out_ref, s_out_ref, kprev_ref, vprev_ref, s_ref):
    tm = h_ref.shape[0]
    first = pl.program_id(1) == 0

    @pl.when(first)
    def _():
        kprev_ref[...] = jnp.zeros(kprev_ref.shape, F32)
        vprev_ref[...] = jnp.zeros(vprev_ref.shape, F32)
        s_ref[...] = jnp.zeros(s_ref.shape, F32)

    h = h_ref[...]
    u_blocks = _normed_blocks(h_ref, gm_ref[...], NORM_BLOCKS)
    u = jnp.concatenate(u_blocks, axis=0)

    pool = _Pool()
    lb = _lower_bound(lb_ref[...])
    for kv in range(N_KV_HEADS):
        pool.add(("q", kv), _q_cols(u if kv else u_blocks, win_ref, kv, seg2_ref[...], qg_ref[...]))
    pool.add("kv", _kv_cols(u, win_ref, seg_ref[...], kg_ref[...]))
    for pair in range(HG_HEADS // 2):
        pool.add(("hproj", pair), _hgrn_proj(u, win_ref, pair))
    pool.add("gates", _gate_stream(u, win_ref))
    first_keys = [("q", kv) for kv in range(N_KV_HEADS)] + ["kv"] + [("hproj", p) for p in range(HG_HEADS // 2)]
    while not pool.has(first_keys):
        pool.round()
    for hd in range(HG_HEADS):
        qr, fr, ir, gr = (a[:, (hd % 2) * HG_DIM:(hd % 2 + 1) * HG_DIM] for a in pool.out[("hproj", hd // 2)])
        pool.add(("hgrn", hd), _hgrn_head(qr, fr, ir, gr, lb[:, hd * HG_DIM:(hd + 1) * HG_DIM], hd, s_ref, s_out_ref,
                                          hg_ref[...], ones_ref[...], tri_ref[...]))

    q = [pool.out[("q", j // 2)][:, (j % 2) * LANES:(j % 2 + 1) * LANES] for j in range(SPLIT_SIZES[0] // LANES)]
    k, v = pool.out["kv"]
    kk_all = jnp.concatenate([kprev_ref[...], k], axis=0)
    vv_all = jnp.concatenate([vprev_ref[...], v], axis=0)
    k_tail, v_tail = k[tm - WINDOW:], v[tm - WINDOW:]
    kprev_ref[...] = k_tail
    vprev_ref[...] = v_tail
    k_out_ref[...] = k_tail
    v_out_ref[...] = v_tail
    lane = lax.broadcasted_iota(jnp.int32, kk_all.shape, 1)
    k_lo0 = jnp.where(lane < HEAD_DIM, kk_all, 0.0)
    k_hi1 = jnp.where(lane >= HEAD_DIM, kk_all, 0.0)
    k_halves = ((k_lo0.astype(BF16), pltpu.roll(k_lo0, HEAD_DIM, 1).astype(BF16)),
                (pltpu.roll(k_hi1, HEAD_DIM, 1).astype(BF16), k_hi1.astype(BF16)))
    v_t = vv_all.T.astype(BF16)

    n_blk = tm // WINDOW
    chain_keys = lambda blk: [("att", blk, kv, half) for kv in range(N_KV_HEADS) for half in range(2)]
    for blk in range(n_blk):
        lo = blk * WINDOW
        variant = jnp.where(first, 0, 1) if blk == 0 else 1
        for kv in range(N_KV_HEADS):
            q_pair = jnp.concatenate([q[2 * kv + j][lo:lo + WINDOW] for j in range(GROUP // 2)], axis=0)
            for half in range(2):
                pool.add(("att", blk, kv, half),
                         _attn_chain(q_pair, k_halves[kv][half][lo:lo + 2 * WINDOW], v_t[:, lo:lo + 2 * WINDOW],
                                     bias_ref.at[variant, kv, half], sink_ref[GROUP * kv + half],
                                     sink_ref[GROUP * kv + 2 + half], kv))

    waiting = list(range(n_blk))
    while pool.live:
        pool.round()
        for blk in [b for b in waiting if pool.has(chain_keys(b))]:
            waiting.remove(blk)
            parts = []
            for hq in range(N_Q_HEADS):
                kv, j, half = hq // GROUP, (hq % GROUP) // 2, hq % 2
                parts.append(pool.out[("att", blk, kv, half)][:, j * WINDOW:(j + 1) * WINDOW])
            pool.add(("up", blk), _attn_up(parts, wua_ref))

    up_a = jnp.concatenate([pool.out[("up", blk)] for blk in range(n_blk)], axis=0)
    sga, sgb = pool.out["gates"]
    heads = [pool.out[("hgrn", hd)] for hd in range(HG_HEADS)]

    def finish(orr_heads):
        orr = jnp.concatenate(orr_heads, axis=1)
        merged = sga * up_a + sgb * _dot(orr, wur_ref[...])
        h2_ref[...] = h + _dot(merged.astype(BF16), wo_ref[...])

    finish([o for o, _, _ in heads])

    worst = functools.reduce(jnp.minimum, [w for _, w, _ in heads])

    @pl.when(jnp.min(worst) < -EXP_CLAMP)
    def _():
        finish([_hgrn_redo(redo, hg_ref[...], ones_ref[...]) for _, _, redo in heads])


def _mix(h, consts, batch, seq, tm):
    t, d = h.shape
    nt = seq // tm
    row = pl.BlockSpec((tm, d), lambda b, i: (b * nt + i, 0))
    win_out = pl.BlockSpec((WINDOW, LANES), lambda b, i: (b, 0))
    return pl.pallas_call(
        _mix_kernel,
        grid=(batch, nt),
        in_specs=[row] + [pl.BlockSpec(memory_space=pltpu.SMEM) if a.ndim == 1 else _const_spec(a) for a in consts],
        out_specs=[row, win_out, win_out, pl.BlockSpec((1, HG_HEADS, HG_DIM, HG_DIM), lambda b, i: (b, 0, 0, 0))],
        out_shape=[jax.ShapeDtypeStruct((t, d), F32),
                   jax.ShapeDtypeStruct((batch * WINDOW, LANES), F32),
                   jax.ShapeDtypeStruct((batch * WINDOW, LANES), F32),
                   jax.ShapeDtypeStruct((batch, HG_HEADS, HG_DIM, HG_DIM), F32)],
        scratch_shapes=[pltpu.VMEM((WINDOW, LANES), F32), pltpu.VMEM((WINDOW, LANES), F32),
                        pltpu.VMEM((HG_HEADS, HG_DIM, HG_DIM), F32)],
        compiler_params=pltpu.CompilerParams(dimension_semantics=("parallel", "arbitrary"),
                                             vmem_limit_bytes=VMEM_LIMIT),
        name="mix",
    )(h, *consts)


def _proj_kernel(h_ref, gm_ref, win_ref, qg_ref, kg_ref, seg_ref,
                 q_ref, k_ref, v_ref, qr_ref, fr_ref, ir_ref, gr_ref):
    u = _rms_rows(h_ref[...], gm_ref[...]).astype(BF16)
    q, k, v = _project_qkv(u, win_ref, qg_ref[...], kg_ref[...], seg_ref[...])
    q_ref[...] = q
    k_ref[...] = k
    v_ref[...] = v
    qr_ref[...] = _dot(u, win_ref[:, O_QR:O_FR])
    fr_ref[...] = _dot(u, win_ref[:, O_FR:O_IR])
    ir_ref[...] = _dot(u, win_ref[:, O_IR:O_GR]).astype(ir_ref.dtype)
    gr_ref[...] = _dot(u, win_ref[:, O_GR:O_GA])


def _proj(h, gm, w_in, qg, kg, seg, tm):
    t, d = h.shape
    row = lambda w: pl.BlockSpec((tm, w), lambda i: (i, 0))
    widths = SPLIT_SIZES[:7]
    dtypes = (BF16, F32, F32, F32, F32, BF16, F32)
    return pl.pallas_call(
        _proj_kernel,
        grid=(t // tm,),
        in_specs=[row(d)] + [_const_spec(a) for a in (gm, w_in, qg, kg, seg)],
        out_specs=[row(w) for w in widths],
        out_shape=[jax.ShapeDtypeStruct((t, w), dt) for w, dt in zip(widths, dtypes)],
        compiler_params=pltpu.CompilerParams(dimension_semantics=("parallel",), vmem_limit_bytes=VMEM_LIMIT),
        name="proj",
    )(h, gm, w_in, qg, kg, seg)


def _out_kernel(h_ref, att_ref, orr_ref, gm_ref, win_ref, wua_ref, wur_ref, wo_ref, h2_ref):
    h = h_ref[...]
    u = _rms_rows(h, gm_ref[...]).astype(BF16)
    sga = jax.nn.sigmoid(_dot(u, win_ref[:, O_GA:O_GB]))
    sgb = jax.nn.sigmoid(_dot(u, win_ref[:, O_GB:O_END]))
    h2_ref[...] = _merge_out(h, sga, sgb, att_ref[...], orr_ref[...], wua_ref, wur_ref, wo_ref)


def _out(h, att, orr, consts, tm):
    t, d = h.shape
    row = lambda a: pl.BlockSpec((tm, a.shape[1]), lambda i: (i, 0))
    return pl.pallas_call(
        _out_kernel,
        grid=(t // tm,),
        in_specs=[row(a) for a in (h, att, orr)] + [_const_spec(a) for a in consts],
        out_specs=pl.BlockSpec((tm, d), lambda i: (i, 0)),
        out_shape=jax.ShapeDtypeStruct((t, d), F32),
        compiler_params=pltpu.CompilerParams(dimension_semantics=("parallel",), vmem_limit_bytes=VMEM_LIMIT),
        name="out",
    )(h, att, orr, *consts)


def _bucket_map():
    dist = np.arange(WINDOW)[:, None] + WINDOW - np.arange(2 * WINDOW)[None, :]
    valid = (dist >= 0) & (dist <= WINDOW)
    return np.where(valid, _t5_bucket(dist), -1).astype(np.int32)


def _stacked(bias, rows):
    return bias.reshape(N_KV_HEADS, GROUP * rows, bias.shape[-1])


def kernel(x_prompt, x_sample, cache_win_k, cache_win_v, state_hgrn, ffn1_norm, ffn1_w1, ffn1_w3, ffn1_w2, mix_norm, w_in, q_norm, k_norm, sinks, rel_bias_table, hgrn_lb_logits, hg_norm, w_up_attn, w_up_hgrn, w_out, ffn2_norm, ffn2_w1, ffn2_w3, ffn2_w2):
    depth = ffn1_norm.shape[0]
    assert depth == 1 and hgrn_lb_logits.shape[0] == 2, "single-layer step only"
    batch, seq, d = x_prompt.shape
    bd, ld, _ = x_sample.shape

    bf = lambda w: w[0].astype(BF16)
    row = lambda g: g[0].reshape(1, -1).astype(F32)
    w_in_b = bf(w_in)
    wua, wur, wo = bf(w_up_attn), bf(w_up_hgrn), bf(w_out)
    g1, gm, g2 = row(ffn1_norm), row(mix_norm), row(ffn2_norm)
    qg = jnp.tile(row(q_norm), (1, LANES // HEAD_DIM)) * (HEAD_DIM ** -0.5 * LOG2E)
    kg = jnp.tile(row(k_norm), (1, LANES // HEAD_DIM))
    head_of_lane = np.arange(2 * LANES) // HEAD_DIM
    seg2 = jnp.asarray((head_of_lane[:, None] == head_of_lane[None, :]).astype(np.float32), BF16)
    seg = seg2[:LANES, :LANES]
    qg2 = jnp.tile(qg, (1, 2))
    ones = jnp.ones((HG_DIM, HG_DIM), BF16)
    tri = jnp.asarray(np.tril(np.ones((HG_CHUNK, HG_CHUNK), np.float32)), BF16)
    lb_logits = hgrn_lb_logits.astype(F32)
    hg_gain = row(hg_norm)

    later = _rel_bias(rel_bias_table.astype(F32) * LOG2E, np.ascontiguousarray(_bucket_map().T))
    first = jnp.where(np.arange(2 * WINDOW)[None, :, None] >= WINDOW, later, NEG_BIG)
    def paired(b):
        b = b.reshape(N_KV_HEADS, GROUP // 2, 2, 2 * WINDOW, WINDOW)
        return b.transpose(0, 2, 3, 1, 4).reshape(N_KV_HEADS, 2, 2 * WINDOW, 2 * WINDOW)
    bias_p = jnp.stack([paired(first), paired(later)])
    bias_s = _stacked(later[:, :WINDOW + ld, :ld].transpose(0, 2, 1), ld)
    sink_1d = sinks[0].astype(F32) * LOG2E
    sink = sink_1d.reshape(N_KV_HEADS, GROUP, 1)
    sink_s = jnp.repeat(sink, ld, axis=1).reshape(N_KV_HEADS, GROUP * ld, 1)

    h_p, w1a, w3a, w2a = _ffn_cast(x_prompt.reshape(batch * seq, d), g1, ffn1_w1[0], ffn1_w3[0], ffn1_w2[0], TM_FFN)
    h_s = _ffn(x_sample.reshape(bd * ld, d), g1, w1a, w3a, w2a, TM_FFN)

    consts = (gm, w_in_b, qg2, kg, seg, seg2, bias_p, sink_1d, lb_logits, hg_gain, tri, ones, wua, wur, wo)
    h2_p, k_p, v_p, s_p = _mix(h_p, consts, batch, seq, TM_MIX)
    win = lambda a: a.reshape(1, batch, WINDOW, N_KV_HEADS, HEAD_DIM)

    q, k, v, qr, fr, ir, gr = _proj(h_s, gm, w_in_b, qg, kg, seg, TM_SAMPLE)
    ck = cache_win_k[0].reshape(bd, WINDOW, LANES)
    cv = cache_win_v[0].reshape(bd, WINDOW, LANES)
    att, nk, nv = _swa_sample(q, k, v, ck, cv, bias_s, sink_s, ld, SEQS_PER_STEP)
    orr, s_s = _hgrn_sample(qr, fr, ir, gr, state_hgrn[0], lb_logits, hg_gain, ld, SEQS_PER_STEP)
    h2_s = _out(h_s, att, orr, (gm, w_in_b, wua, wur, wo), TM_SAMPLE)

    y_p, w1b, w3b, w2b = _ffn_cast(h2_p, g2, ffn2_w1[0], ffn2_w3[0], ffn2_w2[0], TM_FFN)
    y_s = _ffn(h2_s, g2, w1b, w3b, w2b, TM_FFN)
    unwin = lambda a: a.reshape(1, bd, WINDOW, N_KV_HEADS, HEAD_DIM)
    return (y_p.reshape(batch, seq, d), y_s.reshape(bd, ld, d), win(k_p), win(v_p), s_p[None],
            unwin(nk), unwin(nv), s_s[None])
```

```python
import functools

import numpy as np
import jax
import jax.numpy as jnp
from jax import lax
from jax.experimental import pallas as pl
from jax.experimental.pallas import tpu as pltpu

F32 = jnp.float32
BF16 = jnp.bfloat16

LANES = 128
HEAD_DIM = 64
N_Q_HEADS = 8
N_KV_HEADS = 2
GROUP = N_Q_HEADS // N_KV_HEADS
WINDOW = 128
N_BUCKETS = 32
MAX_DISTANCE = 128
HG_HEADS = 4
HG_DIM = 128
HG_CHUNK = 128
HG_SUB = 128
EPS = 1e-6
F_TINY = 1e-30
NEG_BIG = -1e30
EXP_CLAMP = 120.0
LOG2E = 1.4426950408889634
SPLIT_SIZES = (512, 128, 128, 512, 512, 512, 512, 1024, 1024)
SPLIT_OFFS = tuple(int(v) for v in np.cumsum((0,) + SPLIT_SIZES))
O_Q, O_K, O_V, O_QR, O_FR, O_IR, O_GR, O_GA, O_GB, O_END = SPLIT_OFFS
VMEM_LIMIT = 56 * 1024 * 1024
TM_FFN = 1024
NORM_BLOCKS = 4
TM_MIX = 512
TM_SAMPLE = 256
SEQS_PER_STEP = 16


def _dot(a, b):
    return jnp.dot(a, b, preferred_element_type=F32)


def _dot_nt(a, b):
    return lax.dot_general(a, b, (((1,), (1,)), ((), ())), preferred_element_type=F32)


def _split2_ldot(w, x):
    hi = x.astype(BF16)
    lo = (x - hi.astype(F32)).astype(BF16)
    return _dot(w, hi) + _dot(w, lo)


def _rms_rows(x, g):
    ms = jnp.mean(x * x, axis=-1, keepdims=True)
    return x * lax.rsqrt(ms + EPS) * g


def _seg_rms(x, seg, inv_n, g):
    ms = _dot((x * x).astype(BF16), seg) * inv_n
    return x * lax.rsqrt(ms + EPS) * g


class _Pool:
    def __init__(self):
        self.live = []
        self.out = {}

    def add(self, key, gen):
        self.live.append((key, gen))

    def round(self):
        still = []
        for key, gen in self.live:
            try:
                next(gen)
                still.append((key, gen))
            except StopIteration as done:
                self.out[key] = done.value
        self.live = still

    def has(self, keys):
        return all(k in self.out for k in keys)


def _const_spec(a):
    nd = a.ndim
    return pl.BlockSpec(a.shape, lambda *_: (0,) * nd, pipeline_mode=pl.Buffered(1))


def _ff_chunks(d_ff, step=1024):
    return tuple((lo, min(lo + step, d_ff)) for lo in range(0, d_ff, step))


def _swiglu(xn_blocks, w1_ref, w3_ref, w2_ref):
    xn = None
    acc = None
    for c, (lo, hi) in enumerate(_ff_chunks(w1_ref.shape[1])):
        if c == 0:
            a = jnp.concatenate([_dot(blk, w1_ref[:, lo:hi]) for blk in xn_blocks], axis=0)
            b = jnp.concatenate([_dot(blk, w3_ref[:, lo:hi]) for blk in xn_blocks], axis=0)
            xn = jnp.concatenate(xn_blocks, axis=0)
        else:
            a = _dot(xn, w1_ref[:, lo:hi])
            b = _dot(xn, w3_ref[:, lo:hi])
        part = _dot((jax.nn.silu(a) * b).astype(BF16), w2_ref[lo:hi, :])
        acc = part if acc is None else acc + part
    return acc


def _normed_blocks(x_ref, g, blocks):
    rows = x_ref.shape[0] // blocks
    return [_rms_rows(x_ref[r * rows:(r + 1) * rows, :], g).astype(BF16) for r in range(blocks)]


def _ffn_kernel(x_ref, g1_ref, w1_ref, w3_ref, w2_ref, h_ref):
    xn = _normed_blocks(x_ref, g1_ref[...], NORM_BLOCKS)
    h_ref[...] = x_ref[...] + 0.5 * _swiglu(xn, w1_ref, w3_ref, w2_ref)


CAST_CHUNKS = 8


def _cast_copies(pairs, stage_refs, sem):
    out = []
    for (src, dst), stage in zip(pairs, stage_refs):
        rows = src.shape[0] // CAST_CHUNKS
        for c in range(CAST_CHUNKS):
            slot = len(out) % 2
            staged = stage.at[slot]
            out.append((pltpu.make_async_copy(src.at[pl.ds(c * rows, rows), :], staged, sem.at[slot]),
                        staged, dst.at[pl.ds(c * rows, rows), :]))
    return out


def _ffn_cast_kernel(x_ref, g1_ref, w1_hbm, w3_hbm, w2_hbm, h_ref, w1_out, w3_out, w2_out,
                     w1_ref, w3_ref, w2_ref, stage_in_ref, stage_out_ref, sem_in, sem_out):
    step = pl.program_id(0)
    resident = ((w1_ref, w1_out), (w3_ref, w3_out), (w2_ref, w2_out))
    writebacks = [pltpu.make_async_copy(src, dst, sem_out.at[k]) for k, (src, dst) in enumerate(resident)]

    @pl.when(step == 0)
    def _():
        copies = _cast_copies(((w1_hbm, w1_ref), (w3_hbm, w3_ref), (w2_hbm, w2_ref)),
                              (stage_in_ref, stage_in_ref, stage_out_ref), sem_in)
        copies[0][0].start()
        for j, (dma, staged, dst) in enumerate(copies):
            if j + 1 < len(copies):
                copies[j + 1][0].start()
            dma.wait()
            dst[...] = staged[...].astype(BF16)
        for wb in writebacks:
            wb.start()

    xn = _normed_blocks(x_ref, g1_ref[...], NORM_BLOCKS)
    h_ref[...] = x_ref[...] + 0.5 * _swiglu(xn, w1_ref, w3_ref, w2_ref)

    @pl.when(step == pl.num_programs(0) - 1)
    def _():
        for wb in writebacks:
            wb.wait()


def _ffn_cast(x, g1, w1, w3, w2, tm):
    t, d = x.shape
    d_ff = w1.shape[1]
    row = pl.BlockSpec((tm, d), lambda i: (i, 0))
    hbm = pl.BlockSpec(memory_space=pl.ANY)
    bf = lambda w: jax.ShapeDtypeStruct(w.shape, BF16)
    return pl.pallas_call(
        _ffn_cast_kernel,
        grid=(t // tm,),
        in_specs=[row, _const_spec(g1), hbm, hbm, hbm],
        out_specs=[row, hbm, hbm, hbm],
        out_shape=[jax.ShapeDtypeStruct((t, d), F32), bf(w1), bf(w3), bf(w2)],
        scratch_shapes=[pltpu.VMEM(w1.shape, BF16), pltpu.VMEM(w3.shape, BF16), pltpu.VMEM(w2.shape, BF16),
                        pltpu.VMEM((2, d // CAST_CHUNKS, d_ff), F32), pltpu.VMEM((2, d_ff // CAST_CHUNKS, d), F32),
                        pltpu.SemaphoreType.DMA((2,)), pltpu.SemaphoreType.DMA((3,))],
        compiler_params=pltpu.CompilerParams(dimension_semantics=("arbitrary",), vmem_limit_bytes=VMEM_LIMIT),
        name="ffn_cast",
    )(x, g1, w1, w3, w2)


def _ffn(x, g1, w1, w3, w2, tm):
    t, d = x.shape
    row = pl.BlockSpec((tm, d), lambda i: (i, 0))
    return pl.pallas_call(
        _ffn_kernel,
        grid=(t // tm,),
        in_specs=[row] + [_const_spec(a) for a in (g1, w1, w3, w2)],
        out_specs=row,
        out_shape=jax.ShapeDtypeStruct((t, d), F32),
        compiler_params=pltpu.CompilerParams(dimension_semantics=("parallel",), vmem_limit_bytes=VMEM_LIMIT),
        name="ffn",
    )(x, g1, w1, w3, w2)


def _t5_bucket(dist):
    max_exact = N_BUCKETS // 2
    d = np.maximum(dist, 0)
    large = max_exact + (np.log(np.maximum(d, 1) / max_exact) / np.log(MAX_DISTANCE / max_exact)
                         * (N_BUCKETS - max_exact)).astype(np.int32)
    large = np.minimum(large, N_BUCKETS - 1)
    return np.where(d < max_exact, d, large).astype(np.int32)


def _bias_kernel(table_ref, bucket_ref, out_ref):
    bucket = bucket_ref[...]
    masked = jnp.where(bucket < 0, NEG_BIG, 0.0).astype(F32)
    for h in range(N_Q_HEADS):
        acc = masked
        for b in range(N_BUCKETS):
            acc = acc + jnp.where(bucket == b, table_ref[b, h], 0.0)
        out_ref[h] = acc


def _rel_bias(table, bucket_map):
    r, c = bucket_map.shape
    return pl.pallas_call(
        _bias_kernel,
        in_specs=[pl.BlockSpec(memory_space=pltpu.SMEM), pl.BlockSpec((r, c), lambda: (0, 0))],
        out_specs=pl.BlockSpec((N_Q_HEADS, r, c), lambda: (0, 0, 0)),
        out_shape=jax.ShapeDtypeStruct((N_Q_HEADS, r, c), F32),
        name="rel_bias",
    )(table, jnp.asarray(bucket_map))


def _project_qkv(u, win_ref, qg, kg, seg):
    qs = []
    for s in range(SPLIT_SIZES[0] // LANES):
        x = _dot(u, win_ref[:, O_Q + s * LANES:O_Q + (s + 1) * LANES])
        qs.append(_seg_rms(x, seg, 1.0 / HEAD_DIM, qg).astype(BF16))
    k = _seg_rms(_dot(u, win_ref[:, O_K:O_V]), seg, 1.0 / HEAD_DIM, kg)
    v = _dot(u, win_ref[:, O_V:O_QR])
    return jnp.concatenate(qs, axis=1), k, v


def _stack_group(q, kv):
    a = q[:, (2 * kv) * LANES:(2 * kv + 1) * LANES]
    b = q[:, (2 * kv + 1) * LANES:(2 * kv + 2) * LANES]
    ar, br = pltpu.roll(a, HEAD_DIM, 1), pltpu.roll(b, HEAD_DIM, 1)
    parts = (a, ar, b, br) if kv == 0 else (ar, a, br, b)
    return jnp.concatenate(parts, axis=0)


def _kv_half(x, kv, fill):
    lane = lax.broadcasted_iota(jnp.int32, x.shape, 1)
    keep = (lane < HEAD_DIM) if kv == 0 else (lane >= HEAD_DIM)
    return jnp.where(keep, x, fill)


def _attend_chain(q, kk, vv, bias_ref, sink_ref, r, kv):
    qs = _stack_group(q.astype(F32), kv).astype(BF16)
    s = _dot_nt(qs, _kv_half(kk, kv, 0.0).astype(BF16)) + bias_ref[kv]
    yield
    sink = sink_ref[kv]
    m = jnp.maximum(jnp.max(s, axis=-1, keepdims=True), sink)
    e = jnp.exp2(s - m).astype(BF16)
    yield
    o = _dot(e, _kv_half(vv, kv, 1.0).astype(BF16))
    yield
    o = o / (pltpu.roll(o, HEAD_DIM, 1) + jnp.exp2(sink - m))
    lane = lax.broadcasted_iota(jnp.int32, (r, LANES), 1)
    slabs = []
    for pair in range(GROUP // 2):
        a, b = o[(2 * pair) * r:(2 * pair + 1) * r], o[(2 * pair + 1) * r:(2 * pair + 2) * r]
        if kv == 0:
            slabs.append(jnp.where(lane < HEAD_DIM, a, pltpu.roll(b, HEAD_DIM, 1)))
        else:
            slabs.append(jnp.where(lane < HEAD_DIM, pltpu.roll(a, HEAD_DIM, 1), b))
    return slabs


def _swa_sample_kernel(q_ref, kn_ref, vn_ref, ck_ref, cv_ref, bias_ref, sink_ref, o_ref, nk_ref, nv_ref, *, ld):
    pool = _Pool()
    nseq = ck_ref.shape[0]
    for i in range(nseq):
        rows = slice(i * ld, (i + 1) * ld)
        kk = jnp.concatenate([ck_ref[i], kn_ref[rows, :]], axis=0)
        vv = jnp.concatenate([cv_ref[i], vn_ref[rows, :]], axis=0)
        nk_ref[i] = kk[ld:, :]
        nv_ref[i] = vv[ld:, :]
        for kv in range(N_KV_HEADS):
            pool.add((i, kv), _attend_chain(q_ref[rows, :], kk, vv, bias_ref, sink_ref, ld, kv))
    while pool.live:
        pool.round()
    for i in range(nseq):
        slabs = [s for kv in range(N_KV_HEADS) for s in pool.out[(i, kv)]]
        o_ref[i * ld:(i + 1) * ld, :] = jnp.concatenate(slabs, axis=1).astype(o_ref.dtype)


def _swa_sample(q, k, v, cache_k, cache_v, bias, sink_col, ld, sb):
    bd = cache_k.shape[0]
    tok = lambda w: pl.BlockSpec((sb * ld, w), lambda i: (i, 0))
    cache = pl.BlockSpec((sb, WINDOW, LANES), lambda i: (i, 0, 0))
    return pl.pallas_call(
        functools.partial(_swa_sample_kernel, ld=ld),
        grid=(bd // sb,),
        in_specs=[tok(4 * LANES), tok(LANES), tok(LANES), cache, cache,
                  pl.BlockSpec(bias.shape, lambda i: (0, 0, 0)), pl.BlockSpec(sink_col.shape, lambda i: (0, 0, 0))],
        out_specs=[tok(4 * LANES), cache, cache],
        out_shape=[jax.ShapeDtypeStruct(q.shape, BF16), jax.ShapeDtypeStruct(cache_k.shape, F32),
                   jax.ShapeDtypeStruct(cache_v.shape, F32)],
        compiler_params=pltpu.CompilerParams(dimension_semantics=("parallel",)),
        name="swa_sample",
    )(q, k, v, cache_k, cache_v, bias, sink_col)


def _lower_bound(lb_logits):
    z = lb_logits - jnp.max(lb_logits, axis=0, keepdims=True)
    e = jnp.exp(z)
    return e[0:1, :] / jnp.sum(e, axis=0, keepdims=True)


def _gates(qr, fr, lb):
    f = lb + (1.0 - lb) * jax.nn.sigmoid(fr)
    return jax.nn.silu(qr), 1.0 - f, jnp.log2(jnp.maximum(f, F_TINY))


def _head_out(o, gr, hg_gain, ones_seg):
    return _seg_rms(o, ones_seg, 1.0 / HG_DIM, hg_gain) * jax.nn.silu(gr)


def _hgrn_proj(u, win_ref, pair):
    cols = lambda off: slice(off + 2 * pair * HG_DIM, off + 2 * (pair + 1) * HG_DIM)
    qr = _dot(u, win_ref[:, cols(O_QR)])
    fr = _dot(u, win_ref[:, cols(O_FR)])
    yield
    ir = _dot(u, win_ref[:, cols(O_IR)]).astype(BF16)
    gr = _dot(u, win_ref[:, cols(O_GR)])
    yield
    return qr, fr, ir, gr


def _hgrn_head(qr, fr, vals, gr, lb, hd, s_ref, s_out_ref, hg_gain, ones, tri):
    qf, kk, g = _gates(qr, fr, lb)
    nchunk = qr.shape[0] // HG_CHUNK
    chunks = [slice(c * HG_CHUNK, (c + 1) * HG_CHUNK) for c in range(nchunk)]
    yield
    gcs = [_split2_ldot(tri, g[rows]) for rows in chunks]
    yield
    q_in, upd, decay, scores = [], [], [], []
    worst = jnp.zeros((1, HG_DIM), F32)
    for rows, gc in zip(chunks, gcs):
        g_last = gc[HG_CHUNK - 1:HG_CHUNK, :]
        q_in.append((qf[rows] * jnp.exp2(gc)).astype(BF16))
        k_end_t = (kk[rows] * jnp.exp2(g_last - gc)).T
        upd.append(_dot(k_end_t.astype(BF16), vals[rows]))
        decay.append(jnp.broadcast_to(jnp.exp2(g_last), (HG_DIM, HG_DIM)).T)
        qc, kc = qf[rows], kk[rows]
        for i in range(HG_CHUNK // HG_SUB):
            lo, hi = i * HG_SUB, (i + 1) * HG_SUB
            mid = lo + HG_SUB // 2
            g_start = gc[lo - 1:lo, :] if i else jnp.zeros((1, HG_DIM), F32)
            g_mid = gc[mid - 1:mid, :]
            worst = jnp.minimum(worst, jnp.minimum(gc[hi - 1:hi, :] - g_mid, g_mid - g_start))
            q_hat = qc[lo:hi] * jnp.exp2(jnp.minimum(gc[lo:hi] - g_mid, EXP_CLAMP))
            k_hat = kc[:hi] * jnp.exp2(jnp.minimum(g_mid - gc[:hi], EXP_CLAMP))
            a = _dot_nt(q_hat.astype(BF16), k_hat.astype(BF16))
            row = lax.broadcasted_iota(jnp.int32, (HG_SUB, hi), 0)
            col = lax.broadcasted_iota(jnp.int32, (HG_SUB, hi), 1)
            scores.append(jnp.where(col <= row + lo, a, 0.0).astype(BF16))
    yield
    intra = []
    nsub = HG_CHUNK // HG_SUB
    for c, rows in enumerate(chunks):
        vc = vals[rows]
        parts = [_dot(scores[c * nsub + i], vc[:(i + 1) * HG_SUB]) for i in range(nsub)]
        intra.append(jnp.concatenate(parts, axis=0))
    yield
    s = s_ref[hd]
    inter = []
    for c in range(nchunk):
        inter.append(_dot(q_in[c], s.astype(BF16)))
        s = decay[c] * s + upd[c]
    s_ref[hd] = s
    s_out_ref[0, hd] = s
    yield
    o = jnp.concatenate([a + b for a, b in zip(inter, intra)], axis=0)
    redo = dict(qf=qf, kk=kk, gcs=gcs, vals=vals, inter=inter, gr=gr)
    return _head_out(o, gr, hg_gain, ones).astype(BF16), worst, redo


def _exact_intra(qc, kc, gc, vc):
    c = qc.shape[0]
    group = 8
    row = lax.broadcasted_iota(jnp.int32, (c, c), 0)
    col = lax.broadcasted_iota(jnp.int32, (c, c), 1)
    col_g = lax.broadcasted_iota(jnp.int32, (group, c), 1)
    earlier = []
    for i in range(c // group):
        lo, hi = i * group, (i + 1) * group
        g_base = gc[lo - 1:lo, :] if i else jnp.zeros((1, HG_DIM), F32)
        q_hat = qc[lo:hi] * jnp.exp2(gc[lo:hi] - g_base)
        k_hat = kc * jnp.exp2(jnp.minimum(g_base - gc, 0.0))
        earlier.append(jnp.where(col_g < lo, _dot_nt(q_hat.astype(BF16), k_hat.astype(BF16)), 0.0))
    a = jnp.concatenate(earlier, axis=0)
    qb = qc.astype(BF16)
    g3 = gc.reshape(c // group, group, HG_DIM)
    own = (col <= row) & (col >= row - lax.rem(row, group))
    for p in range(group):
        g_p = jnp.broadcast_to(g3[:, p:p + 1, :], g3.shape).reshape(c, HG_DIM)
        k_p = kc * jnp.exp2(jnp.minimum(g_p - gc, 0.0))
        a = a + jnp.where(own & (lax.rem(row, group) == p), _dot_nt(qb, k_p.astype(BF16)), 0.0)
    return _dot(a.astype(BF16), vc)


def _hgrn_redo(redo, hg_gain, ones):
    outs = []
    for c, gc in enumerate(redo["gcs"]):
        rows = slice(c * HG_CHUNK, (c + 1) * HG_CHUNK)
        outs.append(redo["inter"][c] + _exact_intra(redo["qf"][rows], redo["kk"][rows], gc, redo["vals"][rows]))
    return _head_out(jnp.concatenate(outs, axis=0), redo["gr"], hg_gain, ones).astype(BF16)


def _attn_chain(q_pair, k_half, v_t, bias_ref, sink_a, sink_b, kv):
    s = _dot_nt(k_half, q_pair) + bias_ref[...]
    yield
    col = lax.broadcasted_iota(jnp.int32, (1, s.shape[1]), 1)
    sink = jnp.where(col < WINDOW, sink_a, sink_b)
    m = jnp.maximum(jnp.max(s, axis=0, keepdims=True), sink)
    e = jnp.exp2(s - m).astype(BF16)
    yield
    o_full = _dot(v_t, e)
    other = (1 - kv) * HEAD_DIM
    denom = o_full[other:other + 1] + jnp.exp2(sink - m)
    yield
    return o_full[kv * HEAD_DIM:(kv + 1) * HEAD_DIM] * (1.0 / denom)


def _gate_stream(u, win_ref, step=256):
    res = []
    for off in (O_GA, O_GB):
        cols = []
        for lo in range(0, SPLIT_SIZES[7], step):
            cols.append(jax.nn.sigmoid(_dot(u, win_ref[:, off + lo:off + lo + step])))
            yield
        res.append(jnp.concatenate(cols, axis=1))
    return res


def _q_cols(u, win_ref, kv, seg2, gain2):
    w = win_ref[:, O_Q + 2 * kv * LANES:O_Q + 2 * (kv + 1) * LANES]
    x = jnp.concatenate([_dot(blk, w) for blk in u], axis=0) if isinstance(u, list) else _dot(u, w)
    yield
    ms = _dot((x * x).astype(BF16), seg2) * (1.0 / HEAD_DIM)
    yield
    return (x * lax.rsqrt(ms + EPS) * gain2).astype(BF16)


def _kv_cols(u, win_ref, seg, gain):
    x = _dot(u, win_ref[:, O_K:O_QR])
    yield
    k, v = x[:, :LANES], x[:, LANES:]
    ms = _dot((k * k).astype(BF16), seg) * (1.0 / HEAD_DIM)
    yield
    return k * lax.rsqrt(ms + EPS) * gain, v


def _attn_up(parts, wua_ref):
    att = jnp.concatenate(parts, axis=0).T.astype(BF16)
    yield
    return _dot(att, wua_ref[...])


def _hgrn_sample_kernel(qr_ref, fr_ref, ir_ref, gr_ref, s0_ref, lb_ref, hg_ref, tri_ref, tot_ref, ones_ref,
                        o_ref, s_out_ref, *, ld):
    nseq = s0_ref.shape[0]
    r = nseq * ld
    lb = _lower_bound(lb_ref[...])
    qf_all, kk_all, g_all = _gates(qr_ref[...], fr_ref[...], lb)
    tri, tot = tri_ref[...], tot_ref[...]
    causal = tri.astype(F32) > 0
    lane = lax.broadcasted_iota(jnp.int32, (HG_DIM, r), 1)
    pos = lax.rem(lax.broadcasted_iota(jnp.int32, (r, r), 0), ld)

    def head(hd):
        lanes = slice(hd * HG_DIM, (hd + 1) * HG_DIM)
        qf, kk, v = qf_all[:, lanes], kk_all[:, lanes], ir_ref[:, lanes]
        g = _split2_ldot(tri, g_all[:, lanes])
        g_last = _split2_ldot(tot, g_all[:, lanes])
        yield
        q_t = (qf * jnp.exp2(g)).astype(BF16)
        qb = qf.astype(BF16)
        g3 = g.reshape(nseq, ld, HG_DIM)
        a = jnp.zeros((r, r), F32)
        for p in range(ld):
            g_p = jnp.broadcast_to(g3[:, p:p + 1, :], g3.shape).reshape(r, HG_DIM)
            k_p = kk * jnp.exp2(jnp.minimum(g_p - g, 0.0))
            a = a + jnp.where(causal & (pos == p), _dot_nt(qb, k_p.astype(BF16)), 0.0)
        k_end_t = (kk * jnp.exp2(g_last - g)).T
        decay_t = jnp.exp2(g_last).T
        yield
        intra = _dot(a.astype(BF16), v)
        outs = []
        for i in range(nseq):
            s_prev = s0_ref[i, hd]
            outs.append(_dot(q_t[i * ld:(i + 1) * ld], s_prev.astype(BF16)))
            own = (lane >= i * ld) & (lane < (i + 1) * ld)
            upd = _dot(jnp.where(own, k_end_t, 0.0).astype(BF16), v)
            s_out_ref[i, hd] = decay_t[:, i * ld:i * ld + 1] * s_prev + upd
            if i % 2:
                yield
        o = jnp.concatenate(outs, axis=0) + intra
        yield
        o_ref[:, lanes] = _head_out(o, gr_ref[:, lanes], hg_ref[...], ones_ref[...]).astype(o_ref.dtype)

    pool = _Pool()
    for hd in range(HG_HEADS):
        pool.add(hd, head(hd))
    while pool.live:
        pool.round()


def _hgrn_sample(qr, fr, ir, gr, s0, lb_logits, hg_gain, ld, sb):
    bd = s0.shape[0]
    r = sb * ld
    seq = np.arange(r) // ld
    same = seq[:, None] == seq[None, :]
    tri = jnp.asarray((same & (np.arange(r)[:, None] >= np.arange(r)[None, :])).astype(np.float32), BF16)
    tot = jnp.asarray(same.astype(np.float32), BF16)
    ones = jnp.ones((HG_DIM, HG_DIM), BF16)
    tok = pl.BlockSpec((r, HG_HEADS * HG_DIM), lambda i: (i, 0))
    state = pl.BlockSpec((sb, HG_HEADS, HG_DIM, HG_DIM), lambda i: (i, 0, 0, 0))
    const = lambda a: pl.BlockSpec(a.shape, lambda i: (0,) * a.ndim)
    return pl.pallas_call(
        functools.partial(_hgrn_sample_kernel, ld=ld),
        grid=(bd // sb,),
        in_specs=[tok, tok, tok, tok, state, const(lb_logits), const(hg_gain), const(tri), const(tot), const(ones)],
        out_specs=[tok, state],
        out_shape=[jax.ShapeDtypeStruct(qr.shape, BF16), jax.ShapeDtypeStruct(s0.shape, F32)],
        compiler_params=pltpu.CompilerParams(dimension_semantics=("parallel",)),
        name="hgrn_sample",
    )(qr, fr, ir, gr, s0, lb_logits, hg_gain, tri, tot, ones)


def _merge_out(h, sga, sgb, att, orr, wua_ref, wur_ref, wo_ref):
    merged = sga * _dot(att, wua_ref[...]) + sgb * _dot(orr, wur_ref[...])
    return h + _dot(merged.astype(BF16), wo_ref[...])


def _mix_kernel(h_ref, gm_ref, win_ref, qg_ref, kg_ref, seg_ref, seg2_ref, bias_ref, sink_ref, lb_ref, hg_ref,
                tri_ref, ones_ref, wua_ref, wur_ref, wo_ref,
                h2_ref, k_out_ref, v_out_ref, s_out_ref, kprev_ref, vprev_ref, s_ref):
    tm = h_ref.shape[0]
    first = pl.program_id(1) == 0

    @pl.when(first)
    def _():
        kprev_ref[...] = jnp.zeros(kprev_ref.shape, F32)
        vprev_ref[...] = jnp.zeros(vprev_ref.shape, F32)
        s_ref[...] = jnp.zeros(s_ref.shape, F32)

    h = h_ref[...]
    u_blocks = _normed_blocks(h_ref, gm_ref[...], NORM_BLOCKS)
    u = jnp.concatenate(u_blocks, axis=0)

    pool = _Pool()
    lb = _lower_bound(lb_ref[...])
    for kv in range(N_KV_HEADS):
        pool.add(("q", kv), _q_cols(u if kv else u_blocks, win_ref, kv, seg2_ref[...], qg_ref[...]))
    pool.add("kv", _kv_cols(u, win_ref, seg_ref[...], kg_ref[...]))
    for pair in range(HG_HEADS // 2):
        pool.add(("hproj", pair), _hgrn_proj(u, win_ref, pair))
    pool.add("gates", _gate_stream(u, win_ref))
    first_keys = [("q", kv) for kv in range(N_KV_HEADS)] + ["kv"] + [("hproj", p) for p in range(HG_HEADS // 2)]
    while not pool.has(first_keys):
        pool.round()
    for hd in range(HG_HEADS):
        qr, fr, ir, gr = (a[:, (hd % 2) * HG_DIM:(hd % 2 + 1) * HG_DIM] for a in pool.out[("hproj", hd // 2)])
        pool.add(("hgrn", hd), _hgrn_head(qr, fr, ir, gr, lb[:, hd * HG_DIM:(hd + 1) * HG_DIM], hd, s_ref, s_out_ref,
                                          hg_ref[...], ones_ref[...], tri_ref[...]))

    q = [pool.out[("q", j // 2)][:, (j % 2) * LANES:(j % 2 + 1) * LANES] for j in range(SPLIT_SIZES[0] // LANES)]
    k, v = pool.out["kv"]
    kk_all = jnp.concatenate([kprev_ref[...], k], axis=0)
    vv_all = jnp.concatenate([vprev_ref[...], v], axis=0)
    k_tail, v_tail = k[tm - WINDOW:], v[tm - WINDOW:]
    kprev_ref[...] = k_tail
    vprev_ref[...] = v_tail
    k_out_ref[...] = k_tail
    v_out_ref[...] = v_tail
    lane = lax.broadcasted_iota(jnp.int32, kk_all.shape, 1)
    k_lo0 = jnp.where(lane < HEAD_DIM, kk_all, 0.0)
    k_hi1 = jnp.where(lane >= HEAD_DIM, kk_all, 0.0)
    k_halves = ((k_lo0.astype(BF16), pltpu.roll(k_lo0, HEAD_DIM, 1).astype(BF16)),
                (pltpu.roll(k_hi1, HEAD_DIM, 1).astype(BF16), k_hi1.astype(BF16)))
    v_rows = vv_all.T
    dim = lax.broadcasted_iota(jnp.int32, v_rows.shape, 0)
    v_t = (jnp.where(dim < HEAD_DIM, v_rows, 1.0).astype(BF16),
           jnp.where(dim >= HEAD_DIM, v_rows, 1.0).astype(BF16))

    n_blk = tm // WINDOW
    chain_keys = lambda blk: [("att", blk, kv, half) for kv in range(N_KV_HEADS) for half in range(2)]
    for blk in range(n_blk):
        lo = blk * WINDOW
        variant = jnp.where(first, 0, 1) if blk == 0 else 1
        for kv in range(N_KV_HEADS):
            q_pair = jnp.concatenate([q[2 * kv + j][lo:lo + WINDOW] for j in range(GROUP // 2)], axis=0)
            for half in range(2):
                pool.add(("att", blk, kv, half),
                         _attn_chain(q_pair, k_halves[kv][half][lo:lo + 2 * WINDOW], v_t[kv][:, lo:lo + 2 * WINDOW],
                                     bias_ref.at[variant, kv, half], sink_ref[GROUP * kv + half],
                                     sink_ref[GROUP * kv + 2 + half], kv))

    waiting = list(range(n_blk))
    while pool.live:
        pool.round()
        for blk in [b for b in waiting if pool.has(chain_keys(b))]:
            waiting.remove(blk)
            parts = []
            for hq in range(N_Q_HEADS):
                kv, j, half = hq // GROUP, (hq % GROUP) // 2, hq % 2
                parts.append(pool.out[("att", blk, kv, half)][:, j * WINDOW:(j + 1) * WINDOW])
            pool.add(("up", blk), _attn_up(parts, wua_ref))

    up_a = jnp.concatenate([pool.out[("up", blk)] for blk in range(n_blk)], axis=0)
    sga, sgb = pool.out["gates"]
    heads = [pool.out[("hgrn", hd)] for hd in range(HG_HEADS)]

    def finish(orr_heads):
        orr = jnp.concatenate(orr_heads, axis=1)
        merged = sga * up_a + sgb * _dot(orr, wur_ref[...])
        h2_ref[...] = h + _dot(merged.astype(BF16), wo_ref[...])

    finish([o for o, _, _ in heads])

    worst = functools.reduce(jnp.minimum, [w for _, w, _ in heads])

    @pl.when(jnp.min(worst) < -EXP_CLAMP)
    def _():
        finish([_hgrn_redo(redo, hg_ref[...], ones_ref[...]) for _, _, redo in heads])


def _mix(h, consts, batch, seq, tm):
    t, d = h.shape
    nt = seq // tm
    row = pl.BlockSpec((tm, d), lambda b, i: (b * nt + i, 0))
    win_out = pl.BlockSpec((WINDOW, LANES), lambda b, i: (b, 0))
    return pl.pallas_call(
        _mix_kernel,
        grid=(batch, nt),
        in_specs=[row] + [pl.BlockSpec(memory_space=pltpu.SMEM) if a.ndim == 1 else _const_spec(a) for a in consts],
        out_specs=[row, win_out, win_out, pl.BlockSpec((1, HG_HEADS, HG_DIM, HG_DIM), lambda b, i: (b, 0, 0, 0))],
        out_shape=[jax.ShapeDtypeStruct((t, d), F32),
                   jax.ShapeDtypeStruct((batch * WINDOW, LANES), F32),
                   jax.ShapeDtypeStruct((batch * WINDOW, LANES), F32),
                   jax.ShapeDtypeStruct((batch, HG_HEADS, HG_DIM, HG_DIM), F32)],
        scratch_shapes=[pltpu.VMEM((WINDOW, LANES), F32), pltpu.VMEM((WINDOW, LANES), F32),
                        pltpu.VMEM((HG_HEADS, HG_DIM, HG_DIM), F32)],
        compiler_params=pltpu.CompilerParams(dimension_semantics=("parallel", "arbitrary"),
                                             vmem_limit_bytes=VMEM_LIMIT),
        name="mix",
    )(h, *consts)


def _proj_kernel(h_ref, gm_ref, win_ref, qg_ref, kg_ref, seg_ref,
                 q_ref, k_ref, v_ref, qr_ref, fr_ref, ir_ref, gr_ref):
    u = _rms_rows(h_ref[...], gm_ref[...]).astype(BF16)
    q, k, v = _project_qkv(u, win_ref, qg_ref[...], kg_ref[...], seg_ref[...])
    q_ref[...] = q
    k_ref[...] = k
    v_ref[...] = v
    qr_ref[...] = _dot(u, win_ref[:, O_QR:O_FR])
    fr_ref[...] = _dot(u, win_ref[:, O_FR:O_IR])
    ir_ref[...] = _dot(u, win_ref[:, O_IR:O_GR]).astype(ir_ref.dtype)
    gr_ref[...] = _dot(u, win_ref[:, O_GR:O_GA])


def _proj(h, gm, w_in, qg, kg, seg, tm):
    t, d = h.shape
    row = lambda w: pl.BlockSpec((tm, w), lambda i: (i, 0))
    widths = SPLIT_SIZES[:7]
    dtypes = (BF16, F32, F32, F32, F32, BF16, F32)
    return pl.pallas_call(
        _proj_kernel,
        grid=(t // tm,),
        in_specs=[row(d)] + [_const_spec(a) for a in (gm, w_in, qg, kg, seg)],
        out_specs=[row(w) for w in widths],
        out_shape=[jax.ShapeDtypeStruct((t, w), dt) for w, dt in zip(widths, dtypes)],
        compiler_params=pltpu.CompilerParams(dimension_semantics=("parallel",), vmem_limit_bytes=VMEM_LIMIT),
        name="proj",
    )(h, gm, w_in, qg, kg, seg)


def _out_kernel(h_ref, att_ref, orr_ref, gm_ref, win_ref, wua_ref, wur_ref, wo_ref, h2_ref):
    h = h_ref[...]
    u = _rms_rows(h, gm_ref[...]).astype(BF16)
    sga = jax.nn.sigmoid(_dot(u, win_ref[:, O_GA:O_GB]))
    sgb = jax.nn.sigmoid(_dot(u, win_ref[:, O_GB:O_END]))
    h2_ref[...] = _merge_out(h, sga, sgb, att_ref[...], orr_ref[...], wua_ref, wur_ref, wo_ref)


def _out(h, att, orr, consts, tm):
    t, d = h.shape
    row = lambda a: pl.BlockSpec((tm, a.shape[1]), lambda i: (i, 0))
    return pl.pallas_call(
        _out_kernel,
        grid=(t // tm,),
        in_specs=[row(a) for a in (h, att, orr)] + [_const_spec(a) for a in consts],
        out_specs=pl.BlockSpec((tm, d), lambda i: (i, 0)),
        out_shape=jax.ShapeDtypeStruct((t, d), F32),
        compiler_params=pltpu.CompilerParams(dimension_semantics=("parallel",), vmem_limit_bytes=VMEM_LIMIT),
        name="out",
    )(h, att, orr, *consts)


def _bucket_map():
    dist = np.arange(WINDOW)[:, None] + WINDOW - np.arange(2 * WINDOW)[None, :]
    valid = (dist >= 0) & (dist <= WINDOW)
    return np.where(valid, _t5_bucket(dist), -1).astype(np.int32)


def _stacked(bias, rows):
    return bias.reshape(N_KV_HEADS, GROUP * rows, bias.shape[-1])


def kernel(x_prompt, x_sample, cache_win_k, cache_win_v, state_hgrn, ffn1_norm, ffn1_w1, ffn1_w3, ffn1_w2, mix_norm, w_in, q_norm, k_norm, sinks, rel_bias_table, hgrn_lb_logits, hg_norm, w_up_attn, w_up_hgrn, w_out, ffn2_norm, ffn2_w1, ffn2_w3, ffn2_w2):
    depth = ffn1_norm.shape[0]
    assert depth == 1 and hgrn_lb_logits.shape[0] == 2, "single-layer step only"
    batch, seq, d = x_prompt.shape
    bd, ld, _ = x_sample.shape

    bf = lambda w: w[0].astype(BF16)
    row = lambda g: g[0].reshape(1, -1).astype(F32)
    w_in_b = bf(w_in)
    wua, wur, wo = bf(w_up_attn), bf(w_up_hgrn), bf(w_out)
    g1, gm, g2 = row(ffn1_norm), row(mix_norm), row(ffn2_norm)
    qg = jnp.tile(row(q_norm), (1, LANES // HEAD_DIM)) * (HEAD_DIM ** -0.5 * LOG2E)
    kg = jnp.tile(row(k_norm), (1, LANES // HEAD_DIM))
    head_of_lane = np.arange(2 * LANES) // HEAD_DIM
    seg2 = jnp.asarray((head_of_lane[:, None] == head_of_lane[None, :]).astype(np.float32), BF16)
    seg = seg2[:LANES, :LANES]
    qg2 = jnp.tile(qg, (1, 2))
    ones = jnp.ones((HG_DIM, HG_DIM), BF16)
    tri = jnp.asarray(np.tril(np.ones((HG_CHUNK, HG_CHUNK), np.float32)), BF16)
    lb_logits = hgrn_lb_logits.astype(F32)
    hg_gain = row(hg_norm)

    later = _rel_bias(rel_bias_table.astype(F32) * LOG2E, np.ascontiguousarray(_bucket_map().T))
    first = jnp.where(np.arange(2 * WINDOW)[None, :, None] >= WINDOW, later, NEG_BIG)
    def paired(b):
        b = b.reshape(N_KV_HEADS, GROUP // 2, 2, 2 * WINDOW, WINDOW)
        return b.transpose(0, 2, 3, 1, 4).reshape(N_KV_HEADS, 2, 2 * WINDOW, 2 * WINDOW)
    bias_p = jnp.stack([paired(first), paired(later)])
    bias_s = _stacked(later[:, :WINDOW + ld, :ld].transpose(0, 2, 1), ld)
    sink_1d = sinks[0].astype(F32) * LOG2E
    sink = sink_1d.reshape(N_KV_HEADS, GROUP, 1)
    sink_s = jnp.repeat(sink, ld, axis=1).reshape(N_KV_HEADS, GROUP * ld, 1)

    h_p, w1a, w3a, w2a = _ffn_cast(x_prompt.reshape(batch * seq, d), g1, ffn1_w1[0], ffn1_w3[0], ffn1_w2[0], TM_FFN)
    h_s = _ffn(x_sample.reshape(bd * ld, d), g1, w1a, w3a, w2a, TM_FFN)

    consts = (gm, w_in_b, qg2, kg, seg, seg2, bias_p, sink_1d, lb_logits, hg_gain, tri, ones, wua, wur, wo)
    h2_p, k_p, v_p, s_p = _mix(h_p, consts, batch, seq, TM_MIX)
    win = lambda a: a.reshape(1, batch, WINDOW, N_KV_HEADS, HEAD_DIM)

    q, k, v, qr, fr, ir, gr = _proj(h_s, gm, w_in_b, qg, kg, seg, TM_SAMPLE)
    ck = cache_win_k[0].reshape(bd, WINDOW, LANES)
    cv = cache_win_v[0].reshape(bd, WINDOW, LANES)
    att, nk, nv = _swa_sample(q, k, v, ck, cv, bias_s, sink_s, ld, SEQS_PER_STEP)
    orr, s_s = _hgrn_sample(qr, fr, ir, gr, state_hgrn[0], lb_logits, hg_gain, ld, SEQS_PER_STEP)
    h2_s = _out(h_s, att, orr, (gm, w_in_b, wua, wur, wo), TM_SAMPLE)

    y_p, w1b, w3b, w2b = _ffn_cast(h2_p, g2, ffn2_w1[0], ffn2_w3[0], ffn2_w2[0], TM_FFN)
    y_s = _ffn(h2_s, g2, w1b, w3b, w2b, TM_FFN)
    unwin = lambda a: a.reshape(1, bd, WINDOW, N_KV_HEADS, HEAD_DIM)
    return (y_p.reshape(batch, seq, d), y_s.reshape(bd, ld, d), win(k_p), win(v_p), s_p[None],
            unwin(nk), unwin(nv), s_s[None])
```

```python
import functools

import numpy as np
import jax
import jax.numpy as jnp
from jax import lax
from jax.experimental import pallas as pl
from jax.experimental.pallas import tpu as pltpu

F32 = jnp.float32
BF16 = jnp.bfloat16

LANES = 128
HEAD_DIM = 64
N_Q_HEADS = 8
N_KV_HEADS = 2
GROUP = N_Q_HEADS // N_KV_HEADS
WINDOW = 128
N_BUCKETS = 32
MAX_DISTANCE = 128
HG_HEADS = 4
HG_DIM = 128
HG_CHUNK = 128
HG_SUB = 128
EPS = 1e-6
F_TINY = 1e-30
NEG_BIG = -1e30
EXP_CLAMP = 120.0
LOG2E = 1.4426950408889634
SPLIT_SIZES = (512, 128, 128, 512, 512, 512, 512, 1024, 1024)
SPLIT_OFFS = tuple(int(v) for v in np.cumsum((0,) + SPLIT_SIZES))
O_Q, O_K, O_V, O_QR, O_FR, O_IR, O_GR, O_GA, O_GB, O_END = SPLIT_OFFS
VMEM_LIMIT = 56 * 1024 * 1024
TM_FFN = 1024
NORM_BLOCKS = 4
TM_MIX = 512
TM_SAMPLE = 256
SEQS_PER_STEP = 16


def _dot(a, b):
    return jnp.dot(a, b, preferred_element_type=F32)


def _dot_nt(a, b):
    return lax.dot_general(a, b, (((1,), (1,)), ((), ())), preferred_element_type=F32)


def _split2_ldot(w, x):
    hi = x.astype(BF16)
    lo = (x - hi.astype(F32)).astype(BF16)
    return _dot(w, hi) + _dot(w, lo)


def _rms_rows(x, g):
    ms = jnp.mean(x * x, axis=-1, keepdims=True)
    return x * lax.rsqrt(ms + EPS) * g


def _seg_rms(x, seg, inv_n, g):
    ms = _dot((x * x).astype(BF16), seg) * inv_n
    return x * lax.rsqrt(ms + EPS) * g


class _Pool:
    def __init__(self):
        self.live = []
        self.out = {}

    def add(self, key, gen):
        self.live.append((key, gen))

    def round(self):
        still = []
        for key, gen in self.live:
            try:
                next(gen)
                still.append((key, gen))
            except StopIteration as done:
                self.out[key] = done.value
        self.live = still

    def has(self, keys):
        return all(k in self.out for k in keys)


def _const_spec(a):
    nd = a.ndim
    return pl.BlockSpec(a.shape, lambda *_: (0,) * nd, pipeline_mode=pl.Buffered(1))


def _ff_chunks(d_ff, step=1024):
    return tuple((lo, min(lo + step, d_ff)) for lo in range(0, d_ff, step))


def _swiglu(xn_blocks, w1_ref, w3_ref, w2_ref):
    xn = None
    acc = None
    for c, (lo, hi) in enumerate(_ff_chunks(w1_ref.shape[1])):
        if c == 0:
            a = jnp.concatenate([_dot(blk, w1_ref[:, lo:hi]) for blk in xn_blocks], axis=0)
            b = jnp.concatenate([_dot(blk, w3_ref[:, lo:hi]) for blk in xn_blocks], axis=0)
            xn = jnp.concatenate(xn_blocks, axis=0)
        else:
            a = _dot(xn, w1_ref[:, lo:hi])
            b = _dot(xn, w3_ref[:, lo:hi])
        part = _dot((jax.nn.silu(a) * b).astype(BF16), w2_ref[lo:hi, :])
        acc = part if acc is None else acc + part
    return acc


def _normed_blocks(x_ref, g, blocks):
    rows = x_ref.shape[0] // blocks
    return [_rms_rows(x_ref[r * rows:(r + 1) * rows, :], g).astype(BF16) for r in range(blocks)]


def _ffn_kernel(x_ref, g1_ref, w1_ref, w3_ref, w2_ref, h_ref):
    xn = _normed_blocks(x_ref, g1_ref[...], NORM_BLOCKS)
    h_ref[...] = x_ref[...] + 0.5 * _swiglu(xn, w1_ref, w3_ref, w2_ref)


CAST_CHUNKS = 8


def _cast_copies(pairs, stage_refs, sem):
    out = []
    for (src, dst), stage in zip(pairs, stage_refs):
        rows = src.shape[0] // CAST_CHUNKS
        for c in range(CAST_CHUNKS):
            slot = len(out) % 2
            staged = stage.at[slot]
            out.append((pltpu.make_async_copy(src.at[pl.ds(c * rows, rows), :], staged, sem.at[slot]),
                        staged, dst.at[pl.ds(c * rows, rows), :]))
    return out


def _ffn_cast_kernel(x_ref, g1_ref, w1_hbm, w3_hbm, w2_hbm, h_ref, w1_out, w3_out, w2_out,
                     w1_ref, w3_ref, w2_ref, stage_in_ref, stage_out_ref, sem_in, sem_out):
    step = pl.program_id(0)
    resident = ((w1_ref, w1_out), (w3_ref, w3_out), (w2_ref, w2_out))
    writebacks = [pltpu.make_async_copy(src, dst, sem_out.at[k]) for k, (src, dst) in enumerate(resident)]

    @pl.when(step == 0)
    def _():
        copies = _cast_copies(((w1_hbm, w1_ref), (w3_hbm, w3_ref), (w2_hbm, w2_ref)),
                              (stage_in_ref, stage_in_ref, stage_out_ref), sem_in)
        copies[0][0].start()
        for j, (dma, staged, dst) in enumerate(copies):
            if j + 1 < len(copies):
                copies[j + 1][0].start()
            dma.wait()
            dst[...] = staged[...].astype(BF16)
        for wb in writebacks:
            wb.start()

    xn = _normed_blocks(x_ref, g1_ref[...], NORM_BLOCKS)
    h_ref[...] = x_ref[...] + 0.5 * _swiglu(xn, w1_ref, w3_ref, w2_ref)

    @pl.when(step == pl.num_programs(0) - 1)
    def _():
        for wb in writebacks:
            wb.wait()


def _ffn_cast(x, g1, w1, w3, w2, tm):
    t, d = x.shape
    d_ff = w1.shape[1]
    row = pl.BlockSpec((tm, d), lambda i: (i, 0))
    hbm = pl.BlockSpec(memory_space=pl.ANY)
    bf = lambda w: jax.ShapeDtypeStruct(w.shape, BF16)
    return pl.pallas_call(
        _ffn_cast_kernel,
        grid=(t // tm,),
        in_specs=[row, _const_spec(g1), hbm, hbm, hbm],
        out_specs=[row, hbm, hbm, hbm],
        out_shape=[jax.ShapeDtypeStruct((t, d), F32), bf(w1), bf(w3), bf(w2)],
        scratch_shapes=[pltpu.VMEM(w1.shape, BF16), pltpu.VMEM(w3.shape, BF16), pltpu.VMEM(w2.shape, BF16),
                        pltpu.VMEM((2, d // CAST_CHUNKS, d_ff), F32), pltpu.VMEM((2, d_ff // CAST_CHUNKS, d), F32),
                        pltpu.SemaphoreType.DMA((2,)), pltpu.SemaphoreType.DMA((3,))],
        compiler_params=pltpu.CompilerParams(dimension_semantics=("arbitrary",), vmem_limit_bytes=VMEM_LIMIT),
        name="ffn_cast",
    )(x, g1, w1, w3, w2)


def _ffn(x, g1, w1, w3, w2, tm):
    t, d = x.shape
    row = pl.BlockSpec((tm, d), lambda i: (i, 0))
    return pl.pallas_call(
        _ffn_kernel,
        grid=(t // tm,),
        in_specs=[row] + [_const_spec(a) for a in (g1, w1, w3, w2)],
        out_specs=row,
        out_shape=jax.ShapeDtypeStruct((t, d), F32),
        compiler_params=pltpu.CompilerParams(dimension_semantics=("parallel",), vmem_limit_bytes=VMEM_LIMIT),
        name="ffn",
    )(x, g1, w1, w3, w2)


def _t5_bucket(dist):
    max_exact = N_BUCKETS // 2
    d = np.maximum(dist, 0)
    large = max_exact + (np.log(np.maximum(d, 1) / max_exact) / np.log(MAX_DISTANCE / max_exact)
                         * (N_BUCKETS - max_exact)).astype(np.int32)
    large = np.minimum(large, N_BUCKETS - 1)
    return np.where(d < max_exact, d, large).astype(np.int32)


def _bias_kernel(table_ref, bucket_ref, out_ref):
    bucket = bucket_ref[...]
    masked = jnp.where(bucket < 0, NEG_BIG, 0.0).astype(F32)
    for h in range(N_Q_HEADS):
        acc = masked
        for b in range(N_BUCKETS):
            acc = acc + jnp.where(bucket == b, table_ref[b, h], 0.0)
        out_ref[h] = acc


def _rel_bias(table, bucket_map):
    r, c = bucket_map.shape
    return pl.pallas_call(
        _bias_kernel,
        in_specs=[pl.BlockSpec(memory_space=pltpu.SMEM), pl.BlockSpec((r, c), lambda: (0, 0))],
        out_specs=pl.BlockSpec((N_Q_HEADS, r, c), lambda: (0, 0, 0)),
        out_shape=jax.ShapeDtypeStruct((N_Q_HEADS, r, c), F32),
        name="rel_bias",
    )(table, jnp.asarray(bucket_map))


def _project_qkv(u, win_ref, qg, kg, seg):
    qs = []
    for s in range(SPLIT_SIZES[0] // LANES):
        x = _dot(u, win_ref[:, O_Q + s * LANES:O_Q + (s + 1) * LANES])
        qs.append(_seg_rms(x, seg, 1.0 / HEAD_DIM, qg).astype(BF16))
    k = _seg_rms(_dot(u, win_ref[:, O_K:O_V]), seg, 1.0 / HEAD_DIM, kg)
    v = _dot(u, win_ref[:, O_V:O_QR])
    return jnp.concatenate(qs, axis=1), k, v


def _stack_group(q, kv):
    a = q[:, (2 * kv) * LANES:(2 * kv + 1) * LANES]
    b = q[:, (2 * kv + 1) * LANES:(2 * kv + 2) * LANES]
    ar, br = pltpu.roll(a, HEAD_DIM, 1), pltpu.roll(b, HEAD_DIM, 1)
    parts = (a, ar, b, br) if kv == 0 else (ar, a, br, b)
    return jnp.concatenate(parts, axis=0)


def _kv_half(x, kv, fill):
    lane = lax.broadcasted_iota(jnp.int32, x.shape, 1)
    keep = (lane < HEAD_DIM) if kv == 0 else (lane >= HEAD_DIM)
    return jnp.where(keep, x, fill)


def _attend_chain(q, kk, vv, bias_ref, sink_ref, r, kv):
    qs = _stack_group(q.astype(F32), kv).astype(BF16)
    s = _dot_nt(qs, _kv_half(kk, kv, 0.0).astype(BF16)) + bias_ref[kv]
    yield
    sink = sink_ref[kv]
    m = jnp.maximum(jnp.max(s, axis=-1, keepdims=True), sink)
    e = jnp.exp2(s - m).astype(BF16)
    yield
    o = _dot(e, _kv_half(vv, kv, 1.0).astype(BF16))
    yield
    o = o / (pltpu.roll(o, HEAD_DIM, 1) + jnp.exp2(sink - m))
    lane = lax.broadcasted_iota(jnp.int32, (r, LANES), 1)
    slabs = []
    for pair in range(GROUP // 2):
        a, b = o[(2 * pair) * r:(2 * pair + 1) * r], o[(2 * pair + 1) * r:(2 * pair + 2) * r]
        if kv == 0:
            slabs.append(jnp.where(lane < HEAD_DIM, a, pltpu.roll(b, HEAD_DIM, 1)))
        else:
            slabs.append(jnp.where(lane < HEAD_DIM, pltpu.roll(a, HEAD_DIM, 1), b))
    return slabs


def _swa_sample_kernel(q_ref, kn_ref, vn_ref, ck_ref, cv_ref, bias_ref, sink_ref, o_ref, nk_ref, nv_ref, *, ld):
    pool = _Pool()
    nseq = ck_ref.shape[0]
    for i in range(nseq):
        rows = slice(i * ld, (i + 1) * ld)
        kk = jnp.concatenate([ck_ref[i], kn_ref[rows, :]], axis=0)
        vv = jnp.concatenate([cv_ref[i], vn_ref[rows, :]], axis=0)
        nk_ref[i] = kk[ld:, :]
        nv_ref[i] = vv[ld:, :]
        for kv in range(N_KV_HEADS):
            pool.add((i, kv), _attend_chain(q_ref[rows, :], kk, vv, bias_ref, sink_ref, ld, kv))
    while pool.live:
        pool.round()
    for i in range(nseq):
        slabs = [s for kv in range(N_KV_HEADS) for s in pool.out[(i, kv)]]
        o_ref[i * ld:(i + 1) * ld, :] = jnp.concatenate(slabs, axis=1).astype(o_ref.dtype)


def _swa_sample(q, k, v, cache_k, cache_v, bias, sink_col, ld, sb):
    bd = cache_k.shape[0]
    tok = lambda w: pl.BlockSpec((sb * ld, w), lambda i: (i, 0))
    cache = pl.BlockSpec((sb, WINDOW, LANES), lambda i: (i, 0, 0))
    return pl.pallas_call(
        functools.partial(_swa_sample_kernel, ld=ld),
        grid=(bd // sb,),
        in_specs=[tok(4 * LANES), tok(LANES), tok(LANES), cache, cache,
                  pl.BlockSpec(bias.shape, lambda i: (0, 0, 0)), pl.BlockSpec(sink_col.shape, lambda i: (0, 0, 0))],
        out_specs=[tok(4 * LANES), cache, cache],
        out_shape=[jax.ShapeDtypeStruct(q.shape, BF16), jax.ShapeDtypeStruct(cache_k.shape, F32),
                   jax.ShapeDtypeStruct(cache_v.shape, F32)],
        compiler_params=pltpu.CompilerParams(dimension_semantics=("parallel",)),
        name="swa_sample",
    )(q, k, v, cache_k, cache_v, bias, sink_col)


def _lower_bound(lb_logits):
    z = lb_logits - jnp.max(lb_logits, axis=0, keepdims=True)
    e = jnp.exp(z)
    return e[0:1, :] / jnp.sum(e, axis=0, keepdims=True)


def _gates(qr, fr, lb):
    f = lb + (1.0 - lb) * jax.nn.sigmoid(fr)
    return jax.nn.silu(qr), 1.0 - f, jnp.log2(jnp.maximum(f, F_TINY))


def _head_out(o, sgr, hg_gain, ones_seg):
    return _seg_rms(o, ones_seg, 1.0 / HG_DIM, hg_gain) * sgr


def _hgrn_proj(u, win_ref, pair, lb):
    cols = lambda off: slice(off + 2 * pair * HG_DIM, off + 2 * (pair + 1) * HG_DIM)
    blocks = u if isinstance(u, list) else [u]
    qr = jnp.concatenate([_dot(blk, win_ref[:, cols(O_QR)]) for blk in blocks], axis=0)
    u = jnp.concatenate(blocks, axis=0)
    fr = _dot(u, win_ref[:, cols(O_FR)])
    yield
    qf, kk, g = _gates(qr, fr, lb[:, 2 * pair * HG_DIM:2 * (pair + 1) * HG_DIM])
    ir = _dot(u, win_ref[:, cols(O_IR)]).astype(BF16)
    gr = _dot(u, win_ref[:, cols(O_GR)])
    yield
    return qf, kk, g, ir, jax.nn.silu(gr)


def _hgrn_head(qf, kk, g, vals, sgr, hd, s_ref, s_out_ref, hg_gain, ones, tri):
    nchunk = qf.shape[0] // HG_CHUNK
    chunks = [slice(c * HG_CHUNK, (c + 1) * HG_CHUNK) for c in range(nchunk)]
    gcs = [_split2_ldot(tri, g[rows]) for rows in chunks]
    yield
    q_in, upd, decay, scores = [], [], [], []
    worst = jnp.zeros((1, HG_DIM), F32)
    for rows, gc in zip(chunks, gcs):
        g_last = gc[HG_CHUNK - 1:HG_CHUNK, :]
        q_in.append((qf[rows] * jnp.exp2(gc)).astype(BF16))
        k_end_t = (kk[rows] * jnp.exp2(g_last - gc)).T
        upd.append(_dot(k_end_t.astype(BF16), vals[rows]))
        decay.append(jnp.broadcast_to(jnp.exp2(g_last), (HG_DIM, HG_DIM)).T)
        qc, kc = qf[rows], kk[rows]
        for i in range(HG_CHUNK // HG_SUB):
            lo, hi = i * HG_SUB, (i + 1) * HG_SUB
            mid = lo + HG_SUB // 2
            g_start = gc[lo - 1:lo, :] if i else jnp.zeros((1, HG_DIM), F32)
            g_mid = gc[mid - 1:mid, :]
            worst = jnp.minimum(worst, jnp.minimum(gc[hi - 1:hi, :] - g_mid, g_mid - g_start))
            q_hat = qc[lo:hi] * jnp.exp2(jnp.minimum(gc[lo:hi] - g_mid, EXP_CLAMP))
            k_hat = kc[:hi] * jnp.exp2(jnp.minimum(g_mid - gc[:hi], EXP_CLAMP))
            a = _dot_nt(q_hat.astype(BF16), k_hat.astype(BF16))
            row = lax.broadcasted_iota(jnp.int32, (HG_SUB, hi), 0)
            col = lax.broadcasted_iota(jnp.int32, (HG_SUB, hi), 1)
            scores.append(jnp.where(col <= row + lo, a, 0.0).astype(BF16))
    yield
    intra = []
    nsub = HG_CHUNK // HG_SUB
    for c, rows in enumerate(chunks):
        vc = vals[rows]
        parts = [_dot(scores[c * nsub + i], vc[:(i + 1) * HG_SUB]) for i in range(nsub)]
        intra.append(jnp.concatenate(parts, axis=0))
    yield
    s = s_ref[hd]
    inter = []
    for c in range(nchunk):
        inter.append(_dot(q_in[c], s.astype(BF16)))
        s = decay[c] * s + upd[c]
    s_ref[hd] = s
    s_out_ref[0, hd] = s
    yield
    o = jnp.concatenate([a + b for a, b in zip(inter, intra)], axis=0)
    redo = dict(qf=qf, kk=kk, gcs=gcs, vals=vals, inter=inter, sgr=sgr)
    return _head_out(o, sgr, hg_gain, ones).astype(BF16), worst, redo


def _exact_intra(qc, kc, gc, vc):
    c = qc.shape[0]
    group = 8
    row = lax.broadcasted_iota(jnp.int32, (c, c), 0)
    col = lax.broadcasted_iota(jnp.int32, (c, c), 1)
    col_g = lax.broadcasted_iota(jnp.int32, (group, c), 1)
    earlier = []
    for i in range(c // group):
        lo, hi = i * group, (i + 1) * group
        g_base = gc[lo - 1:lo, :] if i else jnp.zeros((1, HG_DIM), F32)
        q_hat = qc[lo:hi] * jnp.exp2(gc[lo:hi] - g_base)
        k_hat = kc * jnp.exp2(jnp.minimum(g_base - gc, 0.0))
        earlier.append(jnp.where(col_g < lo, _dot_nt(q_hat.astype(BF16), k_hat.astype(BF16)), 0.0))
    a = jnp.concatenate(earlier, axis=0)
    qb = qc.astype(BF16)
    g3 = gc.reshape(c // group, group, HG_DIM)
    own = (col <= row) & (col >= row - lax.rem(row, group))
    for p in range(group):
        g_p = jnp.broadcast_to(g3[:, p:p + 1, :], g3.shape).reshape(c, HG_DIM)
        k_p = kc * jnp.exp2(jnp.minimum(g_p - gc, 0.0))
        a = a + jnp.where(own & (lax.rem(row, group) == p), _dot_nt(qb, k_p.astype(BF16)), 0.0)
    return _dot(a.astype(BF16), vc)


def _hgrn_redo(redo, hg_gain, ones):
    outs = []
    for c, gc in enumerate(redo["gcs"]):
        rows = slice(c * HG_CHUNK, (c + 1) * HG_CHUNK)
        outs.append(redo["inter"][c] + _exact_intra(redo["qf"][rows], redo["kk"][rows], gc, redo["vals"][rows]))
    return _head_out(jnp.concatenate(outs, axis=0), redo["sgr"], hg_gain, ones).astype(BF16)


def _attn_chain(q_pair, k_half, v_t, bias_ref, sink_a, sink_b, kv):
    s = _dot_nt(k_half, q_pair) + bias_ref[...]
    yield
    col = lax.broadcasted_iota(jnp.int32, (1, s.shape[1]), 1)
    sink = jnp.where(col < WINDOW, sink_a, sink_b)
    m = jnp.maximum(jnp.max(s, axis=0, keepdims=True), sink)
    e = jnp.exp2(s - m).astype(BF16)
    yield
    o_full = _dot(v_t, e)
    other = (1 - kv) * HEAD_DIM
    denom = o_full[other:other + 1] + jnp.exp2(sink - m)
    yield
    return o_full[kv * HEAD_DIM:(kv + 1) * HEAD_DIM] * (1.0 / denom)


def _gate_stream(u, win_ref, step=256):
    res = []
    for off in (O_GA, O_GB):
        cols = []
        for lo in range(0, SPLIT_SIZES[7], step):
            cols.append(jax.nn.sigmoid(_dot(u, win_ref[:, off + lo:off + lo + step])))
            yield
        res.append(jnp.concatenate(cols, axis=1))
    return res


def _q_cols(u, win_ref, kv, seg2, gain2):
    w = win_ref[:, O_Q + 2 * kv * LANES:O_Q + 2 * (kv + 1) * LANES]
    x = jnp.concatenate([_dot(blk, w) for blk in u], axis=0) if isinstance(u, list) else _dot(u, w)
    yield
    ms = _dot((x * x).astype(BF16), seg2) * (1.0 / HEAD_DIM)
    yield
    return (x * lax.rsqrt(ms + EPS) * gain2).astype(BF16)


def _kv_cols(u, win_ref, seg, gain):
    x = _dot(u, win_ref[:, O_K:O_QR])
    yield
    k, v = x[:, :LANES], x[:, LANES:]
    ms = _dot((k * k).astype(BF16), seg) * (1.0 / HEAD_DIM)
    yield
    return k * lax.rsqrt(ms + EPS) * gain, v


def _attn_up(parts, wua_ref):
    att = jnp.concatenate(parts, axis=0).T.astype(BF16)
    yield
    return _dot(att, wua_ref[...])


def _hgrn_sample_kernel(qr_ref, fr_ref, ir_ref, gr_ref, s0_ref, lb_ref, hg_ref, tri_ref, tot_ref, ones_ref,
                        o_ref, s_out_ref, *, ld):
    nseq = s0_ref.shape[0]
    r = nseq * ld
    lb = _lower_bound(lb_ref[...])
    qf_all, kk_all, g_all = _gates(qr_ref[...], fr_ref[...], lb)
    tri, tot = tri_ref[...], tot_ref[...]
    causal = tri.astype(F32) > 0
    lane = lax.broadcasted_iota(jnp.int32, (HG_DIM, r), 1)
    pos = lax.rem(lax.broadcasted_iota(jnp.int32, (r, r), 0), ld)

    def head(hd):
        lanes = slice(hd * HG_DIM, (hd + 1) * HG_DIM)
        qf, kk, v = qf_all[:, lanes], kk_all[:, lanes], ir_ref[:, lanes]
        g = _split2_ldot(tri, g_all[:, lanes])
        g_last = _split2_ldot(tot, g_all[:, lanes])
        yield
        q_t = (qf * jnp.exp2(g)).astype(BF16)
        qb = qf.astype(BF16)
        g3 = g.reshape(nseq, ld, HG_DIM)
        a = jnp.zeros((r, r), F32)
        for p in range(ld):
            g_p = jnp.broadcast_to(g3[:, p:p + 1, :], g3.shape).reshape(r, HG_DIM)
            k_p = kk * jnp.exp2(jnp.minimum(g_p - g, 0.0))
            a = a + jnp.where(causal & (pos == p), _dot_nt(qb, k_p.astype(BF16)), 0.0)
        k_end_t = (kk * jnp.exp2(g_last - g)).T
        decay_t = jnp.exp2(g_last).T
        yield
        intra = _dot(a.astype(BF16), v)
        outs = []
        for i in range(nseq):
            s_prev = s0_ref[i, hd]
            outs.append(_dot(q_t[i * ld:(i + 1) * ld], s_prev.astype(BF16)))
            own = (lane >= i * ld) & (lane < (i + 1) * ld)
            upd = _dot(jnp.where(own, k_end_t, 0.0).astype(BF16), v)
            s_out_ref[i, hd] = decay_t[:, i * ld:i * ld + 1] * s_prev + upd
            if i % 2:
                yield
        o = jnp.concatenate(outs, axis=0) + intra
        yield
        o_ref[:, lanes] = _head_out(o, jax.nn.silu(gr_ref[:, lanes]), hg_ref[...], ones_ref[...]).astype(o_ref.dtype)

    pool = _Pool()
    for hd in range(HG_HEADS):
        pool.add(hd, head(hd))
    while pool.live:
        pool.round()


def _hgrn_sample(qr, fr, ir, gr, s0, lb_logits, hg_gain, ld, sb):
    bd = s0.shape[0]
    r = sb * ld
    seq = np.arange(r) // ld
    same = seq[:, None] == seq[None, :]
    tri = jnp.asarray((same & (np.arange(r)[:, None] >= np.arange(r)[None, :])).astype(np.float32), BF16)
    tot = jnp.asarray(same.astype(np.float32), BF16)
    ones = jnp.ones((HG_DIM, HG_DIM), BF16)
    tok = pl.BlockSpec((r, HG_HEADS * HG_DIM), lambda i: (i, 0))
    state = pl.BlockSpec((sb, HG_HEADS, HG_DIM, HG_DIM), lambda i: (i, 0, 0, 0))
    const = lambda a: pl.BlockSpec(a.shape, lambda i: (0,) * a.ndim)
    return pl.pallas_call(
        functools.partial(_hgrn_sample_kernel, ld=ld),
        grid=(bd // sb,),
        in_specs=[tok, tok, tok, tok, state, const(lb_logits), const(hg_gain), const(tri), const(tot), const(ones)],
        out_specs=[tok, state],
        out_shape=[jax.ShapeDtypeStruct(qr.shape, BF16), jax.ShapeDtypeStruct(s0.shape, F32)],
        compiler_params=pltpu.CompilerParams(dimension_semantics=("parallel",)),
        name="hgrn_sample",
    )(qr, fr, ir, gr, s0, lb_logits, hg_gain, tri, tot, ones)


def _merge_out(h, sga, sgb, att, orr, wua_ref, wur_ref, wo_ref):
    merged = sga * _dot(att, wua_ref[...]) + sgb * _dot(orr, wur_ref[...])
    return h + _dot(merged.astype(BF16), wo_ref[...])


def _mix_kernel(h_ref, gm_ref, win_ref, qg_ref, kg_ref, seg_ref, seg2_ref, bias_ref, sink_ref, lb_ref, hg_ref,
                tri_ref, ones_ref, wua_ref, wur_ref, wo_ref,
                h2_ref, k_out_ref, v_out_ref, s_out_ref, kprev_ref, vprev_ref, s_ref):
    tm = h_ref.shape[0]
    first = pl.program_id(1) == 0

    @pl.when(first)
    def _():
        kprev_ref[...] = jnp.zeros(kprev_ref.shape, F32)
        vprev_ref[...] = jnp.zeros(vprev_ref.shape, F32)
        s_ref[...] = jnp.zeros(s_ref.shape, F32)

    h = h_ref[...]
    u_blocks = _normed_blocks(h_ref, gm_ref[...], NORM_BLOCKS)
    u = jnp.concatenate(u_blocks, axis=0)

    pool = _Pool()
    lb = _lower_bound(lb_ref[...])
    for pair in range(HG_HEADS // 2):
        pool.add(("hproj", pair), _hgrn_proj(u_blocks if pair == 0 else u, win_ref, pair, lb))
    while not pool.has([("hproj", p) for p in range(HG_HEADS // 2)]):
        pool.round()
    for hd in range(HG_HEADS):
        qf, kk, g, ir, sgr = (a[:, (hd % 2) * HG_DIM:(hd % 2 + 1) * HG_DIM] for a in pool.out[("hproj", hd // 2)])
        pool.add(("hgrn", hd), _hgrn_head(qf, kk, g, ir, sgr, hd, s_ref, s_out_ref, hg_ref[...], ones_ref[...],
                                          tri_ref[...]))
    for kv in range(N_KV_HEADS):
        pool.add(("q", kv), _q_cols(u, win_ref, kv, seg2_ref[...], qg_ref[...]))
    pool.add("kv", _kv_cols(u, win_ref, seg_ref[...], kg_ref[...]))
    pool.add("gates", _gate_stream(u, win_ref))
    while not pool.has([("q", kv) for kv in range(N_KV_HEADS)] + ["kv"]):
        pool.round()

    q = [pool.out[("q", j // 2)][:, (j % 2) * LANES:(j % 2 + 1) * LANES] for j in range(SPLIT_SIZES[0] // LANES)]
    k, v = pool.out["kv"]
    kk_all = jnp.concatenate([kprev_ref[...], k], axis=0)
    vv_all = jnp.concatenate([vprev_ref[...], v], axis=0)
    k_tail, v_tail = k[tm - WINDOW:], v[tm - WINDOW:]
    kprev_ref[...] = k_tail
    vprev_ref[...] = v_tail
    k_out_ref[...] = k_tail
    v_out_ref[...] = v_tail
    lane = lax.broadcasted_iota(jnp.int32, kk_all.shape, 1)
    k_lo0 = jnp.where(lane < HEAD_DIM, kk_all, 0.0)
    k_hi1 = jnp.where(lane >= HEAD_DIM, kk_all, 0.0)
    k_halves = ((k_lo0.astype(BF16), pltpu.roll(k_lo0, HEAD_DIM, 1).astype(BF16)),
                (pltpu.roll(k_hi1, HEAD_DIM, 1).astype(BF16), k_hi1.astype(BF16)))
    v_rows = vv_all.T
    dim = lax.broadcasted_iota(jnp.int32, v_rows.shape, 0)
    v_t = (jnp.where(dim < HEAD_DIM, v_rows, 1.0).astype(BF16),
           jnp.where(dim >= HEAD_DIM, v_rows, 1.0).astype(BF16))

    n_blk = tm // WINDOW
    chain_keys = lambda blk: [("att", blk, kv, half) for kv in range(N_KV_HEADS) for half in range(2)]
    for blk in range(n_blk):
        lo = blk * WINDOW
        variant = jnp.where(first, 0, 1) if blk == 0 else 1
        for kv in range(N_KV_HEADS):
            q_pair = jnp.concatenate([q[2 * kv + j][lo:lo + WINDOW] for j in range(GROUP // 2)], axis=0)
            for half in range(2):
                pool.add(("att", blk, kv, half),
                         _attn_chain(q_pair, k_halves[kv][half][lo:lo + 2 * WINDOW], v_t[kv][:, lo:lo + 2 * WINDOW],
                                     bias_ref.at[variant, kv, half], sink_ref[GROUP * kv + half],
                                     sink_ref[GROUP * kv + 2 + half], kv))

    waiting = list(range(n_blk))
    while pool.live:
        pool.round()
        for blk in [b for b in waiting if pool.has(chain_keys(b))]:
            waiting.remove(blk)
            parts = []
            for hq in range(N_Q_HEADS):
                kv, j, half = hq // GROUP, (hq % GROUP) // 2, hq % 2
                parts.append(pool.out[("att", blk, kv, half)][:, j * WINDOW:(j + 1) * WINDOW])
            pool.add(("up", blk), _attn_up(parts, wua_ref))

    up_a = jnp.concatenate([pool.out[("up", blk)] for blk in range(n_blk)], axis=0)
    sga, sgb = pool.out["gates"]
    heads = [pool.out[("hgrn", hd)] for hd in range(HG_HEADS)]

    def finish(orr_heads):
        orr = jnp.concatenate(orr_heads, axis=1)
        merged = sga * up_a + sgb * _dot(orr, wur_ref[...])
        h2_ref[...] = h + _dot(merged.astype(BF16), wo_ref[...])

    finish([o for o, _, _ in heads])

    worst = functools.reduce(jnp.minimum, [w for _, w, _ in heads])

    @pl.when(jnp.min(worst) < -EXP_CLAMP)
    def _():
        finish([_hgrn_redo(redo, hg_ref[...], ones_ref[...]) for _, _, redo in heads])


def _mix(h, consts, batch, seq, tm):
    t, d = h.shape
    nt = seq // tm
    row = pl.BlockSpec((tm, d), lambda b, i: (b * nt + i, 0))
    win_out = pl.BlockSpec((WINDOW, LANES), lambda b, i: (b, 0))
    return pl.pallas_call(
        _mix_kernel,
        grid=(batch, nt),
        in_specs=[row] + [pl.BlockSpec(memory_space=pltpu.SMEM) if a.ndim == 1 else _const_spec(a) for a in consts],
        out_specs=[row, win_out, win_out, pl.BlockSpec((1, HG_HEADS, HG_DIM, HG_DIM), lambda b, i: (b, 0, 0, 0))],
        out_shape=[jax.ShapeDtypeStruct((t, d), F32),
                   jax.ShapeDtypeStruct((batch * WINDOW, LANES), F32),
                   jax.ShapeDtypeStruct((batch * WINDOW, LANES), F32),
                   jax.ShapeDtypeStruct((batch, HG_HEADS, HG_DIM, HG_DIM), F32)],
        scratch_shapes=[pltpu.VMEM((WINDOW, LANES), F32), pltpu.VMEM((WINDOW, LANES), F32),
                        pltpu.VMEM((HG_HEADS, HG_DIM, HG_DIM), F32)],
        compiler_params=pltpu.CompilerParams(dimension_semantics=("parallel", "arbitrary"),
                                             vmem_limit_bytes=VMEM_LIMIT),
        name="mix",
    )(h, *consts)


def _proj_kernel(h_ref, gm_ref, win_ref, qg_ref, kg_ref, seg_ref,
                 q_ref, k_ref, v_ref, qr_ref, fr_ref, ir_ref, gr_ref):
    u = _rms_rows(h_ref[...], gm_ref[...]).astype(BF16)
    q, k, v = _project_qkv(u, win_ref, qg_ref[...], kg_ref[...], seg_ref[...])
    q_ref[...] = q
    k_ref[...] = k
    v_ref[...] = v
    qr_ref[...] = _dot(u, win_ref[:, O_QR:O_FR])
    fr_ref[...] = _dot(u, win_ref[:, O_FR:O_IR])
    ir_ref[...] = _dot(u, win_ref[:, O_IR:O_GR]).astype(ir_ref.dtype)
    gr_ref[...] = _dot(u, win_ref[:, O_GR:O_GA])


def _proj(h, gm, w_in, qg, kg, seg, tm):
    t, d = h.shape
    row = lambda w: pl.BlockSpec((tm, w), lambda i: (i, 0))
    widths = SPLIT_SIZES[:7]
    dtypes = (BF16, F32, F32, F32, F32, BF16, F32)
    return pl.pallas_call(
        _proj_kernel,
        grid=(t // tm,),
        in_specs=[row(d)] + [_const_spec(a) for a in (gm, w_in, qg, kg, seg)],
        out_specs=[row(w) for w in widths],
        out_shape=[jax.ShapeDtypeStruct((t, w), dt) for w, dt in zip(widths, dtypes)],
        compiler_params=pltpu.CompilerParams(dimension_semantics=("parallel",), vmem_limit_bytes=VMEM_LIMIT),
        name="proj",
    )(h, gm, w_in, qg, kg, seg)


def _out_kernel(h_ref, att_ref, orr_ref, gm_ref, win_ref, wua_ref, wur_ref, wo_ref, h2_ref):
    h = h_ref[...]
    u = _rms_rows(h, gm_ref[...]).astype(BF16)
    sga = jax.nn.sigmoid(_dot(u, win_ref[:, O_GA:O_GB]))
    sgb = jax.nn.sigmoid(_dot(u, win_ref[:, O_GB:O_END]))
    h2_ref[...] = _merge_out(h, sga, sgb, att_ref[...], orr_ref[...], wua_ref, wur_ref, wo_ref)


def _out(h, att, orr, consts, tm):
    t, d = h.shape
    row = lambda a: pl.BlockSpec((tm, a.shape[1]), lambda i: (i, 0))
    return pl.pallas_call(
        _out_kernel,
        grid=(t // tm,),
        in_specs=[row(a) for a in (h, att, orr)] + [_const_spec(a) for a in consts],
        out_specs=pl.BlockSpec((tm, d), lambda i: (i, 0)),
        out_shape=jax.ShapeDtypeStruct((t, d), F32),
        compiler_params=pltpu.CompilerParams(dimension_semantics=("parallel",), vmem_limit_bytes=VMEM_LIMIT),
        name="out",
    )(h, att, orr, *consts)


def _bucket_map():
    dist = np.arange(WINDOW)[:, None] + WINDOW - np.arange(2 * WINDOW)[None, :]
    valid = (dist >= 0) & (dist <= WINDOW)
    return np.where(valid, _t5_bucket(dist), -1).astype(np.int32)


def _stacked(bias, rows):
    return bias.reshape(N_KV_HEADS, GROUP * rows, bias.shape[-1])


def kernel(x_prompt, x_sample, cache_win_k, cache_win_v, state_hgrn, ffn1_norm, ffn1_w1, ffn1_w3, ffn1_w2, mix_norm, w_in, q_norm, k_norm, sinks, rel_bias_table, hgrn_lb_logits, hg_norm, w_up_attn, w_up_hgrn, w_out, ffn2_norm, ffn2_w1, ffn2_w3, ffn2_w2):
    depth = ffn1_norm.shape[0]
    assert depth == 1 and hgrn_lb_logits.shape[0] == 2, "single-layer step only"
    batch, seq, d = x_prompt.shape
    bd, ld, _ = x_sample.shape

    bf = lambda w: w[0].astype(BF16)
    row = lambda g: g[0].reshape(1, -1).astype(F32)
    w_in_b = bf(w_in)
    wua, wur, wo = bf(w_up_attn), bf(w_up_hgrn), bf(w_out)
    g1, gm, g2 = row(ffn1_norm), row(mix_norm), row(ffn2_norm)
    qg = jnp.tile(row(q_norm), (1, LANES // HEAD_DIM)) * (HEAD_DIM ** -0.5 * LOG2E)
    kg = jnp.tile(row(k_norm), (1, LANES // HEAD_DIM))
    head_of_lane = np.arange(2 * LANES) // HEAD_DIM
    seg2 = jnp.asarray((head_of_lane[:, None] == head_of_lane[None, :]).astype(np.float32), BF16)
    seg = seg2[:LANES, :LANES]
    qg2 = jnp.tile(qg, (1, 2))
    ones = jnp.ones((HG_DIM, HG_DIM), BF16)
    tri = jnp.asarray(np.tril(np.ones((HG_CHUNK, HG_CHUNK), np.float32)), BF16)
    lb_logits = hgrn_lb_logits.astype(F32)
    hg_gain = row(hg_norm)

    later = _rel_bias(rel_bias_table.astype(F32) * LOG2E, np.ascontiguousarray(_bucket_map().T))
    first = jnp.where(np.arange(2 * WINDOW)[None, :, None] >= WINDOW, later, NEG_BIG)
    def paired(b):
        b = b.reshape(N_KV_HEADS, GROUP // 2, 2, 2 * WINDOW, WINDOW)
        return b.transpose(0, 2, 3, 1, 4).reshape(N_KV_HEADS, 2, 2 * WINDOW, 2 * WINDOW)
    bias_p = jnp.stack([paired(first), paired(later)])
    bias_s = _stacked(later[:, :WINDOW + ld, :ld].transpose(0, 2, 1), ld)
    sink_1d = sinks[0].astype(F32) * LOG2E
    sink = sink_1d.reshape(N_KV_HEADS, GROUP, 1)
    sink_s = jnp.repeat(sink, ld, axis=1).reshape(N_KV_HEADS, GROUP * ld, 1)

    h_p, w1a, w3a, w2a = _ffn_cast(x_prompt.reshape(batch * seq, d), g1, ffn1_w1[0], ffn1_w3[0], ffn1_w2[0], TM_FFN)
    h_s = _ffn(x_sample.reshape(bd * ld, d), g1, w1a, w3a, w2a, TM_FFN)

    consts = (gm, w_in_b, qg2, kg, seg, seg2, bias_p, sink_1d, lb_logits, hg_gain, tri, ones, wua, wur, wo)
    h2_p, k_p, v_p, s_p = _mix(h_p, consts, batch, seq, TM_MIX)
    win = lambda a: a.reshape(1, batch, WINDOW, N_KV_HEADS, HEAD_DIM)

    q, k, v, qr, fr, ir, gr = _proj(h_s, gm, w_in_b, qg, kg, seg, TM_SAMPLE)
    ck = cache_win_k[0].reshape(bd, WINDOW, LANES)
    cv = cache_win_v[0].reshape(bd, WINDOW, LANES)
    att, nk, nv = _swa_sample(q, k, v, ck, cv, bias_s, sink_s, ld, SEQS_PER_STEP)
    orr, s_s = _hgrn_sample(qr, fr, ir, gr, state_hgrn[0], lb_logits, hg_gain, ld, SEQS_PER_STEP)
    h2_s = _out(h_s, att, orr, (gm, w_in_b, wua, wur, wo), TM_SAMPLE)

    y_p, w1b, w3b, w2b = _ffn_cast(h2_p, g2, ffn2_w1[0], ffn2_w3[0], ffn2_w2[0], TM_FFN)
    y_s = _ffn(h2_s, g2, w1b, w3b, w2b, TM_FFN)
    unwin = lambda a: a.reshape(1, bd, WINDOW, N_KV_HEADS, HEAD_DIM)
    return (y_p.reshape(batch, seq, d), y_s.reshape(bd, ld, d), win(k_p), win(v_p), s_p[None],
            unwin(nk), unwin(nv), s_s[None])
```

```python
import functools

import numpy as np
import jax
import jax.numpy as jnp
from jax import lax
from jax.experimental import pallas as pl
from jax.experimental.pallas import tpu as pltpu

F32 = jnp.float32
BF16 = jnp.bfloat16

LANES = 128
HEAD_DIM = 64
N_Q_HEADS = 8
N_KV_HEADS = 2
GROUP = N_Q_HEADS // N_KV_HEADS
WINDOW = 128
N_BUCKETS = 32
MAX_DISTANCE = 128
HG_HEADS = 4
HG_DIM = 128
HG_CHUNK = 128
HG_SUB = 128
EPS = 1e-6
F_TINY = 1e-30
NEG_BIG = -1e30
EXP_CLAMP = 120.0
LOG2E = 1.4426950408889634
SPLIT_SIZES = (512, 128, 128, 512, 512, 512, 512, 1024, 1024)
SPLIT_OFFS = tuple(int(v) for v in np.cumsum((0,) + SPLIT_SIZES))
O_Q, O_K, O_V, O_QR, O_FR, O_IR, O_GR, O_GA, O_GB, O_END = SPLIT_OFFS
VMEM_LIMIT = 56 * 1024 * 1024
TM_FFN = 1024
NORM_BLOCKS = 4
TM_MIX = 512
TM_SAMPLE = 256
SEQS_PER_STEP = 16


def _dot(a, b):
    return jnp.dot(a, b, preferred_element_type=F32)


def _dot_nt(a, b):
    return lax.dot_general(a, b, (((1,), (1,)), ((), ())), preferred_element_type=F32)


def _split2_ldot(w, x):
    hi = x.astype(BF16)
    lo = (x - hi.astype(F32)).astype(BF16)
    return _dot(w, hi) + _dot(w, lo)


def _rms_rows(x, g):
    ms = jnp.mean(x * x, axis=-1, keepdims=True)
    return x * lax.rsqrt(ms + EPS) * g


def _seg_rms(x, seg, inv_n, g):
    ms = _dot((x * x).astype(BF16), seg) * inv_n
    return x * lax.rsqrt(ms + EPS) * g


class _Pool:
    def __init__(self):
        self.live = []
        self.out = {}

    def add(self, key, gen):
        self.live.append((key, gen))

    def round(self):
        still = []
        for key, gen in self.live:
            try:
                next(gen)
                still.append((key, gen))
            except StopIteration as done:
                self.out[key] = done.value
        self.live = still

    def has(self, keys):
        return all(k in self.out for k in keys)


def _const_spec(a):
    nd = a.ndim
    return pl.BlockSpec(a.shape, lambda *_: (0,) * nd, pipeline_mode=pl.Buffered(1))


def _ff_chunks(d_ff, step=1024):
    return tuple((lo, min(lo + step, d_ff)) for lo in range(0, d_ff, step))


def _swiglu(xn_blocks, w1_ref, w3_ref, w2_ref):
    xn = None
    acc = None
    for c, (lo, hi) in enumerate(_ff_chunks(w1_ref.shape[1])):
        if c == 0:
            a = jnp.concatenate([_dot(blk, w1_ref[:, lo:hi]) for blk in xn_blocks], axis=0)
            b = jnp.concatenate([_dot(blk, w3_ref[:, lo:hi]) for blk in xn_blocks], axis=0)
            xn = jnp.concatenate(xn_blocks, axis=0)
        else:
            a = _dot(xn, w1_ref[:, lo:hi])
            b = _dot(xn, w3_ref[:, lo:hi])
        part = _dot((jax.nn.silu(a) * b).astype(BF16), w2_ref[lo:hi, :])
        acc = part if acc is None else acc + part
    return acc


def _normed_blocks(x_ref, g, blocks):
    rows = x_ref.shape[0] // blocks
    return [_rms_rows(x_ref[r * rows:(r + 1) * rows, :], g).astype(BF16) for r in range(blocks)]


def _ffn_kernel(x_ref, g1_ref, w1_ref, w3_ref, w2_ref, h_ref):
    xn = _normed_blocks(x_ref, g1_ref[...], NORM_BLOCKS)
    h_ref[...] = x_ref[...] + 0.5 * _swiglu(xn, w1_ref, w3_ref, w2_ref)


CAST_CHUNKS = 8


def _cast_copies(pairs, stage_refs, sem):
    out = []
    for (src, dst), stage in zip(pairs, stage_refs):
        rows = src.shape[0] // CAST_CHUNKS
        for c in range(CAST_CHUNKS):
            slot = len(out) % 2
            staged = stage.at[slot]
            out.append((pltpu.make_async_copy(src.at[pl.ds(c * rows, rows), :], staged, sem.at[slot]),
                        staged, dst.at[pl.ds(c * rows, rows), :]))
    return out


def _ffn_cast_kernel(x_ref, g1_ref, w1_hbm, w3_hbm, w2_hbm, h_ref, w1_out, w3_out, w2_out,
                     w1_ref, w3_ref, w2_ref, stage_in_ref, stage_out_ref, sem_in, sem_out):
    step = pl.program_id(0)
    resident = ((w1_ref, w1_out), (w3_ref, w3_out), (w2_ref, w2_out))
    writebacks = [pltpu.make_async_copy(src, dst, sem_out.at[k]) for k, (src, dst) in enumerate(resident)]

    @pl.when(step == 0)
    def _():
        copies = _cast_copies(((w1_hbm, w1_ref), (w3_hbm, w3_ref), (w2_hbm, w2_ref)),
                              (stage_in_ref, stage_in_ref, stage_out_ref), sem_in)
        copies[0][0].start()
        for j, (dma, staged, dst) in enumerate(copies):
            if j + 1 < len(copies):
                copies[j + 1][0].start()
            dma.wait()
            dst[...] = staged[...].astype(BF16)
        for wb in writebacks:
            wb.start()

    xn = _normed_blocks(x_ref, g1_ref[...], NORM_BLOCKS)
    h_ref[...] = x_ref[...] + 0.5 * _swiglu(xn, w1_ref, w3_ref, w2_ref)

    @pl.when(step == pl.num_programs(0) - 1)
    def _():
        for wb in writebacks:
            wb.wait()


def _ffn_cast(x, g1, w1, w3, w2, tm):
    t, d = x.shape
    d_ff = w1.shape[1]
    row = pl.BlockSpec((tm, d), lambda i: (i, 0))
    hbm = pl.BlockSpec(memory_space=pl.ANY)
    bf = lambda w: jax.ShapeDtypeStruct(w.shape, BF16)
    return pl.pallas_call(
        _ffn_cast_kernel,
        grid=(t // tm,),
        in_specs=[row, _const_spec(g1), hbm, hbm, hbm],
        out_specs=[row, hbm, hbm, hbm],
        out_shape=[jax.ShapeDtypeStruct((t, d), F32), bf(w1), bf(w3), bf(w2)],
        scratch_shapes=[pltpu.VMEM(w1.shape, BF16), pltpu.VMEM(w3.shape, BF16), pltpu.VMEM(w2.shape, BF16),
                        pltpu.VMEM((2, d // CAST_CHUNKS, d_ff), F32), pltpu.VMEM((2, d_ff // CAST_CHUNKS, d), F32),
                        pltpu.SemaphoreType.DMA((2,)), pltpu.SemaphoreType.DMA((3,))],
        compiler_params=pltpu.CompilerParams(dimension_semantics=("arbitrary",), vmem_limit_bytes=VMEM_LIMIT),
        name="ffn_cast",
    )(x, g1, w1, w3, w2)


def _ffn(x, g1, w1, w3, w2, tm):
    t, d = x.shape
    row = pl.BlockSpec((tm, d), lambda i: (i, 0))
    return pl.pallas_call(
        _ffn_kernel,
        grid=(t // tm,),
        in_specs=[row] + [_const_spec(a) for a in (g1, w1, w3, w2)],
        out_specs=row,
        out_shape=jax.ShapeDtypeStruct((t, d), F32),
        compiler_params=pltpu.CompilerParams(dimension_semantics=("parallel",), vmem_limit_bytes=VMEM_LIMIT),
        name="ffn",
    )(x, g1, w1, w3, w2)


def _t5_bucket(dist):
    max_exact = N_BUCKETS // 2
    d = np.maximum(dist, 0)
    large = max_exact + (np.log(np.maximum(d, 1) / max_exact) / np.log(MAX_DISTANCE / max_exact)
                         * (N_BUCKETS - max_exact)).astype(np.int32)
    large = np.minimum(large, N_BUCKETS - 1)
    return np.where(d < max_exact, d, large).astype(np.int32)


def _bias_kernel(table_ref, bucket_ref, out_ref):
    bucket = bucket_ref[...]
    masked = jnp.where(bucket < 0, NEG_BIG, 0.0).astype(F32)
    for h in range(N_Q_HEADS):
        acc = masked
        for b in range(N_BUCKETS):
            acc = acc + jnp.where(bucket == b, table_ref[b, h], 0.0)
        out_ref[h] = acc


def _rel_bias(table, bucket_map):
    r, c = bucket_map.shape
    return pl.pallas_call(
        _bias_kernel,
        in_specs=[pl.BlockSpec(memory_space=pltpu.SMEM), pl.BlockSpec((r, c), lambda: (0, 0))],
        out_specs=pl.BlockSpec((N_Q_HEADS, r, c), lambda: (0, 0, 0)),
        out_shape=jax.ShapeDtypeStruct((N_Q_HEADS, r, c), F32),
        name="rel_bias",
    )(table, jnp.asarray(bucket_map))


def _project_qkv(u, win_ref, qg, kg, seg):
    qs = []
    for s in range(SPLIT_SIZES[0] // LANES):
        x = _dot(u, win_ref[:, O_Q + s * LANES:O_Q + (s + 1) * LANES])
        qs.append(_seg_rms(x, seg, 1.0 / HEAD_DIM, qg).astype(BF16))
    k = _seg_rms(_dot(u, win_ref[:, O_K:O_V]), seg, 1.0 / HEAD_DIM, kg)
    v = _dot(u, win_ref[:, O_V:O_QR])
    return jnp.concatenate(qs, axis=1), k, v


def _stack_group(q, kv):
    a = q[:, (2 * kv) * LANES:(2 * kv + 1) * LANES]
    b = q[:, (2 * kv + 1) * LANES:(2 * kv + 2) * LANES]
    ar, br = pltpu.roll(a, HEAD_DIM, 1), pltpu.roll(b, HEAD_DIM, 1)
    parts = (a, ar, b, br) if kv == 0 else (ar, a, br, b)
    return jnp.concatenate(parts, axis=0)


def _kv_half(x, kv, fill):
    lane = lax.broadcasted_iota(jnp.int32, x.shape, 1)
    keep = (lane < HEAD_DIM) if kv == 0 else (lane >= HEAD_DIM)
    return jnp.where(keep, x, fill)


def _attend_chain(q, kk, vv, bias_ref, sink_ref, r, kv):
    qs = _stack_group(q.astype(F32), kv).astype(BF16)
    s = _dot_nt(qs, _kv_half(kk, kv, 0.0).astype(BF16)) + bias_ref[kv]
    yield
    sink = sink_ref[kv]
    m = jnp.maximum(jnp.max(s, axis=-1, keepdims=True), sink)
    e = jnp.exp2(s - m).astype(BF16)
    yield
    o = _dot(e, _kv_half(vv, kv, 1.0).astype(BF16))
    yield
    o = o / (pltpu.roll(o, HEAD_DIM, 1) + jnp.exp2(sink - m))
    lane = lax.broadcasted_iota(jnp.int32, (r, LANES), 1)
    slabs = []
    for pair in range(GROUP // 2):
        a, b = o[(2 * pair) * r:(2 * pair + 1) * r], o[(2 * pair + 1) * r:(2 * pair + 2) * r]
        if kv == 0:
            slabs.append(jnp.where(lane < HEAD_DIM, a, pltpu.roll(b, HEAD_DIM, 1)))
        else:
            slabs.append(jnp.where(lane < HEAD_DIM, pltpu.roll(a, HEAD_DIM, 1), b))
    return slabs


def _swa_sample_kernel(q_ref, kn_ref, vn_ref, ck_ref, cv_ref, bias_ref, sink_ref, o_ref, nk_ref, nv_ref, *, ld):
    pool = _Pool()
    nseq = ck_ref.shape[0]
    for i in range(nseq):
        rows = slice(i * ld, (i + 1) * ld)
        kk = jnp.concatenate([ck_ref[i], kn_ref[rows, :]], axis=0)
        vv = jnp.concatenate([cv_ref[i], vn_ref[rows, :]], axis=0)
        nk_ref[i] = kk[ld:, :]
        nv_ref[i] = vv[ld:, :]
        for kv in range(N_KV_HEADS):
            pool.add((i, kv), _attend_chain(q_ref[rows, :], kk, vv, bias_ref, sink_ref, ld, kv))
    while pool.live:
        pool.round()
    for i in range(nseq):
        slabs = [s for kv in range(N_KV_HEADS) for s in pool.out[(i, kv)]]
        o_ref[i * ld:(i + 1) * ld, :] = jnp.concatenate(slabs, axis=1).astype(o_ref.dtype)


def _swa_sample(q, k, v, cache_k, cache_v, bias, sink_col, ld, sb):
    bd = cache_k.shape[0]
    tok = lambda w: pl.BlockSpec((sb * ld, w), lambda i: (i, 0))
    cache = pl.BlockSpec((sb, WINDOW, LANES), lambda i: (i, 0, 0))
    return pl.pallas_call(
        functools.partial(_swa_sample_kernel, ld=ld),
        grid=(bd // sb,),
        in_specs=[tok(4 * LANES), tok(LANES), tok(LANES), cache, cache,
                  pl.BlockSpec(bias.shape, lambda i: (0, 0, 0)), pl.BlockSpec(sink_col.shape, lambda i: (0, 0, 0))],
        out_specs=[tok(4 * LANES), cache, cache],
        out_shape=[jax.ShapeDtypeStruct(q.shape, BF16), jax.ShapeDtypeStruct(cache_k.shape, F32),
                   jax.ShapeDtypeStruct(cache_v.shape, F32)],
        compiler_params=pltpu.CompilerParams(dimension_semantics=("parallel",)),
        name="swa_sample",
    )(q, k, v, cache_k, cache_v, bias, sink_col)


def _lower_bound(lb_logits):
    z = lb_logits - jnp.max(lb_logits, axis=0, keepdims=True)
    e = jnp.exp(z)
    return e[0:1, :] / jnp.sum(e, axis=0, keepdims=True)


def _gates(qr, fr, lb):
    f = lb + (1.0 - lb) * jax.nn.sigmoid(fr)
    return jax.nn.silu(qr), 1.0 - f, jnp.log2(jnp.maximum(f, F_TINY))


def _head_out(o, sgr, hg_gain, ones_seg):
    return _seg_rms(o, ones_seg, 1.0 / HG_DIM, hg_gain) * sgr


def _hgrn_proj(u, win_ref, pair, lb):
    cols = lambda off: slice(off + 2 * pair * HG_DIM, off + 2 * (pair + 1) * HG_DIM)
    blocks = u if isinstance(u, list) else [u]
    qr = jnp.concatenate([_dot(blk, win_ref[:, cols(O_QR)]) for blk in blocks], axis=0)
    u = jnp.concatenate(blocks, axis=0)
    fr = _dot(u, win_ref[:, cols(O_FR)])
    yield
    qf, kk, g = _gates(qr, fr, lb[:, 2 * pair * HG_DIM:2 * (pair + 1) * HG_DIM])
    ir = _dot(u, win_ref[:, cols(O_IR)]).astype(BF16)
    gr = _dot(u, win_ref[:, cols(O_GR)])
    yield
    return qf, kk, g, ir, jax.nn.silu(gr)


def _hgrn_head(qf, kk, g, vals, sgr, hd, s_ref, s_out_ref, hg_gain, ones, tri):
    nchunk = qf.shape[0] // HG_CHUNK
    chunks = [slice(c * HG_CHUNK, (c + 1) * HG_CHUNK) for c in range(nchunk)]
    gcs = [_split2_ldot(tri, g[rows]) for rows in chunks]
    yield
    q_in, upd, decay, scores = [], [], [], []
    worst = jnp.zeros((1, HG_DIM), F32)
    for rows, gc in zip(chunks, gcs):
        g_last = gc[HG_CHUNK - 1:HG_CHUNK, :]
        q_in.append((qf[rows] * jnp.exp2(gc)).astype(BF16))
        k_end_t = (kk[rows] * jnp.exp2(g_last - gc)).T
        upd.append(_dot(k_end_t.astype(BF16), vals[rows]))
        decay.append(jnp.broadcast_to(jnp.exp2(g_last), (HG_DIM, HG_DIM)).T)
        qc, kc = qf[rows], kk[rows]
        for i in range(HG_CHUNK // HG_SUB):
            lo, hi = i * HG_SUB, (i + 1) * HG_SUB
            mid = lo + HG_SUB // 2
            g_start = gc[lo - 1:lo, :] if i else jnp.zeros((1, HG_DIM), F32)
            g_mid = gc[mid - 1:mid, :]
            worst = jnp.minimum(worst, jnp.minimum(gc[hi - 1:hi, :] - g_mid, g_mid - g_start))
            q_hat = qc[lo:hi] * jnp.exp2(jnp.minimum(gc[lo:hi] - g_mid, EXP_CLAMP))
            k_hat = kc[:hi] * jnp.exp2(jnp.minimum(g_mid - gc[:hi], EXP_CLAMP))
            a = _dot_nt(q_hat.astype(BF16), k_hat.astype(BF16))
            row = lax.broadcasted_iota(jnp.int32, (HG_SUB, hi), 0)
            col = lax.broadcasted_iota(jnp.int32, (HG_SUB, hi), 1)
            scores.append(jnp.where(col <= row + lo, a, 0.0).astype(BF16))
    yield
    intra = []
    nsub = HG_CHUNK // HG_SUB
    for c, rows in enumerate(chunks):
        vc = vals[rows]
        parts = [_dot(scores[c * nsub + i], vc[:(i + 1) * HG_SUB]) for i in range(nsub)]
        intra.append(jnp.concatenate(parts, axis=0))
    yield
    s = s_ref[hd]
    inter = []
    for c in range(nchunk):
        inter.append(_dot(q_in[c], s.astype(BF16)))
        s = decay[c] * s + upd[c]
    s_ref[hd] = s
    s_out_ref[0, hd] = s
    yield
    o = jnp.concatenate([a + b for a, b in zip(inter, intra)], axis=0)
    redo = dict(qf=qf, kk=kk, gcs=gcs, vals=vals, inter=inter, sgr=sgr)
    return _head_out(o, sgr, hg_gain, ones).astype(BF16), worst, redo


def _exact_intra(qc, kc, gc, vc):
    c = qc.shape[0]
    group = 8
    row = lax.broadcasted_iota(jnp.int32, (c, c), 0)
    col = lax.broadcasted_iota(jnp.int32, (c, c), 1)
    col_g = lax.broadcasted_iota(jnp.int32, (group, c), 1)
    earlier = []
    for i in range(c // group):
        lo, hi = i * group, (i + 1) * group
        g_base = gc[lo - 1:lo, :] if i else jnp.zeros((1, HG_DIM), F32)
        q_hat = qc[lo:hi] * jnp.exp2(gc[lo:hi] - g_base)
        k_hat = kc * jnp.exp2(jnp.minimum(g_base - gc, 0.0))
        earlier.append(jnp.where(col_g < lo, _dot_nt(q_hat.astype(BF16), k_hat.astype(BF16)), 0.0))
    a = jnp.concatenate(earlier, axis=0)
    qb = qc.astype(BF16)
    g3 = gc.reshape(c // group, group, HG_DIM)
    own = (col <= row) & (col >= row - lax.rem(row, group))
    for p in range(group):
        g_p = jnp.broadcast_to(g3[:, p:p + 1, :], g3.shape).reshape(c, HG_DIM)
        k_p = kc * jnp.exp2(jnp.minimum(g_p - gc, 0.0))
        a = a + jnp.where(own & (lax.rem(row, group) == p), _dot_nt(qb, k_p.astype(BF16)), 0.0)
    return _dot(a.astype(BF16), vc)


def _hgrn_redo(redo, hg_gain, ones):
    outs = []
    for c, gc in enumerate(redo["gcs"]):
        rows = slice(c * HG_CHUNK, (c + 1) * HG_CHUNK)
        outs.append(redo["inter"][c] + _exact_intra(redo["qf"][rows], redo["kk"][rows], gc, redo["vals"][rows]))
    return _head_out(jnp.concatenate(outs, axis=0), redo["sgr"], hg_gain, ones).astype(BF16)


def _attn_chain(q_pair, k_half, v_t, bias_ref, sink_a, sink_b, kv):
    s = _dot_nt(k_half, q_pair) + bias_ref[...]
    yield
    col = lax.broadcasted_iota(jnp.int32, (1, s.shape[1]), 1)
    sink = jnp.where(col < WINDOW, sink_a, sink_b)
    m = jnp.maximum(jnp.max(s, axis=0, keepdims=True), sink)
    e = jnp.exp2(s - m).astype(BF16)
    yield
    o_full = _dot(v_t, e)
    other = (1 - kv) * HEAD_DIM
    denom = o_full[other:other + 1] + jnp.exp2(sink - m)
    yield
    return o_full[kv * HEAD_DIM:(kv + 1) * HEAD_DIM] * (1.0 / denom)


def _gate_stream(u, win_ref, step=256):
    res = []
    for off in (O_GA, O_GB):
        cols = []
        for lo in range(0, SPLIT_SIZES[7], step):
            cols.append(jax.nn.sigmoid(_dot(u, win_ref[:, off + lo:off + lo + step])))
            yield
        res.append(jnp.concatenate(cols, axis=1))
    return res


def _q_cols(u, win_ref, kv, seg2, gain2):
    w = win_ref[:, O_Q + 2 * kv * LANES:O_Q + 2 * (kv + 1) * LANES]
    x = jnp.concatenate([_dot(blk, w) for blk in u], axis=0) if isinstance(u, list) else _dot(u, w)
    yield
    ms = _dot((x * x).astype(BF16), seg2) * (1.0 / HEAD_DIM)
    yield
    return (x * lax.rsqrt(ms + EPS) * gain2).astype(BF16)


def _kv_cols(u, win_ref, seg, gain):
    x = _dot(u, win_ref[:, O_K:O_QR])
    yield
    k, v = x[:, :LANES], x[:, LANES:]
    ms = _dot((k * k).astype(BF16), seg) * (1.0 / HEAD_DIM)
    yield
    return k * lax.rsqrt(ms + EPS) * gain, v


def _attn_up(parts, wua_ref):
    att = jnp.concatenate(parts, axis=0).T.astype(BF16)
    yield
    return _dot(att, wua_ref[...])


def _hgrn_sample_kernel(qr_ref, fr_ref, ir_ref, gr_ref, s0_ref, lb_ref, hg_ref, tri_ref, tot_ref, ones_ref,
                        o_ref, s_out_ref, *, ld):
    nseq = s0_ref.shape[0]
    r = nseq * ld
    lb = _lower_bound(lb_ref[...])
    qf_all, kk_all, g_all = _gates(qr_ref[...], fr_ref[...], lb)
    tri, tot = tri_ref[...], tot_ref[...]
    causal = tri.astype(F32) > 0
    lane = lax.broadcasted_iota(jnp.int32, (HG_DIM, r), 1)
    pos = lax.rem(lax.broadcasted_iota(jnp.int32, (r, r), 0), ld)

    def head(hd):
        lanes = slice(hd * HG_DIM, (hd + 1) * HG_DIM)
        qf, kk, v = qf_all[:, lanes], kk_all[:, lanes], ir_ref[:, lanes]
        g = _split2_ldot(tri, g_all[:, lanes])
        g_last = _split2_ldot(tot, g_all[:, lanes])
        yield
        q_t = (qf * jnp.exp2(g)).astype(BF16)
        qb = qf.astype(BF16)
        g3 = g.reshape(nseq, ld, HG_DIM)
        a = jnp.zeros((r, r), F32)
        for p in range(ld):
            g_p = jnp.broadcast_to(g3[:, p:p + 1, :], g3.shape).reshape(r, HG_DIM)
            k_p = kk * jnp.exp2(jnp.minimum(g_p - g, 0.0))
            a = a + jnp.where(causal & (pos == p), _dot_nt(qb, k_p.astype(BF16)), 0.0)
        k_end_t = (kk * jnp.exp2(g_last - g)).T
        decay_t = jnp.exp2(g_last).T
        yield
        intra = _dot(a.astype(BF16), v)
        outs = []
        for i in range(nseq):
            s_prev = s0_ref[i, hd]
            outs.append(_dot(q_t[i * ld:(i + 1) * ld], s_prev.astype(BF16)))
            own = (lane >= i * ld) & (lane < (i + 1) * ld)
            upd = _dot(jnp.where(own, k_end_t, 0.0).astype(BF16), v)
            s_out_ref[i, hd] = decay_t[:, i * ld:i * ld + 1] * s_prev + upd
            if i % 2:
                yield
        o = jnp.concatenate(outs, axis=0) + intra
        yield
        o_ref[:, lanes] = _head_out(o, jax.nn.silu(gr_ref[:, lanes]), hg_ref[...], ones_ref[...]).astype(o_ref.dtype)

    pool = _Pool()
    for hd in range(HG_HEADS):
        pool.add(hd, head(hd))
    while pool.live:
        pool.round()


def _hgrn_sample(qr, fr, ir, gr, s0, lb_logits, hg_gain, ld, sb):
    bd = s0.shape[0]
    r = sb * ld
    seq = np.arange(r) // ld
    same = seq[:, None] == seq[None, :]
    tri = jnp.asarray((same & (np.arange(r)[:, None] >= np.arange(r)[None, :])).astype(np.float32), BF16)
    tot = jnp.asarray(same.astype(np.float32), BF16)
    ones = jnp.ones((HG_DIM, HG_DIM), BF16)
    tok = pl.BlockSpec((r, HG_HEADS * HG_DIM), lambda i: (i, 0))
    state = pl.BlockSpec((sb, HG_HEADS, HG_DIM, HG_DIM), lambda i: (i, 0, 0, 0))
    const = lambda a: pl.BlockSpec(a.shape, lambda i: (0,) * a.ndim)
    return pl.pallas_call(
        functools.partial(_hgrn_sample_kernel, ld=ld),
        grid=(bd // sb,),
        in_specs=[tok, tok, tok, tok, state, const(lb_logits), const(hg_gain), const(tri), const(tot), const(ones)],
        out_specs=[tok, state],
        out_shape=[jax.ShapeDtypeStruct(qr.shape, BF16), jax.ShapeDtypeStruct(s0.shape, F32)],
        compiler_params=pltpu.CompilerParams(dimension_semantics=("parallel",)),
        name="hgrn_sample",
    )(qr, fr, ir, gr, s0, lb_logits, hg_gain, tri, tot, ones)


def _merge_out(h, sga, sgb, att, orr, wua_ref, wur_ref, wo_ref):
    merged = sga * _dot(att, wua_ref[...]) + sgb * _dot(orr, wur_ref[...])
    return h + _dot(merged.astype(BF16), wo_ref[...])


def _mix_kernel(h_ref, gm_ref, win_ref, qg_ref, kg_ref, seg_ref, seg2_ref, bias_ref, sink_ref, lb_ref, hg_ref,
                tri_ref, ones_ref, wua_ref, wur_ref, wo_ref,
                h2_ref, k_out_ref, v_out_ref, s_out_ref, kprev_ref, vprev_ref, s_ref):
    tm = h_ref.shape[0]
    first = pl.program_id(1) == 0

    @pl.when(first)
    def _():
        kprev_ref[...] = jnp.zeros(kprev_ref.shape, F32)
        vprev_ref[...] = jnp.zeros(vprev_ref.shape, F32)
        s_ref[...] = jnp.zeros(s_ref.shape, F32)

    h = h_ref[...]
    u_blocks = _normed_blocks(h_ref, gm_ref[...], NORM_BLOCKS)
    u = jnp.concatenate(u_blocks, axis=0)

    pool = _Pool()
    lb = _lower_bound(lb_ref[...])
    for pair in range(HG_HEADS // 2):
        pool.add(("hproj", pair), _hgrn_proj(u_blocks if pair == 0 else u, win_ref, pair, lb))
    while not pool.has([("hproj", p) for p in range(HG_HEADS // 2)]):
        pool.round()
    for hd in range(HG_HEADS):
        qf, kk, g, ir, sgr = (a[:, (hd % 2) * HG_DIM:(hd % 2 + 1) * HG_DIM] for a in pool.out[("hproj", hd // 2)])
        pool.add(("hgrn", hd), _hgrn_head(qf, kk, g, ir, sgr, hd, s_ref, s_out_ref, hg_ref[...], ones_ref[...],
                                          tri_ref[...]))
    for kv in range(N_KV_HEADS):
        pool.add(("q", kv), _q_cols(u, win_ref, kv, seg2_ref[...], qg_ref[...]))
    pool.add("kv", _kv_cols(u, win_ref, seg_ref[...], kg_ref[...]))
    pool.add("gates", _gate_stream(u, win_ref))
    while not pool.has([("q", kv) for kv in range(N_KV_HEADS)] + ["kv"]):
        pool.round()

    q = [pool.out[("q", j // 2)][:, (j % 2) * LANES:(j % 2 + 1) * LANES] for j in range(SPLIT_SIZES[0] // LANES)]
    k, v = pool.out["kv"]
    kk_all = jnp.concatenate([kprev_ref[...], k], axis=0)
    vv_all = jnp.concatenate([vprev_ref[...], v], axis=0)
    k_tail, v_tail = k[tm - WINDOW:], v[tm - WINDOW:]
    kprev_ref[...] = k_tail
    vprev_ref[...] = v_tail
    k_out_ref[...] = k_tail
    v_out_ref[...] = v_tail
    lane = lax.broadcasted_iota(jnp.int32, kk_all.shape, 1)
    k_lo0 = jnp.where(lane < HEAD_DIM, kk_all, 0.0)
    k_hi1 = jnp.where(lane >= HEAD_DIM, kk_all, 0.0)
    k_halves = ((k_lo0.astype(BF16), pltpu.roll(k_lo0, HEAD_DIM, 1).astype(BF16)),
                (pltpu.roll(k_hi1, HEAD_DIM, 1).astype(BF16), k_hi1.astype(BF16)))
    v_rows = vv_all.T
    dim = lax.broadcasted_iota(jnp.int32, v_rows.shape, 0)
    v_t = (jnp.where(dim < HEAD_DIM, v_rows, 1.0).astype(BF16),
           jnp.where(dim >= HEAD_DIM, v_rows, 1.0).astype(BF16))

    n_blk = tm // WINDOW
    chain_keys = lambda blk: [("att", blk, kv, half) for kv in range(N_KV_HEADS) for half in range(2)]

    def add_block(blk):
        lo = blk * WINDOW
        variant = jnp.where(first, 0, 1) if blk == 0 else 1
        for kv in range(N_KV_HEADS):
            q_pair = jnp.concatenate([q[2 * kv + j][lo:lo + WINDOW] for j in range(GROUP // 2)], axis=0)
            for half in range(2):
                pool.add(("att", blk, kv, half),
                         _attn_chain(q_pair, k_halves[kv][half][lo:lo + 2 * WINDOW], v_t[kv][:, lo:lo + 2 * WINDOW],
                                     bias_ref.at[variant, kv, half], sink_ref[GROUP * kv + half],
                                     sink_ref[GROUP * kv + 2 + half], kv))

    pending = list(range(n_blk))
    waiting = list(range(n_blk))
    while pool.live or pending:
        if pending:
            add_block(pending.pop(0))
        pool.round()
        for blk in [b for b in waiting if pool.has(chain_keys(b))]:
            waiting.remove(blk)
            parts = []
            for hq in range(N_Q_HEADS):
                kv, j, half = hq // GROUP, (hq % GROUP) // 2, hq % 2
                parts.append(pool.out[("att", blk, kv, half)][:, j * WINDOW:(j + 1) * WINDOW])
            pool.add(("up", blk), _attn_up(parts, wua_ref))

    up_a = jnp.concatenate([pool.out[("up", blk)] for blk in range(n_blk)], axis=0)
    sga, sgb = pool.out["gates"]
    heads = [pool.out[("hgrn", hd)] for hd in range(HG_HEADS)]

    def finish(orr_heads):
        orr = jnp.concatenate(orr_heads, axis=1)
        merged = sga * up_a + sgb * _dot(orr, wur_ref[...])
        h2_ref[...] = h + _dot(merged.astype(BF16), wo_ref[...])

    finish([o for o, _, _ in heads])

    worst = functools.reduce(jnp.minimum, [w for _, w, _ in heads])

    @pl.when(jnp.min(worst) < -EXP_CLAMP)
    def _():
        finish([_hgrn_redo(redo, hg_ref[...], ones_ref[...]) for _, _, redo in heads])


def _mix(h, consts, batch, seq, tm):
    t, d = h.shape
    nt = seq // tm
    row = pl.BlockSpec((tm, d), lambda b, i: (b * nt + i, 0))
    win_out = pl.BlockSpec((WINDOW, LANES), lambda b, i: (b, 0))
    return pl.pallas_call(
        _mix_kernel,
        grid=(batch, nt),
        in_specs=[row] + [pl.BlockSpec(memory_space=pltpu.SMEM) if a.ndim == 1 else _const_spec(a) for a in consts],
        out_specs=[row, win_out, win_out, pl.BlockSpec((1, HG_HEADS, HG_DIM, HG_DIM), lambda b, i: (b, 0, 0, 0))],
        out_shape=[jax.ShapeDtypeStruct((t, d), F32),
                   jax.ShapeDtypeStruct((batch * WINDOW, LANES), F32),
                   jax.ShapeDtypeStruct((batch * WINDOW, LANES), F32),
                   jax.ShapeDtypeStruct((batch, HG_HEADS, HG_DIM, HG_DIM), F32)],
        scratch_shapes=[pltpu.VMEM((WINDOW, LANES), F32), pltpu.VMEM((WINDOW, LANES), F32),
                        pltpu.VMEM((HG_HEADS, HG_DIM, HG_DIM), F32)],
        compiler_params=pltpu.CompilerParams(dimension_semantics=("parallel", "arbitrary"),
                                             vmem_limit_bytes=VMEM_LIMIT),
        name="mix",
    )(h, *consts)


def _proj_kernel(h_ref, gm_ref, win_ref, qg_ref, kg_ref, seg_ref,
                 q_ref, k_ref, v_ref, qr_ref, fr_ref, ir_ref, gr_ref):
    u = _rms_rows(h_ref[...], gm_ref[...]).astype(BF16)
    q, k, v = _project_qkv(u, win_ref, qg_ref[...], kg_ref[...], seg_ref[...])
    q_ref[...] = q
    k_ref[...] = k
    v_ref[...] = v
    qr_ref[...] = _dot(u, win_ref[:, O_QR:O_FR])
    fr_ref[...] = _dot(u, win_ref[:, O_FR:O_IR])
    ir_ref[...] = _dot(u, win_ref[:, O_IR:O_GR]).astype(ir_ref.dtype)
    gr_ref[...] = _dot(u, win_ref[:, O_GR:O_GA])


def _proj(h, gm, w_in, qg, kg, seg, tm):
    t, d = h.shape
    row = lambda w: pl.BlockSpec((tm, w), lambda i: (i, 0))
    widths = SPLIT_SIZES[:7]
    dtypes = (BF16, F32, F32, F32, F32, BF16, F32)
    return pl.pallas_call(
        _proj_kernel,
        grid=(t // tm,),
        in_specs=[row(d)] + [_const_spec(a) for a in (gm, w_in, qg, kg, seg)],
        out_specs=[row(w) for w in widths],
        out_shape=[jax.ShapeDtypeStruct((t, w), dt) for w, dt in zip(widths, dtypes)],
        compiler_params=pltpu.CompilerParams(dimension_semantics=("parallel",), vmem_limit_bytes=VMEM_LIMIT),
        name="proj",
    )(h, gm, w_in, qg, kg, seg)


def _out_kernel(h_ref, att_ref, orr_ref, gm_ref, win_ref, wua_ref, wur_ref, wo_ref, h2_ref):
    h = h_ref[...]
    u = _rms_rows(h, gm_ref[...]).astype(BF16)
    sga = jax.nn.sigmoid(_dot(u, win_ref[:, O_GA:O_GB]))
    sgb = jax.nn.sigmoid(_dot(u, win_ref[:, O_GB:O_END]))
    h2_ref[...] = _merge_out(h, sga, sgb, att_ref[...], orr_ref[...], wua_ref, wur_ref, wo_ref)


def _out(h, att, orr, consts, tm):
    t, d = h.shape
    row = lambda a: pl.BlockSpec((tm, a.shape[1]), lambda i: (i, 0))
    return pl.pallas_call(
        _out_kernel,
        grid=(t // tm,),
        in_specs=[row(a) for a in (h, att, orr)] + [_const_spec(a) for a in consts],
        out_specs=pl.BlockSpec((tm, d), lambda i: (i, 0)),
        out_shape=jax.ShapeDtypeStruct((t, d), F32),
        compiler_params=pltpu.CompilerParams(dimension_semantics=("parallel",), vmem_limit_bytes=VMEM_LIMIT),
        name="out",
    )(h, att, orr, *consts)


def _bucket_map():
    dist = np.arange(WINDOW)[:, None] + WINDOW - np.arange(2 * WINDOW)[None, :]
    valid = (dist >= 0) & (dist <= WINDOW)
    return np.where(valid, _t5_bucket(dist), -1).astype(np.int32)


def _stacked(bias, rows):
    return bias.reshape(N_KV_HEADS, GROUP * rows, bias.shape[-1])


def kernel(x_prompt, x_sample, cache_win_k, cache_win_v, state_hgrn, ffn1_norm, ffn1_w1, ffn1_w3, ffn1_w2, mix_norm, w_in, q_norm, k_norm, sinks, rel_bias_table, hgrn_lb_logits, hg_norm, w_up_attn, w_up_hgrn, w_out, ffn2_norm, ffn2_w1, ffn2_w3, ffn2_w2):
    depth = ffn1_norm.shape[0]
    assert depth == 1 and hgrn_lb_logits.shape[0] == 2, "single-layer step only"
    batch, seq, d = x_prompt.shape
    bd, ld, _ = x_sample.shape

    bf = lambda w: w[0].astype(BF16)
    row = lambda g: g[0].reshape(1, -1).astype(F32)
    w_in_b = bf(w_in)
    wua, wur, wo = bf(w_up_attn), bf(w_up_hgrn), bf(w_out)
    g1, gm, g2 = row(ffn1_norm), row(mix_norm), row(ffn2_norm)
    qg = jnp.tile(row(q_norm), (1, LANES // HEAD_DIM)) * (HEAD_DIM ** -0.5 * LOG2E)
    kg = jnp.tile(row(k_norm), (1, LANES // HEAD_DIM))
    head_of_lane = np.arange(2 * LANES) // HEAD_DIM
    seg2 = jnp.asarray((head_of_lane[:, None] == head_of_lane[None, :]).astype(np.float32), BF16)
    seg = seg2[:LANES, :LANES]
    qg2 = jnp.tile(qg, (1, 2))
    ones = jnp.ones((HG_DIM, HG_DIM), BF16)
    tri = jnp.asarray(np.tril(np.ones((HG_CHUNK, HG_CHUNK), np.float32)), BF16)
    lb_logits = hgrn_lb_logits.astype(F32)
    hg_gain = row(hg_norm)

    later = _rel_bias(rel_bias_table.astype(F32) * LOG2E, np.ascontiguousarray(_bucket_map().T))
    first = jnp.where(np.arange(2 * WINDOW)[None, :, None] >= WINDOW, later, NEG_BIG)
    def paired(b):
        b = b.reshape(N_KV_HEADS, GROUP // 2, 2, 2 * WINDOW, WINDOW)
        return b.transpose(0, 2, 3, 1, 4).reshape(N_KV_HEADS, 2, 2 * WINDOW, 2 * WINDOW)
    bias_p = jnp.stack([paired(first), paired(later)])
    bias_s = _stacked(later[:, :WINDOW + ld, :ld].transpose(0, 2, 1), ld)
    sink_1d = sinks[0].astype(F32) * LOG2E
    sink = sink_1d.reshape(N_KV_HEADS, GROUP, 1)
    sink_s = jnp.repeat(sink, ld, axis=1).reshape(N_KV_HEADS, GROUP * ld, 1)

    h_p, w1a, w3a, w2a = _ffn_cast(x_prompt.reshape(batch * seq, d), g1, ffn1_w1[0], ffn1_w3[0], ffn1_w2[0], TM_FFN)
    h_s = _ffn(x_sample.reshape(bd * ld, d), g1, w1a, w3a, w2a, TM_FFN)

    consts = (gm, w_in_b, qg2, kg, seg, seg2, bias_p, sink_1d, lb_logits, hg_gain, tri, ones, wua, wur, wo)
    h2_p, k_p, v_p, s_p = _mix(h_p, consts, batch, seq, TM_MIX)
    win = lambda a: a.reshape(1, batch, WINDOW, N_KV_HEADS, HEAD_DIM)

    q, k, v, qr, fr, ir, gr = _proj(h_s, gm, w_in_b, qg, kg, seg, TM_SAMPLE)
    ck = cache_win_k[0].reshape(bd, WINDOW, LANES)
    cv = cache_win_v[0].reshape(bd, WINDOW, LANES)
    att, nk, nv = _swa_sample(q, k, v, ck, cv, bias_s, sink_s, ld, SEQS_PER_STEP)
    orr, s_s = _hgrn_sample(qr, fr, ir, gr, state_hgrn[0], lb_logits, hg_gain, ld, SEQS_PER_STEP)
    h2_s = _out(h_s, att, orr, (gm, w_in_b, wua, wur, wo), TM_SAMPLE)

    y_p, w1b, w3b, w2b = _ffn_cast(h2_p, g2, ffn2_w1[0], ffn2_w3[0], ffn2_w2[0], TM_FFN)
    y_s = _ffn(h2_s, g2, w1b, w3b, w2b, TM_FFN)
    unwin = lambda a: a.reshape(1, bd, WINDOW, N_KV_HEADS, HEAD_DIM)
    return (y_p.reshape(batch, seq, d), y_s.reshape(bd, ld, d), win(k_p), win(v_p), s_p[None],
            unwin(nk), unwin(nv), s_s[None])
```
